```python
import jax
import jax.numpy as jnp
from jax import lax
import numpy as np

D_MODEL = 1024
BATCH = 8
SEQ = 2048
DEPTH = 2

HEAD_DIM = 64
N_HEADS = 8
N_KV = 2
HPG = N_HEADS // N_KV
ATTN_W = N_HEADS * HEAD_DIM
KV_W = N_KV * HEAD_DIM
CONV_CH = D_MODEL - ATTN_W
CONV_W = 3
D_MIX = ATTN_W + CONV_CH
CMP_BLOCK = 32
CMP_STRIDE = 16
CMP_HIDDEN = 256
SEL_BLOCK = 64
SEL_TOPN = 16
WINDOW = 512
WIN_QBLOCK = 128
SEL_QCHUNK = 64
D_FF = 2816
N_EXPERTS = 8
TOP_K = 2
D_FF_EXPERT = 1408
N_DENSE = (DEPTH + 1) // 2
N_MOE = DEPTH // 2
ALPHA = (2 * DEPTH) ** 0.25
BETA = (8 * DEPTH) ** -0.25
LN_EPS = 1e-5
NEG = -1e30
FORCE = 1e9
SPLIT_WIDTHS = (ATTN_W, KV_W, KV_W, KV_W, KV_W, KV_W, KV_W, 3 * N_HEADS, CONV_CH, CONV_CH, CONV_CH)
D_IN = sum(SPLIT_WIDTHS)

kernel_name = 'hymba_nsa_shortconv_deepnorm_moe'


def layer_norm(x, g, b):
    xf = x.astype(jnp.float32)
    mu = jnp.mean(xf, -1, keepdims=True)
    var = jnp.mean(jnp.square(xf - mu), -1, keepdims=True)
    return ((xf - mu) * lax.rsqrt(var + LN_EPS) * g + b).astype(x.dtype)


def compress(k, pos, w1, b1, w2, b2):
    bsz, g, s, dh = k.shape
    ratio = CMP_BLOCK // CMP_STRIDE
    n_cmp = s // CMP_STRIDE - ratio + 1
    blocks = k.reshape(bsz, g, s // CMP_STRIDE, CMP_STRIDE, dh)
    win = jnp.concatenate([blocks[:, :, r:r + n_cmp] for r in range(ratio)], axis=3)
    win = (win + pos).reshape(bsz, g, n_cmp, CMP_BLOCK * dh)
    return jax.nn.gelu(win @ w1 + b1) @ w2 + b2


def cmp_attention(q, kc, vc):
    s_len = q.shape[3]
    n = kc.shape[2]
    t = jnp.arange(s_len)
    ends = jnp.arange(n) * CMP_STRIDE + CMP_BLOCK - 1
    mask = ends[None, :] <= t[:, None]
    sc = jnp.einsum('bghsd,bgnd->bghsn', q, kc).astype(jnp.float32) * HEAD_DIM ** -0.5
    p = jax.nn.softmax(jnp.where(mask, sc, NEG), axis=-1) * mask
    o = jnp.einsum('bghsn,bgnd->bghsd', p.astype(vc.dtype), vc)
    return o, p


def select_blocks(p_cmp, s_len):
    n = p_cmp.shape[-1]
    n_sel = s_len // SEL_BLOCK
    cs = jnp.arange(n) * CMP_STRIDE
    ss = jnp.arange(n_sel) * SEL_BLOCK
    overlap = ((cs[:, None] < ss[None, :] + SEL_BLOCK) & (cs[:, None] + CMP_BLOCK > ss[None, :])).astype(jnp.float32)
    imp = jnp.einsum('bghsn,nj->bgsj', p_cmp, overlap)
    t = jnp.arange(s_len)[:, None]
    j = jnp.arange(n_sel)[None, :]
    cur = t // SEL_BLOCK
    valid = j * SEL_BLOCK <= t
    forced = (j == 0) | (j == cur) | (j == cur - 1)
    score = jnp.where(forced, FORCE, jnp.where(valid, imp, NEG))
    _, idx = lax.top_k(score, min(SEL_TOPN, n_sel))
    return idx


def sel_attention(q, ks, vs, idx):
    bsz, g, hpg, s_len, dh = q.shape
    n_sel = s_len // SEL_BLOCK
    k_top = idx.shape[-1]
    nc = s_len // SEL_QCHUNK
    kb = ks.reshape(bsz, g, n_sel, SEL_BLOCK, dh)
    vb = vs.reshape(bsz, g, n_sel, SEL_BLOCK, dh)
    qc = q.reshape(bsz, g, hpg, nc, SEL_QCHUNK, dh).transpose(3, 0, 1, 2, 4, 5)
    ic = idx.reshape(bsz, g, nc, SEL_QCHUNK, k_top).transpose(2, 0, 1, 3, 4)
    starts = jnp.arange(nc) * SEL_QCHUNK
    gather = jax.vmap(jax.vmap(lambda blk, ix: blk[ix]))

    def chunk(args):
        qq, ii, start = args
        kg = gather(kb, ii)
        vg = gather(vb, ii)
        sc = jnp.einsum('bghqd,bgqkrd->bghqkr', qq, kg).astype(jnp.float32) * HEAD_DIM ** -0.5
        pos = ii[..., None] * SEL_BLOCK + jnp.arange(SEL_BLOCK)
        tq = start + jnp.arange(SEL_QCHUNK)
        mask = (pos <= tq[None, None, :, None, None])[:, :, None]
        sc = jnp.where(mask, sc, NEG).reshape(bsz, g, hpg, SEL_QCHUNK, k_top * SEL_BLOCK)
        p = jax.nn.softmax(sc, axis=-1).reshape(bsz, g, hpg, SEL_QCHUNK, k_top, SEL_BLOCK)
        return jnp.einsum('bghqkr,bgqkrd->bghqd', p.astype(vg.dtype), vg)

    o = lax.map(chunk, (qc, ic, starts))
    return o.transpose(1, 2, 3, 0, 4, 5).reshape(bsz, g, hpg, s_len, dh)


def win_attention(q, kw, vw):
    bsz, g, hpg, s_len, dh = q.shape
    nb = s_len // WIN_QBLOCK
    span = WINDOW // WIN_QBLOCK
    ksz = (span + 1) * WIN_QBLOCK
    padw = ((0, 0), (0, 0), (WINDOW, 0), (0, 0))
    kp = jnp.pad(kw, padw).reshape(bsz, g, nb + span, WIN_QBLOCK, dh)
    vp = jnp.pad(vw, padw).reshape(bsz, g, nb + span, WIN_QBLOCK, dh)
    kband = jnp.concatenate([kp[:, :, i:i + nb] for i in range(span + 1)], axis=3)
    vband = jnp.concatenate([vp[:, :, i:i + nb] for i in range(span + 1)], axis=3)
    qb = q.reshape(bsz, g, hpg, nb, WIN_QBLOCK, dh)
    sc = jnp.einsum('bghnqd,bgnkd->bghnqk', qb, kband).astype(jnp.float32) * HEAD_DIM ** -0.5
    blk = jnp.arange(nb)[:, None, None] * WIN_QBLOCK
    tq = blk + jnp.arange(WIN_QBLOCK)[None, :, None]
    sk = blk - WINDOW + jnp.arange(ksz)[None, None, :]
    mask = (sk <= tq) & (tq - sk < WINDOW) & (sk >= 0)
    p = jax.nn.softmax(jnp.where(mask, sc, NEG), axis=-1)
    o = jnp.einsum('bghnqk,bgnkd->bghnqd', p.astype(vband.dtype), vband)
    return o.reshape(bsz, g, hpg, s_len, dh)


def short_conv(u, b_gate, c_gate, w):
    s_len = u.shape[1]
    z = jnp.pad(c_gate * u, ((0, 0), (CONV_W - 1, 0), (0, 0)))
    y = z[:, 0:s_len] * w[0]
    for i in range(1, CONV_W):
        y = y + z[:, i:i + s_len] * w[i]
    return b_gate * y


def hybrid_mixer(h, w_in, cmp_pos, cmp_w1, cmp_b1, cmp_w2, cmp_b2, conv_w, w_o):
    bsz, s_len, _ = h.shape
    z = h @ w_in
    offsets = np.cumsum(SPLIT_WIDTHS)[:-1].tolist()
    q, kc, vc, ks, vs, kw, vw, gates, u, bg, cg = jnp.split(z, offsets, axis=-1)
    q = q.reshape(bsz, s_len, N_KV, HPG, HEAD_DIM).transpose(0, 2, 3, 1, 4)

    def kv(t):
        return t.reshape(bsz, s_len, N_KV, HEAD_DIM).transpose(0, 2, 1, 3)

    kc_c = compress(kv(kc), cmp_pos[0], cmp_w1[0], cmp_b1[0], cmp_w2[0], cmp_b2[0])
    vc_c = compress(kv(vc), cmp_pos[1], cmp_w1[1], cmp_b1[1], cmp_w2[1], cmp_b2[1])
    o_cmp, p_cmp = cmp_attention(q, kc_c, vc_c)
    idx = select_blocks(p_cmp, s_len)
    o_sel = sel_attention(q, kv(ks), kv(vs), idx)
    o_win = win_attention(q, kv(kw), kv(vw))
    g = jax.nn.sigmoid(gates).reshape(bsz, s_len, 3, N_KV, HPG).transpose(2, 0, 3, 4, 1)[..., None]
    o_attn = g[0] * o_cmp + g[1] * o_sel + g[2] * o_win
    o_attn = o_attn.transpose(0, 3, 1, 2, 4).reshape(bsz, s_len, ATTN_W)
    o_conv = short_conv(u, bg, cg, conv_w)
    return jnp.concatenate([o_attn, o_conv], axis=-1) @ w_o


def swiglu(h, wg, wu, wd):
    return (jax.nn.silu(h @ wg) * (h @ wu)) @ wd


def moe_swiglu(h, w_router, wg, wu, wd):
    logits = (h @ w_router).astype(jnp.float32)
    top_v, top_i = lax.top_k(logits, TOP_K)
    w_top = jax.nn.softmax(top_v, axis=-1)
    gate = jnp.sum(jax.nn.one_hot(top_i, N_EXPERTS, dtype=jnp.float32) * w_top[..., None], axis=-2)
    out = jnp.zeros_like(h)
    for e in range(N_EXPERTS):
        out = out + gate[..., e:e + 1].astype(h.dtype) * swiglu(h, wg[e], wu[e], wd[e])
    return out


def setup_inputs(seed: int = 0) -> dict:
    key = jax.random.key(seed)
    ks = jax.random.split(key, 24)

    def nrm(k, shape, scale):
        return jax.random.normal(k, shape, jnp.float32) * scale

    return {
        'x': nrm(ks[0], (BATCH, SEQ, D_MODEL), 1.0),
        'ln_in_g': 1.0 + nrm(ks[1], (D_MODEL,), 0.02),
        'ln_in_b': nrm(ks[2], (D_MODEL,), 0.02),
        'w_in': nrm(ks[3], (DEPTH, D_MODEL, D_IN), D_MODEL ** -0.5),
        'cmp_pos': nrm(ks[4], (DEPTH, 2, CMP_BLOCK, HEAD_DIM), 0.1),
        'cmp_w1': nrm(ks[5], (DEPTH, 2, CMP_BLOCK * HEAD_DIM, CMP_HIDDEN), (CMP_BLOCK * HEAD_DIM) ** -0.5),
        'cmp_b1': nrm(ks[6], (DEPTH, 2, CMP_HIDDEN), 0.02),
        'cmp_w2': nrm(ks[7], (DEPTH, 2, CMP_HIDDEN, HEAD_DIM), CMP_HIDDEN ** -0.5),
        'cmp_b2': nrm(ks[8], (DEPTH, 2, HEAD_DIM), 0.02),
        'conv_w': nrm(ks[9], (DEPTH, CONV_W, CONV_CH), CONV_W ** -0.5),
        'w_o': nrm(ks[10], (DEPTH, D_MIX, D_MODEL), BETA * D_MIX ** -0.5),
        'ln1_g': 1.0 + nrm(ks[11], (DEPTH, D_MODEL), 0.02),
        'ln1_b': nrm(ks[12], (DEPTH, D_MODEL), 0.02),
        'ln2_g': 1.0 + nrm(ks[13], (DEPTH, D_MODEL), 0.02),
        'ln2_b': nrm(ks[14], (DEPTH, D_MODEL), 0.02),
        'ffn_wg': nrm(ks[15], (N_DENSE, D_MODEL, D_FF), D_MODEL ** -0.5),
        'ffn_wu': nrm(ks[16], (N_DENSE, D_MODEL, D_FF), D_MODEL ** -0.5),
        'ffn_wd': nrm(ks[17], (N_DENSE, D_FF, D_MODEL), BETA * D_FF ** -0.5),
        'moe_router': nrm(ks[18], (N_MOE, D_MODEL, N_EXPERTS), D_MODEL ** -0.5),
        'moe_wg': nrm(ks[19], (N_MOE, N_EXPERTS, D_MODEL, D_FF_EXPERT), D_MODEL ** -0.5),
        'moe_wu': nrm(ks[20], (N_MOE, N_EXPERTS, D_MODEL, D_FF_EXPERT), D_MODEL ** -0.5),
        'moe_wd': nrm(ks[21], (N_MOE, N_EXPERTS, D_FF_EXPERT, D_MODEL), BETA * D_FF_EXPERT ** -0.5),
    }


def reference(x, ln_in_g, ln_in_b, w_in, cmp_pos, cmp_w1, cmp_b1, cmp_w2, cmp_b2, conv_w, w_o,
              ln1_g, ln1_b, ln2_g, ln2_b, ffn_wg, ffn_wu, ffn_wd,
              moe_router, moe_wg, moe_wu, moe_wd):
    h = layer_norm(x, ln_in_g, ln_in_b)
    for l in range(DEPTH):
        m = hybrid_mixer(h, w_in[l], cmp_pos[l], cmp_w1[l], cmp_b1[l], cmp_w2[l], cmp_b2[l],
                         conv_w[l], w_o[l])
        h = layer_norm(ALPHA * h + m, ln1_g[l], ln1_b[l])
        if l % 2 == 0:
            f = swiglu(h, ffn_wg[l // 2], ffn_wu[l // 2], ffn_wd[l // 2])
        else:
            f = moe_swiglu(h, moe_router[l // 2], moe_wg[l // 2], moe_wu[l // 2], moe_wd[l // 2])
        h = layer_norm(ALPHA * h + f, ln2_g[l], ln2_b[l])
    return h
```

```python
import functools

import numpy as np
import jax
import jax.numpy as jnp
from jax import lax
from jax.experimental import pallas as pl
from jax.experimental.pallas import tpu as pltpu

F32 = jnp.float32
BF16 = jnp.bfloat16

D_MODEL = 1024
HEAD_DIM = 64
N_HEADS = 8
N_KV = 2
HPG = N_HEADS // N_KV
ATTN_W = N_HEADS * HEAD_DIM
KV_W = N_KV * HEAD_DIM
CONV_CH = D_MODEL - ATTN_W
CONV_W = 3
CMP_BLOCK = 32
CMP_STRIDE = 16
CMP_HIDDEN = 256
SEL_BLOCK = 64
SEL_TOPN = 16
WINDOW = 512
D_FF = 2816
N_EXPERTS = 8
D_FF_EXPERT = 1408
LN_EPS = 1e-5
NEG = -1e30
FORCE = 1e9

LANES = 128
HPAD = LANES
Q_SCALE = HEAD_DIM ** -0.5

C_Q = 0
C_KVC = C_Q + N_HEADS * HPAD
C_KS = C_KVC + 2 * KV_W
C_VS = C_KS + N_KV * HPAD
C_KW = C_VS + N_KV * HPAD
C_VW = C_KW + N_KV * HPAD
C_U = C_VW + N_KV * HPAD
C_B = C_U + CONV_CH
C_C = C_B + CONV_CH
C_G = C_C + CONV_CH
C_END = C_G + LANES

TM_PROJ = 512
TQ = 256
TK = 256
VMEM_LIMIT = 56 * 1024 * 1024


def _cparams(sem):
    return pltpu.CompilerParams(dimension_semantics=sem, vmem_limit_bytes=VMEM_LIMIT)


def _ln(x, g, b):
    mu = jnp.mean(x, -1, keepdims=True)
    xc = x - mu
    var = jnp.mean(xc * xc, -1, keepdims=True)
    return xc * lax.rsqrt(var + LN_EPS) * g + b


def _dot(a, b):
    return jnp.dot(a, b, preferred_element_type=F32)


def _dot_nt(a, b):
    return lax.dot_general(a, b, (((1,), (1,)), ((), ())), preferred_element_type=F32)


def _resident(shape):
    nd = len(shape)
    return pl.BlockSpec(shape, lambda *_: (0,) * nd, pipeline_mode=pl.Buffered(1))


def _in_proj_kernel(*refs, pre_ln, tiles_per_seq, tm):
    if pre_ln:
        (x_ref, g_ref, b_ref, w_ref, cw_ref, h_ref, q_ref, kvc_ref, ks_ref, vs_ref,
         kw_ref, vw_ref, gate_ref, oc_ref, carry_ref) = refs
        h = _ln(x_ref[...], g_ref[...], b_ref[...])
        h_ref[...] = h
    else:
        (x_ref, w_ref, cw_ref, q_ref, kvc_ref, ks_ref, vs_ref,
         kw_ref, vw_ref, gate_ref, oc_ref, carry_ref) = refs
        h = x_ref[...]
    hb = h.astype(BF16)

    def proj(lo, hi):
        return _dot(hb, w_ref[:, lo:hi])

    q_ref[...] = (proj(C_Q, C_KVC) * Q_SCALE).astype(BF16)
    kvc = proj(C_KVC, C_KS)
    for j in range(4):
        kvc_ref[j] = kvc[:, j * HEAD_DIM:(j + 1) * HEAD_DIM]

    seq_tile = pl.program_id(0) % tiles_per_seq
    pos = lax.broadcasted_iota(jnp.int32, (tm, N_KV * HPAD), 0) + seq_tile * tm
    lane = lax.broadcasted_iota(jnp.int32, (tm, N_KV * HPAD), 1) % HPAD
    onehot = jnp.where(pos // SEL_BLOCK == lane - HEAD_DIM, 1.0, 0.0)
    in_tag = (lane >= HEAD_DIM) & (lane < HEAD_DIM + 32)
    ks_ref[...] = jnp.where(in_tag, onehot, proj(C_KS, C_VS)).astype(BF16)
    vs_ref[...] = proj(C_VS, C_KW).astype(BF16)
    kw_ref[...] = proj(C_KW, C_VW).astype(BF16)
    vw_ref[...] = proj(C_VW, C_U).astype(BF16)

    cu = proj(C_C, C_G) * proj(C_U, C_B)

    @pl.when(seq_tile == 0)
    def _():
        carry_ref[...] = jnp.zeros_like(carry_ref)

    prev = carry_ref[...]
    row = lax.broadcasted_iota(jnp.int32, (tm, CONV_CH), 0)
    s1 = jnp.where(row == 0, prev[7:8], pltpu.roll(cu, 1, 0))
    s2 = jnp.where(row == 0, prev[6:7], jnp.where(row == 1, prev[7:8], pltpu.roll(cu, 2, 0)))
    y = s2 * cw_ref[0:1, :] + s1 * cw_ref[1:2, :] + cu * cw_ref[2:3, :]
    oc_ref[...] = (proj(C_B, C_C) * y).astype(BF16)
    carry_ref[...] = cu[tm - 8:tm]
    gate_ref[...] = jax.nn.sigmoid(proj(C_G, C_END))


def _in_proj(x, ln_g, ln_b, w_r, conv_w, *, seq, pre_ln):
    t = x.shape[0]
    tm = TM_PROJ
    nt = t // tm
    row = lambda w: pl.BlockSpec((tm, w), lambda i: (i, 0))
    in_specs = [row(D_MODEL)]
    args = [x]
    if pre_ln:
        in_specs += [_resident((1, D_MODEL)), _resident((1, D_MODEL))]
        args += [ln_g, ln_b]
    in_specs += [_resident((D_MODEL, C_END)), _resident((8, CONV_CH))]
    args += [w_r, conv_w]
    out_shape, out_specs = [], []
    if pre_ln:
        out_shape.append(jax.ShapeDtypeStruct((t, D_MODEL), F32))
        out_specs.append(row(D_MODEL))
    out_shape += [
        jax.ShapeDtypeStruct((t, N_HEADS * HPAD), BF16),
        jax.ShapeDtypeStruct((4, t, HEAD_DIM), F32),
        jax.ShapeDtypeStruct((t, N_KV * HPAD), BF16),
        jax.ShapeDtypeStruct((t, N_KV * HPAD), BF16),
        jax.ShapeDtypeStruct((t, N_KV * HPAD), BF16),
        jax.ShapeDtypeStruct((t, N_KV * HPAD), BF16),
        jax.ShapeDtypeStruct((t, LANES), F32),
        jax.ShapeDtypeStruct((t, CONV_CH), BF16),
    ]
    out_specs += [
        row(N_HEADS * HPAD),
        pl.BlockSpec((4, tm, HEAD_DIM), lambda i: (0, i, 0)),
        row(N_KV * HPAD), row(N_KV * HPAD), row(N_KV * HPAD), row(N_KV * HPAD),
        row(LANES), row(CONV_CH),
    ]
    return pl.pallas_call(
        functools.partial(_in_proj_kernel, pre_ln=pre_ln, tiles_per_seq=seq // tm, tm=tm),
        grid=(nt,),
        in_specs=in_specs,
        out_specs=out_specs,
        out_shape=out_shape,
        scratch_shapes=[pltpu.VMEM((8, CONV_CH), F32)],
        compiler_params=_cparams(("arbitrary",)),
        name="in_proj_ln" if pre_ln else "in_proj",
    )(*args)


def _prep_w_in(w_in):
    d = w_in.shape[0]
    o = 0

    def take(n):
        nonlocal o
        s = w_in[:, o:o + n]
        o += n
        return s

    def pad_heads(s, n):
        s = s.reshape(d, n, HEAD_DIM)
        return jnp.pad(s, ((0, 0), (0, 0), (0, HPAD - HEAD_DIM))).reshape(d, n * HPAD)

    q = pad_heads(take(ATTN_W), N_HEADS)
    kc, vc = take(KV_W), take(KV_W)
    ks, vs, kw, vw = (pad_heads(take(KV_W), N_KV) for _ in range(4))
    gates = jnp.pad(take(3 * N_HEADS), ((0, 0), (0, LANES - 3 * N_HEADS)))
    u, bg, cg = take(CONV_CH), take(CONV_CH), take(CONV_CH)
    return jnp.concatenate([q, kc, vc, ks, vs, kw, vw, u, bg, cg, gates], axis=1).astype(BF16)


def _compress_kernel(x_ref, pos_ref, w1_ref, b1_ref, w2_ref, b2_ref, o_ref):
    half = CMP_STRIDE * HEAD_DIM
    x = x_ref[0, 0]
    pos = pos_ref[0]
    xa = (x + pos[:, :half]).astype(BF16)
    xb = (x + pos[:, half:]).astype(BF16)
    a = _dot(xa, w1_ref[0, :half, :])
    b = _dot(xb, w1_ref[0, half:, :])
    n = x.shape[0]
    hid = a + pltpu.roll(b, n - 1, 0) + b1_ref[0]
    act = jax.nn.gelu(hid).astype(BF16)
    o_ref[0, 0] = (_dot(act, w2_ref[0]) + b2_ref[0]).astype(BF16)


def _compress(kvc, cmp_pos, cmp_w1, cmp_b1, cmp_w2, cmp_b2, *, batch, seq):
    nb = seq // CMP_STRIDE
    half = CMP_STRIDE * HEAD_DIM
    x = kvc.reshape(4, batch, nb, half)
    pos = cmp_pos.reshape(2, 1, CMP_BLOCK * HEAD_DIM)
    w1 = cmp_w1.astype(BF16)
    b1 = cmp_b1.reshape(2, 1, CMP_HIDDEN)
    w2 = jnp.pad(cmp_w2, ((0, 0), (0, 0), (0, HPAD - HEAD_DIM))).astype(BF16)
    b2 = jnp.pad(cmp_b2, ((0, 0), (0, HPAD - HEAD_DIM))).reshape(2, 1, HPAD)
    kind = lambda j, b: (j // N_KV, 0, 0)
    return pl.pallas_call(
        _compress_kernel,
        grid=(4, batch),
        in_specs=[
            pl.BlockSpec((1, 1, nb, half), lambda j, b: (j, b, 0, 0)),
            pl.BlockSpec((1, 1, 2 * half), kind),
            pl.BlockSpec((1, 2 * half, CMP_HIDDEN), kind),
            pl.BlockSpec((1, 1, CMP_HIDDEN), kind),
            pl.BlockSpec((1, CMP_HIDDEN, HPAD), kind),
            pl.BlockSpec((1, 1, HPAD), kind),
        ],
        out_specs=pl.BlockSpec((1, 1, nb, HPAD), lambda j, b: (j, b, 0, 0)),
        out_shape=jax.ShapeDtypeStruct((4, batch, nb, HPAD), BF16),
        compiler_params=_cparams(("arbitrary", "arbitrary")),
        name="compress",
    )(x, pos, w1, b1, w2, b2)


def _gate_col(gates, idx):
    lane = lax.broadcasted_iota(jnp.int32, gates.shape, 1)
    return jnp.sum(jnp.where(lane == idx, gates, 0.0), axis=-1, keepdims=True)


def _store_heads(o_ref, heads):
    for p in range(HPG // 2):
        pair = heads[2 * p] + pltpu.roll(heads[2 * p + 1], HEAD_DIM, 1)
        o_ref[:, p * LANES:(p + 1) * LANES] = pair.astype(o_ref.dtype)


def _overlap_t():
    n = np.arange(LANES)
    j = np.arange(32)
    cs = n * CMP_STRIDE
    ss = j * SEL_BLOCK
    ov = (cs[None, :] < ss[:, None] + SEL_BLOCK) & (cs[None, :] + CMP_BLOCK > ss[:, None])
    return jnp.asarray(ov, dtype=BF16)


def _cmp_sel_kernel(q_ref, kc_ref, vc_ref, gate_ref, ov_ref, o_ref, selb_ref, *, tq, n_sel):
    g = pl.program_id(1)
    i = pl.program_id(2)
    kc = kc_ref[0, 0]
    vc = vc_ref[0, 0]
    gates = gate_ref[...]
    t = lax.broadcasted_iota(jnp.int32, (tq, LANES), 0) + i * tq
    n = lax.broadcasted_iota(jnp.int32, (tq, LANES), 1)
    mask = n * CMP_STRIDE + (CMP_BLOCK - 1) <= t
    maskf = mask.astype(F32)
    psum = jnp.zeros((tq, LANES), F32)
    heads = []
    for hh in range(HPG):
        s = _dot_nt(q_ref[:, hh * HPAD:(hh + 1) * HPAD], kc)
        s = jnp.where(mask, s, NEG)
        e = jnp.exp(s - jnp.max(s, axis=-1, keepdims=True))
        p = e / jnp.sum(e, axis=-1, keepdims=True) * maskf
        psum = psum + p
        heads.append(_dot(p.astype(BF16), vc) * _gate_col(gates, g * HPG + hh))
    _store_heads(o_ref, heads)

    ov = ov_ref[...]
    p_hi = psum.astype(BF16)
    r1 = psum - p_hi.astype(F32)
    p_mid = r1.astype(BF16)
    p_lo = (r1 - p_mid.astype(F32)).astype(BF16)
    imp = _dot_nt(ov, p_hi) + _dot_nt(ov, p_mid) + _dot_nt(ov, p_lo)

    jt = lax.broadcasted_iota(jnp.int32, (n_sel, tq), 0)
    tt = lax.broadcasted_iota(jnp.int32, (n_sel, tq), 1) + i * tq
    cur = tt // SEL_BLOCK
    valid = jt * SEL_BLOCK <= tt
    forced = (jt == 0) | (jt == cur) | (jt == cur - 1)
    score = jnp.where(forced, FORCE, jnp.where(valid, imp, NEG))
    rank = jnp.zeros((n_sel, tq), jnp.int32)
    for k in range(n_sel):
        sk = score[k:k + 1, :]
        ahead = (sk > score) | ((sk == score) & (jt > k))
        rank = rank + ahead.astype(jnp.int32)
    bias = jnp.where(rank < SEL_TOPN, 0.0, NEG)
    full = jnp.concatenate(
        [jnp.zeros((HEAD_DIM, tq), F32), bias, jnp.zeros((HPAD - HEAD_DIM - n_sel, tq), F32)], axis=0)
    selb_ref[...] = full.T.astype(BF16)


def _cmp_sel(q, kvcmp, gates, *, batch, seq):
    t = q.shape[0]
    tq = TQ
    nq = seq // tq
    n_sel = seq // SEL_BLOCK
    nb = kvcmp.shape[2]
    assert nb == LANES and n_sel == 32
    rowblk = lambda w: pl.BlockSpec((tq, w), lambda b, g, i: (b * nq + i, g))
    return pl.pallas_call(
        functools.partial(_cmp_sel_kernel, tq=tq, n_sel=n_sel),
        grid=(batch, N_KV, nq),
        in_specs=[
            rowblk(HPG * HPAD),
            pl.BlockSpec((1, 1, nb, HPAD), lambda b, g, i: (g, b, 0, 0)),
            pl.BlockSpec((1, 1, nb, HPAD), lambda b, g, i: (N_KV + g, b, 0, 0)),
            pl.BlockSpec((tq, LANES), lambda b, g, i: (b * nq + i, 0)),
            pl.BlockSpec((n_sel, LANES), lambda b, g, i: (0, 0)),
        ],
        out_specs=[rowblk(HPG * HEAD_DIM), rowblk(LANES)],
        out_shape=[jax.ShapeDtypeStruct((t, ATTN_W), BF16),
                   jax.ShapeDtypeStruct((t, N_KV * LANES), BF16)],
        compiler_params=_cparams(("arbitrary", "arbitrary", "arbitrary")),
        name="cmp_sel",
    )(q, kvcmp, kvcmp, gates, _overlap_t())


def _stack_heads(q_ref, selb=None):
    qs = []
    for hh in range(HPG):
        qh = q_ref[:, hh * HPAD:(hh + 1) * HPAD]
        if selb is not None:
            lane = lax.broadcasted_iota(jnp.int32, qh.shape, 1)
            qh = jnp.where(lane >= HEAD_DIM, selb, qh)
        qs.append(qh)
    return jnp.concatenate(qs, axis=0)


def _sel_attn_kernel(q_ref, selb_ref, k_ref, v_ref, gate_ref, o_ref, *, tq, tk):
    g = pl.program_id(1)
    i = pl.program_id(2)
    qa = _stack_heads(q_ref, selb_ref[...])
    rows = HPG * tq

    def step(j, carry, diag):
        m, l, acc = carry
        start = pl.multiple_of(j * tk, tk)
        k = k_ref[pl.ds(start, tk), :]
        v = v_ref[pl.ds(start, tk), :]
        s = _dot_nt(qa, k)
        if diag:
            r = lax.broadcasted_iota(jnp.int32, (rows, tk), 0) % tq
            c = lax.broadcasted_iota(jnp.int32, (rows, tk), 1)
            s = jnp.where(c <= r, s, NEG)
        m_new = jnp.maximum(m, jnp.max(s, axis=-1, keepdims=True))
        alpha = jnp.exp(m - m_new)
        p = jnp.exp(s - m_new)
        l = alpha * l + jnp.sum(p, axis=-1, keepdims=True)
        acc = alpha * acc + _dot(p.astype(BF16), v)
        return m_new, l, acc

    init = (jnp.full((rows, 1), NEG, F32), jnp.zeros((rows, 1), F32), jnp.zeros((rows, HPAD), F32))
    carry = lax.fori_loop(0, i, lambda j, c: step(j, c, False), init)
    _, l, acc = step(i, carry, True)
    o = acc / l
    gates = gate_ref[...]
    heads = [o[hh * tq:(hh + 1) * tq] * _gate_col(gates, N_HEADS + g * HPG + hh) for hh in range(HPG)]
    _store_heads(o_ref, heads)


def _sel_attn(q, selb, ks, vs, gates, *, batch, seq):
    t = q.shape[0]
    tq, tk = TQ, TK
    assert tq == tk
    nq = seq // tq
    rowblk = lambda w: pl.BlockSpec((tq, w), lambda b, g, i: (b * nq + i, g))
    seqblk = pl.BlockSpec((seq, HPAD), lambda b, g, i: (b, g))
    return pl.pallas_call(
        functools.partial(_sel_attn_kernel, tq=tq, tk=tk),
        grid=(batch, N_KV, nq),
        in_specs=[rowblk(HPG * HPAD), rowblk(LANES), seqblk, seqblk,
                  pl.BlockSpec((tq, LANES), lambda b, g, i: (b * nq + i, 0))],
        out_specs=rowblk(HPG * HEAD_DIM),
        out_shape=jax.ShapeDtypeStruct((t, ATTN_W), BF16),
        compiler_params=_cparams(("arbitrary", "arbitrary", "arbitrary")),
        name="sel_attn",
    )(q, selb, ks, vs, gates)


def _win_attn_kernel(q_ref, k_ref, v_ref, gate_ref, o_ref, *, tq, tk):
    g = pl.program_id(1)
    i = pl.program_id(2)
    qa = _stack_heads(q_ref)
    rows = HPG * tq
    r = lax.broadcasted_iota(jnp.int32, (rows, tk), 0) % tq
    c = lax.broadcasted_iota(jnp.int32, (rows, tk), 1)
    masks = (((c > r) & (i >= 2)), jnp.broadcast_to(i >= 1, (rows, tk)), c <= r)
    ss, vs = [], []
    for d, mask in zip((2, 1, 0), masks):
        start = pl.multiple_of(jnp.maximum(i - d, 0) * tk, tk)
        s = _dot_nt(qa, k_ref[pl.ds(start, tk), :])
        ss.append(jnp.where(mask, s, NEG))
        vs.append(v_ref[pl.ds(start, tk), :])
    m = jnp.maximum(jnp.maximum(jnp.max(ss[0], axis=-1, keepdims=True),
                                jnp.max(ss[1], axis=-1, keepdims=True)),
                    jnp.max(ss[2], axis=-1, keepdims=True))
    ps = [jnp.exp(s - m) for s in ss]
    l = (jnp.sum(ps[0], axis=-1, keepdims=True) + jnp.sum(ps[1], axis=-1, keepdims=True)
         + jnp.sum(ps[2], axis=-1, keepdims=True))
    acc = _dot(ps[0].astype(BF16), vs[0]) + _dot(ps[1].astype(BF16), vs[1]) + _dot(ps[2].astype(BF16), vs[2])
    o = acc / l
    gates = gate_ref[...]
    heads = [o[hh * tq:(hh + 1) * tq] * _gate_col(gates, 2 * N_HEADS + g * HPG + hh) for hh in range(HPG)]
    _store_heads(o_ref, heads)


def _win_attn(q, kw, vw, gates, *, batch, seq):
    t = q.shape[0]
    tq, tk = TQ, TK
    assert tq == tk and WINDOW == 2 * tk
    nq = seq // tq
    rowblk = lambda w: pl.BlockSpec((tq, w), lambda b, g, i: (b * nq + i, g))
    seqblk = pl.BlockSpec((seq, HPAD), lambda b, g, i: (b, g))
    return pl.pallas_call(
        functools.partial(_win_attn_kernel, tq=tq, tk=tk),
        grid=(batch, N_KV, nq),
        in_specs=[rowblk(HPG * HPAD), seqblk, seqblk,
                  pl.BlockSpec((tq, LANES), lambda b, g, i: (b * nq + i, 0))],
        out_specs=rowblk(HPG * HEAD_DIM),
        out_shape=jax.ShapeDtypeStruct((t, ATTN_W), BF16),
        compiler_params=_cparams(("arbitrary", "arbitrary", "arbitrary")),
        name="win_attn",
    )(q, kw, vw, gates)


def _out_proj_kernel(h_ref, oc_ref, os_ref, ow_ref, ocv_ref, w_ref, g_ref, b_ref, o_ref, *, alpha):
    oa = (oc_ref[...].astype(F32) + os_ref[...].astype(F32) + ow_ref[...].astype(F32)).astype(BF16)
    m = _dot(oa, w_ref[:ATTN_W, :]) + _dot(ocv_ref[...], w_ref[ATTN_W:, :])
    o_ref[...] = _ln(alpha * h_ref[...] + m, g_ref[...], b_ref[...])


def _out_proj(h, o_cmp, o_sel, o_win, o_conv, w_o, ln_g, ln_b, *, alpha):
    t = h.shape[0]
    tm = TM_PROJ
    row = lambda w: pl.BlockSpec((tm, w), lambda i: (i, 0))
    return pl.pallas_call(
        functools.partial(_out_proj_kernel, alpha=alpha),
        grid=(t // tm,),
        in_specs=[row(D_MODEL), row(ATTN_W), row(ATTN_W), row(ATTN_W), row(CONV_CH),
                  _resident((D_MODEL, D_MODEL)), _resident((1, D_MODEL)), _resident((1, D_MODEL))],
        out_specs=row(D_MODEL),
        out_shape=jax.ShapeDtypeStruct((t, D_MODEL), F32),
        compiler_params=_cparams(("arbitrary",)),
        name="out_proj",
    )(h, o_cmp, o_sel, o_win, o_conv, w_o, ln_g, ln_b)


def _ffn_kernel(h_ref, wg_ref, wu_ref, wd_ref, g_ref, b_ref, o_ref, *, alpha):
    h = h_ref[...]
    hb = h.astype(BF16)
    a = (jax.nn.silu(_dot(hb, wg_ref[...])) * _dot(hb, wu_ref[...])).astype(BF16)
    o_ref[...] = _ln(alpha * h + _dot(a, wd_ref[...]), g_ref[...], b_ref[...])


def _ffn(h, wg, wu, wd, ln_g, ln_b, *, alpha):
    t = h.shape[0]
    tm = TM_PROJ
    row = pl.BlockSpec((tm, D_MODEL), lambda i: (i, 0))
    return pl.pallas_call(
        functools.partial(_ffn_kernel, alpha=alpha),
        grid=(t // tm,),
        in_specs=[row, _resident((D_MODEL, D_FF)), _resident((D_MODEL, D_FF)), _resident((D_FF, D_MODEL)),
                  _resident((1, D_MODEL)), _resident((1, D_MODEL))],
        out_specs=row,
        out_shape=jax.ShapeDtypeStruct((t, D_MODEL), F32),
        compiler_params=_cparams(("arbitrary",)),
        name="ffn",
    )(h, wg, wu, wd, ln_g, ln_b)


def _router_gates(h, wr_ref):
    h_hi = h.astype(BF16)
    h_lo = (h - h_hi.astype(F32)).astype(BF16)
    logits = _dot(h_hi, wr_ref[0]) + _dot(h_lo, wr_ref[0]) + _dot(h_hi, wr_ref[1])
    lane = lax.broadcasted_iota(jnp.int32, logits.shape, 1)
    logits = jnp.where(lane < N_EXPERTS, logits, -jnp.inf)
    m1 = jnp.max(logits, axis=-1, keepdims=True)
    i1 = jnp.min(jnp.where(logits == m1, lane, LANES), axis=-1, keepdims=True)
    rest = jnp.where(lane == i1, -jnp.inf, logits)
    m2 = jnp.max(rest, axis=-1, keepdims=True)
    i2 = jnp.min(jnp.where(rest == m2, lane, LANES), axis=-1, keepdims=True)
    e2 = jnp.exp(m2 - m1)
    den = 1.0 + e2
    return jnp.where(lane == i1, 1.0 / den, 0.0) + jnp.where(lane == i2, e2 / den, 0.0)


def _moe_kernel(h_ref, wr_ref, wg_ref, wu_ref, wd_ref, g_ref, b_ref, o_ref, gate_ref, acc_ref, *, alpha):
    e = pl.program_id(1)
    h = h_ref[...]

    @pl.when(e == 0)
    def _():
        gate_ref[...] = _router_gates(h, wr_ref)
        acc_ref[...] = jnp.zeros_like(acc_ref)

    hb = h.astype(BF16)
    a = (jax.nn.silu(_dot(hb, wg_ref[0])) * _dot(hb, wu_ref[0])).astype(BF16)
    acc_ref[...] += _gate_col(gate_ref[...], e) * _dot(a, wd_ref[0])

    @pl.when(e == N_EXPERTS - 1)
    def _():
        o_ref[...] = _ln(alpha * h + acc_ref[...], g_ref[...], b_ref[...])


def _moe(h, w_router, wg, wu, wd, ln_g, ln_b, *, alpha):
    t = h.shape[0]
    tm = TM_PROJ
    row = pl.BlockSpec((tm, D_MODEL), lambda i, e: (i, 0))
    wr = jnp.pad(w_router, ((0, 0), (0, LANES - N_EXPERTS)))
    wr_hi = wr.astype(BF16)
    wr_lo = (wr - wr_hi.astype(F32)).astype(BF16)
    wr2 = jnp.stack([wr_hi, wr_lo])
    return pl.pallas_call(
        functools.partial(_moe_kernel, alpha=alpha),
        grid=(t // tm, N_EXPERTS),
        in_specs=[row, pl.BlockSpec((2, D_MODEL, LANES), lambda i, e: (0, 0, 0)),
                  pl.BlockSpec((1, D_MODEL, D_FF_EXPERT), lambda i, e: (e, 0, 0)),
                  pl.BlockSpec((1, D_MODEL, D_FF_EXPERT), lambda i, e: (e, 0, 0)),
                  pl.BlockSpec((1, D_FF_EXPERT, D_MODEL), lambda i, e: (e, 0, 0)),
                  pl.BlockSpec((1, D_MODEL), lambda i, e: (0, 0)),
                  pl.BlockSpec((1, D_MODEL), lambda i, e: (0, 0))],
        out_specs=row,
        out_shape=jax.ShapeDtypeStruct((t, D_MODEL), F32),
        scratch_shapes=[pltpu.VMEM((tm, LANES), F32), pltpu.VMEM((tm, D_MODEL), F32)],
        compiler_params=_cparams(("arbitrary", "arbitrary")),
        name="moe",
    )(h, wr2, wg, wu, wd, ln_g, ln_b)


def kernel(x, ln_in_g, ln_in_b, w_in, cmp_pos, cmp_w1, cmp_b1, cmp_w2, cmp_b2, conv_w, w_o, ln1_g, ln1_b, ln2_g, ln2_b, ffn_wg, ffn_wu, ffn_wd, moe_router, moe_wg, moe_wu, moe_wd):
    batch, seq, d = x.shape
    depth = w_in.shape[0]
    assert d == D_MODEL and seq % TM_PROJ == 0 and seq // CMP_STRIDE == LANES
    alpha = (2 * depth) ** 0.25
    t = batch * seq
    vec = lambda v: v.reshape(1, D_MODEL)
    h = x.reshape(t, d)
    for l in range(depth):
        w_r = _prep_w_in(w_in[l])
        cw = jnp.pad(conv_w[l], ((0, 8 - CONV_W), (0, 0)))
        outs = _in_proj(h, vec(ln_in_g), vec(ln_in_b), w_r, cw, seq=seq, pre_ln=(l == 0))
        if l == 0:
            h, outs = outs[0], outs[1:]
        q, kvc, ks, vs, kw, vw, gates, o_conv = outs
        kvcmp = _compress(kvc, cmp_pos[l], cmp_w1[l], cmp_b1[l], cmp_w2[l], cmp_b2[l], batch=batch, seq=seq)
        o_cmp, selb = _cmp_sel(q, kvcmp, gates, batch=batch, seq=seq)
        o_sel = _sel_attn(q, selb, ks, vs, gates, batch=batch, seq=seq)
        o_win = _win_attn(q, kw, vw, gates, batch=batch, seq=seq)
        h = _out_proj(h, o_cmp, o_sel, o_win, o_conv, w_o[l].astype(BF16), vec(ln1_g[l]), vec(ln1_b[l]),
                      alpha=alpha)
        if l % 2 == 0:
            h = _ffn(h, ffn_wg[l // 2].astype(BF16), ffn_wu[l // 2].astype(BF16), ffn_wd[l // 2].astype(BF16),
                     vec(ln2_g[l]), vec(ln2_b[l]), alpha=alpha)
        else:
            h = _moe(h, moe_router[l // 2], moe_wg[l // 2].astype(BF16), moe_wu[l // 2].astype(BF16),
                     moe_wd[l // 2].astype(BF16), vec(ln2_g[l]), vec(ln2_b[l]), alpha=alpha)
    return h.reshape(batch, seq, d)
```

```python
import functools

import numpy as np
import jax
import jax.numpy as jnp
from jax import lax
from jax.experimental import pallas as pl
from jax.experimental.pallas import tpu as pltpu

F32 = jnp.float32
BF16 = jnp.bfloat16

D_MODEL = 1024
HEAD_DIM = 64
N_HEADS = 8
N_KV = 2
HPG = N_HEADS // N_KV
ATTN_W = N_HEADS * HEAD_DIM
KV_W = N_KV * HEAD_DIM
CONV_CH = D_MODEL - ATTN_W
CONV_W = 3
CMP_BLOCK = 32
CMP_STRIDE = 16
CMP_HIDDEN = 256
SEL_BLOCK = 64
SEL_TOPN = 16
WINDOW = 512
D_FF = 2816
N_EXPERTS = 8
D_FF_EXPERT = 1408
LN_EPS = 1e-5
NEG = -1e30
FORCE = 1e9

LANES = 128
HPAD = LANES
Q_SCALE = HEAD_DIM ** -0.5

C_Q = 0
C_KVC = C_Q + N_HEADS * HPAD
C_KS = C_KVC + 2 * KV_W
C_VS = C_KS + N_KV * HPAD
C_KW = C_VS + N_KV * HPAD
C_VW = C_KW + N_KV * HPAD
C_U = C_VW + N_KV * HPAD
C_B = C_U + CONV_CH
C_C = C_B + CONV_CH
C_G = C_C + CONV_CH
C_END = C_G + LANES

TM_PROJ = 512
TQ = 256
TK = 256
VMEM_LIMIT = 56 * 1024 * 1024


def _cparams(sem):
    return pltpu.CompilerParams(dimension_semantics=sem, vmem_limit_bytes=VMEM_LIMIT)


def _ln(x, g, b):
    mu = jnp.mean(x, -1, keepdims=True)
    xc = x - mu
    var = jnp.mean(xc * xc, -1, keepdims=True)
    return xc * lax.rsqrt(var + LN_EPS) * g + b


def _dot(a, b):
    return jnp.dot(a, b, preferred_element_type=F32)


def _dot_nt(a, b):
    return lax.dot_general(a, b, (((1,), (1,)), ((), ())), preferred_element_type=F32)


def _resident(shape):
    nd = len(shape)
    return pl.BlockSpec(shape, lambda *_: (0,) * nd, pipeline_mode=pl.Buffered(1))


def _in_proj_kernel(*refs, pre_ln, tiles_per_seq, tm):
    if pre_ln:
        (x_ref, g_ref, b_ref, w_ref, cw_ref, h_ref, q_ref, kvc_ref, ks_ref, vs_ref,
         kw_ref, vw_ref, gate_ref, oc_ref, carry_ref) = refs
        h = _ln(x_ref[...], g_ref[...], b_ref[...])
        h_ref[...] = h
    else:
        (x_ref, w_ref, cw_ref, q_ref, kvc_ref, ks_ref, vs_ref,
         kw_ref, vw_ref, gate_ref, oc_ref, carry_ref) = refs
        h = x_ref[...]
    hb = h.astype(BF16)

    def proj(lo, hi):
        return _dot(hb, w_ref[:, lo:hi])

    q_ref[...] = (proj(C_Q, C_KVC) * Q_SCALE).astype(BF16)
    kvc = proj(C_KVC, C_KS)
    for j in range(4):
        kvc_ref[j] = kvc[:, j * HEAD_DIM:(j + 1) * HEAD_DIM]

    seq_tile = pl.program_id(0) % tiles_per_seq
    pos = lax.broadcasted_iota(jnp.int32, (tm, N_KV * HPAD), 0) + seq_tile * tm
    lane = lax.broadcasted_iota(jnp.int32, (tm, N_KV * HPAD), 1) % HPAD
    onehot = jnp.where(pos // SEL_BLOCK == lane - HEAD_DIM, 1.0, 0.0)
    in_tag = (lane >= HEAD_DIM) & (lane < HEAD_DIM + 32)
    ks_ref[...] = jnp.where(in_tag, onehot, proj(C_KS, C_VS)).astype(BF16)
    vs_ref[...] = proj(C_VS, C_KW).astype(BF16)
    kw_ref[...] = proj(C_KW, C_VW).astype(BF16)
    vw_ref[...] = proj(C_VW, C_U).astype(BF16)

    cu = proj(C_C, C_G) * proj(C_U, C_B)

    @pl.when(seq_tile == 0)
    def _():
        carry_ref[...] = jnp.zeros_like(carry_ref)

    prev = carry_ref[...]
    row = lax.broadcasted_iota(jnp.int32, (tm, CONV_CH), 0)
    s1 = jnp.where(row == 0, prev[7:8], pltpu.roll(cu, 1, 0))
    s2 = jnp.where(row == 0, prev[6:7], jnp.where(row == 1, prev[7:8], pltpu.roll(cu, 2, 0)))
    y = s2 * cw_ref[0:1, :] + s1 * cw_ref[1:2, :] + cu * cw_ref[2:3, :]
    oc_ref[...] = (proj(C_B, C_C) * y).astype(BF16)
    carry_ref[...] = cu[tm - 8:tm]
    gate_ref[...] = jax.nn.sigmoid(proj(C_G, C_END))


def _in_proj(x, ln_g, ln_b, w_r, conv_w, *, seq, pre_ln):
    t = x.shape[0]
    tm = TM_PROJ
    nt = t // tm
    row = lambda w: pl.BlockSpec((tm, w), lambda i: (i, 0))
    in_specs = [row(D_MODEL)]
    args = [x]
    if pre_ln:
        in_specs += [_resident((1, D_MODEL)), _resident((1, D_MODEL))]
        args += [ln_g, ln_b]
    in_specs += [_resident((D_MODEL, C_END)), _resident((8, CONV_CH))]
    args += [w_r, conv_w]
    out_shape, out_specs = [], []
    if pre_ln:
        out_shape.append(jax.ShapeDtypeStruct((t, D_MODEL), F32))
        out_specs.append(row(D_MODEL))
    out_shape += [
        jax.ShapeDtypeStruct((t, N_HEADS * HPAD), BF16),
        jax.ShapeDtypeStruct((4, t, HEAD_DIM), F32),
        jax.ShapeDtypeStruct((t, N_KV * HPAD), BF16),
        jax.ShapeDtypeStruct((t, N_KV * HPAD), BF16),
        jax.ShapeDtypeStruct((t, N_KV * HPAD), BF16),
        jax.ShapeDtypeStruct((t, N_KV * HPAD), BF16),
        jax.ShapeDtypeStruct((t, LANES), F32),
        jax.ShapeDtypeStruct((t, CONV_CH), BF16),
    ]
    out_specs += [
        row(N_HEADS * HPAD),
        pl.BlockSpec((4, tm, HEAD_DIM), lambda i: (0, i, 0)),
        row(N_KV * HPAD), row(N_KV * HPAD), row(N_KV * HPAD), row(N_KV * HPAD),
        row(LANES), row(CONV_CH),
    ]
    return pl.pallas_call(
        functools.partial(_in_proj_kernel, pre_ln=pre_ln, tiles_per_seq=seq // tm, tm=tm),
        grid=(nt,),
        in_specs=in_specs,
        out_specs=out_specs,
        out_shape=out_shape,
        scratch_shapes=[pltpu.VMEM((8, CONV_CH), F32)],
        compiler_params=_cparams(("arbitrary",)),
        name="in_proj_ln" if pre_ln else "in_proj",
    )(*args)


def _prep_w_in(w_in):
    d = w_in.shape[0]
    o = 0

    def take(n):
        nonlocal o
        s = w_in[:, o:o + n]
        o += n
        return s

    def pad_heads(s, n):
        s = s.reshape(d, n, HEAD_DIM)
        return jnp.pad(s, ((0, 0), (0, 0), (0, HPAD - HEAD_DIM))).reshape(d, n * HPAD)

    q = pad_heads(take(ATTN_W), N_HEADS)
    kc, vc = take(KV_W), take(KV_W)
    ks, vs, kw, vw = (pad_heads(take(KV_W), N_KV) for _ in range(4))
    gates = jnp.pad(take(3 * N_HEADS), ((0, 0), (0, LANES - 3 * N_HEADS)))
    u, bg, cg = take(CONV_CH), take(CONV_CH), take(CONV_CH)
    return jnp.concatenate([q, kc, vc, ks, vs, kw, vw, u, bg, cg, gates], axis=1).astype(BF16)


def _compress_kernel(x_ref, pos_ref, w1_ref, b1_ref, w2_ref, b2_ref, o_ref):
    half = CMP_STRIDE * HEAD_DIM
    x = x_ref[0, 0]
    pos = pos_ref[0]
    xa = (x + pos[:, :half]).astype(BF16)
    xb = (x + pos[:, half:]).astype(BF16)
    a = _dot(xa, w1_ref[0, :half, :])
    b = _dot(xb, w1_ref[0, half:, :])
    n = x.shape[0]
    hid = a + pltpu.roll(b, n - 1, 0) + b1_ref[0]
    act = jax.nn.gelu(hid).astype(BF16)
    o_ref[0, 0] = (_dot(act, w2_ref[0]) + b2_ref[0]).astype(BF16)


def _compress(kvc, cmp_pos, cmp_w1, cmp_b1, cmp_w2, cmp_b2, *, batch, seq):
    nb = seq // CMP_STRIDE
    half = CMP_STRIDE * HEAD_DIM
    x = kvc.reshape(4, batch, nb, half)
    pos = cmp_pos.reshape(2, 1, CMP_BLOCK * HEAD_DIM)
    w1 = cmp_w1.astype(BF16)
    b1 = cmp_b1.reshape(2, 1, CMP_HIDDEN)
    w2 = jnp.pad(cmp_w2, ((0, 0), (0, 0), (0, HPAD - HEAD_DIM))).astype(BF16)
    b2 = jnp.pad(cmp_b2, ((0, 0), (0, HPAD - HEAD_DIM))).reshape(2, 1, HPAD)
    kind = lambda j, b: (j // N_KV, 0, 0)
    return pl.pallas_call(
        _compress_kernel,
        grid=(4, batch),
        in_specs=[
            pl.BlockSpec((1, 1, nb, half), lambda j, b: (j, b, 0, 0)),
            pl.BlockSpec((1, 1, 2 * half), kind),
            pl.BlockSpec((1, 2 * half, CMP_HIDDEN), kind),
            pl.BlockSpec((1, 1, CMP_HIDDEN), kind),
            pl.BlockSpec((1, CMP_HIDDEN, HPAD), kind),
            pl.BlockSpec((1, 1, HPAD), kind),
        ],
        out_specs=pl.BlockSpec((1, 1, nb, HPAD), lambda j, b: (j, b, 0, 0)),
        out_shape=jax.ShapeDtypeStruct((4, batch, nb, HPAD), BF16),
        compiler_params=_cparams(("arbitrary", "arbitrary")),
        name="compress",
    )(x, pos, w1, b1, w2, b2)


def _gate_col(gates, idx):
    lane = lax.broadcasted_iota(jnp.int32, gates.shape, 1)
    return jnp.sum(jnp.where(lane == idx, gates, 0.0), axis=-1, keepdims=True)


def _store_heads(o_ref, heads):
    for p in range(HPG // 2):
        pair = heads[2 * p] + pltpu.roll(heads[2 * p + 1], HEAD_DIM, 1)
        o_ref[:, p * LANES:(p + 1) * LANES] = pair.astype(o_ref.dtype)


def _overlap_t():
    n = np.arange(LANES)
    j = np.arange(32)
    cs = n * CMP_STRIDE
    ss = j * SEL_BLOCK
    ov = (cs[None, :] < ss[:, None] + SEL_BLOCK) & (cs[None, :] + CMP_BLOCK > ss[:, None])
    return jnp.asarray(ov, dtype=BF16)


def _cmp_sel_kernel(q_ref, kc_ref, vc_ref, gate_ref, ov_ref, o_ref, selb_ref, *, tq, n_sel):
    g = pl.program_id(1)
    i = pl.program_id(2)
    kc = kc_ref[0, 0]
    vc = vc_ref[0, 0]
    gates = gate_ref[...]
    t = lax.broadcasted_iota(jnp.int32, (tq, LANES), 0) + i * tq
    n = lax.broadcasted_iota(jnp.int32, (tq, LANES), 1)
    mask = n * CMP_STRIDE + (CMP_BLOCK - 1) <= t
    maskf = mask.astype(F32)
    psum = jnp.zeros((tq, LANES), F32)
    heads = []
    for hh in range(HPG):
        s = _dot_nt(q_ref[:, hh * HPAD:(hh + 1) * HPAD], kc)
        s = jnp.where(mask, s, NEG)
        e = jnp.exp(s - jnp.max(s, axis=-1, keepdims=True))
        p = e / jnp.sum(e, axis=-1, keepdims=True) * maskf
        psum = psum + p
        heads.append(_dot(p.astype(BF16), vc) * _gate_col(gates, g * HPG + hh))
    _store_heads(o_ref, heads)

    ov = ov_ref[...]
    p_hi = psum.astype(BF16)
    r1 = psum - p_hi.astype(F32)
    p_mid = r1.astype(BF16)
    p_lo = (r1 - p_mid.astype(F32)).astype(BF16)
    imp = _dot_nt(ov, p_hi) + _dot_nt(ov, p_mid) + _dot_nt(ov, p_lo)

    jt = lax.broadcasted_iota(jnp.int32, (n_sel, tq), 0)
    tt = lax.broadcasted_iota(jnp.int32, (n_sel, tq), 1) + i * tq
    cur = tt // SEL_BLOCK
    valid = jt * SEL_BLOCK <= tt
    forced = (jt == 0) | (jt == cur) | (jt == cur - 1)
    score = jnp.where(forced, FORCE, jnp.where(valid, imp, NEG))
    rank = jnp.zeros((n_sel, tq), jnp.int32)
    for k in range(n_sel):
        sk = score[k:k + 1, :]
        ahead = (sk > score) | ((sk == score) & (jt > k))
        rank = rank + ahead.astype(jnp.int32)
    bias = jnp.where(rank < SEL_TOPN, 0.0, NEG)
    full = jnp.concatenate(
        [jnp.zeros((HEAD_DIM, tq), F32), bias, jnp.zeros((HPAD - HEAD_DIM - n_sel, tq), F32)], axis=0)
    selb_ref[...] = full.T.astype(BF16)


def _cmp_sel(q, kvcmp, gates, *, batch, seq):
    t = q.shape[0]
    tq = TQ
    nq = seq // tq
    n_sel = seq // SEL_BLOCK
    nb = kvcmp.shape[2]
    assert nb == LANES and n_sel == 32
    rowblk = lambda w: pl.BlockSpec((tq, w), lambda b, g, i: (b * nq + i, g))
    return pl.pallas_call(
        functools.partial(_cmp_sel_kernel, tq=tq, n_sel=n_sel),
        grid=(batch, N_KV, nq),
        in_specs=[
            rowblk(HPG * HPAD),
            pl.BlockSpec((1, 1, nb, HPAD), lambda b, g, i: (g, b, 0, 0)),
            pl.BlockSpec((1, 1, nb, HPAD), lambda b, g, i: (N_KV + g, b, 0, 0)),
            pl.BlockSpec((tq, LANES), lambda b, g, i: (b * nq + i, 0)),
            pl.BlockSpec((n_sel, LANES), lambda b, g, i: (0, 0)),
        ],
        out_specs=[rowblk(HPG * HEAD_DIM), rowblk(LANES)],
        out_shape=[jax.ShapeDtypeStruct((t, ATTN_W), BF16),
                   jax.ShapeDtypeStruct((t, N_KV * LANES), BF16)],
        compiler_params=_cparams(("arbitrary", "arbitrary", "arbitrary")),
        name="cmp_sel",
    )(q, kvcmp, kvcmp, gates, _overlap_t())


def _stack_heads(q_ref, selb=None):
    qs = []
    for hh in range(HPG):
        qh = q_ref[:, hh * HPAD:(hh + 1) * HPAD]
        if selb is not None:
            lane = lax.broadcasted_iota(jnp.int32, qh.shape, 1)
            qh = jnp.where(lane >= HEAD_DIM, selb, qh)
        qs.append(qh)
    return jnp.concatenate(qs, axis=0)


def _sel_attn_kernel(q_ref, selb_ref, k_ref, v_ref, gate_ref, o_ref, *, tq, tk):
    g = pl.program_id(1)
    i = pl.program_id(2)
    qa = _stack_heads(q_ref, selb_ref[...])
    rows = HPG * tq

    def step(j, carry, diag):
        m, l, acc = carry
        start = pl.multiple_of(j * tk, tk)
        k = k_ref[pl.ds(start, tk), :]
        v = v_ref[pl.ds(start, tk), :]
        s = _dot_nt(qa, k)
        if diag:
            r = lax.broadcasted_iota(jnp.int32, (rows, tk), 0) % tq
            c = lax.broadcasted_iota(jnp.int32, (rows, tk), 1)
            s = jnp.where(c <= r, s, NEG)
        m_new = jnp.maximum(m, jnp.max(s, axis=-1, keepdims=True))
        alpha = jnp.exp(m - m_new)
        p = jnp.exp(s - m_new)
        l = alpha * l + jnp.sum(p, axis=-1, keepdims=True)
        acc = alpha * acc + _dot(p.astype(BF16), v)
        return m_new, l, acc

    init = (jnp.full((rows, 1), NEG, F32), jnp.zeros((rows, 1), F32), jnp.zeros((rows, HPAD), F32))
    carry = lax.fori_loop(0, i, lambda j, c: step(j, c, False), init)
    _, l, acc = step(i, carry, True)
    o = acc / l
    gates = gate_ref[...]
    heads = [o[hh * tq:(hh + 1) * tq] * _gate_col(gates, N_HEADS + g * HPG + hh) for hh in range(HPG)]
    _store_heads(o_ref, heads)


def _sel_attn(q, selb, ks, vs, gates, *, batch, seq):
    t = q.shape[0]
    tq, tk = TQ, TK
    assert tq == tk
    nq = seq // tq
    rowblk = lambda w: pl.BlockSpec((tq, w), lambda b, g, i: (b * nq + i, g))
    seqblk = pl.BlockSpec((seq, HPAD), lambda b, g, i: (b, g))
    return pl.pallas_call(
        functools.partial(_sel_attn_kernel, tq=tq, tk=tk),
        grid=(batch, N_KV, nq),
        in_specs=[rowblk(HPG * HPAD), rowblk(LANES), seqblk, seqblk,
                  pl.BlockSpec((tq, LANES), lambda b, g, i: (b * nq + i, 0))],
        out_specs=rowblk(HPG * HEAD_DIM),
        out_shape=jax.ShapeDtypeStruct((t, ATTN_W), BF16),
        compiler_params=_cparams(("arbitrary", "arbitrary", "arbitrary")),
        name="sel_attn",
    )(q, selb, ks, vs, gates)


def _win_attn_kernel(q_ref, k_ref, v_ref, gate_ref, o_ref, *, tq, tk):
    g = pl.program_id(1)
    i = pl.program_id(2)
    qa = _stack_heads(q_ref)
    rows = HPG * tq
    r = lax.broadcasted_iota(jnp.int32, (rows, tk), 0) % tq
    c = lax.broadcasted_iota(jnp.int32, (rows, tk), 1)
    masks = (((c > r) & (i >= 2)), jnp.broadcast_to(i >= 1, (rows, tk)), c <= r)
    ss, vs = [], []
    for d, mask in zip((2, 1, 0), masks):
        start = pl.multiple_of(jnp.maximum(i - d, 0) * tk, tk)
        s = _dot_nt(qa, k_ref[pl.ds(start, tk), :])
        ss.append(jnp.where(mask, s, NEG))
        vs.append(v_ref[pl.ds(start, tk), :])
    m = jnp.maximum(jnp.maximum(jnp.max(ss[0], axis=-1, keepdims=True),
                                jnp.max(ss[1], axis=-1, keepdims=True)),
                    jnp.max(ss[2], axis=-1, keepdims=True))
    ps = [jnp.exp(s - m) for s in ss]
    l = (jnp.sum(ps[0], axis=-1, keepdims=True) + jnp.sum(ps[1], axis=-1, keepdims=True)
         + jnp.sum(ps[2], axis=-1, keepdims=True))
    acc = _dot(ps[0].astype(BF16), vs[0]) + _dot(ps[1].astype(BF16), vs[1]) + _dot(ps[2].astype(BF16), vs[2])
    o = acc / l
    gates = gate_ref[...]
    heads = [o[hh * tq:(hh + 1) * tq] * _gate_col(gates, 2 * N_HEADS + g * HPG + hh) for hh in range(HPG)]
    _store_heads(o_ref, heads)


def _win_attn(q, kw, vw, gates, *, batch, seq):
    t = q.shape[0]
    tq, tk = TQ, TK
    assert tq == tk and WINDOW == 2 * tk
    nq = seq // tq
    rowblk = lambda w: pl.BlockSpec((tq, w), lambda b, g, i: (b * nq + i, g))
    seqblk = pl.BlockSpec((seq, HPAD), lambda b, g, i: (b, g))
    return pl.pallas_call(
        functools.partial(_win_attn_kernel, tq=tq, tk=tk),
        grid=(batch, N_KV, nq),
        in_specs=[rowblk(HPG * HPAD), seqblk, seqblk,
                  pl.BlockSpec((tq, LANES), lambda b, g, i: (b * nq + i, 0))],
        out_specs=rowblk(HPG * HEAD_DIM),
        out_shape=jax.ShapeDtypeStruct((t, ATTN_W), BF16),
        compiler_params=_cparams(("arbitrary", "arbitrary", "arbitrary")),
        name="win_attn",
    )(q, kw, vw, gates)


def _out_proj_kernel(h_ref, oc_ref, os_ref, ow_ref, ocv_ref, w_ref, g_ref, b_ref, o_ref, *, alpha):
    oa = (oc_ref[...].astype(F32) + os_ref[...].astype(F32) + ow_ref[...].astype(F32)).astype(BF16)
    m = _dot(oa, w_ref[:ATTN_W, :]) + _dot(ocv_ref[...], w_ref[ATTN_W:, :])
    o_ref[...] = _ln(alpha * h_ref[...] + m, g_ref[...], b_ref[...])


def _out_proj(h, o_cmp, o_sel, o_win, o_conv, w_o, ln_g, ln_b, *, alpha):
    t = h.shape[0]
    tm = TM_PROJ
    row = lambda w: pl.BlockSpec((tm, w), lambda i: (i, 0))
    return pl.pallas_call(
        functools.partial(_out_proj_kernel, alpha=alpha),
        grid=(t // tm,),
        in_specs=[row(D_MODEL), row(ATTN_W), row(ATTN_W), row(ATTN_W), row(CONV_CH),
                  _resident((D_MODEL, D_MODEL)), _resident((1, D_MODEL)), _resident((1, D_MODEL))],
        out_specs=row(D_MODEL),
        out_shape=jax.ShapeDtypeStruct((t, D_MODEL), F32),
        compiler_params=_cparams(("arbitrary",)),
        name="out_proj",
    )(h, o_cmp, o_sel, o_win, o_conv, w_o, ln_g, ln_b)


def _ffn_kernel(h_ref, wg_ref, wu_ref, wd_ref, g_ref, b_ref, o_ref, *, alpha):
    h = h_ref[...]
    hb = h.astype(BF16)
    a = (jax.nn.silu(_dot(hb, wg_ref[...])) * _dot(hb, wu_ref[...])).astype(BF16)
    o_ref[...] = _ln(alpha * h + _dot(a, wd_ref[...]), g_ref[...], b_ref[...])


def _ffn(h, wg, wu, wd, ln_g, ln_b, *, alpha):
    t = h.shape[0]
    tm = TM_PROJ
    row = pl.BlockSpec((tm, D_MODEL), lambda i: (i, 0))
    return pl.pallas_call(
        functools.partial(_ffn_kernel, alpha=alpha),
        grid=(t // tm,),
        in_specs=[row, _resident((D_MODEL, D_FF)), _resident((D_MODEL, D_FF)), _resident((D_FF, D_MODEL)),
                  _resident((1, D_MODEL)), _resident((1, D_MODEL))],
        out_specs=row,
        out_shape=jax.ShapeDtypeStruct((t, D_MODEL), F32),
        compiler_params=_cparams(("arbitrary",)),
        name="ffn",
    )(h, wg, wu, wd, ln_g, ln_b)


TMX = 512
CH = 16
LROWS = 2 * TMX + N_EXPERTS * CH
TMR = 512


def _route_kernel(h_ref, wr_ref, ltri_ref, ustr_ref, meta_ref, metat_ref, cnt_ref, *, tm):
    h = h_ref[...]
    h_hi = h.astype(BF16)
    h_lo = (h - h_hi.astype(F32)).astype(BF16)
    logits = _dot(h_hi, wr_ref[0]) + _dot(h_lo, wr_ref[0]) + _dot(h_hi, wr_ref[1])
    lane = lax.broadcasted_iota(jnp.int32, logits.shape, 1)
    logits = jnp.where(lane < N_EXPERTS, logits, -jnp.inf)
    m1 = jnp.max(logits, axis=-1, keepdims=True)
    i1 = jnp.min(jnp.where(logits == m1, lane, LANES), axis=-1, keepdims=True)
    rest = jnp.where(lane == i1, -jnp.inf, logits)
    m2 = jnp.max(rest, axis=-1, keepdims=True)
    i2 = jnp.min(jnp.where(rest == m2, lane, LANES), axis=-1, keepdims=True)
    e2 = jnp.exp(m2 - m1)
    den = 1.0 + e2
    w1 = 1.0 / den
    w2 = e2 / den

    routed = (lane == i1) | (lane == i2)
    cnt = _dot(ltri_ref[...], routed.astype(BF16))
    n = cnt[tm - 1:tm, :]
    padded = jnp.floor((n + (CH - 1)) * (1.0 / CH)) * CH
    seg_off = _dot(jnp.broadcast_to(padded, (8, LANES)).astype(BF16), ustr_ref[...])[0:1]
    dest = seg_off + cnt - 1.0
    d1 = jnp.sum(jnp.where(lane == i1, dest, 0.0), axis=-1, keepdims=True)
    d2 = jnp.sum(jnp.where(lane == i2, dest, 0.0), axis=-1, keepdims=True)
    meta = jnp.where(lane == 0, d1, jnp.where(lane == 1, d2, jnp.where(lane == 2, w1, jnp.where(lane == 3, w2, 0.0))))
    meta_ref[...] = meta
    metat_ref[...] = meta.T[0:8, :]
    cnt_ref[0] = jnp.broadcast_to(n, (8, LANES))


def _moe_route(h, w_router):
    t = h.shape[0]
    tm = TMX
    nt = t // tm
    wr = jnp.pad(w_router, ((0, 0), (0, LANES - N_EXPERTS)))
    wr_hi = wr.astype(BF16)
    wr_lo = (wr - wr_hi.astype(F32)).astype(BF16)
    wr2 = jnp.stack([wr_hi, wr_lo])
    ltri = jnp.asarray(np.tril(np.ones((tm, tm), np.float32)), dtype=BF16)
    ustr = jnp.asarray(np.triu(np.ones((LANES, LANES), np.float32), 1), dtype=BF16)
    return pl.pallas_call(
        functools.partial(_route_kernel, tm=tm),
        grid=(nt,),
        in_specs=[pl.BlockSpec((tm, D_MODEL), lambda i: (i, 0)), _resident((2, D_MODEL, LANES)),
                  _resident((tm, tm)), _resident((LANES, LANES))],
        out_specs=[pl.BlockSpec((tm, LANES), lambda i: (i, 0)),
                   pl.BlockSpec((8, tm), lambda i: (0, i)),
                   pl.BlockSpec((1, 8, LANES), lambda i: (i, 0, 0))],
        out_shape=[jax.ShapeDtypeStruct((t, LANES), F32), jax.ShapeDtypeStruct((8, t), F32),
                   jax.ShapeDtypeStruct((nt, 8, LANES), F32)],
        compiler_params=_cparams(("arbitrary",)),
        name="moe_route",
    )(h, wr2, ltri, ustr)


def _moe_plan(cnt, n_row_tiles):
    n = cnt[:, 0, :N_EXPERTS].astype(jnp.int32)
    p = (n + CH - 1) // CH * CH
    tot = p.sum(0)
    tot_pad = (tot + TMR - 1) // TMR * TMR
    gend = jnp.cumsum(tot_pad)
    gstart = gend - tot_pad
    goff = gstart[None, :] + jnp.cumsum(p, 0) - p
    loff = jnp.cumsum(p, 1) - p
    n_used = gend[-1:] // TMR
    tile_start = jnp.arange(n_row_tiles, dtype=jnp.int32) * TMR
    texp = jnp.minimum(jnp.sum(tile_start[:, None] >= gend[None, :], axis=1), N_EXPERTS - 1).astype(jnp.int32)
    texp = jnp.where(jnp.arange(n_row_tiles) < n_used[0], texp, texp[jnp.maximum(n_used[0] - 1, 0)])
    flat = lambda a: a.reshape(-1).astype(jnp.int32)
    gap_start = jnp.concatenate([gstart + tot, gend[-1:]])
    gap_rows = jnp.concatenate([tot_pad - tot, n_row_tiles * TMR - gend[-1:]])
    return dict(loff=flat(loff), goff=flat(goff), nch=flat(p // CH), gap_start=flat(gap_start),
                gap_nch=flat(gap_rows // CH), texp=flat(texp), n_used=flat(n_used))


def _chunk_copies(src_ref, dst_ref, sem, src_off, dst_off, n, src_step=CH):
    def body(c, carry):
        s = pl.multiple_of(src_off + c * src_step, CH)
        d = pl.multiple_of(dst_off + c * CH, CH)
        pltpu.make_async_copy(src_ref.at[pl.ds(s, CH)], dst_ref.at[pl.ds(d, CH)], sem).start()
        return carry
    lax.fori_loop(0, n, body, 0)


def _wait_chunks(src_ref, dst_ref, sem, n):
    def body(c, carry):
        pltpu.make_async_copy(src_ref.at[pl.ds(0, CH)], dst_ref.at[pl.ds(0, CH)], sem).wait()
        return carry
    lax.fori_loop(0, n, body, 0)


def _dispatch_kernel(loff, goff, nch, gap_start, gap_nch, h_ref, metat_ref, xs_ref, xc_ref, z_ref, sem, *, tm):
    i = pl.program_id(0)
    d1 = metat_ref[0:1, :]
    d2 = metat_ref[1:2, :]
    r = lax.broadcasted_iota(jnp.int32, (LROWS, tm), 0).astype(F32)
    onehot = ((r == d1) | (r == d2)).astype(BF16)
    xc_ref[...] = _dot(onehot, h_ref[...].astype(BF16)).astype(BF16)
    total = 0
    for e in range(N_EXPERTS):
        n = nch[i * N_EXPERTS + e]
        _chunk_copies(xc_ref, xs_ref, sem, loff[i * N_EXPERTS + e], goff[i * N_EXPERTS + e], n)
        total = total + n

    @pl.when(i == pl.num_programs(0) - 1)
    def _():
        z_ref[...] = jnp.zeros_like(z_ref)
        gaps = 0
        for e in range(N_EXPERTS + 1):
            _chunk_copies(z_ref, xs_ref, sem, 0, gap_start[e], gap_nch[e], src_step=0)
            gaps = gaps + gap_nch[e]
        _wait_chunks(z_ref, xs_ref, sem, gaps)

    _wait_chunks(xc_ref, xs_ref, sem, total)


def _moe_dispatch(h, metat, plan, n_rows):
    t = h.shape[0]
    tm = TMX
    grid_spec = pltpu.PrefetchScalarGridSpec(
        num_scalar_prefetch=5,
        grid=(t // tm,),
        in_specs=[pl.BlockSpec((tm, D_MODEL), lambda i, *_: (i, 0)),
                  pl.BlockSpec((8, tm), lambda i, *_: (0, i))],
        out_specs=pl.BlockSpec(memory_space=pl.ANY),
        scratch_shapes=[pltpu.VMEM((LROWS, D_MODEL), BF16), pltpu.VMEM((CH, D_MODEL), BF16),
                        pltpu.SemaphoreType.DMA(())],
    )
    return pl.pallas_call(
        functools.partial(_dispatch_kernel, tm=tm),
        grid_spec=grid_spec,
        out_shape=jax.ShapeDtypeStruct((n_rows, D_MODEL), BF16),
        compiler_params=_cparams(("arbitrary",)),
        name="moe_dispatch",
    )(plan["loff"], plan["goff"], plan["nch"], plan["gap_start"], plan["gap_nch"], h, metat)


def _experts_kernel(texp, n_used, x_ref, wg_ref, wu_ref, wd_ref, y_ref):
    used = pl.program_id(0) < n_used[0]

    @pl.when(used)
    def _():
        x = x_ref[...]
        a = (jax.nn.silu(_dot(x, wg_ref[0])) * _dot(x, wu_ref[0])).astype(BF16)
        y_ref[...] = _dot(a, wd_ref[0]).astype(BF16)

    @pl.when(jnp.logical_not(used))
    def _():
        y_ref[...] = jnp.zeros_like(y_ref)


def _moe_experts(xs, wg, wu, wd, plan):
    n_rows = xs.shape[0]
    rows = lambda r, texp, n_used: (jnp.minimum(r, n_used[0] - 1), 0)
    wspec = lambda shape: pl.BlockSpec((1,) + shape, lambda r, texp, n_used: (texp[r], 0, 0))
    grid_spec = pltpu.PrefetchScalarGridSpec(
        num_scalar_prefetch=2,
        grid=(n_rows // TMR,),
        in_specs=[pl.BlockSpec((TMR, D_MODEL), rows), wspec((D_MODEL, D_FF_EXPERT)),
                  wspec((D_MODEL, D_FF_EXPERT)), wspec((D_FF_EXPERT, D_MODEL))],
        out_specs=pl.BlockSpec((TMR, D_MODEL), lambda r, texp, n_used: (r, 0)),
    )
    return pl.pallas_call(
        _experts_kernel,
        grid_spec=grid_spec,
        out_shape=jax.ShapeDtypeStruct((n_rows, D_MODEL), BF16),
        compiler_params=_cparams(("arbitrary",)),
        name="moe_experts",
    )(plan["texp"], plan["n_used"], xs, wg, wu, wd)


def _combine_kernel(loff, goff, nch, h_ref, meta_ref, ys_ref, g_ref, b_ref, o_ref, yb_ref, sem, *, tm, alpha):
    i = pl.program_id(0)

    @pl.when(i == 0)
    def _():
        yb_ref[...] = jnp.zeros_like(yb_ref)

    total = 0
    for e in range(N_EXPERTS):
        n = nch[i * N_EXPERTS + e]
        _chunk_copies(ys_ref, yb_ref, sem, goff[i * N_EXPERTS + e], loff[i * N_EXPERTS + e], n)
        total = total + n
    meta = meta_ref[...]
    r = lax.broadcasted_iota(jnp.int32, (tm, LROWS), 1).astype(F32)
    pick1 = (r == meta[:, 0:1]).astype(BF16)
    pick2 = (r == meta[:, 1:2]).astype(BF16)
    _wait_chunks(ys_ref, yb_ref, sem, total)
    yb = yb_ref[...]
    f = meta[:, 2:3] * _dot(pick1, yb) + meta[:, 3:4] * _dot(pick2, yb)
    o_ref[...] = _ln(alpha * h_ref[...] + f, g_ref[...], b_ref[...])


def _moe_combine(h, meta, ys, plan, ln_g, ln_b, *, alpha):
    t = h.shape[0]
    tm = TMX
    grid_spec = pltpu.PrefetchScalarGridSpec(
        num_scalar_prefetch=3,
        grid=(t // tm,),
        in_specs=[pl.BlockSpec((tm, D_MODEL), lambda i, *_: (i, 0)),
                  pl.BlockSpec((tm, LANES), lambda i, *_: (i, 0)),
                  pl.BlockSpec(memory_space=pl.ANY),
                  pl.BlockSpec((1, D_MODEL), lambda i, *_: (0, 0)),
                  pl.BlockSpec((1, D_MODEL), lambda i, *_: (0, 0))],
        out_specs=pl.BlockSpec((tm, D_MODEL), lambda i, *_: (i, 0)),
        scratch_shapes=[pltpu.VMEM((LROWS, D_MODEL), BF16), pltpu.SemaphoreType.DMA(())],
    )
    return pl.pallas_call(
        functools.partial(_combine_kernel, tm=tm, alpha=alpha),
        grid_spec=grid_spec,
        out_shape=jax.ShapeDtypeStruct((t, D_MODEL), F32),
        compiler_params=_cparams(("arbitrary",)),
        name="moe_combine",
    )(plan["loff"], plan["goff"], plan["nch"], h, meta, ys, ln_g, ln_b)


def _moe(h, w_router, wg, wu, wd, ln_g, ln_b, *, alpha):
    t = h.shape[0]
    nt = t // TMX
    max_rows = 2 * t + nt * N_EXPERTS * (CH - 1) + N_EXPERTS * (TMR - CH)
    n_row_tiles = -(-max_rows // TMR)
    meta, metat, cnt = _moe_route(h, w_router)
    plan = _moe_plan(cnt, n_row_tiles)
    xs = _moe_dispatch(h, metat, plan, n_row_tiles * TMR)
    ys = _moe_experts(xs, wg, wu, wd, plan)
    return _moe_combine(h, meta, ys, plan, ln_g, ln_b, alpha=alpha)


def kernel(x, ln_in_g, ln_in_b, w_in, cmp_pos, cmp_w1, cmp_b1, cmp_w2, cmp_b2, conv_w, w_o, ln1_g, ln1_b, ln2_g, ln2_b, ffn_wg, ffn_wu, ffn_wd, moe_router, moe_wg, moe_wu, moe_wd):
    batch, seq, d = x.shape
    depth = w_in.shape[0]
    assert d == D_MODEL and seq % TM_PROJ == 0 and seq // CMP_STRIDE == LANES
    alpha = (2 * depth) ** 0.25
    t = batch * seq
    vec = lambda v: v.reshape(1, D_MODEL)
    h = x.reshape(t, d)
    for l in range(depth):
        w_r = _prep_w_in(w_in[l])
        cw = jnp.pad(conv_w[l], ((0, 8 - CONV_W), (0, 0)))
        outs = _in_proj(h, vec(ln_in_g), vec(ln_in_b), w_r, cw, seq=seq, pre_ln=(l == 0))
        if l == 0:
            h, outs = outs[0], outs[1:]
        q, kvc, ks, vs, kw, vw, gates, o_conv = outs
        kvcmp = _compress(kvc, cmp_pos[l], cmp_w1[l], cmp_b1[l], cmp_w2[l], cmp_b2[l], batch=batch, seq=seq)
        o_cmp, selb = _cmp_sel(q, kvcmp, gates, batch=batch, seq=seq)
        o_sel = _sel_attn(q, selb, ks, vs, gates, batch=batch, seq=seq)
        o_win = _win_attn(q, kw, vw, gates, batch=batch, seq=seq)
        h = _out_proj(h, o_cmp, o_sel, o_win, o_conv, w_o[l].astype(BF16), vec(ln1_g[l]), vec(ln1_b[l]),
                      alpha=alpha)
        if l % 2 == 0:
            h = _ffn(h, ffn_wg[l // 2].astype(BF16), ffn_wu[l // 2].astype(BF16), ffn_wd[l // 2].astype(BF16),
                     vec(ln2_g[l]), vec(ln2_b[l]), alpha=alpha)
        else:
            h = _moe(h, moe_router[l // 2], moe_wg[l // 2].astype(BF16), moe_wu[l // 2].astype(BF16),
                     moe_wd[l // 2].astype(BF16), vec(ln2_g[l]), vec(ln2_b[l]), alpha=alpha)
    return h.reshape(batch, seq, d)
```

```python
import functools

import numpy as np
import jax
import jax.numpy as jnp
from jax import lax
from jax.experimental import pallas as pl
from jax.experimental.pallas import tpu as pltpu

F32 = jnp.float32
BF16 = jnp.bfloat16

D_MODEL = 1024
HEAD_DIM = 64
N_HEADS = 8
N_KV = 2
HPG = N_HEADS // N_KV
ATTN_W = N_HEADS * HEAD_DIM
KV_W = N_KV * HEAD_DIM
CONV_CH = D_MODEL - ATTN_W
CONV_W = 3
CMP_BLOCK = 32
CMP_STRIDE = 16
CMP_HIDDEN = 256
SEL_BLOCK = 64
SEL_TOPN = 16
WINDOW = 512
D_FF = 2816
N_EXPERTS = 8
D_FF_EXPERT = 1408
LN_EPS = 1e-5
NEG = -1e30
FORCE = 1e9

LANES = 128
HPAD = LANES
Q_SCALE = HEAD_DIM ** -0.5

C_Q = 0
C_KVC = C_Q + N_HEADS * HPAD
C_KS = C_KVC + 2 * KV_W
C_KW = C_KS + N_KV * HPAD
C_U = C_KW + N_KV * HPAD
C_B = C_U + CONV_CH
C_C = C_B + CONV_CH
C_G = C_C + CONV_CH
C_END = C_G + LANES

TM_PROJ = 512
TQ = 256
TK = 256
VMEM_LIMIT = 56 * 1024 * 1024


def _cparams(sem):
    return pltpu.CompilerParams(dimension_semantics=sem, vmem_limit_bytes=VMEM_LIMIT)


def _ln(x, g, b):
    mu = jnp.mean(x, -1, keepdims=True)
    xc = x - mu
    var = jnp.mean(xc * xc, -1, keepdims=True)
    return xc * lax.rsqrt(var + LN_EPS) * g + b


def _dot(a, b):
    return jnp.dot(a, b, preferred_element_type=F32)


def _dot_nt(a, b):
    return lax.dot_general(a, b, (((1,), (1,)), ((), ())), preferred_element_type=F32)


def _resident(shape):
    nd = len(shape)
    return pl.BlockSpec(shape, lambda *_: (0,) * nd, pipeline_mode=pl.Buffered(1))


def _in_proj_kernel(*refs, pre_ln, tiles_per_seq, tm):
    if pre_ln:
        (x_ref, g_ref, b_ref, w_ref, wvt_ref, cw_ref, h_ref, q_ref, kvc_ref, ks_ref, vs_ref,
         kw_ref, vw_ref, gate_ref, oc_ref, carry_ref) = refs
        h = _ln(x_ref[...], g_ref[...], b_ref[...])
        h_ref[...] = h
    else:
        (x_ref, w_ref, wvt_ref, cw_ref, q_ref, kvc_ref, ks_ref, vs_ref,
         kw_ref, vw_ref, gate_ref, oc_ref, carry_ref) = refs
        h = x_ref[...]
    hb = h.astype(BF16)

    def proj(lo, hi):
        return _dot(hb, w_ref[:, lo:hi])

    q_ref[...] = (proj(C_Q, C_KVC) * Q_SCALE).astype(BF16)
    kvc = proj(C_KVC, C_KS)
    for j in range(4):
        kvc_ref[j] = kvc[:, j * HEAD_DIM:(j + 1) * HEAD_DIM]

    seq_tile = pl.program_id(0) % tiles_per_seq
    pos = lax.broadcasted_iota(jnp.int32, (tm, N_KV * HPAD), 0) + seq_tile * tm
    lane = lax.broadcasted_iota(jnp.int32, (tm, N_KV * HPAD), 1) % HPAD
    onehot = jnp.where(pos // SEL_BLOCK == lane - HEAD_DIM, 1.0, 0.0)
    in_tag = (lane >= HEAD_DIM) & (lane < HEAD_DIM + 32)
    ks_ref[...] = jnp.where(in_tag, onehot, proj(C_KS, C_KW)).astype(BF16)
    kw_ref[...] = proj(C_KW, C_U).astype(BF16)

    vrow = lax.broadcasted_iota(jnp.int32, (N_KV * HPAD, tm), 0) % HPAD
    for n, ref in enumerate((vs_ref, vw_ref)):
        vt = _dot_nt(wvt_ref[n * N_KV * HPAD:(n + 1) * N_KV * HPAD, :], hb)
        vt = jnp.where(vrow == HEAD_DIM, 1.0, vt).astype(BF16)
        for c in range(tm // TK):
            ref[c] = vt[:, c * TK:(c + 1) * TK]

    cu = proj(C_C, C_G) * proj(C_U, C_B)

    @pl.when(seq_tile == 0)
    def _():
        carry_ref[...] = jnp.zeros_like(carry_ref)

    prev = carry_ref[...]
    row = lax.broadcasted_iota(jnp.int32, (tm, CONV_CH), 0)
    s1 = jnp.where(row == 0, prev[7:8], pltpu.roll(cu, 1, 0))
    s2 = jnp.where(row == 0, prev[6:7], jnp.where(row == 1, prev[7:8], pltpu.roll(cu, 2, 0)))
    y = s2 * cw_ref[0:1, :] + s1 * cw_ref[1:2, :] + cu * cw_ref[2:3, :]
    oc_ref[...] = (proj(C_B, C_C) * y).astype(BF16)
    carry_ref[...] = cu[tm - 8:tm]
    gate_ref[...] = jax.nn.sigmoid(proj(C_G, C_END))


def _in_proj(x, ln_g, ln_b, w_r, w_vt, conv_w, *, seq, pre_ln):
    t = x.shape[0]
    tm = TM_PROJ
    nt = t // tm
    row = lambda w: pl.BlockSpec((tm, w), lambda i: (i, 0))
    in_specs = [row(D_MODEL)]
    args = [x]
    if pre_ln:
        in_specs += [_resident((1, D_MODEL)), _resident((1, D_MODEL))]
        args += [ln_g, ln_b]
    in_specs += [_resident((D_MODEL, C_END)), _resident((2 * N_KV * HPAD, D_MODEL)), _resident((8, CONV_CH))]
    args += [w_r, w_vt, conv_w]
    out_shape, out_specs = [], []
    if pre_ln:
        out_shape.append(jax.ShapeDtypeStruct((t, D_MODEL), F32))
        out_specs.append(row(D_MODEL))
    out_shape += [
        jax.ShapeDtypeStruct((t, N_HEADS * HPAD), BF16),
        jax.ShapeDtypeStruct((4, t, HEAD_DIM), F32),
        jax.ShapeDtypeStruct((t, N_KV * HPAD), BF16),
        jax.ShapeDtypeStruct((t // TK, N_KV * HPAD, TK), BF16),
        jax.ShapeDtypeStruct((t, N_KV * HPAD), BF16),
        jax.ShapeDtypeStruct((t // TK, N_KV * HPAD, TK), BF16),
        jax.ShapeDtypeStruct((t, LANES), F32),
        jax.ShapeDtypeStruct((t, CONV_CH), BF16),
    ]
    vt_spec = pl.BlockSpec((tm // TK, N_KV * HPAD, TK), lambda i: (i, 0, 0))
    out_specs += [
        row(N_HEADS * HPAD),
        pl.BlockSpec((4, tm, HEAD_DIM), lambda i: (0, i, 0)),
        row(N_KV * HPAD), vt_spec, row(N_KV * HPAD), vt_spec,
        row(LANES), row(CONV_CH),
    ]
    return pl.pallas_call(
        functools.partial(_in_proj_kernel, pre_ln=pre_ln, tiles_per_seq=seq // tm, tm=tm),
        grid=(nt,),
        in_specs=in_specs,
        out_specs=out_specs,
        out_shape=out_shape,
        scratch_shapes=[pltpu.VMEM((8, CONV_CH), F32)],
        compiler_params=_cparams(("arbitrary",)),
        name="in_proj_ln" if pre_ln else "in_proj",
    )(*args)


def _prep_w_in(w_in):
    d = w_in.shape[0]
    o = 0

    def take(n):
        nonlocal o
        s = w_in[:, o:o + n]
        o += n
        return s

    def pad_heads(s, n):
        s = s.reshape(d, n, HEAD_DIM)
        return jnp.pad(s, ((0, 0), (0, 0), (0, HPAD - HEAD_DIM))).reshape(d, n * HPAD)

    q = pad_heads(take(ATTN_W), N_HEADS)
    kc, vc = take(KV_W), take(KV_W)
    ks, vs, kw, vw = (pad_heads(take(KV_W), N_KV) for _ in range(4))
    gates = jnp.pad(take(3 * N_HEADS), ((0, 0), (0, LANES - 3 * N_HEADS)))
    u, bg, cg = take(CONV_CH), take(CONV_CH), take(CONV_CH)
    w_r = jnp.concatenate([q, kc, vc, ks, kw, u, bg, cg, gates], axis=1).astype(BF16)
    return w_r, jnp.concatenate([vs, vw], axis=1).T.astype(BF16)


def _compress_kernel(x_ref, pos_ref, w1_ref, b1_ref, w2_ref, b2_ref, o_ref):
    half = CMP_STRIDE * HEAD_DIM
    x = x_ref[0, 0]
    pos = pos_ref[0]
    xa = (x + pos[:, :half]).astype(BF16)
    xb = (x + pos[:, half:]).astype(BF16)
    a = _dot(xa, w1_ref[0, :half, :])
    b = _dot(xb, w1_ref[0, half:, :])
    n = x.shape[0]
    hid = a + pltpu.roll(b, n - 1, 0) + b1_ref[0]
    act = jax.nn.gelu(hid).astype(BF16)
    o_ref[0, 0] = (_dot(act, w2_ref[0]) + b2_ref[0]).astype(BF16)


def _compress(kvc, cmp_pos, cmp_w1, cmp_b1, cmp_w2, cmp_b2, *, batch, seq):
    nb = seq // CMP_STRIDE
    half = CMP_STRIDE * HEAD_DIM
    x = kvc.reshape(4, batch, nb, half)
    pos = cmp_pos.reshape(2, 1, CMP_BLOCK * HEAD_DIM)
    w1 = cmp_w1.astype(BF16)
    b1 = cmp_b1.reshape(2, 1, CMP_HIDDEN)
    w2 = jnp.pad(cmp_w2, ((0, 0), (0, 0), (0, HPAD - HEAD_DIM))).astype(BF16)
    b2 = jnp.pad(cmp_b2, ((0, 0), (0, HPAD - HEAD_DIM))).reshape(2, 1, HPAD)
    kind = lambda j, b: (j // N_KV, 0, 0)
    return pl.pallas_call(
        _compress_kernel,
        grid=(4, batch),
        in_specs=[
            pl.BlockSpec((1, 1, nb, half), lambda j, b: (j, b, 0, 0)),
            pl.BlockSpec((1, 1, 2 * half), kind),
            pl.BlockSpec((1, 2 * half, CMP_HIDDEN), kind),
            pl.BlockSpec((1, 1, CMP_HIDDEN), kind),
            pl.BlockSpec((1, CMP_HIDDEN, HPAD), kind),
            pl.BlockSpec((1, 1, HPAD), kind),
        ],
        out_specs=pl.BlockSpec((1, 1, nb, HPAD), lambda j, b: (j, b, 0, 0)),
        out_shape=jax.ShapeDtypeStruct((4, batch, nb, HPAD), BF16),
        compiler_params=_cparams(("arbitrary", "arbitrary")),
        name="compress",
    )(x, pos, w1, b1, w2, b2)


def _gate_col(gates, idx):
    lane = lax.broadcasted_iota(jnp.int32, gates.shape, 1)
    return jnp.sum(jnp.where(lane == idx, gates, 0.0), axis=-1, keepdims=True)


def _store_heads(o_ref, heads):
    for p in range(HPG // 2):
        pair = heads[2 * p] + pltpu.roll(heads[2 * p + 1], HEAD_DIM, 1)
        o_ref[:, p * LANES:(p + 1) * LANES] = pair.astype(o_ref.dtype)


def _overlap_t():
    n = np.arange(LANES)
    j = np.arange(32)
    cs = n * CMP_STRIDE
    ss = j * SEL_BLOCK
    ov = (cs[None, :] < ss[:, None] + SEL_BLOCK) & (cs[None, :] + CMP_BLOCK > ss[:, None])
    return jnp.asarray(ov, dtype=BF16)


def _cmp_sel_kernel(q_ref, kc_ref, vc_ref, gate_ref, ov_ref, o_ref, selb_ref, *, tq, n_sel):
    g = pl.program_id(1)
    i = pl.program_id(2)
    kc = kc_ref[0, 0]
    vc = vc_ref[0, 0]
    gates = gate_ref[...]
    t = lax.broadcasted_iota(jnp.int32, (tq, LANES), 0) + i * tq
    n = lax.broadcasted_iota(jnp.int32, (tq, LANES), 1)
    mask = n * CMP_STRIDE + (CMP_BLOCK - 1) <= t
    maskf = mask.astype(F32)
    psum = jnp.zeros((tq, LANES), F32)
    heads = []
    for hh in range(HPG):
        s = _dot_nt(q_ref[:, hh * HPAD:(hh + 1) * HPAD], kc)
        s = jnp.where(mask, s, NEG)
        e = jnp.exp(s - jnp.max(s, axis=-1, keepdims=True))
        p = e / jnp.sum(e, axis=-1, keepdims=True) * maskf
        psum = psum + p
        heads.append(_dot(p.astype(BF16), vc) * _gate_col(gates, g * HPG + hh))
    _store_heads(o_ref, heads)

    ov = ov_ref[...]
    p_hi = psum.astype(BF16)
    r1 = psum - p_hi.astype(F32)
    p_mid = r1.astype(BF16)
    p_lo = (r1 - p_mid.astype(F32)).astype(BF16)
    imp = _dot_nt(ov, p_hi) + _dot_nt(ov, p_mid) + _dot_nt(ov, p_lo)

    jt = lax.broadcasted_iota(jnp.int32, (n_sel, tq), 0)
    tt = lax.broadcasted_iota(jnp.int32, (n_sel, tq), 1) + i * tq
    cur = tt // SEL_BLOCK
    valid = jt * SEL_BLOCK <= tt
    forced = (jt == 0) | (jt == cur) | (jt == cur - 1)
    score = jnp.where(forced, FORCE, jnp.where(valid, imp, NEG))
    rank = jnp.zeros((n_sel, tq), jnp.int32)
    for k in range(n_sel):
        sk = score[k:k + 1, :]
        ahead = (sk > score) | ((sk == score) & (jt > k))
        rank = rank + ahead.astype(jnp.int32)
    bias = jnp.where(rank < SEL_TOPN, 0.0, NEG)
    full = jnp.concatenate(
        [jnp.zeros((HEAD_DIM, tq), F32), bias, jnp.zeros((HPAD - HEAD_DIM - n_sel, tq), F32)], axis=0)
    selb_ref[...] = full.T.astype(BF16)


def _cmp_sel(q, kvcmp, gates, *, batch, seq):
    t = q.shape[0]
    tq = TQ
    nq = seq // tq
    n_sel = seq // SEL_BLOCK
    nb = kvcmp.shape[2]
    assert nb == LANES and n_sel == 32
    rowblk = lambda w: pl.BlockSpec((tq, w), lambda b, g, i: (b * nq + i, g))
    return pl.pallas_call(
        functools.partial(_cmp_sel_kernel, tq=tq, n_sel=n_sel),
        grid=(batch, N_KV, nq),
        in_specs=[
            rowblk(HPG * HPAD),
            pl.BlockSpec((1, 1, nb, HPAD), lambda b, g, i: (g, b, 0, 0)),
            pl.BlockSpec((1, 1, nb, HPAD), lambda b, g, i: (N_KV + g, b, 0, 0)),
            pl.BlockSpec((tq, LANES), lambda b, g, i: (b * nq + i, 0)),
            pl.BlockSpec((n_sel, LANES), lambda b, g, i: (0, 0)),
        ],
        out_specs=[rowblk(HPG * HEAD_DIM), rowblk(LANES)],
        out_shape=[jax.ShapeDtypeStruct((t, ATTN_W), BF16),
                   jax.ShapeDtypeStruct((t, N_KV * LANES), BF16)],
        compiler_params=_cparams(("arbitrary", "arbitrary", "arbitrary")),
        name="cmp_sel",
    )(q, kvcmp, kvcmp, gates, _overlap_t())


def _attn_tile(k, vt, qa, state, mask):
    ss = [_dot_nt(k, qa[hh]) for hh in range(HPG)]
    if mask is not None:
        ss = [jnp.where(mask, s, NEG) for s in ss]
    m_new = [jnp.maximum(state[hh][0], jnp.max(ss[hh], axis=0, keepdims=True)) for hh in range(HPG)]
    ps = [jnp.exp(ss[hh] - m_new[hh]).astype(BF16) for hh in range(HPG)]
    pv = [_dot(vt, ps[hh]) for hh in range(HPG)]
    return tuple((m_new[hh], jnp.exp(state[hh][0] - m_new[hh]) * state[hh][1] + pv[hh]) for hh in range(HPG))


def _attn_init(tq):
    return tuple((jnp.full((1, tq), NEG, F32), jnp.zeros((HPAD, tq), F32)) for _ in range(HPG))


def _attn_finish(state, gate_ref, o_ref, first_gate):
    gt = gate_ref[...].T
    sub = lax.broadcasted_iota(jnp.int32, gt.shape, 0)
    outs = []
    for hh in range(HPG):
        _, acc = state[hh]
        gate = jnp.sum(jnp.where(sub == first_gate + hh, gt, 0.0), axis=0, keepdims=True)
        outs.append(acc[:HEAD_DIM] / acc[HEAD_DIM:HEAD_DIM + 1] * gate)
    o_ref[...] = jnp.concatenate(outs, axis=0).T.astype(o_ref.dtype)


def _sel_attn_kernel(q_ref, selb_ref, k_ref, vt_ref, gate_ref, o_ref, *, tq, tk):
    g = pl.program_id(1)
    i = pl.program_id(2)
    selb = selb_ref[...]
    lane = lax.broadcasted_iota(jnp.int32, selb.shape, 1)
    qa = [jnp.where(lane >= HEAD_DIM, selb, q_ref[:, hh * HPAD:(hh + 1) * HPAD]) for hh in range(HPG)]
    causal = (lax.broadcasted_iota(jnp.int32, (tk, tq), 0) <= lax.broadcasted_iota(jnp.int32, (tk, tq), 1))

    def tile(j, state, mask):
        k = k_ref[pl.ds(pl.multiple_of(j * tk, tk), tk), :]
        return _attn_tile(k, vt_ref[j], qa, state, mask)

    state = lax.fori_loop(0, i, lambda j, st: tile(j, st, None), _attn_init(tq))
    state = tile(i, state, causal)
    _attn_finish(state, gate_ref, o_ref, N_HEADS + g * HPG)


def _sel_attn(q, selb, ks, vs, gates, *, batch, seq):
    t = q.shape[0]
    tq, tk = TQ, TK
    assert tq == tk
    nq = seq // tq
    rowblk = lambda w: pl.BlockSpec((tq, w), lambda b, g, i: (b * nq + i, g))
    seqblk = pl.BlockSpec((seq, HPAD), lambda b, g, i: (b, g))
    vtblk = pl.BlockSpec((seq // tk, HPAD, tk), lambda b, g, i: (b, g, 0))
    return pl.pallas_call(
        functools.partial(_sel_attn_kernel, tq=tq, tk=tk),
        grid=(batch, N_KV, nq),
        in_specs=[rowblk(HPG * HPAD), rowblk(LANES), seqblk, vtblk,
                  pl.BlockSpec((tq, LANES), lambda b, g, i: (b * nq + i, 0))],
        out_specs=rowblk(HPG * HEAD_DIM),
        out_shape=jax.ShapeDtypeStruct((t, ATTN_W), BF16),
        compiler_params=_cparams(("arbitrary", "arbitrary", "arbitrary")),
        name="sel_attn",
    )(q, selb, ks, vs, gates)


def _win_attn_kernel(q_ref, k_ref, vt_ref, gate_ref, o_ref, *, tq, tk):
    g = pl.program_id(1)
    i = pl.program_id(2)
    qa = [q_ref[:, hh * HPAD:(hh + 1) * HPAD] for hh in range(HPG)]
    key = lax.broadcasted_iota(jnp.int32, (tk, tq), 0)
    qry = lax.broadcasted_iota(jnp.int32, (tk, tq), 1)

    def tile(j, state, mask):
        k = k_ref[pl.ds(pl.multiple_of(j * tk, tk), tk), :]
        return _attn_tile(k, vt_ref[j], qa, state, mask)

    state = tile(i, _attn_init(tq), key <= qry)
    state = tile(jnp.maximum(i - 1, 0), state, jnp.broadcast_to(i >= 1, (tk, tq)))
    state = tile(jnp.maximum(i - 2, 0), state, (key > qry) & (i >= 2))
    _attn_finish(state, gate_ref, o_ref, 2 * N_HEADS + g * HPG)


def _win_attn(q, kw, vw, gates, *, batch, seq):
    t = q.shape[0]
    tq, tk = TQ, TK
    assert tq == tk and WINDOW == 2 * tk
    nq = seq // tq
    rowblk = lambda w: pl.BlockSpec((tq, w), lambda b, g, i: (b * nq + i, g))
    seqblk = pl.BlockSpec((seq, HPAD), lambda b, g, i: (b, g))
    vtblk = pl.BlockSpec((seq // tk, HPAD, tk), lambda b, g, i: (b, g, 0))
    return pl.pallas_call(
        functools.partial(_win_attn_kernel, tq=tq, tk=tk),
        grid=(batch, N_KV, nq),
        in_specs=[rowblk(HPG * HPAD), seqblk, vtblk,
                  pl.BlockSpec((tq, LANES), lambda b, g, i: (b * nq + i, 0))],
        out_specs=rowblk(HPG * HEAD_DIM),
        out_shape=jax.ShapeDtypeStruct((t, ATTN_W), BF16),
        compiler_params=_cparams(("arbitrary", "arbitrary", "arbitrary")),
        name="win_attn",
    )(q, kw, vw, gates)


def _out_proj_kernel(h_ref, oc_ref, os_ref, ow_ref, ocv_ref, w_ref, g_ref, b_ref, o_ref, *, alpha):
    oa = (oc_ref[...].astype(F32) + os_ref[...].astype(F32) + ow_ref[...].astype(F32)).astype(BF16)
    m = _dot(oa, w_ref[:ATTN_W, :]) + _dot(ocv_ref[...], w_ref[ATTN_W:, :])
    o_ref[...] = _ln(alpha * h_ref[...] + m, g_ref[...], b_ref[...])


def _out_proj(h, o_cmp, o_sel, o_win, o_conv, w_o, ln_g, ln_b, *, alpha):
    t = h.shape[0]
    tm = TM_PROJ
    row = lambda w: pl.BlockSpec((tm, w), lambda i: (i, 0))
    return pl.pallas_call(
        functools.partial(_out_proj_kernel, alpha=alpha),
        grid=(t // tm,),
        in_specs=[row(D_MODEL), row(ATTN_W), row(ATTN_W), row(ATTN_W), row(CONV_CH),
                  _resident((D_MODEL, D_MODEL)), _resident((1, D_MODEL)), _resident((1, D_MODEL))],
        out_specs=row(D_MODEL),
        out_shape=jax.ShapeDtypeStruct((t, D_MODEL), F32),
        compiler_params=_cparams(("arbitrary",)),
        name="out_proj",
    )(h, o_cmp, o_sel, o_win, o_conv, w_o, ln_g, ln_b)


def _ffn_kernel(h_ref, wg_ref, wu_ref, wd_ref, g_ref, b_ref, o_ref, *, alpha):
    h = h_ref[...]
    hb = h.astype(BF16)
    a = (jax.nn.silu(_dot(hb, wg_ref[...])) * _dot(hb, wu_ref[...])).astype(BF16)
    o_ref[...] = _ln(alpha * h + _dot(a, wd_ref[...]), g_ref[...], b_ref[...])


def _ffn(h, wg, wu, wd, ln_g, ln_b, *, alpha):
    t = h.shape[0]
    tm = TM_PROJ
    row = pl.BlockSpec((tm, D_MODEL), lambda i: (i, 0))
    return pl.pallas_call(
        functools.partial(_ffn_kernel, alpha=alpha),
        grid=(t // tm,),
        in_specs=[row, _resident((D_MODEL, D_FF)), _resident((D_MODEL, D_FF)), _resident((D_FF, D_MODEL)),
                  _resident((1, D_MODEL)), _resident((1, D_MODEL))],
        out_specs=row,
        out_shape=jax.ShapeDtypeStruct((t, D_MODEL), F32),
        compiler_params=_cparams(("arbitrary",)),
        name="ffn",
    )(h, wg, wu, wd, ln_g, ln_b)


TMX = 512
CH = 16
LROWS = 2 * TMX + N_EXPERTS * CH
TMR = 512


def _route_kernel(h_ref, wr_ref, ltri_ref, ustr_ref, meta_ref, metat_ref, cnt_ref, *, tm):
    h = h_ref[...]
    h_hi = h.astype(BF16)
    h_lo = (h - h_hi.astype(F32)).astype(BF16)
    logits = _dot(h_hi, wr_ref[0]) + _dot(h_lo, wr_ref[0]) + _dot(h_hi, wr_ref[1])
    lane = lax.broadcasted_iota(jnp.int32, logits.shape, 1)
    logits = jnp.where(lane < N_EXPERTS, logits, -jnp.inf)
    m1 = jnp.max(logits, axis=-1, keepdims=True)
    i1 = jnp.min(jnp.where(logits == m1, lane, LANES), axis=-1, keepdims=True)
    rest = jnp.where(lane == i1, -jnp.inf, logits)
    m2 = jnp.max(rest, axis=-1, keepdims=True)
    i2 = jnp.min(jnp.where(rest == m2, lane, LANES), axis=-1, keepdims=True)
    e2 = jnp.exp(m2 - m1)
    den = 1.0 + e2
    w1 = 1.0 / den
    w2 = e2 / den

    routed = (lane == i1) | (lane == i2)
    cnt = _dot(ltri_ref[...], routed.astype(BF16))
    n = cnt[tm - 1:tm, :]
    padded = jnp.floor((n + (CH - 1)) * (1.0 / CH)) * CH
    seg_off = _dot(jnp.broadcast_to(padded, (8, LANES)).astype(BF16), ustr_ref[...])[0:1]
    dest = seg_off + cnt - 1.0
    d1 = jnp.sum(jnp.where(lane == i1, dest, 0.0), axis=-1, keepdims=True)
    d2 = jnp.sum(jnp.where(lane == i2, dest, 0.0), axis=-1, keepdims=True)
    meta = jnp.where(lane == 0, d1, jnp.where(lane == 1, d2, jnp.where(lane == 2, w1, jnp.where(lane == 3, w2, 0.0))))
    meta_ref[...] = meta
    metat_ref[...] = meta.T[0:8, :]
    cnt_ref[0] = jnp.broadcast_to(n, (8, LANES))


def _moe_route(h, w_router):
    t = h.shape[0]
    tm = TMX
    nt = t // tm
    wr = jnp.pad(w_router, ((0, 0), (0, LANES - N_EXPERTS)))
    wr_hi = wr.astype(BF16)
    wr_lo = (wr - wr_hi.astype(F32)).astype(BF16)
    wr2 = jnp.stack([wr_hi, wr_lo])
    ltri = jnp.asarray(np.tril(np.ones((tm, tm), np.float32)), dtype=BF16)
    ustr = jnp.asarray(np.triu(np.ones((LANES, LANES), np.float32), 1), dtype=BF16)
    return pl.pallas_call(
        functools.partial(_route_kernel, tm=tm),
        grid=(nt,),
        in_specs=[pl.BlockSpec((tm, D_MODEL), lambda i: (i, 0)), _resident((2, D_MODEL, LANES)),
                  _resident((tm, tm)), _resident((LANES, LANES))],
        out_specs=[pl.BlockSpec((tm, LANES), lambda i: (i, 0)),
                   pl.BlockSpec((8, tm), lambda i: (0, i)),
                   pl.BlockSpec((1, 8, LANES), lambda i: (i, 0, 0))],
        out_shape=[jax.ShapeDtypeStruct((t, LANES), F32), jax.ShapeDtypeStruct((8, t), F32),
                   jax.ShapeDtypeStruct((nt, 8, LANES), F32)],
        compiler_params=_cparams(("arbitrary",)),
        name="moe_route",
    )(h, wr2, ltri, ustr)


def _moe_plan(cnt, n_row_tiles):
    n = cnt[:, 0, :N_EXPERTS].astype(jnp.int32)
    p = (n + CH - 1) // CH * CH
    tot = p.sum(0)
    tot_pad = (tot + TMR - 1) // TMR * TMR
    gend = jnp.cumsum(tot_pad)
    gstart = gend - tot_pad
    goff = gstart[None, :] + jnp.cumsum(p, 0) - p
    loff = jnp.cumsum(p, 1) - p
    n_used = gend[-1:] // TMR
    tile_start = jnp.arange(n_row_tiles, dtype=jnp.int32) * TMR
    texp = jnp.minimum(jnp.sum(tile_start[:, None] >= gend[None, :], axis=1), N_EXPERTS - 1).astype(jnp.int32)
    texp = jnp.where(jnp.arange(n_row_tiles) < n_used[0], texp, texp[jnp.maximum(n_used[0] - 1, 0)])
    flat = lambda a: a.reshape(-1).astype(jnp.int32)
    gap_start = jnp.concatenate([gstart + tot, gend[-1:]])
    gap_rows = jnp.concatenate([tot_pad - tot, n_row_tiles * TMR - gend[-1:]])
    return dict(loff=flat(loff), goff=flat(goff), nch=flat(p // CH), gap_start=flat(gap_start),
                gap_nch=flat(gap_rows // CH), texp=flat(texp), n_used=flat(n_used))


def _chunk_copies(src_ref, dst_ref, sem, src_off, dst_off, n, src_step=CH):
    def body(c, carry):
        s = pl.multiple_of(src_off + c * src_step, CH)
        d = pl.multiple_of(dst_off + c * CH, CH)
        pltpu.make_async_copy(src_ref.at[pl.ds(s, CH)], dst_ref.at[pl.ds(d, CH)], sem).start()
        return carry
    lax.fori_loop(0, n, body, 0)


def _wait_chunks(src_ref, dst_ref, sem, n):
    def body(c, carry):
        pltpu.make_async_copy(src_ref.at[pl.ds(0, CH)], dst_ref.at[pl.ds(0, CH)], sem).wait()
        return carry
    lax.fori_loop(0, n, body, 0)


def _dispatch_kernel(loff, goff, nch, gap_start, gap_nch, h_ref, metat_ref, xs_ref, xc_ref, z_ref, sem, *, tm):
    i = pl.program_id(0)
    d1 = metat_ref[0:1, :]
    d2 = metat_ref[1:2, :]
    r = lax.broadcasted_iota(jnp.int32, (LROWS, tm), 0).astype(F32)
    onehot = ((r == d1) | (r == d2)).astype(BF16)
    xc_ref[...] = _dot(onehot, h_ref[...].astype(BF16)).astype(BF16)
    total = 0
    for e in range(N_EXPERTS):
        n = nch[i * N_EXPERTS + e]
        _chunk_copies(xc_ref, xs_ref, sem, loff[i * N_EXPERTS + e], goff[i * N_EXPERTS + e], n)
        total = total + n

    @pl.when(i == pl.num_programs(0) - 1)
    def _():
        z_ref[...] = jnp.zeros_like(z_ref)
        gaps = 0
        for e in range(N_EXPERTS + 1):
            _chunk_copies(z_ref, xs_ref, sem, 0, gap_start[e], gap_nch[e], src_step=0)
            gaps = gaps + gap_nch[e]
        _wait_chunks(z_ref, xs_ref, sem, gaps)

    _wait_chunks(xc_ref, xs_ref, sem, total)


def _moe_dispatch(h, metat, plan, n_rows):
    t = h.shape[0]
    tm = TMX
    grid_spec = pltpu.PrefetchScalarGridSpec(
        num_scalar_prefetch=5,
        grid=(t // tm,),
        in_specs=[pl.BlockSpec((tm, D_MODEL), lambda i, *_: (i, 0)),
                  pl.BlockSpec((8, tm), lambda i, *_: (0, i))],
        out_specs=pl.BlockSpec(memory_space=pl.ANY),
        scratch_shapes=[pltpu.VMEM((LROWS, D_MODEL), BF16), pltpu.VMEM((CH, D_MODEL), BF16),
                        pltpu.SemaphoreType.DMA(())],
    )
    return pl.pallas_call(
        functools.partial(_dispatch_kernel, tm=tm),
        grid_spec=grid_spec,
        out_shape=jax.ShapeDtypeStruct((n_rows, D_MODEL), BF16),
        compiler_params=_cparams(("arbitrary",)),
        name="moe_dispatch",
    )(plan["loff"], plan["goff"], plan["nch"], plan["gap_start"], plan["gap_nch"], h, metat)


def _experts_kernel(texp, n_used, x_ref, wg_ref, wu_ref, wd_ref, y_ref):
    used = pl.program_id(0) < n_used[0]

    @pl.when(used)
    def _():
        x = x_ref[...]
        a = (jax.nn.silu(_dot(x, wg_ref[0])) * _dot(x, wu_ref[0])).astype(BF16)
        y_ref[...] = _dot(a, wd_ref[0]).astype(BF16)

    @pl.when(jnp.logical_not(used))
    def _():
        y_ref[...] = jnp.zeros_like(y_ref)


def _moe_experts(xs, wg, wu, wd, plan):
    n_rows = xs.shape[0]
    rows = lambda r, texp, n_used: (jnp.minimum(r, n_used[0] - 1), 0)
    wspec = lambda shape: pl.BlockSpec((1,) + shape, lambda r, texp, n_used: (texp[r], 0, 0))
    grid_spec = pltpu.PrefetchScalarGridSpec(
        num_scalar_prefetch=2,
        grid=(n_rows // TMR,),
        in_specs=[pl.BlockSpec((TMR, D_MODEL), rows), wspec((D_MODEL, D_FF_EXPERT)),
                  wspec((D_MODEL, D_FF_EXPERT)), wspec((D_FF_EXPERT, D_MODEL))],
        out_specs=pl.BlockSpec((TMR, D_MODEL), lambda r, texp, n_used: (r, 0)),
    )
    return pl.pallas_call(
        _experts_kernel,
        grid_spec=grid_spec,
        out_shape=jax.ShapeDtypeStruct((n_rows, D_MODEL), BF16),
        compiler_params=_cparams(("arbitrary",)),
        name="moe_experts",
    )(plan["texp"], plan["n_used"], xs, wg, wu, wd)


def _combine_kernel(loff, goff, nch, h_ref, meta_ref, ys_ref, g_ref, b_ref, o_ref, yb_ref, sem, *, tm, alpha):
    i = pl.program_id(0)

    @pl.when(i == 0)
    def _():
        yb_ref[...] = jnp.zeros_like(yb_ref)

    total = 0
    for e in range(N_EXPERTS):
        n = nch[i * N_EXPERTS + e]
        _chunk_copies(ys_ref, yb_ref, sem, goff[i * N_EXPERTS + e], loff[i * N_EXPERTS + e], n)
        total = total + n
    meta = meta_ref[...]
    r = lax.broadcasted_iota(jnp.int32, (tm, LROWS), 1).astype(F32)
    pick1 = (r == meta[:, 0:1]).astype(BF16)
    pick2 = (r == meta[:, 1:2]).astype(BF16)
    _wait_chunks(ys_ref, yb_ref, sem, total)
    yb = yb_ref[...]
    f = meta[:, 2:3] * _dot(pick1, yb) + meta[:, 3:4] * _dot(pick2, yb)
    o_ref[...] = _ln(alpha * h_ref[...] + f, g_ref[...], b_ref[...])


def _moe_combine(h, meta, ys, plan, ln_g, ln_b, *, alpha):
    t = h.shape[0]
    tm = TMX
    grid_spec = pltpu.PrefetchScalarGridSpec(
        num_scalar_prefetch=3,
        grid=(t // tm,),
        in_specs=[pl.BlockSpec((tm, D_MODEL), lambda i, *_: (i, 0)),
                  pl.BlockSpec((tm, LANES), lambda i, *_: (i, 0)),
                  pl.BlockSpec(memory_space=pl.ANY),
                  pl.BlockSpec((1, D_MODEL), lambda i, *_: (0, 0)),
                  pl.BlockSpec((1, D_MODEL), lambda i, *_: (0, 0))],
        out_specs=pl.BlockSpec((tm, D_MODEL), lambda i, *_: (i, 0)),
        scratch_shapes=[pltpu.VMEM((LROWS, D_MODEL), BF16), pltpu.SemaphoreType.DMA(())],
    )
    return pl.pallas_call(
        functools.partial(_combine_kernel, tm=tm, alpha=alpha),
        grid_spec=grid_spec,
        out_shape=jax.ShapeDtypeStruct((t, D_MODEL), F32),
        compiler_params=_cparams(("arbitrary",)),
        name="moe_combine",
    )(plan["loff"], plan["goff"], plan["nch"], h, meta, ys, ln_g, ln_b)


def _moe(h, w_router, wg, wu, wd, ln_g, ln_b, *, alpha):
    t = h.shape[0]
    nt = t // TMX
    max_rows = 2 * t + nt * N_EXPERTS * (CH - 1) + N_EXPERTS * (TMR - CH)
    n_row_tiles = -(-max_rows // TMR)
    meta, metat, cnt = _moe_route(h, w_router)
    plan = _moe_plan(cnt, n_row_tiles)
    xs = _moe_dispatch(h, metat, plan, n_row_tiles * TMR)
    ys = _moe_experts(xs, wg, wu, wd, plan)
    return _moe_combine(h, meta, ys, plan, ln_g, ln_b, alpha=alpha)


def kernel(x, ln_in_g, ln_in_b, w_in, cmp_pos, cmp_w1, cmp_b1, cmp_w2, cmp_b2, conv_w, w_o, ln1_g, ln1_b, ln2_g, ln2_b, ffn_wg, ffn_wu, ffn_wd, moe_router, moe_wg, moe_wu, moe_wd):
    batch, seq, d = x.shape
    depth = w_in.shape[0]
    assert d == D_MODEL and seq % TM_PROJ == 0 and seq // CMP_STRIDE == LANES
    alpha = (2 * depth) ** 0.25
    t = batch * seq
    vec = lambda v: v.reshape(1, D_MODEL)
    h = x.reshape(t, d)
    for l in range(depth):
        w_r, w_vt = _prep_w_in(w_in[l])
        cw = jnp.pad(conv_w[l], ((0, 8 - CONV_W), (0, 0)))
        outs = _in_proj(h, vec(ln_in_g), vec(ln_in_b), w_r, w_vt, cw, seq=seq, pre_ln=(l == 0))
        if l == 0:
            h, outs = outs[0], outs[1:]
        q, kvc, ks, vs, kw, vw, gates, o_conv = outs
        kvcmp = _compress(kvc, cmp_pos[l], cmp_w1[l], cmp_b1[l], cmp_w2[l], cmp_b2[l], batch=batch, seq=seq)
        o_cmp, selb = _cmp_sel(q, kvcmp, gates, batch=batch, seq=seq)
        o_sel = _sel_attn(q, selb, ks, vs, gates, batch=batch, seq=seq)
        o_win = _win_attn(q, kw, vw, gates, batch=batch, seq=seq)
        h = _out_proj(h, o_cmp, o_sel, o_win, o_conv, w_o[l].astype(BF16), vec(ln1_g[l]), vec(ln1_b[l]),
                      alpha=alpha)
        if l % 2 == 0:
            h = _ffn(h, ffn_wg[l // 2].astype(BF16), ffn_wu[l // 2].astype(BF16), ffn_wd[l // 2].astype(BF16),
                     vec(ln2_g[l]), vec(ln2_b[l]), alpha=alpha)
        else:
            h = _moe(h, moe_router[l // 2], moe_wg[l // 2].astype(BF16), moe_wu[l // 2].astype(BF16),
                     moe_wd[l // 2].astype(BF16), vec(ln2_g[l]), vec(ln2_b[l]), alpha=alpha)
    return h.reshape(batch, seq, d)
```

```python
import functools

import numpy as np
import jax
import jax.numpy as jnp
from jax import lax
from jax.experimental import pallas as pl
from jax.experimental.pallas import tpu as pltpu

F32 = jnp.float32
BF16 = jnp.bfloat16

D_MODEL = 1024
HEAD_DIM = 64
N_HEADS = 8
N_KV = 2
HPG = N_HEADS // N_KV
ATTN_W = N_HEADS * HEAD_DIM
KV_W = N_KV * HEAD_DIM
CONV_CH = D_MODEL - ATTN_W
CONV_W = 3
CMP_BLOCK = 32
CMP_STRIDE = 16
CMP_HIDDEN = 256
SEL_BLOCK = 64
SEL_TOPN = 16
WINDOW = 512
D_FF = 2816
N_EXPERTS = 8
D_FF_EXPERT = 1408
LN_EPS = 1e-5
NEG = -1e30
FORCE = 1e9

LANES = 128
HPAD = LANES
Q_SCALE = HEAD_DIM ** -0.5

C_Q = 0
C_KVC = C_Q + N_HEADS * HPAD
C_KS = C_KVC + 2 * KV_W
C_KW = C_KS + N_KV * HPAD
C_U = C_KW + N_KV * HPAD
C_B = C_U + CONV_CH
C_C = C_B + CONV_CH
C_G = C_C + CONV_CH
C_END = C_G + LANES

TM_PROJ = 512
TQ = 256
TK = 256
VMEM_LIMIT = 56 * 1024 * 1024


def _cparams(sem):
    return pltpu.CompilerParams(dimension_semantics=sem, vmem_limit_bytes=VMEM_LIMIT)


def _ln(x, g, b):
    mu = jnp.mean(x, -1, keepdims=True)
    xc = x - mu
    var = jnp.mean(xc * xc, -1, keepdims=True)
    return xc * lax.rsqrt(var + LN_EPS) * g + b


def _dot(a, b):
    return jnp.dot(a, b, preferred_element_type=F32)


def _dot_nt(a, b):
    return lax.dot_general(a, b, (((1,), (1,)), ((), ())), preferred_element_type=F32)


def _resident(shape):
    nd = len(shape)
    return pl.BlockSpec(shape, lambda *_: (0,) * nd, pipeline_mode=pl.Buffered(1))


def _in_proj_kernel(*refs, pre_ln, tiles_per_seq, tm):
    if pre_ln:
        (x_ref, g_ref, b_ref, w_ref, wvt_ref, cw_ref, h_ref, q_ref, kvc_ref, ks_ref, vs_ref,
         kw_ref, vw_ref, gate_ref, oc_ref, carry_ref) = refs
        h = _ln(x_ref[...], g_ref[...], b_ref[...])
        h_ref[...] = h
    else:
        (x_ref, w_ref, wvt_ref, cw_ref, q_ref, kvc_ref, ks_ref, vs_ref,
         kw_ref, vw_ref, gate_ref, oc_ref, carry_ref) = refs
        h = x_ref[...]
    hb = h.astype(BF16)

    def proj(lo, hi):
        return _dot(hb, w_ref[:, lo:hi])

    q_ref[...] = (proj(C_Q, C_KVC) * Q_SCALE).astype(BF16)
    kvc = proj(C_KVC, C_KS)
    for j in range(4):
        kvc_ref[j] = kvc[:, j * HEAD_DIM:(j + 1) * HEAD_DIM]

    seq_tile = pl.program_id(0) % tiles_per_seq
    pos = lax.broadcasted_iota(jnp.int32, (tm, N_KV * HPAD), 0) + seq_tile * tm
    lane = lax.broadcasted_iota(jnp.int32, (tm, N_KV * HPAD), 1) % HPAD
    onehot = jnp.where(pos // SEL_BLOCK == lane - HEAD_DIM, 1.0, 0.0)
    in_tag = (lane >= HEAD_DIM) & (lane < HEAD_DIM + 32)
    ks_ref[...] = jnp.where(in_tag, onehot, proj(C_KS, C_KW)).astype(BF16)
    kw_ref[...] = proj(C_KW, C_U).astype(BF16)

    vrow = lax.broadcasted_iota(jnp.int32, (N_KV * HPAD, tm), 0) % HPAD
    for n, ref in enumerate((vs_ref, vw_ref)):
        vt = _dot_nt(wvt_ref[n * N_KV * HPAD:(n + 1) * N_KV * HPAD, :], hb)
        vt = jnp.where(vrow == HEAD_DIM, 1.0, vt).astype(BF16)
        for c in range(tm // TK):
            ref[c] = vt[:, c * TK:(c + 1) * TK]

    cu = proj(C_C, C_G) * proj(C_U, C_B)

    @pl.when(seq_tile == 0)
    def _():
        carry_ref[...] = jnp.zeros_like(carry_ref)

    prev = carry_ref[...]
    row = lax.broadcasted_iota(jnp.int32, (tm, CONV_CH), 0)
    s1 = jnp.where(row == 0, prev[7:8], pltpu.roll(cu, 1, 0))
    s2 = jnp.where(row == 0, prev[6:7], jnp.where(row == 1, prev[7:8], pltpu.roll(cu, 2, 0)))
    y = s2 * cw_ref[0:1, :] + s1 * cw_ref[1:2, :] + cu * cw_ref[2:3, :]
    oc_ref[...] = (proj(C_B, C_C) * y).astype(BF16)
    carry_ref[...] = cu[tm - 8:tm]
    gate_ref[...] = jax.nn.sigmoid(proj(C_G, C_END))


def _in_proj(x, ln_g, ln_b, w_r, w_vt, conv_w, *, seq, pre_ln):
    t = x.shape[0]
    tm = TM_PROJ
    nt = t // tm
    row = lambda w: pl.BlockSpec((tm, w), lambda i: (i, 0))
    in_specs = [row(D_MODEL)]
    args = [x]
    if pre_ln:
        in_specs += [_resident((1, D_MODEL)), _resident((1, D_MODEL))]
        args += [ln_g, ln_b]
    in_specs += [_resident((D_MODEL, C_END)), _resident((2 * N_KV * HPAD, D_MODEL)), _resident((8, CONV_CH))]
    args += [w_r, w_vt, conv_w]
    out_shape, out_specs = [], []
    if pre_ln:
        out_shape.append(jax.ShapeDtypeStruct((t, D_MODEL), F32))
        out_specs.append(row(D_MODEL))
    out_shape += [
        jax.ShapeDtypeStruct((t, N_HEADS * HPAD), BF16),
        jax.ShapeDtypeStruct((4, t, HEAD_DIM), F32),
        jax.ShapeDtypeStruct((t, N_KV * HPAD), BF16),
        jax.ShapeDtypeStruct((t // TK, N_KV * HPAD, TK), BF16),
        jax.ShapeDtypeStruct((t, N_KV * HPAD), BF16),
        jax.ShapeDtypeStruct((t // TK, N_KV * HPAD, TK), BF16),
        jax.ShapeDtypeStruct((t, LANES), F32),
        jax.ShapeDtypeStruct((t, CONV_CH), BF16),
    ]
    vt_spec = pl.BlockSpec((tm // TK, N_KV * HPAD, TK), lambda i: (i, 0, 0))
    out_specs += [
        row(N_HEADS * HPAD),
        pl.BlockSpec((4, tm, HEAD_DIM), lambda i: (0, i, 0)),
        row(N_KV * HPAD), vt_spec, row(N_KV * HPAD), vt_spec,
        row(LANES), row(CONV_CH),
    ]
    return pl.pallas_call(
        functools.partial(_in_proj_kernel, pre_ln=pre_ln, tiles_per_seq=seq // tm, tm=tm),
        grid=(nt,),
        in_specs=in_specs,
        out_specs=out_specs,
        out_shape=out_shape,
        scratch_shapes=[pltpu.VMEM((8, CONV_CH), F32)],
        compiler_params=_cparams(("arbitrary",)),
        name="in_proj_ln" if pre_ln else "in_proj",
    )(*args)


def _prep_w_in(w_in):
    d = w_in.shape[0]
    o = 0

    def take(n):
        nonlocal o
        s = w_in[:, o:o + n]
        o += n
        return s

    def pad_heads(s, n):
        s = s.reshape(d, n, HEAD_DIM)
        return jnp.pad(s, ((0, 0), (0, 0), (0, HPAD - HEAD_DIM))).reshape(d, n * HPAD)

    q = pad_heads(take(ATTN_W), N_HEADS)
    kc, vc = take(KV_W), take(KV_W)
    ks, vs, kw, vw = (pad_heads(take(KV_W), N_KV) for _ in range(4))
    gates = jnp.pad(take(3 * N_HEADS), ((0, 0), (0, LANES - 3 * N_HEADS)))
    u, bg, cg = take(CONV_CH), take(CONV_CH), take(CONV_CH)
    w_r = jnp.concatenate([q, kc, vc, ks, kw, u, bg, cg, gates], axis=1).astype(BF16)
    return w_r, jnp.concatenate([vs, vw], axis=1).T.astype(BF16)


def _compress_kernel(x_ref, pos_ref, w1_ref, b1_ref, w2_ref, b2_ref, o_ref):
    half = CMP_STRIDE * HEAD_DIM
    x = x_ref[0, 0]
    pos = pos_ref[0]
    xa = (x + pos[:, :half]).astype(BF16)
    xb = (x + pos[:, half:]).astype(BF16)
    a = _dot(xa, w1_ref[0, :half, :])
    b = _dot(xb, w1_ref[0, half:, :])
    n = x.shape[0]
    hid = a + pltpu.roll(b, n - 1, 0) + b1_ref[0]
    act = jax.nn.gelu(hid).astype(BF16)
    o_ref[0, 0] = (_dot(act, w2_ref[0]) + b2_ref[0]).astype(BF16)


def _compress(kvc, cmp_pos, cmp_w1, cmp_b1, cmp_w2, cmp_b2, *, batch, seq):
    nb = seq // CMP_STRIDE
    half = CMP_STRIDE * HEAD_DIM
    x = kvc.reshape(4, batch, nb, half)
    pos = cmp_pos.reshape(2, 1, CMP_BLOCK * HEAD_DIM)
    w1 = cmp_w1.astype(BF16)
    b1 = cmp_b1.reshape(2, 1, CMP_HIDDEN)
    w2 = jnp.pad(cmp_w2, ((0, 0), (0, 0), (0, HPAD - HEAD_DIM))).astype(BF16)
    b2 = jnp.pad(cmp_b2, ((0, 0), (0, HPAD - HEAD_DIM))).reshape(2, 1, HPAD)
    kind = lambda j, b: (j // N_KV, 0, 0)
    return pl.pallas_call(
        _compress_kernel,
        grid=(4, batch),
        in_specs=[
            pl.BlockSpec((1, 1, nb, half), lambda j, b: (j, b, 0, 0)),
            pl.BlockSpec((1, 1, 2 * half), kind),
            pl.BlockSpec((1, 2 * half, CMP_HIDDEN), kind),
            pl.BlockSpec((1, 1, CMP_HIDDEN), kind),
            pl.BlockSpec((1, CMP_HIDDEN, HPAD), kind),
            pl.BlockSpec((1, 1, HPAD), kind),
        ],
        out_specs=pl.BlockSpec((1, 1, nb, HPAD), lambda j, b: (j, b, 0, 0)),
        out_shape=jax.ShapeDtypeStruct((4, batch, nb, HPAD), BF16),
        compiler_params=_cparams(("arbitrary", "arbitrary")),
        name="compress",
    )(x, pos, w1, b1, w2, b2)


def _gate_col(gates, idx):
    lane = lax.broadcasted_iota(jnp.int32, gates.shape, 1)
    return jnp.sum(jnp.where(lane == idx, gates, 0.0), axis=-1, keepdims=True)


def _store_heads(o_ref, heads):
    for p in range(HPG // 2):
        pair = heads[2 * p] + pltpu.roll(heads[2 * p + 1], HEAD_DIM, 1)
        o_ref[:, p * LANES:(p + 1) * LANES] = pair.astype(o_ref.dtype)


def _overlap_t():
    n = np.arange(LANES)
    j = np.arange(32)
    cs = n * CMP_STRIDE
    ss = j * SEL_BLOCK
    ov = (cs[None, :] < ss[:, None] + SEL_BLOCK) & (cs[None, :] + CMP_BLOCK > ss[:, None])
    return jnp.asarray(ov, dtype=BF16)


def _cmp_sel_kernel(q_ref, kc_ref, vc_ref, gate_ref, ov_ref, o_ref, selb_ref, *, tq, n_sel):
    g = pl.program_id(1)
    i = pl.program_id(2)
    kc = kc_ref[0, 0]
    vc = vc_ref[0, 0]
    gates = gate_ref[...]
    t = lax.broadcasted_iota(jnp.int32, (tq, LANES), 0) + i * tq
    n = lax.broadcasted_iota(jnp.int32, (tq, LANES), 1)
    mask = n * CMP_STRIDE + (CMP_BLOCK - 1) <= t
    maskf = mask.astype(F32)
    psum = jnp.zeros((tq, LANES), F32)
    heads = []
    for hh in range(HPG):
        s = _dot_nt(q_ref[:, hh * HPAD:(hh + 1) * HPAD], kc)
        s = jnp.where(mask, s, NEG)
        e = jnp.exp(s - jnp.max(s, axis=-1, keepdims=True))
        p = e / jnp.sum(e, axis=-1, keepdims=True) * maskf
        psum = psum + p
        heads.append(_dot(p.astype(BF16), vc) * _gate_col(gates, g * HPG + hh))
    _store_heads(o_ref, heads)

    ov = ov_ref[...]
    p_hi = psum.astype(BF16)
    r1 = psum - p_hi.astype(F32)
    p_mid = r1.astype(BF16)
    p_lo = (r1 - p_mid.astype(F32)).astype(BF16)
    imp = _dot_nt(ov, p_hi) + _dot_nt(ov, p_mid) + _dot_nt(ov, p_lo)

    jt = lax.broadcasted_iota(jnp.int32, (n_sel, tq), 0)
    tt = lax.broadcasted_iota(jnp.int32, (n_sel, tq), 1) + i * tq
    cur = tt // SEL_BLOCK
    valid = jt * SEL_BLOCK <= tt
    forced = (jt == 0) | (jt == cur) | (jt == cur - 1)
    score = jnp.where(forced, FORCE, jnp.where(valid, imp, NEG))
    rank = jnp.zeros((n_sel, tq), jnp.int32)
    for k in range(n_sel):
        sk = score[k:k + 1, :]
        ahead = (sk > score) | ((sk == score) & (jt > k))
        rank = rank + ahead.astype(jnp.int32)
    bias = jnp.where(rank < SEL_TOPN, 0.0, NEG)
    full = jnp.concatenate(
        [jnp.zeros((HEAD_DIM, tq), F32), bias, jnp.zeros((HPAD - HEAD_DIM - n_sel, tq), F32)], axis=0)
    selb_ref[...] = full.T.astype(BF16)


def _cmp_sel(q, kvcmp, gates, *, batch, seq):
    t = q.shape[0]
    tq = TQ
    nq = seq // tq
    n_sel = seq // SEL_BLOCK
    nb = kvcmp.shape[2]
    assert nb == LANES and n_sel == 32
    rowblk = lambda w: pl.BlockSpec((tq, w), lambda b, g, i: (b * nq + i, g))
    return pl.pallas_call(
        functools.partial(_cmp_sel_kernel, tq=tq, n_sel=n_sel),
        grid=(batch, N_KV, nq),
        in_specs=[
            rowblk(HPG * HPAD),
            pl.BlockSpec((1, 1, nb, HPAD), lambda b, g, i: (g, b, 0, 0)),
            pl.BlockSpec((1, 1, nb, HPAD), lambda b, g, i: (N_KV + g, b, 0, 0)),
            pl.BlockSpec((tq, LANES), lambda b, g, i: (b * nq + i, 0)),
            pl.BlockSpec((n_sel, LANES), lambda b, g, i: (0, 0)),
        ],
        out_specs=[rowblk(HPG * HEAD_DIM), rowblk(LANES)],
        out_shape=[jax.ShapeDtypeStruct((t, ATTN_W), BF16),
                   jax.ShapeDtypeStruct((t, N_KV * LANES), BF16)],
        compiler_params=_cparams(("arbitrary", "arbitrary", "arbitrary")),
        name="cmp_sel",
    )(q, kvcmp, kvcmp, gates, _overlap_t())


def _attn_scores(k, qa):
    return tuple(_dot_nt(k, qa[hh]) for hh in range(HPG))


def _attn_update(ss, vt, state, mask):
    if mask is not None:
        ss = [jnp.where(mask, s, NEG) for s in ss]
    m_new = [jnp.maximum(state[hh][0], jnp.max(ss[hh], axis=0, keepdims=True)) for hh in range(HPG)]
    ps = [jnp.exp(ss[hh] - m_new[hh]).astype(BF16) for hh in range(HPG)]
    pv = [_dot(vt, ps[hh]) for hh in range(HPG)]
    return tuple((m_new[hh], jnp.exp(state[hh][0] - m_new[hh]) * state[hh][1] + pv[hh]) for hh in range(HPG))


def _attn_init(tq):
    return tuple((jnp.full((1, tq), NEG, F32), jnp.zeros((HPAD, tq), F32)) for _ in range(HPG))


def _attn_finish(state, gate_ref, o_ref, first_gate):
    gt = gate_ref[...].T
    sub = lax.broadcasted_iota(jnp.int32, gt.shape, 0)
    outs = []
    for hh in range(HPG):
        _, acc = state[hh]
        gate = jnp.sum(jnp.where(sub == first_gate + hh, gt, 0.0), axis=0, keepdims=True)
        outs.append(acc[:HEAD_DIM] / acc[HEAD_DIM:HEAD_DIM + 1] * gate)
    o_ref[...] = jnp.concatenate(outs, axis=0).T.astype(o_ref.dtype)


def _sel_attn_kernel(q_ref, selb_ref, k_ref, vt_ref, gate_ref, o_ref, *, tq, tk):
    g = pl.program_id(1)
    i = pl.program_id(2)
    nq = k_ref.shape[0] // tk
    selb = selb_ref[...]
    lane = lax.broadcasted_iota(jnp.int32, selb.shape, 1)
    qa = [jnp.where(lane >= HEAD_DIM, selb, q_ref[:, hh * HPAD:(hh + 1) * HPAD]) for hh in range(HPG)]
    causal = (lax.broadcasted_iota(jnp.int32, (tk, tq), 0) <= lax.broadcasted_iota(jnp.int32, (tk, tq), 1))

    def scores(j):
        return _attn_scores(k_ref[j * tk:(j + 1) * tk, :], qa)

    def run(n):
        ss = scores(0)
        state = _attn_init(tq)
        for j in range(n + 1):
            ss_next = scores(j + 1) if j < n else None
            state = _attn_update(ss, vt_ref[j], state, causal if j == n else None)
            ss = ss_next
        _attn_finish(state, gate_ref, o_ref, N_HEADS + g * HPG)

    for n in range(nq):
        pl.when(i == n)(functools.partial(run, n))


def _sel_attn(q, selb, ks, vs, gates, *, batch, seq):
    t = q.shape[0]
    tq, tk = TQ, TK
    assert tq == tk
    nq = seq // tq
    rowblk = lambda w: pl.BlockSpec((tq, w), lambda b, g, i: (b * nq + i, g))
    seqblk = pl.BlockSpec((seq, HPAD), lambda b, g, i: (b, g))
    vtblk = pl.BlockSpec((seq // tk, HPAD, tk), lambda b, g, i: (b, g, 0))
    return pl.pallas_call(
        functools.partial(_sel_attn_kernel, tq=tq, tk=tk),
        grid=(batch, N_KV, nq),
        in_specs=[rowblk(HPG * HPAD), rowblk(LANES), seqblk, vtblk,
                  pl.BlockSpec((tq, LANES), lambda b, g, i: (b * nq + i, 0))],
        out_specs=rowblk(HPG * HEAD_DIM),
        out_shape=jax.ShapeDtypeStruct((t, ATTN_W), BF16),
        compiler_params=_cparams(("arbitrary", "arbitrary", "arbitrary")),
        name="sel_attn",
    )(q, selb, ks, vs, gates)


def _win_attn_kernel(q_ref, k_ref, vt_ref, gate_ref, o_ref, *, tq, tk):
    g = pl.program_id(1)
    i = pl.program_id(2)
    qa = [q_ref[:, hh * HPAD:(hh + 1) * HPAD] for hh in range(HPG)]
    key = lax.broadcasted_iota(jnp.int32, (tk, tq), 0)
    qry = lax.broadcasted_iota(jnp.int32, (tk, tq), 1)

    def scores(j):
        return _attn_scores(k_ref[pl.ds(pl.multiple_of(j * tk, tk), tk), :], qa)

    def run(n_prev):
        masks = (key <= qry, None, key > qry)
        ss = scores(i)
        state = _attn_init(tq)
        for d in range(n_prev + 1):
            ss_next = scores(i - d - 1) if d < n_prev else None
            state = _attn_update(ss, vt_ref[i - d], state, masks[d])
            ss = ss_next
        _attn_finish(state, gate_ref, o_ref, 2 * N_HEADS + g * HPG)

    pl.when(i == 0)(functools.partial(run, 0))
    pl.when(i == 1)(functools.partial(run, 1))
    pl.when(i >= 2)(functools.partial(run, 2))


def _win_attn(q, kw, vw, gates, *, batch, seq):
    t = q.shape[0]
    tq, tk = TQ, TK
    assert tq == tk and WINDOW == 2 * tk
    nq = seq // tq
    rowblk = lambda w: pl.BlockSpec((tq, w), lambda b, g, i: (b * nq + i, g))
    seqblk = pl.BlockSpec((seq, HPAD), lambda b, g, i: (b, g))
    vtblk = pl.BlockSpec((seq // tk, HPAD, tk), lambda b, g, i: (b, g, 0))
    return pl.pallas_call(
        functools.partial(_win_attn_kernel, tq=tq, tk=tk),
        grid=(batch, N_KV, nq),
        in_specs=[rowblk(HPG * HPAD), seqblk, vtblk,
                  pl.BlockSpec((tq, LANES), lambda b, g, i: (b * nq + i, 0))],
        out_specs=rowblk(HPG * HEAD_DIM),
        out_shape=jax.ShapeDtypeStruct((t, ATTN_W), BF16),
        compiler_params=_cparams(("arbitrary", "arbitrary", "arbitrary")),
        name="win_attn",
    )(q, kw, vw, gates)


def _out_proj_kernel(h_ref, oc_ref, os_ref, ow_ref, ocv_ref, w_ref, g_ref, b_ref, o_ref, *, alpha):
    oa = (oc_ref[...].astype(F32) + os_ref[...].astype(F32) + ow_ref[...].astype(F32)).astype(BF16)
    m = _dot(oa, w_ref[:ATTN_W, :]) + _dot(ocv_ref[...], w_ref[ATTN_W:, :])
    o_ref[...] = _ln(alpha * h_ref[...] + m, g_ref[...], b_ref[...])


def _out_proj(h, o_cmp, o_sel, o_win, o_conv, w_o, ln_g, ln_b, *, alpha):
    t = h.shape[0]
    tm = TM_PROJ
    row = lambda w: pl.BlockSpec((tm, w), lambda i: (i, 0))
    return pl.pallas_call(
        functools.partial(_out_proj_kernel, alpha=alpha),
        grid=(t // tm,),
        in_specs=[row(D_MODEL), row(ATTN_W), row(ATTN_W), row(ATTN_W), row(CONV_CH),
                  _resident((D_MODEL, D_MODEL)), _resident((1, D_MODEL)), _resident((1, D_MODEL))],
        out_specs=row(D_MODEL),
        out_shape=jax.ShapeDtypeStruct((t, D_MODEL), F32),
        compiler_params=_cparams(("arbitrary",)),
        name="out_proj",
    )(h, o_cmp, o_sel, o_win, o_conv, w_o, ln_g, ln_b)


def _ffn_kernel(h_ref, wg_ref, wu_ref, wd_ref, g_ref, b_ref, o_ref, *, alpha):
    h = h_ref[...]
    hb = h.astype(BF16)
    a = (jax.nn.silu(_dot(hb, wg_ref[...])) * _dot(hb, wu_ref[...])).astype(BF16)
    o_ref[...] = _ln(alpha * h + _dot(a, wd_ref[...]), g_ref[...], b_ref[...])


def _ffn(h, wg, wu, wd, ln_g, ln_b, *, alpha):
    t = h.shape[0]
    tm = TM_PROJ
    row = pl.BlockSpec((tm, D_MODEL), lambda i: (i, 0))
    return pl.pallas_call(
        functools.partial(_ffn_kernel, alpha=alpha),
        grid=(t // tm,),
        in_specs=[row, _resident((D_MODEL, D_FF)), _resident((D_MODEL, D_FF)), _resident((D_FF, D_MODEL)),
                  _resident((1, D_MODEL)), _resident((1, D_MODEL))],
        out_specs=row,
        out_shape=jax.ShapeDtypeStruct((t, D_MODEL), F32),
        compiler_params=_cparams(("arbitrary",)),
        name="ffn",
    )(h, wg, wu, wd, ln_g, ln_b)


TMX = 512
CH = 16
LROWS = 2 * TMX + N_EXPERTS * CH
TMR = 512


def _route_kernel(h_ref, wr_ref, ltri_ref, ustr_ref, meta_ref, metat_ref, cnt_ref, *, tm):
    h = h_ref[...]
    h_hi = h.astype(BF16)
    h_lo = (h - h_hi.astype(F32)).astype(BF16)
    logits = _dot(h_hi, wr_ref[0]) + _dot(h_lo, wr_ref[0]) + _dot(h_hi, wr_ref[1])
    lane = lax.broadcasted_iota(jnp.int32, logits.shape, 1)
    logits = jnp.where(lane < N_EXPERTS, logits, -jnp.inf)
    m1 = jnp.max(logits, axis=-1, keepdims=True)
    i1 = jnp.min(jnp.where(logits == m1, lane, LANES), axis=-1, keepdims=True)
    rest = jnp.where(lane == i1, -jnp.inf, logits)
    m2 = jnp.max(rest, axis=-1, keepdims=True)
    i2 = jnp.min(jnp.where(rest == m2, lane, LANES), axis=-1, keepdims=True)
    e2 = jnp.exp(m2 - m1)
    den = 1.0 + e2
    w1 = 1.0 / den
    w2 = e2 / den

    routed = (lane == i1) | (lane == i2)
    cnt = _dot(ltri_ref[...], routed.astype(BF16))
    n = cnt[tm - 1:tm, :]
    padded = jnp.floor((n + (CH - 1)) * (1.0 / CH)) * CH
    seg_off = _dot(jnp.broadcast_to(padded, (8, LANES)).astype(BF16), ustr_ref[...])[0:1]
    dest = seg_off + cnt - 1.0
    d1 = jnp.sum(jnp.where(lane == i1, dest, 0.0), axis=-1, keepdims=True)
    d2 = jnp.sum(jnp.where(lane == i2, dest, 0.0), axis=-1, keepdims=True)
    meta = jnp.where(lane == 0, d1, jnp.where(lane == 1, d2, jnp.where(lane == 2, w1, jnp.where(lane == 3, w2, 0.0))))
    meta_ref[...] = meta
    metat_ref[...] = meta.T[0:8, :]
    cnt_ref[0] = jnp.broadcast_to(n, (8, LANES))


def _moe_route(h, w_router):
    t = h.shape[0]
    tm = TMX
    nt = t // tm
    wr = jnp.pad(w_router, ((0, 0), (0, LANES - N_EXPERTS)))
    wr_hi = wr.astype(BF16)
    wr_lo = (wr - wr_hi.astype(F32)).astype(BF16)
    wr2 = jnp.stack([wr_hi, wr_lo])
    ltri = jnp.asarray(np.tril(np.ones((tm, tm), np.float32)), dtype=BF16)
    ustr = jnp.asarray(np.triu(np.ones((LANES, LANES), np.float32), 1), dtype=BF16)
    return pl.pallas_call(
        functools.partial(_route_kernel, tm=tm),
        grid=(nt,),
        in_specs=[pl.BlockSpec((tm, D_MODEL), lambda i: (i, 0)), _resident((2, D_MODEL, LANES)),
                  _resident((tm, tm)), _resident((LANES, LANES))],
        out_specs=[pl.BlockSpec((tm, LANES), lambda i: (i, 0)),
                   pl.BlockSpec((8, tm), lambda i: (0, i)),
                   pl.BlockSpec((1, 8, LANES), lambda i: (i, 0, 0))],
        out_shape=[jax.ShapeDtypeStruct((t, LANES), F32), jax.ShapeDtypeStruct((8, t), F32),
                   jax.ShapeDtypeStruct((nt, 8, LANES), F32)],
        compiler_params=_cparams(("arbitrary",)),
        name="moe_route",
    )(h, wr2, ltri, ustr)


def _moe_plan(cnt, n_row_tiles):
    n = cnt[:, 0, :N_EXPERTS].astype(jnp.int32)
    p = (n + CH - 1) // CH * CH
    tot = p.sum(0)
    tot_pad = (tot + TMR - 1) // TMR * TMR
    gend = jnp.cumsum(tot_pad)
    gstart = gend - tot_pad
    goff = gstart[None, :] + jnp.cumsum(p, 0) - p
    loff = jnp.cumsum(p, 1) - p
    n_used = gend[-1:] // TMR
    tile_start = jnp.arange(n_row_tiles, dtype=jnp.int32) * TMR
    texp = jnp.minimum(jnp.sum(tile_start[:, None] >= gend[None, :], axis=1), N_EXPERTS - 1).astype(jnp.int32)
    texp = jnp.where(jnp.arange(n_row_tiles) < n_used[0], texp, texp[jnp.maximum(n_used[0] - 1, 0)])
    flat = lambda a: a.reshape(-1).astype(jnp.int32)
    gap_start = jnp.concatenate([gstart + tot, gend[-1:]])
    gap_rows = jnp.concatenate([tot_pad - tot, n_row_tiles * TMR - gend[-1:]])
    return dict(loff=flat(loff), goff=flat(goff), nch=flat(p // CH), gap_start=flat(gap_start),
                gap_nch=flat(gap_rows // CH), texp=flat(texp), n_used=flat(n_used))


def _chunk_copies(src_ref, dst_ref, sem, src_off, dst_off, n, src_step=CH):
    def body(c, carry):
        s = pl.multiple_of(src_off + c * src_step, CH)
        d = pl.multiple_of(dst_off + c * CH, CH)
        pltpu.make_async_copy(src_ref.at[pl.ds(s, CH)], dst_ref.at[pl.ds(d, CH)], sem).start()
        return carry
    lax.fori_loop(0, n, body, 0)


def _wait_chunks(src_ref, dst_ref, sem, n):
    def body(c, carry):
        pltpu.make_async_copy(src_ref.at[pl.ds(0, CH)], dst_ref.at[pl.ds(0, CH)], sem).wait()
        return carry
    lax.fori_loop(0, n, body, 0)


def _dispatch_kernel(loff, goff, nch, gap_start, gap_nch, h_ref, metat_ref, xs_ref, xc_ref, z_ref, sem, *, tm):
    i = pl.program_id(0)
    d1 = metat_ref[0:1, :]
    d2 = metat_ref[1:2, :]
    r = lax.broadcasted_iota(jnp.int32, (LROWS, tm), 0).astype(F32)
    onehot = ((r == d1) | (r == d2)).astype(BF16)
    xc_ref[...] = _dot(onehot, h_ref[...].astype(BF16)).astype(BF16)
    total = 0
    for e in range(N_EXPERTS):
        n = nch[i * N_EXPERTS + e]
        _chunk_copies(xc_ref, xs_ref, sem, loff[i * N_EXPERTS + e], goff[i * N_EXPERTS + e], n)
        total = total + n

    @pl.when(i == pl.num_programs(0) - 1)
    def _():
        z_ref[...] = jnp.zeros_like(z_ref)
        gaps = 0
        for e in range(N_EXPERTS + 1):
            _chunk_copies(z_ref, xs_ref, sem, 0, gap_start[e], gap_nch[e], src_step=0)
            gaps = gaps + gap_nch[e]
        _wait_chunks(z_ref, xs_ref, sem, gaps)

    _wait_chunks(xc_ref, xs_ref, sem, total)


def _moe_dispatch(h, metat, plan, n_rows):
    t = h.shape[0]
    tm = TMX
    grid_spec = pltpu.PrefetchScalarGridSpec(
        num_scalar_prefetch=5,
        grid=(t // tm,),
        in_specs=[pl.BlockSpec((tm, D_MODEL), lambda i, *_: (i, 0)),
                  pl.BlockSpec((8, tm), lambda i, *_: (0, i))],
        out_specs=pl.BlockSpec(memory_space=pl.ANY),
        scratch_shapes=[pltpu.VMEM((LROWS, D_MODEL), BF16), pltpu.VMEM((CH, D_MODEL), BF16),
                        pltpu.SemaphoreType.DMA(())],
    )
    return pl.pallas_call(
        functools.partial(_dispatch_kernel, tm=tm),
        grid_spec=grid_spec,
        out_shape=jax.ShapeDtypeStruct((n_rows, D_MODEL), BF16),
        compiler_params=_cparams(("arbitrary",)),
        name="moe_dispatch",
    )(plan["loff"], plan["goff"], plan["nch"], plan["gap_start"], plan["gap_nch"], h, metat)


def _experts_kernel(texp, n_used, x_ref, wg_ref, wu_ref, wd_ref, y_ref):
    used = pl.program_id(0) < n_used[0]

    @pl.when(used)
    def _():
        x = x_ref[...]
        a = (jax.nn.silu(_dot(x, wg_ref[0])) * _dot(x, wu_ref[0])).astype(BF16)
        y_ref[...] = _dot(a, wd_ref[0]).astype(BF16)

    @pl.when(jnp.logical_not(used))
    def _():
        y_ref[...] = jnp.zeros_like(y_ref)


def _moe_experts(xs, wg, wu, wd, plan):
    n_rows = xs.shape[0]
    rows = lambda r, texp, n_used: (jnp.minimum(r, n_used[0] - 1), 0)
    wspec = lambda shape: pl.BlockSpec((1,) + shape, lambda r, texp, n_used: (texp[r], 0, 0))
    grid_spec = pltpu.PrefetchScalarGridSpec(
        num_scalar_prefetch=2,
        grid=(n_rows // TMR,),
        in_specs=[pl.BlockSpec((TMR, D_MODEL), rows), wspec((D_MODEL, D_FF_EXPERT)),
                  wspec((D_MODEL, D_FF_EXPERT)), wspec((D_FF_EXPERT, D_MODEL))],
        out_specs=pl.BlockSpec((TMR, D_MODEL), lambda r, texp, n_used: (r, 0)),
    )
    return pl.pallas_call(
        _experts_kernel,
        grid_spec=grid_spec,
        out_shape=jax.ShapeDtypeStruct((n_rows, D_MODEL), BF16),
        compiler_params=_cparams(("arbitrary",)),
        name="moe_experts",
    )(plan["texp"], plan["n_used"], xs, wg, wu, wd)


def _combine_kernel(loff, goff, nch, h_ref, meta_ref, ys_ref, g_ref, b_ref, o_ref, yb_ref, sem, *, tm, alpha):
    i = pl.program_id(0)

    @pl.when(i == 0)
    def _():
        yb_ref[...] = jnp.zeros_like(yb_ref)

    total = 0
    for e in range(N_EXPERTS):
        n = nch[i * N_EXPERTS + e]
        _chunk_copies(ys_ref, yb_ref, sem, goff[i * N_EXPERTS + e], loff[i * N_EXPERTS + e], n)
        total = total + n
    meta = meta_ref[...]
    r = lax.broadcasted_iota(jnp.int32, (tm, LROWS), 1).astype(F32)
    pick1 = (r == meta[:, 0:1]).astype(BF16)
    pick2 = (r == meta[:, 1:2]).astype(BF16)
    _wait_chunks(ys_ref, yb_ref, sem, total)
    yb = yb_ref[...]
    f = meta[:, 2:3] * _dot(pick1, yb) + meta[:, 3:4] * _dot(pick2, yb)
    o_ref[...] = _ln(alpha * h_ref[...] + f, g_ref[...], b_ref[...])


def _moe_combine(h, meta, ys, plan, ln_g, ln_b, *, alpha):
    t = h.shape[0]
    tm = TMX
    grid_spec = pltpu.PrefetchScalarGridSpec(
        num_scalar_prefetch=3,
        grid=(t // tm,),
        in_specs=[pl.BlockSpec((tm, D_MODEL), lambda i, *_: (i, 0)),
                  pl.BlockSpec((tm, LANES), lambda i, *_: (i, 0)),
                  pl.BlockSpec(memory_space=pl.ANY),
                  pl.BlockSpec((1, D_MODEL), lambda i, *_: (0, 0)),
                  pl.BlockSpec((1, D_MODEL), lambda i, *_: (0, 0))],
        out_specs=pl.BlockSpec((tm, D_MODEL), lambda i, *_: (i, 0)),
        scratch_shapes=[pltpu.VMEM((LROWS, D_MODEL), BF16), pltpu.SemaphoreType.DMA(())],
    )
    return pl.pallas_call(
        functools.partial(_combine_kernel, tm=tm, alpha=alpha),
        grid_spec=grid_spec,
        out_shape=jax.ShapeDtypeStruct((t, D_MODEL), F32),
        compiler_params=_cparams(("arbitrary",)),
        name="moe_combine",
    )(plan["loff"], plan["goff"], plan["nch"], h, meta, ys, ln_g, ln_b)


def _moe(h, w_router, wg, wu, wd, ln_g, ln_b, *, alpha):
    t = h.shape[0]
    nt = t // TMX
    max_rows = 2 * t + nt * N_EXPERTS * (CH - 1) + N_EXPERTS * (TMR - CH)
    n_row_tiles = -(-max_rows // TMR)
    meta, metat, cnt = _moe_route(h, w_router)
    plan = _moe_plan(cnt, n_row_tiles)
    xs = _moe_dispatch(h, metat, plan, n_row_tiles * TMR)
    ys = _moe_experts(xs, wg, wu, wd, plan)
    return _moe_combine(h, meta, ys, plan, ln_g, ln_b, alpha=alpha)


def kernel(x, ln_in_g, ln_in_b, w_in, cmp_pos, cmp_w1, cmp_b1, cmp_w2, cmp_b2, conv_w, w_o, ln1_g, ln1_b, ln2_g, ln2_b, ffn_wg, ffn_wu, ffn_wd, moe_router, moe_wg, moe_wu, moe_wd):
    batch, seq, d = x.shape
    depth = w_in.shape[0]
    assert d == D_MODEL and seq % TM_PROJ == 0 and seq // CMP_STRIDE == LANES
    alpha = (2 * depth) ** 0.25
    t = batch * seq
    vec = lambda v: v.reshape(1, D_MODEL)
    h = x.reshape(t, d)
    for l in range(depth):
        w_r, w_vt = _prep_w_in(w_in[l])
        cw = jnp.pad(conv_w[l], ((0, 8 - CONV_W), (0, 0)))
        outs = _in_proj(h, vec(ln_in_g), vec(ln_in_b), w_r, w_vt, cw, seq=seq, pre_ln=(l == 0))
        if l == 0:
            h, outs = outs[0], outs[1:]
        q, kvc, ks, vs, kw, vw, gates, o_conv = outs
        kvcmp = _compress(kvc, cmp_pos[l], cmp_w1[l], cmp_b1[l], cmp_w2[l], cmp_b2[l], batch=batch, seq=seq)
        o_cmp, selb = _cmp_sel(q, kvcmp, gates, batch=batch, seq=seq)
        o_sel = _sel_attn(q, selb, ks, vs, gates, batch=batch, seq=seq)
        o_win = _win_attn(q, kw, vw, gates, batch=batch, seq=seq)
        h = _out_proj(h, o_cmp, o_sel, o_win, o_conv, w_o[l].astype(BF16), vec(ln1_g[l]), vec(ln1_b[l]),
                      alpha=alpha)
        if l % 2 == 0:
            h = _ffn(h, ffn_wg[l // 2].astype(BF16), ffn_wu[l // 2].astype(BF16), ffn_wd[l // 2].astype(BF16),
                     vec(ln2_g[l]), vec(ln2_b[l]), alpha=alpha)
        else:
            h = _moe(h, moe_router[l // 2], moe_wg[l // 2].astype(BF16), moe_wu[l // 2].astype(BF16),
                     moe_wd[l // 2].astype(BF16), vec(ln2_g[l]), vec(ln2_b[l]), alpha=alpha)
    return h.reshape(batch, seq, d)
```

```python
import functools

import numpy as np
import jax
import jax.numpy as jnp
from jax import lax
from jax.experimental import pallas as pl
from jax.experimental.pallas import tpu as pltpu

F32 = jnp.float32
BF16 = jnp.bfloat16

D_MODEL = 1024
HEAD_DIM = 64
N_HEADS = 8
N_KV = 2
HPG = N_HEADS // N_KV
ATTN_W = N_HEADS * HEAD_DIM
KV_W = N_KV * HEAD_DIM
CONV_CH = D_MODEL - ATTN_W
CONV_W = 3
CMP_BLOCK = 32
CMP_STRIDE = 16
CMP_HIDDEN = 256
SEL_BLOCK = 64
SEL_TOPN = 16
WINDOW = 512
D_FF = 2816
N_EXPERTS = 8
D_FF_EXPERT = 1408
LN_EPS = 1e-5
NEG = -1e30
FORCE = 1e9

LANES = 128
HPAD = LANES
Q_SCALE = HEAD_DIM ** -0.5

C_Q = 0
C_KVC = C_Q + N_HEADS * HPAD
C_KS = C_KVC + 2 * KV_W
C_KW = C_KS + N_KV * HPAD
C_U = C_KW + N_KV * HPAD
C_B = C_U + CONV_CH
C_C = C_B + CONV_CH
C_G = C_C + CONV_CH
C_END = C_G + LANES

TM_PROJ = 512
TQ = 256
TK = 256
VMEM_LIMIT = 56 * 1024 * 1024


def _cparams(sem):
    return pltpu.CompilerParams(dimension_semantics=sem, vmem_limit_bytes=VMEM_LIMIT)


def _ln(x, g, b):
    mu = jnp.mean(x, -1, keepdims=True)
    xc = x - mu
    var = jnp.mean(xc * xc, -1, keepdims=True)
    return xc * lax.rsqrt(var + LN_EPS) * g + b


def _dot(a, b):
    return jnp.dot(a, b, preferred_element_type=F32)


def _dot_nt(a, b):
    return lax.dot_general(a, b, (((1,), (1,)), ((), ())), preferred_element_type=F32)


def _resident(shape):
    nd = len(shape)
    return pl.BlockSpec(shape, lambda *_: (0,) * nd, pipeline_mode=pl.Buffered(1))


def _in_proj_kernel(*refs, pre_ln, tiles_per_seq, tm):
    if pre_ln:
        (x_ref, g_ref, b_ref, w_ref, wvt_ref, cw_ref, h_ref, q_ref, kvc_ref, ks_ref, vs_ref,
         kw_ref, vw_ref, gate_ref, oc_ref, carry_ref) = refs
        h = _ln(x_ref[...], g_ref[...], b_ref[...])
        h_ref[...] = h
    else:
        (x_ref, w_ref, wvt_ref, cw_ref, q_ref, kvc_ref, ks_ref, vs_ref,
         kw_ref, vw_ref, gate_ref, oc_ref, carry_ref) = refs
        h = x_ref[...]
    hb = h.astype(BF16)

    def proj(lo, hi):
        return _dot(hb, w_ref[:, lo:hi])

    q_ref[...] = (proj(C_Q, C_KVC) * Q_SCALE).astype(BF16)
    kvc = proj(C_KVC, C_KS)
    for j in range(4):
        kvc_ref[j] = kvc[:, j * HEAD_DIM:(j + 1) * HEAD_DIM]

    seq_tile = pl.program_id(0) % tiles_per_seq
    pos = lax.broadcasted_iota(jnp.int32, (tm, N_KV * HPAD), 0) + seq_tile * tm
    lane = lax.broadcasted_iota(jnp.int32, (tm, N_KV * HPAD), 1) % HPAD
    onehot = jnp.where(pos // SEL_BLOCK == lane - HEAD_DIM, 1.0, 0.0)
    in_tag = (lane >= HEAD_DIM) & (lane < HEAD_DIM + 32)
    ks_ref[...] = jnp.where(in_tag, onehot, proj(C_KS, C_KW)).astype(BF16)
    kw_ref[...] = proj(C_KW, C_U).astype(BF16)

    vrow = lax.broadcasted_iota(jnp.int32, (N_KV * HPAD, tm), 0) % HPAD
    for n, ref in enumerate((vs_ref, vw_ref)):
        vt = _dot_nt(wvt_ref[n * N_KV * HPAD:(n + 1) * N_KV * HPAD, :], hb)
        vt = jnp.where(vrow == HEAD_DIM, 1.0, vt).astype(BF16)
        for c in range(tm // TK):
            ref[c] = vt[:, c * TK:(c + 1) * TK]

    cu = proj(C_C, C_G) * proj(C_U, C_B)

    @pl.when(seq_tile == 0)
    def _():
        carry_ref[...] = jnp.zeros_like(carry_ref)

    prev = carry_ref[...]
    row = lax.broadcasted_iota(jnp.int32, (tm, CONV_CH), 0)
    s1 = jnp.where(row == 0, prev[7:8], pltpu.roll(cu, 1, 0))
    s2 = jnp.where(row == 0, prev[6:7], jnp.where(row == 1, prev[7:8], pltpu.roll(cu, 2, 0)))
    y = s2 * cw_ref[0:1, :] + s1 * cw_ref[1:2, :] + cu * cw_ref[2:3, :]
    oc_ref[...] = (proj(C_B, C_C) * y).astype(BF16)
    carry_ref[...] = cu[tm - 8:tm]
    gate_ref[...] = jax.nn.sigmoid(proj(C_G, C_END))


def _in_proj(x, ln_g, ln_b, w_r, w_vt, conv_w, *, seq, pre_ln):
    t = x.shape[0]
    tm = TM_PROJ
    nt = t // tm
    row = lambda w: pl.BlockSpec((tm, w), lambda i: (i, 0))
    in_specs = [row(D_MODEL)]
    args = [x]
    if pre_ln:
        in_specs += [_resident((1, D_MODEL)), _resident((1, D_MODEL))]
        args += [ln_g, ln_b]
    in_specs += [_resident((D_MODEL, C_END)), _resident((2 * N_KV * HPAD, D_MODEL)), _resident((8, CONV_CH))]
    args += [w_r, w_vt, conv_w]
    out_shape, out_specs = [], []
    if pre_ln:
        out_shape.append(jax.ShapeDtypeStruct((t, D_MODEL), F32))
        out_specs.append(row(D_MODEL))
    out_shape += [
        jax.ShapeDtypeStruct((t, N_HEADS * HPAD), BF16),
        jax.ShapeDtypeStruct((4, t, HEAD_DIM), F32),
        jax.ShapeDtypeStruct((t, N_KV * HPAD), BF16),
        jax.ShapeDtypeStruct((t // TK, N_KV * HPAD, TK), BF16),
        jax.ShapeDtypeStruct((t, N_KV * HPAD), BF16),
        jax.ShapeDtypeStruct((t // TK, N_KV * HPAD, TK), BF16),
        jax.ShapeDtypeStruct((t, LANES), F32),
        jax.ShapeDtypeStruct((t, CONV_CH), BF16),
    ]
    vt_spec = pl.BlockSpec((tm // TK, N_KV * HPAD, TK), lambda i: (i, 0, 0))
    out_specs += [
        row(N_HEADS * HPAD),
        pl.BlockSpec((4, tm, HEAD_DIM), lambda i: (0, i, 0)),
        row(N_KV * HPAD), vt_spec, row(N_KV * HPAD), vt_spec,
        row(LANES), row(CONV_CH),
    ]
    return pl.pallas_call(
        functools.partial(_in_proj_kernel, pre_ln=pre_ln, tiles_per_seq=seq // tm, tm=tm),
        grid=(nt,),
        in_specs=in_specs,
        out_specs=out_specs,
        out_shape=out_shape,
        scratch_shapes=[pltpu.VMEM((8, CONV_CH), F32)],
        compiler_params=_cparams(("arbitrary",)),
        name="in_proj_ln" if pre_ln else "in_proj",
    )(*args)


def _prep_w_in(w_in):
    d = w_in.shape[0]
    o = 0

    def take(n):
        nonlocal o
        s = w_in[:, o:o + n]
        o += n
        return s

    def pad_heads(s, n):
        s = s.reshape(d, n, HEAD_DIM)
        return jnp.pad(s, ((0, 0), (0, 0), (0, HPAD - HEAD_DIM))).reshape(d, n * HPAD)

    q = pad_heads(take(ATTN_W), N_HEADS)
    kc, vc = take(KV_W), take(KV_W)
    ks, vs, kw, vw = (pad_heads(take(KV_W), N_KV) for _ in range(4))
    gates = jnp.pad(take(3 * N_HEADS), ((0, 0), (0, LANES - 3 * N_HEADS)))
    u, bg, cg = take(CONV_CH), take(CONV_CH), take(CONV_CH)
    w_r = jnp.concatenate([q, kc, vc, ks, kw, u, bg, cg, gates], axis=1).astype(BF16)
    return w_r, jnp.concatenate([vs, vw], axis=1).T.astype(BF16)


def _compress_kernel(x_ref, pos_ref, w1_ref, b1_ref, w2_ref, b2_ref, w2t_ref, b2t_ref, o_ref, ot_ref):
    half = CMP_STRIDE * HEAD_DIM
    x = x_ref[0, 0]
    pos = pos_ref[0]
    xa = (x + pos[:, :half]).astype(BF16)
    xb = (x + pos[:, half:]).astype(BF16)
    a = _dot(xa, w1_ref[0, :half, :])
    b = _dot(xb, w1_ref[0, half:, :])
    n = x.shape[0]
    hid = a + pltpu.roll(b, n - 1, 0) + b1_ref[0]
    act = jax.nn.gelu(hid).astype(BF16)
    o_ref[0, 0] = (_dot(act, w2_ref[0]) + b2_ref[0]).astype(BF16)
    ot_ref[0, 0] = (_dot_nt(w2t_ref[0], act) + b2t_ref[0]).astype(BF16)


def _compress(kvc, cmp_pos, cmp_w1, cmp_b1, cmp_w2, cmp_b2, *, batch, seq):
    nb = seq // CMP_STRIDE
    half = CMP_STRIDE * HEAD_DIM
    x = kvc.reshape(4, batch, nb, half)
    pos = cmp_pos.reshape(2, 1, CMP_BLOCK * HEAD_DIM)
    w1 = cmp_w1.astype(BF16)
    b1 = cmp_b1.reshape(2, 1, CMP_HIDDEN)
    w2 = jnp.pad(cmp_w2, ((0, 0), (0, 0), (0, HPAD - HEAD_DIM))).astype(BF16)
    b2 = jnp.pad(cmp_b2, ((0, 0), (0, HPAD - HEAD_DIM))).reshape(2, 1, HPAD)
    w2t = jnp.swapaxes(w2, 1, 2)
    b2t = b2.reshape(2, HPAD, 1)
    kind = lambda j, b: (j // N_KV, 0, 0)
    return pl.pallas_call(
        _compress_kernel,
        grid=(4, batch),
        in_specs=[
            pl.BlockSpec((1, 1, nb, half), lambda j, b: (j, b, 0, 0)),
            pl.BlockSpec((1, 1, 2 * half), kind),
            pl.BlockSpec((1, 2 * half, CMP_HIDDEN), kind),
            pl.BlockSpec((1, 1, CMP_HIDDEN), kind),
            pl.BlockSpec((1, CMP_HIDDEN, HPAD), kind),
            pl.BlockSpec((1, 1, HPAD), kind),
            pl.BlockSpec((1, HPAD, CMP_HIDDEN), kind),
            pl.BlockSpec((1, HPAD, 1), kind),
        ],
        out_specs=[pl.BlockSpec((1, 1, nb, HPAD), lambda j, b: (j, b, 0, 0)),
                   pl.BlockSpec((1, 1, HPAD, nb), lambda j, b: (j, b, 0, 0))],
        out_shape=[jax.ShapeDtypeStruct((4, batch, nb, HPAD), BF16),
                   jax.ShapeDtypeStruct((4, batch, HPAD, nb), BF16)],
        compiler_params=_cparams(("arbitrary", "arbitrary")),
        name="compress",
    )(x, pos, w1, b1, w2, b2, w2t, b2t)


def _store_heads_t(heads_t, gate_ref, o_ref, first_gate):
    gt = gate_ref[...].T
    sub = lax.broadcasted_iota(jnp.int32, gt.shape, 0)
    outs = []
    for hh in range(HPG):
        gate = jnp.sum(jnp.where(sub == first_gate + hh, gt, 0.0), axis=0, keepdims=True)
        outs.append(heads_t[hh] * gate)
    o_ref[...] = jnp.concatenate(outs, axis=0).T.astype(o_ref.dtype)


def _overlap_t():
    n = np.arange(LANES)
    j = np.arange(32)
    cs = n * CMP_STRIDE
    ss = j * SEL_BLOCK
    ov = (cs[None, :] < ss[:, None] + SEL_BLOCK) & (cs[None, :] + CMP_BLOCK > ss[:, None])
    return jnp.asarray(ov, dtype=BF16)


def _cmp_sel_kernel(q_ref, kc_ref, vct_ref, gate_ref, ov_ref, o_ref, selb_ref, *, tq, n_sel):
    g = pl.program_id(1)
    i = pl.program_id(2)
    kc = kc_ref[0, 0]
    vct = vct_ref[0, 0]
    n = lax.broadcasted_iota(jnp.int32, (LANES, tq), 0)
    t = lax.broadcasted_iota(jnp.int32, (LANES, tq), 1) + i * tq
    mask = n * CMP_STRIDE + (CMP_BLOCK - 1) <= t
    maskf = mask.astype(F32)
    ss = [jnp.where(mask, _dot_nt(kc, q_ref[:, hh * HPAD:(hh + 1) * HPAD]), NEG) for hh in range(HPG)]
    es = [jnp.exp(s - jnp.max(s, axis=0, keepdims=True)) for s in ss]
    ps = [e / jnp.sum(e, axis=0, keepdims=True) * maskf for e in es]
    heads_t = [_dot(vct, p.astype(BF16))[:HEAD_DIM] for p in ps]
    _store_heads_t(heads_t, gate_ref, o_ref, g * HPG)
    psum = functools.reduce(lambda a, b: a + b, ps)

    ov = ov_ref[...]
    p_hi = psum.astype(BF16)
    r1 = psum - p_hi.astype(F32)
    p_mid = r1.astype(BF16)
    p_lo = (r1 - p_mid.astype(F32)).astype(BF16)
    imp = _dot(ov, p_hi) + _dot(ov, p_mid) + _dot(ov, p_lo)

    jt = lax.broadcasted_iota(jnp.int32, (n_sel, tq), 0)
    tt = lax.broadcasted_iota(jnp.int32, (n_sel, tq), 1) + i * tq
    cur = tt // SEL_BLOCK
    valid = jt * SEL_BLOCK <= tt
    forced = (jt == 0) | (jt == cur) | (jt == cur - 1)
    score = jnp.where(forced, FORCE, jnp.where(valid, imp, NEG))
    rank = jnp.zeros((n_sel, tq), jnp.int32)
    for k in range(n_sel):
        sk = score[k:k + 1, :]
        ahead = (sk > score) | ((sk == score) & (jt > k))
        rank = rank + ahead.astype(jnp.int32)
    bias = jnp.where(rank < SEL_TOPN, 0.0, NEG)
    full = jnp.concatenate(
        [jnp.zeros((HEAD_DIM, tq), F32), bias, jnp.zeros((HPAD - HEAD_DIM - n_sel, tq), F32)], axis=0)
    selb_ref[...] = full.T.astype(BF16)


def _cmp_sel(q, kvcmp, kvcmp_t, gates, *, batch, seq):
    t = q.shape[0]
    tq = TQ
    nq = seq // tq
    n_sel = seq // SEL_BLOCK
    nb = kvcmp.shape[2]
    assert nb == LANES and n_sel == 32
    rowblk = lambda w: pl.BlockSpec((tq, w), lambda b, g, i: (b * nq + i, g))
    return pl.pallas_call(
        functools.partial(_cmp_sel_kernel, tq=tq, n_sel=n_sel),
        grid=(batch, N_KV, nq),
        in_specs=[
            rowblk(HPG * HPAD),
            pl.BlockSpec((1, 1, nb, HPAD), lambda b, g, i: (g, b, 0, 0)),
            pl.BlockSpec((1, 1, HPAD, nb), lambda b, g, i: (N_KV + g, b, 0, 0)),
            pl.BlockSpec((tq, LANES), lambda b, g, i: (b * nq + i, 0)),
            pl.BlockSpec((n_sel, LANES), lambda b, g, i: (0, 0)),
        ],
        out_specs=[rowblk(HPG * HEAD_DIM), rowblk(LANES)],
        out_shape=[jax.ShapeDtypeStruct((t, ATTN_W), BF16),
                   jax.ShapeDtypeStruct((t, N_KV * LANES), BF16)],
        compiler_params=_cparams(("arbitrary", "arbitrary", "arbitrary")),
        name="cmp_sel",
    )(q, kvcmp, kvcmp_t, gates, _overlap_t())


def _attn_scores(k, qa):
    return tuple(_dot_nt(k, qa[hh]) for hh in range(HPG))


def _attn_update(ss, vt, state, mask):
    if mask is not None:
        ss = [jnp.where(mask, s, NEG) for s in ss]
    m_new = [jnp.maximum(state[hh][0], jnp.max(ss[hh], axis=0, keepdims=True)) for hh in range(HPG)]
    ps = [jnp.exp(ss[hh] - m_new[hh]).astype(BF16) for hh in range(HPG)]
    pv = [_dot(vt, ps[hh]) for hh in range(HPG)]
    return tuple((m_new[hh], jnp.exp(state[hh][0] - m_new[hh]) * state[hh][1] + pv[hh]) for hh in range(HPG))


def _attn_init(tq):
    return tuple((jnp.full((1, tq), NEG, F32), jnp.zeros((HPAD, tq), F32)) for _ in range(HPG))


def _attn_finish(state, gate_ref, o_ref, first_gate):
    heads_t = [acc[:HEAD_DIM] / acc[HEAD_DIM:HEAD_DIM + 1] for _, acc in state]
    _store_heads_t(heads_t, gate_ref, o_ref, first_gate)


def _sel_attn_kernel(q_ref, selb_ref, k_ref, vt_ref, gate_ref, o_ref, *, tq, tk):
    g = pl.program_id(1)
    i = pl.program_id(2)
    nq = k_ref.shape[0] // tk
    selb = selb_ref[...]
    lane = lax.broadcasted_iota(jnp.int32, selb.shape, 1)
    qa = [jnp.where(lane >= HEAD_DIM, selb, q_ref[:, hh * HPAD:(hh + 1) * HPAD]) for hh in range(HPG)]
    causal = (lax.broadcasted_iota(jnp.int32, (tk, tq), 0) <= lax.broadcasted_iota(jnp.int32, (tk, tq), 1))

    def scores(j):
        return _attn_scores(k_ref[j * tk:(j + 1) * tk, :], qa)

    def run(n):
        ss = scores(0)
        state = _attn_init(tq)
        for j in range(n + 1):
            ss_next = scores(j + 1) if j < n else None
            state = _attn_update(ss, vt_ref[j], state, causal if j == n else None)
            ss = ss_next
        _attn_finish(state, gate_ref, o_ref, N_HEADS + g * HPG)

    for n in range(nq):
        pl.when(i == n)(functools.partial(run, n))


def _sel_attn(q, selb, ks, vs, gates, *, batch, seq):
    t = q.shape[0]
    tq, tk = TQ, TK
    assert tq == tk
    nq = seq // tq
    rowblk = lambda w: pl.BlockSpec((tq, w), lambda b, g, i: (b * nq + i, g))
    seqblk = pl.BlockSpec((seq, HPAD), lambda b, g, i: (b, g))
    vtblk = pl.BlockSpec((seq // tk, HPAD, tk), lambda b, g, i: (b, g, 0))
    return pl.pallas_call(
        functools.partial(_sel_attn_kernel, tq=tq, tk=tk),
        grid=(batch, N_KV, nq),
        in_specs=[rowblk(HPG * HPAD), rowblk(LANES), seqblk, vtblk,
                  pl.BlockSpec((tq, LANES), lambda b, g, i: (b * nq + i, 0))],
        out_specs=rowblk(HPG * HEAD_DIM),
        out_shape=jax.ShapeDtypeStruct((t, ATTN_W), BF16),
        compiler_params=_cparams(("arbitrary", "arbitrary", "arbitrary")),
        name="sel_attn",
    )(q, selb, ks, vs, gates)


def _win_attn_kernel(q_ref, k_ref, vt_ref, gate_ref, o_ref, *, tq, tk):
    g = pl.program_id(1)
    i = pl.program_id(2)
    qa = [q_ref[:, hh * HPAD:(hh + 1) * HPAD] for hh in range(HPG)]
    key = lax.broadcasted_iota(jnp.int32, (tk, tq), 0)
    qry = lax.broadcasted_iota(jnp.int32, (tk, tq), 1)

    def scores(j):
        return _attn_scores(k_ref[pl.ds(pl.multiple_of(j * tk, tk), tk), :], qa)

    def run(n_prev):
        masks = (key <= qry, None, key > qry)
        ss = scores(i)
        state = _attn_init(tq)
        for d in range(n_prev + 1):
            ss_next = scores(i - d - 1) if d < n_prev else None
            state = _attn_update(ss, vt_ref[i - d], state, masks[d])
            ss = ss_next
        _attn_finish(state, gate_ref, o_ref, 2 * N_HEADS + g * HPG)

    pl.when(i == 0)(functools.partial(run, 0))
    pl.when(i == 1)(functools.partial(run, 1))
    pl.when(i >= 2)(functools.partial(run, 2))


def _win_attn(q, kw, vw, gates, *, batch, seq):
    t = q.shape[0]
    tq, tk = TQ, TK
    assert tq == tk and WINDOW == 2 * tk
    nq = seq // tq
    rowblk = lambda w: pl.BlockSpec((tq, w), lambda b, g, i: (b * nq + i, g))
    seqblk = pl.BlockSpec((seq, HPAD), lambda b, g, i: (b, g))
    vtblk = pl.BlockSpec((seq // tk, HPAD, tk), lambda b, g, i: (b, g, 0))
    return pl.pallas_call(
        functools.partial(_win_attn_kernel, tq=tq, tk=tk),
        grid=(batch, N_KV, nq),
        in_specs=[rowblk(HPG * HPAD), seqblk, vtblk,
                  pl.BlockSpec((tq, LANES), lambda b, g, i: (b * nq + i, 0))],
        out_specs=rowblk(HPG * HEAD_DIM),
        out_shape=jax.ShapeDtypeStruct((t, ATTN_W), BF16),
        compiler_params=_cparams(("arbitrary", "arbitrary", "arbitrary")),
        name="win_attn",
    )(q, kw, vw, gates)


def _out_proj_kernel(h_ref, oc_ref, os_ref, ow_ref, ocv_ref, w_ref, g_ref, b_ref, o_ref, *, alpha):
    oa = (oc_ref[...].astype(F32) + os_ref[...].astype(F32) + ow_ref[...].astype(F32)).astype(BF16)
    m = _dot(oa, w_ref[:ATTN_W, :]) + _dot(ocv_ref[...], w_ref[ATTN_W:, :])
    o_ref[...] = _ln(alpha * h_ref[...] + m, g_ref[...], b_ref[...])


def _out_proj(h, o_cmp, o_sel, o_win, o_conv, w_o, ln_g, ln_b, *, alpha):
    t = h.shape[0]
    tm = TM_PROJ
    row = lambda w: pl.BlockSpec((tm, w), lambda i: (i, 0))
    return pl.pallas_call(
        functools.partial(_out_proj_kernel, alpha=alpha),
        grid=(t // tm,),
        in_specs=[row(D_MODEL), row(ATTN_W), row(ATTN_W), row(ATTN_W), row(CONV_CH),
                  _resident((D_MODEL, D_MODEL)), _resident((1, D_MODEL)), _resident((1, D_MODEL))],
        out_specs=row(D_MODEL),
        out_shape=jax.ShapeDtypeStruct((t, D_MODEL), F32),
        compiler_params=_cparams(("arbitrary",)),
        name="out_proj",
    )(h, o_cmp, o_sel, o_win, o_conv, w_o, ln_g, ln_b)


def _ffn_kernel(h_ref, wg_ref, wu_ref, wd_ref, g_ref, b_ref, o_ref, *, alpha):
    h = h_ref[...]
    hb = h.astype(BF16)
    a = (jax.nn.silu(_dot(hb, wg_ref[...])) * _dot(hb, wu_ref[...])).astype(BF16)
    o_ref[...] = _ln(alpha * h + _dot(a, wd_ref[...]), g_ref[...], b_ref[...])


def _ffn(h, wg, wu, wd, ln_g, ln_b, *, alpha):
    t = h.shape[0]
    tm = TM_PROJ
    row = pl.BlockSpec((tm, D_MODEL), lambda i: (i, 0))
    return pl.pallas_call(
        functools.partial(_ffn_kernel, alpha=alpha),
        grid=(t // tm,),
        in_specs=[row, _resident((D_MODEL, D_FF)), _resident((D_MODEL, D_FF)), _resident((D_FF, D_MODEL)),
                  _resident((1, D_MODEL)), _resident((1, D_MODEL))],
        out_specs=row,
        out_shape=jax.ShapeDtypeStruct((t, D_MODEL), F32),
        compiler_params=_cparams(("arbitrary",)),
        name="ffn",
    )(h, wg, wu, wd, ln_g, ln_b)


TMX = 512
CH = 16
LROWS = 2 * TMX + N_EXPERTS * CH
TMR = 512
XROW = D_MODEL + LANES


def _route_kernel(h_ref, wr_ref, ltri_ref, ustr_ref, meta_ref, metat_ref, cnt_ref, wp_ref, *, tm):
    h = h_ref[...]
    h_hi = h.astype(BF16)
    h_lo = (h - h_hi.astype(F32)).astype(BF16)
    logits = _dot(h_hi, wr_ref[0]) + _dot(h_lo, wr_ref[0]) + _dot(h_hi, wr_ref[1])
    lane = lax.broadcasted_iota(jnp.int32, logits.shape, 1)
    logits = jnp.where(lane < N_EXPERTS, logits, -jnp.inf)
    m1 = jnp.max(logits, axis=-1, keepdims=True)
    i1 = jnp.min(jnp.where(logits == m1, lane, LANES), axis=-1, keepdims=True)
    rest = jnp.where(lane == i1, -jnp.inf, logits)
    m2 = jnp.max(rest, axis=-1, keepdims=True)
    i2 = jnp.min(jnp.where(rest == m2, lane, LANES), axis=-1, keepdims=True)
    e2 = jnp.exp(m2 - m1)
    den = 1.0 + e2
    w1 = 1.0 / den
    w2 = e2 / den

    routed = (lane == i1) | (lane == i2)
    cnt = _dot(ltri_ref[...], routed.astype(BF16))
    n = cnt[tm - 1:tm, :]
    padded = jnp.floor((n + (CH - 1)) * (1.0 / CH)) * CH
    seg_off = _dot(jnp.broadcast_to(padded, (8, LANES)).astype(BF16), ustr_ref[...])[0:1]
    dest = seg_off + cnt - 1.0
    d1 = jnp.sum(jnp.where(lane == i1, dest, 0.0), axis=-1, keepdims=True)
    d2 = jnp.sum(jnp.where(lane == i2, dest, 0.0), axis=-1, keepdims=True)
    meta = jnp.where(lane == 0, d1, jnp.where(lane == 1, d2, jnp.where(lane == 2, w1, jnp.where(lane == 3, w2, 0.0))))
    meta_ref[...] = meta
    metat_ref[...] = meta.T[0:8, :]
    cnt_ref[0] = jnp.broadcast_to(n, (8, LANES))
    for k, w in enumerate((w1, w2)):
        hi = w.astype(BF16).astype(F32)
        mid = (w - hi).astype(BF16).astype(F32)
        lo = (w - hi - mid).astype(BF16).astype(F32)
        wp_ref[k] = jnp.where(lane == 0, hi, jnp.where(lane == 1, mid, jnp.where(lane == 2, lo, 0.0))).astype(BF16)


def _moe_route(h, w_router):
    t = h.shape[0]
    tm = TMX
    nt = t // tm
    wr = jnp.pad(w_router, ((0, 0), (0, LANES - N_EXPERTS)))
    wr_hi = wr.astype(BF16)
    wr_lo = (wr - wr_hi.astype(F32)).astype(BF16)
    wr2 = jnp.stack([wr_hi, wr_lo])
    ltri = jnp.asarray(np.tril(np.ones((tm, tm), np.float32)), dtype=BF16)
    ustr = jnp.asarray(np.triu(np.ones((LANES, LANES), np.float32), 1), dtype=BF16)
    return pl.pallas_call(
        functools.partial(_route_kernel, tm=tm),
        grid=(nt,),
        in_specs=[pl.BlockSpec((tm, D_MODEL), lambda i: (i, 0)), _resident((2, D_MODEL, LANES)),
                  _resident((tm, tm)), _resident((LANES, LANES))],
        out_specs=[pl.BlockSpec((tm, LANES), lambda i: (i, 0)),
                   pl.BlockSpec((8, tm), lambda i: (0, i)),
                   pl.BlockSpec((1, 8, LANES), lambda i: (i, 0, 0)),
                   pl.BlockSpec((2, tm, LANES), lambda i: (0, i, 0))],
        out_shape=[jax.ShapeDtypeStruct((t, LANES), F32), jax.ShapeDtypeStruct((8, t), F32),
                   jax.ShapeDtypeStruct((nt, 8, LANES), F32), jax.ShapeDtypeStruct((2, t, LANES), BF16)],
        compiler_params=_cparams(("arbitrary",)),
        name="moe_route",
    )(h, wr2, ltri, ustr)


def _moe_plan(cnt, n_row_tiles):
    n = cnt[:, 0, :N_EXPERTS].astype(jnp.int32)
    p = (n + CH - 1) // CH * CH
    tot = p.sum(0)
    tot_pad = (tot + TMR - 1) // TMR * TMR
    gend = jnp.cumsum(tot_pad)
    gstart = gend - tot_pad
    goff = gstart[None, :] + jnp.cumsum(p, 0) - p
    loff = jnp.cumsum(p, 1) - p
    n_used = gend[-1:] // TMR
    tile_start = jnp.arange(n_row_tiles, dtype=jnp.int32) * TMR
    texp = jnp.minimum(jnp.sum(tile_start[:, None] >= gend[None, :], axis=1), N_EXPERTS - 1).astype(jnp.int32)
    texp = jnp.where(jnp.arange(n_row_tiles) < n_used[0], texp, texp[jnp.maximum(n_used[0] - 1, 0)])
    flat = lambda a: a.reshape(-1).astype(jnp.int32)
    gap_start = jnp.concatenate([gstart + tot, gend[-1:]])
    gap_rows = jnp.concatenate([tot_pad - tot, n_row_tiles * TMR - gend[-1:]])
    return dict(loff=flat(loff), goff=flat(goff), nch=flat(p // CH), gap_start=flat(gap_start),
                gap_nch=flat(gap_rows // CH), texp=flat(texp), n_used=flat(n_used))


def _chunk_copies(src_ref, dst_ref, sem, src_off, dst_off, n, src_step=CH):
    def body(c, carry):
        s = pl.multiple_of(src_off + c * src_step, CH)
        d = pl.multiple_of(dst_off + c * CH, CH)
        pltpu.make_async_copy(src_ref.at[pl.ds(s, CH)], dst_ref.at[pl.ds(d, CH)], sem).start()
        return carry
    lax.fori_loop(0, n, body, 0)


def _wait_chunks(src_ref, dst_ref, sem, n):
    def body(c, carry):
        pltpu.make_async_copy(src_ref.at[pl.ds(0, CH)], dst_ref.at[pl.ds(0, CH)], sem).wait()
        return carry
    lax.fori_loop(0, n, body, 0)


def _tile_chunks(tile, nch):
    total = 0
    for e in range(N_EXPERTS):
        total = total + nch[tile * N_EXPERTS + e]
    return total


def _start_segments(tile, nch, src_off, dst_off, src_ref, dst_ref, sem):
    for e in range(N_EXPERTS):
        idx = tile * N_EXPERTS + e
        _chunk_copies(src_ref, dst_ref, sem, src_off[idx], dst_off[idx], nch[idx])


def _dispatch_kernel(loff, goff, nch, gap_start, gap_nch, h_ref, metat_ref, wp_ref, xs_ref, xc_ref, z_ref, sem,
                     *, tm):
    i = pl.program_id(0)
    last = pl.num_programs(0) - 1
    slot = i % 2
    r = lax.broadcasted_iota(jnp.int32, (LROWS, tm), 0).astype(F32)
    pick1 = (r == metat_ref[0:1, :]).astype(BF16)
    pick2 = (r == metat_ref[1:2, :]).astype(BF16)
    xc_ref[slot, :, :D_MODEL] = _dot(pick1 + pick2, h_ref[...].astype(BF16)).astype(BF16)
    xc_ref[slot, :, D_MODEL:] = (_dot(pick1, wp_ref[0]) + _dot(pick2, wp_ref[1])).astype(BF16)
    _start_segments(i, nch, loff, goff, xc_ref.at[slot], xs_ref, sem.at[slot])

    @pl.when(i > 0)
    def _():
        _wait_chunks(xc_ref.at[1 - slot], xs_ref, sem.at[1 - slot], _tile_chunks(i - 1, nch))

    @pl.when(i == last)
    def _():
        z_ref[...] = jnp.zeros_like(z_ref)
        gaps = 0
        for e in range(N_EXPERTS + 1):
            _chunk_copies(z_ref, xs_ref, sem.at[2], 0, gap_start[e], gap_nch[e], src_step=0)
            gaps = gaps + gap_nch[e]
        _wait_chunks(z_ref, xs_ref, sem.at[2], gaps)
        _wait_chunks(xc_ref.at[slot], xs_ref, sem.at[slot], _tile_chunks(i, nch))


def _moe_dispatch(h, metat, wparts, plan, n_rows):
    t = h.shape[0]
    tm = TMX
    grid_spec = pltpu.PrefetchScalarGridSpec(
        num_scalar_prefetch=5,
        grid=(t // tm,),
        in_specs=[pl.BlockSpec((tm, D_MODEL), lambda i, *_: (i, 0)),
                  pl.BlockSpec((8, tm), lambda i, *_: (0, i)),
                  pl.BlockSpec((2, tm, LANES), lambda i, *_: (0, i, 0))],
        out_specs=pl.BlockSpec(memory_space=pl.ANY),
        scratch_shapes=[pltpu.VMEM((2, LROWS, XROW), BF16), pltpu.VMEM((CH, XROW), BF16),
                        pltpu.SemaphoreType.DMA((3,))],
    )
    return pl.pallas_call(
        functools.partial(_dispatch_kernel, tm=tm),
        grid_spec=grid_spec,
        out_shape=jax.ShapeDtypeStruct((n_rows, XROW), BF16),
        compiler_params=_cparams(("arbitrary",)),
        name="moe_dispatch",
    )(plan["loff"], plan["goff"], plan["nch"], plan["gap_start"], plan["gap_nch"], h, metat, wparts)


def _experts_kernel(texp, n_used, x_ref, wg_ref, wu_ref, wd_ref, y_ref):
    used = pl.program_id(0) < n_used[0]

    @pl.when(used)
    def _():
        x = x_ref[:, :D_MODEL]
        wp = x_ref[:, D_MODEL:].astype(F32)
        gate = wp[:, 0:1] + wp[:, 1:2] + wp[:, 2:3]
        a = (jax.nn.silu(_dot(x, wg_ref[0])) * _dot(x, wu_ref[0])).astype(BF16)
        y_ref[...] = (gate * _dot(a, wd_ref[0])).astype(BF16)

    @pl.when(jnp.logical_not(used))
    def _():
        y_ref[...] = jnp.zeros_like(y_ref)


def _moe_experts(xs, wg, wu, wd, plan):
    n_rows = xs.shape[0]
    rows = lambda r, texp, n_used: (jnp.minimum(r, n_used[0] - 1), 0)
    wspec = lambda shape: pl.BlockSpec((1,) + shape, lambda r, texp, n_used: (texp[r], 0, 0))
    grid_spec = pltpu.PrefetchScalarGridSpec(
        num_scalar_prefetch=2,
        grid=(n_rows // TMR,),
        in_specs=[pl.BlockSpec((TMR, XROW), rows), wspec((D_MODEL, D_FF_EXPERT)),
                  wspec((D_MODEL, D_FF_EXPERT)), wspec((D_FF_EXPERT, D_MODEL))],
        out_specs=pl.BlockSpec((TMR, D_MODEL), lambda r, texp, n_used: (r, 0)),
    )
    return pl.pallas_call(
        _experts_kernel,
        grid_spec=grid_spec,
        out_shape=jax.ShapeDtypeStruct((n_rows, D_MODEL), BF16),
        compiler_params=_cparams(("arbitrary",)),
        name="moe_experts",
    )(plan["texp"], plan["n_used"], xs, wg, wu, wd)


def _combine_kernel(loff, goff, nch, h_ref, meta_ref, ys_ref, g_ref, b_ref, o_ref, yb_ref, sem, *, tm, alpha):
    i = pl.program_id(0)
    slot = i % 2

    @pl.when(i == 0)
    def _():
        yb_ref[...] = jnp.zeros_like(yb_ref)
        _start_segments(0, nch, goff, loff, ys_ref, yb_ref.at[0], sem.at[0])

    @pl.when(i + 1 < pl.num_programs(0))
    def _():
        _start_segments(i + 1, nch, goff, loff, ys_ref, yb_ref.at[1 - slot], sem.at[1 - slot])

    meta = meta_ref[...]
    r = lax.broadcasted_iota(jnp.int32, (tm, LROWS), 1).astype(F32)
    pick = ((r == meta[:, 0:1]) | (r == meta[:, 1:2])).astype(BF16)
    _wait_chunks(ys_ref, yb_ref.at[slot], sem.at[slot], _tile_chunks(i, nch))
    f = _dot(pick, yb_ref[slot])
    o_ref[...] = _ln(alpha * h_ref[...] + f, g_ref[...], b_ref[...])


def _moe_combine(h, meta, ys, plan, ln_g, ln_b, *, alpha):
    t = h.shape[0]
    tm = TMX
    grid_spec = pltpu.PrefetchScalarGridSpec(
        num_scalar_prefetch=3,
        grid=(t // tm,),
        in_specs=[pl.BlockSpec((tm, D_MODEL), lambda i, *_: (i, 0)),
                  pl.BlockSpec((tm, LANES), lambda i, *_: (i, 0)),
                  pl.BlockSpec(memory_space=pl.ANY),
                  pl.BlockSpec((1, D_MODEL), lambda i, *_: (0, 0)),
                  pl.BlockSpec((1, D_MODEL), lambda i, *_: (0, 0))],
        out_specs=pl.BlockSpec((tm, D_MODEL), lambda i, *_: (i, 0)),
        scratch_shapes=[pltpu.VMEM((2, LROWS, D_MODEL), BF16), pltpu.SemaphoreType.DMA((2,))],
    )
    return pl.pallas_call(
        functools.partial(_combine_kernel, tm=tm, alpha=alpha),
        grid_spec=grid_spec,
        out_shape=jax.ShapeDtypeStruct((t, D_MODEL), F32),
        compiler_params=_cparams(("arbitrary",)),
        name="moe_combine",
    )(plan["loff"], plan["goff"], plan["nch"], h, meta, ys, ln_g, ln_b)


def _moe(h, w_router, wg, wu, wd, ln_g, ln_b, *, alpha):
    t = h.shape[0]
    nt = t // TMX
    max_rows = 2 * t + nt * N_EXPERTS * (CH - 1) + N_EXPERTS * (TMR - CH)
    n_row_tiles = -(-max_rows // TMR)
    meta, metat, cnt, wparts = _moe_route(h, w_router)
    plan = _moe_plan(cnt, n_row_tiles)
    xs = _moe_dispatch(h, metat, wparts, plan, n_row_tiles * TMR)
    ys = _moe_experts(xs, wg, wu, wd, plan)
    return _moe_combine(h, meta, ys, plan, ln_g, ln_b, alpha=alpha)


def kernel(x, ln_in_g, ln_in_b, w_in, cmp_pos, cmp_w1, cmp_b1, cmp_w2, cmp_b2, conv_w, w_o, ln1_g, ln1_b, ln2_g, ln2_b, ffn_wg, ffn_wu, ffn_wd, moe_router, moe_wg, moe_wu, moe_wd):
    batch, seq, d = x.shape
    depth = w_in.shape[0]
    assert d == D_MODEL and seq % TM_PROJ == 0 and seq // CMP_STRIDE == LANES
    alpha = (2 * depth) ** 0.25
    t = batch * seq
    vec = lambda v: v.reshape(1, D_MODEL)
    h = x.reshape(t, d)
    for l in range(depth):
        w_r, w_vt = _prep_w_in(w_in[l])
        cw = jnp.pad(conv_w[l], ((0, 8 - CONV_W), (0, 0)))
        outs = _in_proj(h, vec(ln_in_g), vec(ln_in_b), w_r, w_vt, cw, seq=seq, pre_ln=(l == 0))
        if l == 0:
            h, outs = outs[0], outs[1:]
        q, kvc, ks, vs, kw, vw, gates, o_conv = outs
        kvcmp, kvcmp_t = _compress(kvc, cmp_pos[l], cmp_w1[l], cmp_b1[l], cmp_w2[l], cmp_b2[l],
                                   batch=batch, seq=seq)
        o_cmp, selb = _cmp_sel(q, kvcmp, kvcmp_t, gates, batch=batch, seq=seq)
        o_sel = _sel_attn(q, selb, ks, vs, gates, batch=batch, seq=seq)
        o_win = _win_attn(q, kw, vw, gates, batch=batch, seq=seq)
        h = _out_proj(h, o_cmp, o_sel, o_win, o_conv, w_o[l].astype(BF16), vec(ln1_g[l]), vec(ln1_b[l]),
                      alpha=alpha)
        if l % 2 == 0:
            h = _ffn(h, ffn_wg[l // 2].astype(BF16), ffn_wu[l // 2].astype(BF16), ffn_wd[l // 2].astype(BF16),
                     vec(ln2_g[l]), vec(ln2_b[l]), alpha=alpha)
        else:
            h = _moe(h, moe_router[l // 2], moe_wg[l // 2].astype(BF16), moe_wu[l // 2].astype(BF16),
                     moe_wd[l // 2].astype(BF16), vec(ln2_g[l]), vec(ln2_b[l]), alpha=alpha)
    return h.reshape(batch, seq, d)
```

```python
import functools

import numpy as np
import jax
import jax.numpy as jnp
from jax import lax
from jax.experimental import pallas as pl
from jax.experimental.pallas import tpu as pltpu

F32 = jnp.float32
BF16 = jnp.bfloat16

D_MODEL = 1024
HEAD_DIM = 64
N_HEADS = 8
N_KV = 2
HPG = N_HEADS // N_KV
ATTN_W = N_HEADS * HEAD_DIM
KV_W = N_KV * HEAD_DIM
CONV_CH = D_MODEL - ATTN_W
CONV_W = 3
CMP_BLOCK = 32
CMP_STRIDE = 16
CMP_HIDDEN = 256
SEL_BLOCK = 64
SEL_TOPN = 16
WINDOW = 512
D_FF = 2816
N_EXPERTS = 8
D_FF_EXPERT = 1408
LN_EPS = 1e-5
NEG = -1e30
FORCE = 1e9

LANES = 128
HPAD = LANES
Q_SCALE = HEAD_DIM ** -0.5

C_Q = 0
C_KVC = C_Q + N_HEADS * HPAD
C_KS = C_KVC + 2 * KV_W
C_KW = C_KS + N_KV * HPAD
C_U = C_KW + N_KV * HPAD
C_B = C_U + CONV_CH
C_C = C_B + CONV_CH
C_G = C_C + CONV_CH
C_END = C_G + LANES

TM_PROJ = 512
TQ = 256
TK = 256
VMEM_LIMIT = 56 * 1024 * 1024


def _cparams(sem):
    return pltpu.CompilerParams(dimension_semantics=sem, vmem_limit_bytes=VMEM_LIMIT)


def _ln(x, g, b):
    mu = jnp.mean(x, -1, keepdims=True)
    xc = x - mu
    var = jnp.mean(xc * xc, -1, keepdims=True)
    return xc * lax.rsqrt(var + LN_EPS) * g + b


def _dot(a, b):
    return jnp.dot(a, b, preferred_element_type=F32)


def _dot_nt(a, b):
    return lax.dot_general(a, b, (((1,), (1,)), ((), ())), preferred_element_type=F32)


def _resident(shape):
    nd = len(shape)
    return pl.BlockSpec(shape, lambda *_: (0,) * nd, pipeline_mode=pl.Buffered(1))


def _in_proj_kernel(*refs, pre_ln, tiles_per_seq, tm):
    if pre_ln:
        (x_ref, g_ref, b_ref, w_ref, wvt_ref, cw_ref, h_ref, q_ref, kvc_ref, ks_ref, vs_ref,
         kw_ref, vw_ref, gate_ref, oc_ref, carry_ref) = refs
        h = _ln(x_ref[...], g_ref[...], b_ref[...])
        h_ref[...] = h
    else:
        (x_ref, w_ref, wvt_ref, cw_ref, q_ref, kvc_ref, ks_ref, vs_ref,
         kw_ref, vw_ref, gate_ref, oc_ref, carry_ref) = refs
        h = x_ref[...]
    hb = h.astype(BF16)

    def proj(lo, hi):
        return _dot(hb, w_ref[:, lo:hi])

    q_ref[...] = (proj(C_Q, C_KVC) * Q_SCALE).astype(BF16)
    kvc = proj(C_KVC, C_KS)
    for j in range(4):
        kvc_ref[j] = kvc[:, j * HEAD_DIM:(j + 1) * HEAD_DIM]

    seq_tile = pl.program_id(0) % tiles_per_seq
    pos = lax.broadcasted_iota(jnp.int32, (tm, N_KV * HPAD), 0) + seq_tile * tm
    lane = lax.broadcasted_iota(jnp.int32, (tm, N_KV * HPAD), 1) % HPAD
    onehot = jnp.where(pos // SEL_BLOCK == lane - HEAD_DIM, 1.0, 0.0)
    in_tag = (lane >= HEAD_DIM) & (lane < HEAD_DIM + 32)
    ks_ref[...] = jnp.where(in_tag, onehot, proj(C_KS, C_KW)).astype(BF16)
    kw_ref[...] = proj(C_KW, C_U).astype(BF16)

    vrow = lax.broadcasted_iota(jnp.int32, (N_KV * HPAD, tm), 0) % HPAD
    for n, ref in enumerate((vs_ref, vw_ref)):
        vt = _dot_nt(wvt_ref[n * N_KV * HPAD:(n + 1) * N_KV * HPAD, :], hb)
        vt = jnp.where(vrow == HEAD_DIM, 1.0, vt).astype(BF16)
        for c in range(tm // TK):
            ref[c] = vt[:, c * TK:(c + 1) * TK]

    cu = proj(C_C, C_G) * proj(C_U, C_B)

    @pl.when(seq_tile == 0)
    def _():
        carry_ref[...] = jnp.zeros_like(carry_ref)

    prev = carry_ref[...]
    row = lax.broadcasted_iota(jnp.int32, (tm, CONV_CH), 0)
    s1 = jnp.where(row == 0, prev[7:8], pltpu.roll(cu, 1, 0))
    s2 = jnp.where(row == 0, prev[6:7], jnp.where(row == 1, prev[7:8], pltpu.roll(cu, 2, 0)))
    y = s2 * cw_ref[0:1, :] + s1 * cw_ref[1:2, :] + cu * cw_ref[2:3, :]
    oc_ref[...] = (proj(C_B, C_C) * y).astype(BF16)
    carry_ref[...] = cu[tm - 8:tm]
    gate_ref[...] = jax.nn.sigmoid(proj(C_G, C_END))


def _in_proj(x, ln_g, ln_b, w_r, w_vt, conv_w, *, seq, pre_ln):
    t = x.shape[0]
    tm = TM_PROJ
    nt = t // tm
    row = lambda w: pl.BlockSpec((tm, w), lambda i: (i, 0))
    in_specs = [row(D_MODEL)]
    args = [x]
    if pre_ln:
        in_specs += [_resident((1, D_MODEL)), _resident((1, D_MODEL))]
        args += [ln_g, ln_b]
    in_specs += [_resident((D_MODEL, C_END)), _resident((2 * N_KV * HPAD, D_MODEL)), _resident((8, CONV_CH))]
    args += [w_r, w_vt, conv_w]
    out_shape, out_specs = [], []
    if pre_ln:
        out_shape.append(jax.ShapeDtypeStruct((t, D_MODEL), F32))
        out_specs.append(row(D_MODEL))
    out_shape += [
        jax.ShapeDtypeStruct((t, N_HEADS * HPAD), BF16),
        jax.ShapeDtypeStruct((4, t, HEAD_DIM), F32),
        jax.ShapeDtypeStruct((t, N_KV * HPAD), BF16),
        jax.ShapeDtypeStruct((t // TK, N_KV * HPAD, TK), BF16),
        jax.ShapeDtypeStruct((t, N_KV * HPAD), BF16),
        jax.ShapeDtypeStruct((t // TK, N_KV * HPAD, TK), BF16),
        jax.ShapeDtypeStruct((t, LANES), F32),
        jax.ShapeDtypeStruct((t, CONV_CH), BF16),
    ]
    vt_spec = pl.BlockSpec((tm // TK, N_KV * HPAD, TK), lambda i: (i, 0, 0))
    out_specs += [
        row(N_HEADS * HPAD),
        pl.BlockSpec((4, tm, HEAD_DIM), lambda i: (0, i, 0)),
        row(N_KV * HPAD), vt_spec, row(N_KV * HPAD), vt_spec,
        row(LANES), row(CONV_CH),
    ]
    return pl.pallas_call(
        functools.partial(_in_proj_kernel, pre_ln=pre_ln, tiles_per_seq=seq // tm, tm=tm),
        grid=(nt,),
        in_specs=in_specs,
        out_specs=out_specs,
        out_shape=out_shape,
        scratch_shapes=[pltpu.VMEM((8, CONV_CH), F32)],
        compiler_params=_cparams(("arbitrary",)),
        name="in_proj_ln" if pre_ln else "in_proj",
    )(*args)


def _prep_w_in(w_in):
    d = w_in.shape[0]
    o = 0

    def take(n):
        nonlocal o
        s = w_in[:, o:o + n]
        o += n
        return s

    def pad_heads(s, n):
        s = s.reshape(d, n, HEAD_DIM)
        return jnp.pad(s, ((0, 0), (0, 0), (0, HPAD - HEAD_DIM))).reshape(d, n * HPAD)

    q = pad_heads(take(ATTN_W), N_HEADS)
    kc, vc = take(KV_W), take(KV_W)
    ks, vs, kw, vw = (pad_heads(take(KV_W), N_KV) for _ in range(4))
    gates = jnp.pad(take(3 * N_HEADS), ((0, 0), (0, LANES - 3 * N_HEADS)))
    u, bg, cg = take(CONV_CH), take(CONV_CH), take(CONV_CH)
    w_r = jnp.concatenate([q, kc, vc, ks, kw, u, bg, cg, gates], axis=1).astype(BF16)
    return w_r, jnp.concatenate([vs, vw], axis=1).T.astype(BF16)


def _compress_kernel(x_ref, pos_ref, w1_ref, b1_ref, w2_ref, b2_ref, w2t_ref, b2t_ref, o_ref, ot_ref):
    half = CMP_STRIDE * HEAD_DIM
    x = x_ref[0, 0]
    pos = pos_ref[0]
    xa = (x + pos[:, :half]).astype(BF16)
    xb = (x + pos[:, half:]).astype(BF16)
    a = _dot(xa, w1_ref[0, :half, :])
    b = _dot(xb, w1_ref[0, half:, :])
    n = x.shape[0]
    hid = a + pltpu.roll(b, n - 1, 0) + b1_ref[0]
    act = jax.nn.gelu(hid).astype(BF16)
    o_ref[0, 0] = (_dot(act, w2_ref[0]) + b2_ref[0]).astype(BF16)
    ot_ref[0, 0] = (_dot_nt(w2t_ref[0], act) + b2t_ref[0]).astype(BF16)


def _compress(kvc, cmp_pos, cmp_w1, cmp_b1, cmp_w2, cmp_b2, *, batch, seq):
    nb = seq // CMP_STRIDE
    half = CMP_STRIDE * HEAD_DIM
    x = kvc.reshape(4, batch, nb, half)
    pos = cmp_pos.reshape(2, 1, CMP_BLOCK * HEAD_DIM)
    w1 = cmp_w1.astype(BF16)
    b1 = cmp_b1.reshape(2, 1, CMP_HIDDEN)
    w2 = jnp.pad(cmp_w2, ((0, 0), (0, 0), (0, HPAD - HEAD_DIM))).astype(BF16)
    b2 = jnp.pad(cmp_b2, ((0, 0), (0, HPAD - HEAD_DIM))).reshape(2, 1, HPAD)
    w2t = jnp.swapaxes(w2, 1, 2)
    b2t = b2.reshape(2, HPAD, 1)
    kind = lambda j, b: (j // N_KV, 0, 0)
    return pl.pallas_call(
        _compress_kernel,
        grid=(4, batch),
        in_specs=[
            pl.BlockSpec((1, 1, nb, half), lambda j, b: (j, b, 0, 0)),
            pl.BlockSpec((1, 1, 2 * half), kind),
            pl.BlockSpec((1, 2 * half, CMP_HIDDEN), kind),
            pl.BlockSpec((1, 1, CMP_HIDDEN), kind),
            pl.BlockSpec((1, CMP_HIDDEN, HPAD), kind),
            pl.BlockSpec((1, 1, HPAD), kind),
            pl.BlockSpec((1, HPAD, CMP_HIDDEN), kind),
            pl.BlockSpec((1, HPAD, 1), kind),
        ],
        out_specs=[pl.BlockSpec((1, 1, nb, HPAD), lambda j, b: (j, b, 0, 0)),
                   pl.BlockSpec((1, 1, HPAD, nb), lambda j, b: (j, b, 0, 0))],
        out_shape=[jax.ShapeDtypeStruct((4, batch, nb, HPAD), BF16),
                   jax.ShapeDtypeStruct((4, batch, HPAD, nb), BF16)],
        compiler_params=_cparams(("arbitrary", "arbitrary")),
        name="compress",
    )(x, pos, w1, b1, w2, b2, w2t, b2t)


def _overlap_t():
    n = np.arange(LANES)
    j = np.arange(32)
    cs = n * CMP_STRIDE
    ss = j * SEL_BLOCK
    ov = (cs[None, :] < ss[:, None] + SEL_BLOCK) & (cs[None, :] + CMP_BLOCK > ss[:, None])
    return jnp.asarray(ov, dtype=BF16)


def _cmp_select(qh, kc, vct, ov, i, tq, n_sel):
    n = lax.broadcasted_iota(jnp.int32, (LANES, tq), 0)
    t = lax.broadcasted_iota(jnp.int32, (LANES, tq), 1) + i * tq
    mask = n * CMP_STRIDE + (CMP_BLOCK - 1) <= t
    maskf = mask.astype(F32)
    ss = [jnp.where(mask, _dot_nt(kc, q), NEG) for q in qh]
    es = [jnp.exp(s - jnp.max(s, axis=0, keepdims=True)) for s in ss]
    ps = [e / jnp.sum(e, axis=0, keepdims=True) * maskf for e in es]
    heads_t = [_dot(vct, p.astype(BF16))[:HEAD_DIM] for p in ps]
    psum = functools.reduce(lambda a, b: a + b, ps)

    p_hi = psum.astype(BF16)
    r1 = psum - p_hi.astype(F32)
    p_mid = r1.astype(BF16)
    p_lo = (r1 - p_mid.astype(F32)).astype(BF16)
    imp = _dot(ov, p_hi) + _dot(ov, p_mid) + _dot(ov, p_lo)

    jt = lax.broadcasted_iota(jnp.int32, (n_sel, tq), 0)
    tt = lax.broadcasted_iota(jnp.int32, (n_sel, tq), 1) + i * tq
    cur = tt // SEL_BLOCK
    valid = jt * SEL_BLOCK <= tt
    forced = (jt == 0) | (jt == cur) | (jt == cur - 1)
    score = jnp.where(forced, FORCE, jnp.where(valid, imp, NEG))
    rank = jnp.zeros((n_sel, tq), jnp.int32)
    for k in range(n_sel):
        sk = score[k:k + 1, :]
        ahead = (sk > score) | ((sk == score) & (jt > k))
        rank = rank + ahead.astype(jnp.int32)
    bias = jnp.where(rank < SEL_TOPN, 0.0, NEG)
    full = jnp.concatenate(
        [jnp.zeros((HEAD_DIM, tq), F32), bias, jnp.zeros((HPAD - HEAD_DIM - n_sel, tq), F32)], axis=0)
    return heads_t, full.T.astype(BF16)


def _attn_scores(k, qa):
    return tuple(_dot_nt(k, qa[hh]) for hh in range(HPG))


def _attn_update(ss, vt, state, mask):
    if mask is not None:
        ss = [jnp.where(mask, s, NEG) for s in ss]
    m_new = [jnp.maximum(state[hh][0], jnp.max(ss[hh], axis=0, keepdims=True)) for hh in range(HPG)]
    ps = [jnp.exp(ss[hh] - m_new[hh]).astype(BF16) for hh in range(HPG)]
    pv = [_dot(vt, ps[hh]) for hh in range(HPG)]
    return tuple((m_new[hh], jnp.exp(state[hh][0] - m_new[hh]) * state[hh][1] + pv[hh]) for hh in range(HPG))


def _attn_init(tq):
    return tuple((jnp.full((1, tq), NEG, F32), jnp.zeros((HPAD, tq), F32)) for _ in range(HPG))


def _nsa_attn_kernel(q_ref, kc_ref, vct_ref, ov_ref, ks_ref, vst_ref, kw_ref, vwt_ref, gate_ref, o_ref,
                     *, tq, tk, n_sel):
    g = pl.program_id(1)
    i = pl.program_id(2)
    nq = ks_ref.shape[0] // tk
    qh = [q_ref[:, hh * HPAD:(hh + 1) * HPAD] for hh in range(HPG)]
    heads_cmp, selb = _cmp_select(qh, kc_ref[0, 0], vct_ref[0, 0], ov_ref[...], i, tq, n_sel)
    lane = lax.broadcasted_iota(jnp.int32, selb.shape, 1)
    qsel = [jnp.where(lane >= HEAD_DIM, selb, q) for q in qh]
    key = lax.broadcasted_iota(jnp.int32, (tk, tq), 0)
    qry = lax.broadcasted_iota(jnp.int32, (tk, tq), 1)
    causal = key <= qry
    gt = gate_ref[...].T
    sub = lax.broadcasted_iota(jnp.int32, gt.shape, 0)

    def gate_row(idx):
        return jnp.sum(jnp.where(sub == idx, gt, 0.0), axis=0, keepdims=True)

    def run(n):
        tiles = [("sel", ks_ref, vst_ref, qsel, j, causal if j == n else None) for j in range(n + 1)]
        tiles += [("win", kw_ref, vwt_ref, qh, n - d, (causal, None, key > qry)[d]) for d in range(min(n, 2) + 1)]

        def scores(tile):
            _, k_ref, _, qa, j, _ = tile
            return _attn_scores(k_ref[j * tk:(j + 1) * tk, :], qa)

        state = {"sel": _attn_init(tq), "win": _attn_init(tq)}
        ss = scores(tiles[0])
        for idx, (branch, _, vt_ref, _, j, mask) in enumerate(tiles):
            ss_next = scores(tiles[idx + 1]) if idx + 1 < len(tiles) else None
            state[branch] = _attn_update(ss, vt_ref[j], state[branch], mask)
            ss = ss_next

        outs = []
        for hh in range(HPG):
            o_sel, o_win = (acc[:HEAD_DIM] / acc[HEAD_DIM:HEAD_DIM + 1]
                            for _, acc in (state["sel"][hh], state["win"][hh]))
            head = g * HPG + hh
            outs.append(gate_row(head) * heads_cmp[hh] + gate_row(N_HEADS + head) * o_sel
                        + gate_row(2 * N_HEADS + head) * o_win)
        o_ref[...] = jnp.concatenate(outs, axis=0).T.astype(o_ref.dtype)

    for n in range(nq):
        pl.when(i == n)(functools.partial(run, n))


def _nsa_attn(q, kvcmp, kvcmp_t, ks, vs_t, kw, vw_t, gates, *, batch, seq):
    t = q.shape[0]
    tq, tk = TQ, TK
    assert tq == tk and WINDOW == 2 * tk
    nq = seq // tq
    n_sel = seq // SEL_BLOCK
    nb = kvcmp.shape[2]
    assert nb == LANES and n_sel == 32
    rowblk = lambda w: pl.BlockSpec((tq, w), lambda b, g, i: (b * nq + i, g))
    seqblk = pl.BlockSpec((seq, HPAD), lambda b, g, i: (b, g))
    vtblk = pl.BlockSpec((seq // tk, HPAD, tk), lambda b, g, i: (b, g, 0))
    return pl.pallas_call(
        functools.partial(_nsa_attn_kernel, tq=tq, tk=tk, n_sel=n_sel),
        grid=(batch, N_KV, nq),
        in_specs=[rowblk(HPG * HPAD),
                  pl.BlockSpec((1, 1, nb, HPAD), lambda b, g, i: (g, b, 0, 0)),
                  pl.BlockSpec((1, 1, HPAD, nb), lambda b, g, i: (N_KV + g, b, 0, 0)),
                  pl.BlockSpec((n_sel, LANES), lambda b, g, i: (0, 0)),
                  seqblk, vtblk, seqblk, vtblk,
                  pl.BlockSpec((tq, LANES), lambda b, g, i: (b * nq + i, 0))],
        out_specs=rowblk(HPG * HEAD_DIM),
        out_shape=jax.ShapeDtypeStruct((t, ATTN_W), BF16),
        compiler_params=_cparams(("arbitrary", "arbitrary", "arbitrary")),
        name="nsa_attn",
    )(q, kvcmp, kvcmp_t, _overlap_t(), ks, vs_t, kw, vw_t, gates)


def _out_proj_kernel(h_ref, oa_ref, ocv_ref, w_ref, g_ref, b_ref, o_ref, *, alpha):
    m = _dot(oa_ref[...], w_ref[:ATTN_W, :]) + _dot(ocv_ref[...], w_ref[ATTN_W:, :])
    o_ref[...] = _ln(alpha * h_ref[...] + m, g_ref[...], b_ref[...])


def _out_proj(h, o_attn, o_conv, w_o, ln_g, ln_b, *, alpha):
    t = h.shape[0]
    tm = TM_PROJ
    row = lambda w: pl.BlockSpec((tm, w), lambda i: (i, 0))
    return pl.pallas_call(
        functools.partial(_out_proj_kernel, alpha=alpha),
        grid=(t // tm,),
        in_specs=[row(D_MODEL), row(ATTN_W), row(CONV_CH),
                  _resident((D_MODEL, D_MODEL)), _resident((1, D_MODEL)), _resident((1, D_MODEL))],
        out_specs=row(D_MODEL),
        out_shape=jax.ShapeDtypeStruct((t, D_MODEL), F32),
        compiler_params=_cparams(("arbitrary",)),
        name="out_proj",
    )(h, o_attn, o_conv, w_o, ln_g, ln_b)


def _ffn_kernel(h_ref, wg_ref, wu_ref, wd_ref, g_ref, b_ref, o_ref, *, alpha):
    h = h_ref[...]
    hb = h.astype(BF16)
    a = (jax.nn.silu(_dot(hb, wg_ref[...])) * _dot(hb, wu_ref[...])).astype(BF16)
    o_ref[...] = _ln(alpha * h + _dot(a, wd_ref[...]), g_ref[...], b_ref[...])


def _ffn(h, wg, wu, wd, ln_g, ln_b, *, alpha):
    t = h.shape[0]
    tm = TM_PROJ
    row = pl.BlockSpec((tm, D_MODEL), lambda i: (i, 0))
    return pl.pallas_call(
        functools.partial(_ffn_kernel, alpha=alpha),
        grid=(t // tm,),
        in_specs=[row, _resident((D_MODEL, D_FF)), _resident((D_MODEL, D_FF)), _resident((D_FF, D_MODEL)),
                  _resident((1, D_MODEL)), _resident((1, D_MODEL))],
        out_specs=row,
        out_shape=jax.ShapeDtypeStruct((t, D_MODEL), F32),
        compiler_params=_cparams(("arbitrary",)),
        name="ffn",
    )(h, wg, wu, wd, ln_g, ln_b)


TMX = 512
CH = 16
LROWS = 2 * TMX + N_EXPERTS * CH
TMR = 512
XROW = D_MODEL + LANES


def _route_kernel(h_ref, wr_ref, ltri_ref, ustr_ref, meta_ref, metat_ref, cnt_ref, wp_ref, *, tm):
    h = h_ref[...]
    h_hi = h.astype(BF16)
    h_lo = (h - h_hi.astype(F32)).astype(BF16)
    logits = _dot(h_hi, wr_ref[0]) + _dot(h_lo, wr_ref[0]) + _dot(h_hi, wr_ref[1])
    lane = lax.broadcasted_iota(jnp.int32, logits.shape, 1)
    logits = jnp.where(lane < N_EXPERTS, logits, -jnp.inf)
    m1 = jnp.max(logits, axis=-1, keepdims=True)
    i1 = jnp.min(jnp.where(logits == m1, lane, LANES), axis=-1, keepdims=True)
    rest = jnp.where(lane == i1, -jnp.inf, logits)
    m2 = jnp.max(rest, axis=-1, keepdims=True)
    i2 = jnp.min(jnp.where(rest == m2, lane, LANES), axis=-1, keepdims=True)
    e2 = jnp.exp(m2 - m1)
    den = 1.0 + e2
    w1 = 1.0 / den
    w2 = e2 / den

    routed = (lane == i1) | (lane == i2)
    cnt = _dot(ltri_ref[...], routed.astype(BF16))
    n = cnt[tm - 1:tm, :]
    padded = jnp.floor((n + (CH - 1)) * (1.0 / CH)) * CH
    seg_off = _dot(jnp.broadcast_to(padded, (8, LANES)).astype(BF16), ustr_ref[...])[0:1]
    dest = seg_off + cnt - 1.0
    d1 = jnp.sum(jnp.where(lane == i1, dest, 0.0), axis=-1, keepdims=True)
    d2 = jnp.sum(jnp.where(lane == i2, dest, 0.0), axis=-1, keepdims=True)
    meta = jnp.where(lane == 0, d1, jnp.where(lane == 1, d2, jnp.where(lane == 2, w1, jnp.where(lane == 3, w2, 0.0))))
    meta_ref[...] = meta
    metat_ref[...] = meta.T[0:8, :]
    cnt_ref[0] = jnp.broadcast_to(n, (8, LANES))
    for k, w in enumerate((w1, w2)):
        hi = w.astype(BF16).astype(F32)
        mid = (w - hi).astype(BF16).astype(F32)
        lo = (w - hi - mid).astype(BF16).astype(F32)
        wp_ref[k] = jnp.where(lane == 0, hi, jnp.where(lane == 1, mid, jnp.where(lane == 2, lo, 0.0))).astype(BF16)


def _moe_route(h, w_router):
    t = h.shape[0]
    tm = TMX
    nt = t // tm
    wr = jnp.pad(w_router, ((0, 0), (0, LANES - N_EXPERTS)))
    wr_hi = wr.astype(BF16)
    wr_lo = (wr - wr_hi.astype(F32)).astype(BF16)
    wr2 = jnp.stack([wr_hi, wr_lo])
    ltri = jnp.asarray(np.tril(np.ones((tm, tm), np.float32)), dtype=BF16)
    ustr = jnp.asarray(np.triu(np.ones((LANES, LANES), np.float32), 1), dtype=BF16)
    return pl.pallas_call(
        functools.partial(_route_kernel, tm=tm),
        grid=(nt,),
        in_specs=[pl.BlockSpec((tm, D_MODEL), lambda i: (i, 0)), _resident((2, D_MODEL, LANES)),
                  _resident((tm, tm)), _resident((LANES, LANES))],
        out_specs=[pl.BlockSpec((tm, LANES), lambda i: (i, 0)),
                   pl.BlockSpec((8, tm), lambda i: (0, i)),
                   pl.BlockSpec((1, 8, LANES), lambda i: (i, 0, 0)),
                   pl.BlockSpec((2, tm, LANES), lambda i: (0, i, 0))],
        out_shape=[jax.ShapeDtypeStruct((t, LANES), F32), jax.ShapeDtypeStruct((8, t), F32),
                   jax.ShapeDtypeStruct((nt, 8, LANES), F32), jax.ShapeDtypeStruct((2, t, LANES), BF16)],
        compiler_params=_cparams(("arbitrary",)),
        name="moe_route",
    )(h, wr2, ltri, ustr)


def _moe_plan(cnt, n_row_tiles):
    n = cnt[:, 0, :N_EXPERTS].astype(jnp.int32)
    p = (n + CH - 1) // CH * CH
    tot = p.sum(0)
    tot_pad = (tot + TMR - 1) // TMR * TMR
    gend = jnp.cumsum(tot_pad)
    gstart = gend - tot_pad
    goff = gstart[None, :] + jnp.cumsum(p, 0) - p
    loff = jnp.cumsum(p, 1) - p
    n_used = gend[-1:] // TMR
    tile_start = jnp.arange(n_row_tiles, dtype=jnp.int32) * TMR
    texp = jnp.minimum(jnp.sum(tile_start[:, None] >= gend[None, :], axis=1), N_EXPERTS - 1).astype(jnp.int32)
    texp = jnp.where(jnp.arange(n_row_tiles) < n_used[0], texp, texp[jnp.maximum(n_used[0] - 1, 0)])
    flat = lambda a: a.reshape(-1).astype(jnp.int32)
    gap_start = jnp.concatenate([gstart + tot, gend[-1:]])
    gap_rows = jnp.concatenate([tot_pad - tot, n_row_tiles * TMR - gend[-1:]])
    return dict(loff=flat(loff), goff=flat(goff), nch=flat(p // CH), gap_start=flat(gap_start),
                gap_nch=flat(gap_rows // CH), texp=flat(texp), n_used=flat(n_used))


def _chunk_copies(src_ref, dst_ref, sem, src_off, dst_off, n, src_step=CH):
    def body(c, carry):
        s = pl.multiple_of(src_off + c * src_step, CH)
        d = pl.multiple_of(dst_off + c * CH, CH)
        pltpu.make_async_copy(src_ref.at[pl.ds(s, CH)], dst_ref.at[pl.ds(d, CH)], sem).start()
        return carry
    lax.fori_loop(0, n, body, 0)


def _wait_chunks(src_ref, dst_ref, sem, n):
    def body(c, carry):
        pltpu.make_async_copy(src_ref.at[pl.ds(0, CH)], dst_ref.at[pl.ds(0, CH)], sem).wait()
        return carry
    lax.fori_loop(0, n, body, 0)


def _tile_chunks(tile, nch):
    total = 0
    for e in range(N_EXPERTS):
        total = total + nch[tile * N_EXPERTS + e]
    return total


def _start_segments(tile, nch, src_off, dst_off, src_ref, dst_ref, sem):
    for e in range(N_EXPERTS):
        idx = tile * N_EXPERTS + e
        _chunk_copies(src_ref, dst_ref, sem, src_off[idx], dst_off[idx], nch[idx])


def _dispatch_kernel(loff, goff, nch, gap_start, gap_nch, h_ref, metat_ref, wp_ref, xs_ref, xc_ref, z_ref, sem,
                     *, tm):
    i = pl.program_id(0)
    last = pl.num_programs(0) - 1
    slot = i % 2
    r = lax.broadcasted_iota(jnp.int32, (LROWS, tm), 0).astype(F32)
    pick1 = (r == metat_ref[0:1, :]).astype(BF16)
    pick2 = (r == metat_ref[1:2, :]).astype(BF16)
    xc_ref[slot, :, :D_MODEL] = _dot(pick1 + pick2, h_ref[...].astype(BF16)).astype(BF16)
    xc_ref[slot, :, D_MODEL:] = (_dot(pick1, wp_ref[0]) + _dot(pick2, wp_ref[1])).astype(BF16)
    _start_segments(i, nch, loff, goff, xc_ref.at[slot], xs_ref, sem.at[slot])

    @pl.when(i > 0)
    def _():
        _wait_chunks(xc_ref.at[1 - slot], xs_ref, sem.at[1 - slot], _tile_chunks(i - 1, nch))

    @pl.when(i == last)
    def _():
        z_ref[...] = jnp.zeros_like(z_ref)
        gaps = 0
        for e in range(N_EXPERTS + 1):
            _chunk_copies(z_ref, xs_ref, sem.at[2], 0, gap_start[e], gap_nch[e], src_step=0)
            gaps = gaps + gap_nch[e]
        _wait_chunks(z_ref, xs_ref, sem.at[2], gaps)
        _wait_chunks(xc_ref.at[slot], xs_ref, sem.at[slot], _tile_chunks(i, nch))


def _moe_dispatch(h, metat, wparts, plan, n_rows):
    t = h.shape[0]
    tm = TMX
    grid_spec = pltpu.PrefetchScalarGridSpec(
        num_scalar_prefetch=5,
        grid=(t // tm,),
        in_specs=[pl.BlockSpec((tm, D_MODEL), lambda i, *_: (i, 0)),
                  pl.BlockSpec((8, tm), lambda i, *_: (0, i)),
                  pl.BlockSpec((2, tm, LANES), lambda i, *_: (0, i, 0))],
        out_specs=pl.BlockSpec(memory_space=pl.ANY),
        scratch_shapes=[pltpu.VMEM((2, LROWS, XROW), BF16), pltpu.VMEM((CH, XROW), BF16),
                        pltpu.SemaphoreType.DMA((3,))],
    )
    return pl.pallas_call(
        functools.partial(_dispatch_kernel, tm=tm),
        grid_spec=grid_spec,
        out_shape=jax.ShapeDtypeStruct((n_rows, XROW), BF16),
        compiler_params=_cparams(("arbitrary",)),
        name="moe_dispatch",
    )(plan["loff"], plan["goff"], plan["nch"], plan["gap_start"], plan["gap_nch"], h, metat, wparts)


def _experts_kernel(texp, n_used, x_ref, wg_ref, wu_ref, wd_ref, y_ref):
    used = pl.program_id(0) < n_used[0]

    @pl.when(used)
    def _():
        x = x_ref[:, :D_MODEL]
        wp = x_ref[:, D_MODEL:].astype(F32)
        gate = wp[:, 0:1] + wp[:, 1:2] + wp[:, 2:3]
        a = (jax.nn.silu(_dot(x, wg_ref[0])) * _dot(x, wu_ref[0])).astype(BF16)
        y_ref[...] = (gate * _dot(a, wd_ref[0])).astype(BF16)

    @pl.when(jnp.logical_not(used))
    def _():
        y_ref[...] = jnp.zeros_like(y_ref)


def _moe_experts(xs, wg, wu, wd, plan):
    n_rows = xs.shape[0]
    rows = lambda r, texp, n_used: (jnp.minimum(r, n_used[0] - 1), 0)
    wspec = lambda shape: pl.BlockSpec((1,) + shape, lambda r, texp, n_used: (texp[r], 0, 0))
    grid_spec = pltpu.PrefetchScalarGridSpec(
        num_scalar_prefetch=2,
        grid=(n_rows // TMR,),
        in_specs=[pl.BlockSpec((TMR, XROW), rows), wspec((D_MODEL, D_FF_EXPERT)),
                  wspec((D_MODEL, D_FF_EXPERT)), wspec((D_FF_EXPERT, D_MODEL))],
        out_specs=pl.BlockSpec((TMR, D_MODEL), lambda r, texp, n_used: (r, 0)),
    )
    return pl.pallas_call(
        _experts_kernel,
        grid_spec=grid_spec,
        out_shape=jax.ShapeDtypeStruct((n_rows, D_MODEL), BF16),
        compiler_params=_cparams(("arbitrary",)),
        name="moe_experts",
    )(plan["texp"], plan["n_used"], xs, wg, wu, wd)


def _combine_kernel(loff, goff, nch, h_ref, meta_ref, ys_ref, g_ref, b_ref, o_ref, yb_ref, sem, *, tm, alpha):
    i = pl.program_id(0)
    slot = i % 2

    @pl.when(i == 0)
    def _():
        yb_ref[...] = jnp.zeros_like(yb_ref)
        _start_segments(0, nch, goff, loff, ys_ref, yb_ref.at[0], sem.at[0])

    @pl.when(i + 1 < pl.num_programs(0))
    def _():
        _start_segments(i + 1, nch, goff, loff, ys_ref, yb_ref.at[1 - slot], sem.at[1 - slot])

    meta = meta_ref[...]
    r = lax.broadcasted_iota(jnp.int32, (tm, LROWS), 1).astype(F32)
    pick = ((r == meta[:, 0:1]) | (r == meta[:, 1:2])).astype(BF16)
    _wait_chunks(ys_ref, yb_ref.at[slot], sem.at[slot], _tile_chunks(i, nch))
    f = _dot(pick, yb_ref[slot])
    o_ref[...] = _ln(alpha * h_ref[...] + f, g_ref[...], b_ref[...])


def _moe_combine(h, meta, ys, plan, ln_g, ln_b, *, alpha):
    t = h.shape[0]
    tm = TMX
    grid_spec = pltpu.PrefetchScalarGridSpec(
        num_scalar_prefetch=3,
        grid=(t // tm,),
        in_specs=[pl.BlockSpec((tm, D_MODEL), lambda i, *_: (i, 0)),
                  pl.BlockSpec((tm, LANES), lambda i, *_: (i, 0)),
                  pl.BlockSpec(memory_space=pl.ANY),
                  pl.BlockSpec((1, D_MODEL), lambda i, *_: (0, 0)),
                  pl.BlockSpec((1, D_MODEL), lambda i, *_: (0, 0))],
        out_specs=pl.BlockSpec((tm, D_MODEL), lambda i, *_: (i, 0)),
        scratch_shapes=[pltpu.VMEM((2, LROWS, D_MODEL), BF16), pltpu.SemaphoreType.DMA((2,))],
    )
    return pl.pallas_call(
        functools.partial(_combine_kernel, tm=tm, alpha=alpha),
        grid_spec=grid_spec,
        out_shape=jax.ShapeDtypeStruct((t, D_MODEL), F32),
        compiler_params=_cparams(("arbitrary",)),
        name="moe_combine",
    )(plan["loff"], plan["goff"], plan["nch"], h, meta, ys, ln_g, ln_b)


def _moe(h, w_router, wg, wu, wd, ln_g, ln_b, *, alpha):
    t = h.shape[0]
    nt = t // TMX
    max_rows = 2 * t + nt * N_EXPERTS * (CH - 1) + N_EXPERTS * (TMR - CH)
    n_row_tiles = -(-max_rows // TMR)
    meta, metat, cnt, wparts = _moe_route(h, w_router)
    plan = _moe_plan(cnt, n_row_tiles)
    xs = _moe_dispatch(h, metat, wparts, plan, n_row_tiles * TMR)
    ys = _moe_experts(xs, wg, wu, wd, plan)
    return _moe_combine(h, meta, ys, plan, ln_g, ln_b, alpha=alpha)


def kernel(x, ln_in_g, ln_in_b, w_in, cmp_pos, cmp_w1, cmp_b1, cmp_w2, cmp_b2, conv_w, w_o, ln1_g, ln1_b, ln2_g, ln2_b, ffn_wg, ffn_wu, ffn_wd, moe_router, moe_wg, moe_wu, moe_wd):
    batch, seq, d = x.shape
    depth = w_in.shape[0]
    assert d == D_MODEL and seq % TM_PROJ == 0 and seq // CMP_STRIDE == LANES
    alpha = (2 * depth) ** 0.25
    t = batch * seq
    vec = lambda v: v.reshape(1, D_MODEL)
    h = x.reshape(t, d)
    for l in range(depth):
        w_r, w_vt = _prep_w_in(w_in[l])
        cw = jnp.pad(conv_w[l], ((0, 8 - CONV_W), (0, 0)))
        outs = _in_proj(h, vec(ln_in_g), vec(ln_in_b), w_r, w_vt, cw, seq=seq, pre_ln=(l == 0))
        if l == 0:
            h, outs = outs[0], outs[1:]
        q, kvc, ks, vs, kw, vw, gates, o_conv = outs
        kvcmp, kvcmp_t = _compress(kvc, cmp_pos[l], cmp_w1[l], cmp_b1[l], cmp_w2[l], cmp_b2[l],
                                   batch=batch, seq=seq)
        o_attn = _nsa_attn(q, kvcmp, kvcmp_t, ks, vs, kw, vw, gates, batch=batch, seq=seq)
        h = _out_proj(h, o_attn, o_conv, w_o[l].astype(BF16), vec(ln1_g[l]), vec(ln1_b[l]), alpha=alpha)
        if l % 2 == 0:
            h = _ffn(h, ffn_wg[l // 2].astype(BF16), ffn_wu[l // 2].astype(BF16), ffn_wd[l // 2].astype(BF16),
                     vec(ln2_g[l]), vec(ln2_b[l]), alpha=alpha)
        else:
            h = _moe(h, moe_router[l // 2], moe_wg[l // 2].astype(BF16), moe_wu[l // 2].astype(BF16),
                     moe_wd[l // 2].astype(BF16), vec(ln2_g[l]), vec(ln2_b[l]), alpha=alpha)
    return h.reshape(batch, seq, d)
```

```python
import functools

import numpy as np
import jax
import jax.numpy as jnp
from jax import lax
from jax.experimental import pallas as pl
from jax.experimental.pallas import tpu as pltpu

F32 = jnp.float32
BF16 = jnp.bfloat16

D_MODEL = 1024
HEAD_DIM = 64
N_HEADS = 8
N_KV = 2
HPG = N_HEADS // N_KV
ATTN_W = N_HEADS * HEAD_DIM
KV_W = N_KV * HEAD_DIM
CONV_CH = D_MODEL - ATTN_W
CONV_W = 3
CMP_BLOCK = 32
CMP_STRIDE = 16
CMP_HIDDEN = 256
SEL_BLOCK = 64
SEL_TOPN = 16
WINDOW = 512
D_FF = 2816
N_EXPERTS = 8
D_FF_EXPERT = 1408
LN_EPS = 1e-5
NEG = -1e30
FORCE = 1e9

LANES = 128
HPAD = LANES
Q_SCALE = HEAD_DIM ** -0.5 * np.log2(np.e)

C_Q = 0
C_KVC = C_Q + ATTN_W
C_KS = C_KVC + 2 * KV_W
C_KW = C_KS + KV_W
C_U = C_KW + KV_W
C_B = C_U + CONV_CH
C_C = C_B + CONV_CH
C_G = C_C + CONV_CH
C_END = C_G + LANES

TM_PROJ = 512
TQ = 256
TK = 256
VMEM_LIMIT = 56 * 1024 * 1024


def _cparams(sem):
    return pltpu.CompilerParams(dimension_semantics=sem, vmem_limit_bytes=VMEM_LIMIT)


def _ln(x, g, b):
    mu = jnp.mean(x, -1, keepdims=True)
    xc = x - mu
    var = jnp.mean(xc * xc, -1, keepdims=True)
    return xc * lax.rsqrt(var + LN_EPS) * g + b


def _dot(a, b):
    return jnp.dot(a, b, preferred_element_type=F32)


def _dot_nt(a, b):
    return lax.dot_general(a, b, (((1,), (1,)), ((), ())), preferred_element_type=F32)


def _resident(shape):
    nd = len(shape)
    return pl.BlockSpec(shape, lambda *_: (0,) * nd, pipeline_mode=pl.Buffered(1))


def _spread_heads(z):
    low = lax.broadcasted_iota(jnp.int32, (z.shape[0], LANES), 1) < HEAD_DIM
    tiles = []
    for p in range(z.shape[1] // LANES):
        pair = z[:, p * LANES:(p + 1) * LANES]
        tiles += [jnp.where(low, pair, 0.0), jnp.where(low, pltpu.roll(pair, HEAD_DIM, 1), 0.0)]
    return jnp.concatenate(tiles, axis=1)


def _in_proj_kernel(*refs, pre_ln, tiles_per_seq, tm):
    if pre_ln:
        (x_ref, g_ref, b_ref, w_ref, wvt_ref, cw_ref, h_ref, q_ref, kvc_ref, ks_ref, vs_ref,
         kw_ref, vw_ref, gate_ref, oc_ref, carry_ref) = refs
        h = _ln(x_ref[...], g_ref[...], b_ref[...])
        h_ref[...] = h
    else:
        (x_ref, w_ref, wvt_ref, cw_ref, q_ref, kvc_ref, ks_ref, vs_ref,
         kw_ref, vw_ref, gate_ref, oc_ref, carry_ref) = refs
        h = x_ref[...]
    hb = h.astype(BF16)

    def proj(lo, hi):
        return _dot(hb, w_ref[:, lo:hi])

    q_ref[...] = _spread_heads(proj(C_Q, C_KVC) * Q_SCALE).astype(BF16)
    kvc = proj(C_KVC, C_KS)
    for j in range(4):
        kvc_ref[j] = kvc[:, j * HEAD_DIM:(j + 1) * HEAD_DIM]

    seq_tile = pl.program_id(0) % tiles_per_seq
    pos = lax.broadcasted_iota(jnp.int32, (tm, N_KV * HPAD), 0) + seq_tile * tm
    lane = lax.broadcasted_iota(jnp.int32, (tm, N_KV * HPAD), 1) % HPAD
    onehot = jnp.where(pos // SEL_BLOCK == lane - HEAD_DIM, 1.0, 0.0)
    in_tag = (lane >= HEAD_DIM) & (lane < HEAD_DIM + 32)
    ks_ref[...] = jnp.where(in_tag, onehot, _spread_heads(proj(C_KS, C_KW))).astype(BF16)
    kw_ref[...] = _spread_heads(proj(C_KW, C_U)).astype(BF16)

    vt = _dot_nt(wvt_ref[...], hb)
    tail = jnp.where(lax.broadcasted_iota(jnp.int32, (HPAD - HEAD_DIM, tm), 0) == 0, 1.0, 0.0)
    for n, ref in enumerate((vs_ref, vw_ref)):
        rows = []
        for grp in range(N_KV):
            lo = (n * N_KV + grp) * HEAD_DIM
            rows += [vt[lo:lo + HEAD_DIM], tail]
        full = jnp.concatenate(rows, axis=0).astype(BF16)
        for c in range(tm // TK):
            ref[c] = full[:, c * TK:(c + 1) * TK]

    cu = proj(C_C, C_G) * proj(C_U, C_B)

    @pl.when(seq_tile == 0)
    def _():
        carry_ref[...] = jnp.zeros_like(carry_ref)

    prev = carry_ref[...]
    row = lax.broadcasted_iota(jnp.int32, (tm, CONV_CH), 0)
    s1 = jnp.where(row == 0, prev[7:8], pltpu.roll(cu, 1, 0))
    s2 = jnp.where(row == 0, prev[6:7], jnp.where(row == 1, prev[7:8], pltpu.roll(cu, 2, 0)))
    y = s2 * cw_ref[0:1, :] + s1 * cw_ref[1:2, :] + cu * cw_ref[2:3, :]
    oc_ref[...] = (proj(C_B, C_C) * y).astype(BF16)
    carry_ref[...] = cu[tm - 8:tm]
    gate_ref[...] = jax.nn.sigmoid(proj(C_G, C_END))


def _in_proj(x, ln_g, ln_b, w_r, w_vt, conv_w, *, seq, pre_ln):
    t = x.shape[0]
    tm = TM_PROJ
    nt = t // tm
    row = lambda w: pl.BlockSpec((tm, w), lambda i: (i, 0))
    in_specs = [row(D_MODEL)]
    args = [x]
    if pre_ln:
        in_specs += [_resident((1, D_MODEL)), _resident((1, D_MODEL))]
        args += [ln_g, ln_b]
    in_specs += [_resident((D_MODEL, C_END)), _resident((2 * KV_W, D_MODEL)), _resident((8, CONV_CH))]
    args += [w_r, w_vt, conv_w]
    out_shape, out_specs = [], []
    if pre_ln:
        out_shape.append(jax.ShapeDtypeStruct((t, D_MODEL), F32))
        out_specs.append(row(D_MODEL))
    out_shape += [
        jax.ShapeDtypeStruct((t, N_HEADS * HPAD), BF16),
        jax.ShapeDtypeStruct((4, t, HEAD_DIM), F32),
        jax.ShapeDtypeStruct((t, N_KV * HPAD), BF16),
        jax.ShapeDtypeStruct((t // TK, N_KV * HPAD, TK), BF16),
        jax.ShapeDtypeStruct((t, N_KV * HPAD), BF16),
        jax.ShapeDtypeStruct((t // TK, N_KV * HPAD, TK), BF16),
        jax.ShapeDtypeStruct((t, LANES), F32),
        jax.ShapeDtypeStruct((t, CONV_CH), BF16),
    ]
    vt_spec = pl.BlockSpec((tm // TK, N_KV * HPAD, TK), lambda i: (i, 0, 0))
    out_specs += [
        row(N_HEADS * HPAD),
        pl.BlockSpec((4, tm, HEAD_DIM), lambda i: (0, i, 0)),
        row(N_KV * HPAD), vt_spec, row(N_KV * HPAD), vt_spec,
        row(LANES), row(CONV_CH),
    ]
    return pl.pallas_call(
        functools.partial(_in_proj_kernel, pre_ln=pre_ln, tiles_per_seq=seq // tm, tm=tm),
        grid=(nt,),
        in_specs=in_specs,
        out_specs=out_specs,
        out_shape=out_shape,
        scratch_shapes=[pltpu.VMEM((8, CONV_CH), F32)],
        compiler_params=_cparams(("arbitrary",)),
        name="in_proj_ln" if pre_ln else "in_proj",
    )(*args)


def _prep_w_in(w_in):
    o = 0

    def take(n):
        nonlocal o
        s = w_in[:, o:o + n]
        o += n
        return s

    q, kc, vc, ks, vs, kw, vw = take(ATTN_W), *(take(KV_W) for _ in range(6))
    gates = jnp.pad(take(3 * N_HEADS), ((0, 0), (0, LANES - 3 * N_HEADS)))
    conv = take(3 * CONV_CH)
    w_r = jnp.concatenate([q, kc, vc, ks, kw, conv, gates], axis=1).astype(BF16)
    return w_r, jnp.concatenate([vs, vw], axis=1).T.astype(BF16)


def _compress_kernel(x_ref, pos_ref, w1_ref, b1_ref, w2_ref, b2_ref, w2t_ref, b2t_ref, o_ref, ot_ref):
    half = CMP_STRIDE * HEAD_DIM
    x = x_ref[0, 0]
    pos = pos_ref[0]
    xa = (x + pos[:, :half]).astype(BF16)
    xb = (x + pos[:, half:]).astype(BF16)
    a = _dot(xa, w1_ref[0, :half, :])
    b = _dot(xb, w1_ref[0, half:, :])
    n = x.shape[0]
    hid = a + pltpu.roll(b, n - 1, 0) + b1_ref[0]
    act = jax.nn.gelu(hid).astype(BF16)
    o_ref[0, 0] = (_dot(act, w2_ref[0]) + b2_ref[0]).astype(BF16)
    ot_ref[0, 0] = (_dot_nt(w2t_ref[0], act) + b2t_ref[0]).astype(BF16)


def _compress(kvc, cmp_pos, cmp_w1, cmp_b1, cmp_w2, cmp_b2, *, batch, seq):
    nb = seq // CMP_STRIDE
    half = CMP_STRIDE * HEAD_DIM
    x = kvc.reshape(4, batch, nb, half)
    pos = cmp_pos.reshape(2, 1, CMP_BLOCK * HEAD_DIM)
    w1 = cmp_w1.astype(BF16)
    b1 = cmp_b1.reshape(2, 1, CMP_HIDDEN)
    w2 = jnp.pad(cmp_w2, ((0, 0), (0, 0), (0, HPAD - HEAD_DIM))).astype(BF16)
    b2 = jnp.pad(cmp_b2, ((0, 0), (0, HPAD - HEAD_DIM))).reshape(2, 1, HPAD)
    w2t = jnp.swapaxes(w2, 1, 2)
    b2t = b2.reshape(2, HPAD, 1)
    kind = lambda j, b: (j // N_KV, 0, 0)
    return pl.pallas_call(
        _compress_kernel,
        grid=(4, batch),
        in_specs=[
            pl.BlockSpec((1, 1, nb, half), lambda j, b: (j, b, 0, 0)),
            pl.BlockSpec((1, 1, 2 * half), kind),
            pl.BlockSpec((1, 2 * half, CMP_HIDDEN), kind),
            pl.BlockSpec((1, 1, CMP_HIDDEN), kind),
            pl.BlockSpec((1, CMP_HIDDEN, HPAD), kind),
            pl.BlockSpec((1, 1, HPAD), kind),
            pl.BlockSpec((1, HPAD, CMP_HIDDEN), kind),
            pl.BlockSpec((1, HPAD, 1), kind),
        ],
        out_specs=[pl.BlockSpec((1, 1, nb, HPAD), lambda j, b: (j, b, 0, 0)),
                   pl.BlockSpec((1, 1, HPAD, nb), lambda j, b: (j, b, 0, 0))],
        out_shape=[jax.ShapeDtypeStruct((4, batch, nb, HPAD), BF16),
                   jax.ShapeDtypeStruct((4, batch, HPAD, nb), BF16)],
        compiler_params=_cparams(("arbitrary", "arbitrary")),
        name="compress",
    )(x, pos, w1, b1, w2, b2, w2t, b2t)


def _overlap_t():
    n = np.arange(LANES)
    j = np.arange(32)
    cs = n * CMP_STRIDE
    ss = j * SEL_BLOCK
    ov = (cs[None, :] < ss[:, None] + SEL_BLOCK) & (cs[None, :] + CMP_BLOCK > ss[:, None])
    return jnp.asarray(ov, dtype=BF16)


def _cmp_select(qh, kc, vct, ov, i, tq, n_sel):
    n = lax.broadcasted_iota(jnp.int32, (LANES, tq), 0)
    t = lax.broadcasted_iota(jnp.int32, (LANES, tq), 1) + i * tq
    mask = n * CMP_STRIDE + (CMP_BLOCK - 1) <= t
    maskf = mask.astype(F32)
    ss = [jnp.where(mask, _dot_nt(kc, q), NEG) for q in qh]
    es = [jnp.exp2(s - jnp.max(s, axis=0, keepdims=True)) for s in ss]
    ps = [e / jnp.sum(e, axis=0, keepdims=True) * maskf for e in es]
    heads_t = [_dot(vct, p.astype(BF16))[:HEAD_DIM] for p in ps]
    psum = functools.reduce(lambda a, b: a + b, ps)

    p_hi = psum.astype(BF16)
    r1 = psum - p_hi.astype(F32)
    p_mid = r1.astype(BF16)
    p_lo = (r1 - p_mid.astype(F32)).astype(BF16)
    imp = _dot(ov, p_hi) + _dot(ov, p_mid) + _dot(ov, p_lo)

    jt = lax.broadcasted_iota(jnp.int32, (n_sel, tq), 0)
    tt = lax.broadcasted_iota(jnp.int32, (n_sel, tq), 1) + i * tq
    cur = tt // SEL_BLOCK
    valid = jt * SEL_BLOCK <= tt
    forced = (jt == 0) | (jt == cur) | (jt == cur - 1)
    score = jnp.where(forced, FORCE, jnp.where(valid, imp, NEG))
    rank = jnp.zeros((n_sel, tq), jnp.int32)
    for k in range(n_sel):
        sk = score[k:k + 1, :]
        ahead = (sk > score) | ((sk == score) & (jt > k))
        rank = rank + ahead.astype(jnp.int32)
    bias = jnp.where(rank < SEL_TOPN, 0.0, NEG)
    full = jnp.concatenate(
        [jnp.zeros((HEAD_DIM, tq), F32), bias, jnp.zeros((HPAD - HEAD_DIM - n_sel, tq), F32)], axis=0)
    return heads_t, full.T.astype(BF16)


def _attn_scores(k, qa):
    return tuple(_dot_nt(k, qa[hh]) for hh in range(HPG))


def _attn_update(ss, vt, state, mask):
    if mask is not None:
        ss = [jnp.where(mask, s, NEG) for s in ss]
    m_new = [jnp.maximum(state[hh][0], jnp.max(ss[hh], axis=0, keepdims=True)) for hh in range(HPG)]
    ps = [jnp.exp2(ss[hh] - m_new[hh]).astype(BF16) for hh in range(HPG)]
    pv = [_dot(vt, ps[hh]) for hh in range(HPG)]
    return tuple((m_new[hh], jnp.exp2(state[hh][0] - m_new[hh]) * state[hh][1] + pv[hh]) for hh in range(HPG))


def _attn_init(tq):
    return tuple((jnp.full((1, tq), NEG, F32), jnp.zeros((HPAD, tq), F32)) for _ in range(HPG))


def _nsa_attn_kernel(q_ref, kc_ref, vct_ref, ov_ref, ks_ref, vst_ref, kw_ref, vwt_ref, gate_ref, o_ref,
                     *, tq, tk, n_sel):
    g = pl.program_id(1)
    i = pl.program_id(2)
    nq = ks_ref.shape[0] // tk
    qh = [q_ref[:, hh * HPAD:(hh + 1) * HPAD] for hh in range(HPG)]
    heads_cmp, selb = _cmp_select(qh, kc_ref[0, 0], vct_ref[0, 0], ov_ref[...], i, tq, n_sel)
    lane = lax.broadcasted_iota(jnp.int32, selb.shape, 1)
    qsel = [jnp.where(lane >= HEAD_DIM, selb, q) for q in qh]
    key = lax.broadcasted_iota(jnp.int32, (tk, tq), 0)
    qry = lax.broadcasted_iota(jnp.int32, (tk, tq), 1)
    causal = key <= qry
    gt = gate_ref[...].T
    sub = lax.broadcasted_iota(jnp.int32, gt.shape, 0)

    def gate_row(idx):
        return jnp.sum(jnp.where(sub == idx, gt, 0.0), axis=0, keepdims=True)

    key2 = lax.broadcasted_iota(jnp.int32, (2 * tk, tq), 0)
    pair_causal = key2 - tk <= lax.broadcasted_iota(jnp.int32, (2 * tk, tq), 1)

    def run(n):
        chunks = []
        for j in range(0, n, 2):
            chunks.append(("sel", j, 2, pair_causal if j + 1 == n else None))
        if n % 2 == 0:
            chunks.append(("sel", n, 1, causal))
        if n == 0:
            chunks.append(("win", 0, 1, causal))
        else:
            chunks.append(("win", n - 1, 2, pair_causal))
        if n >= 2:
            chunks.append(("win", n - 2, 1, key > qry))
        refs = {"sel": (ks_ref, vst_ref, qsel), "win": (kw_ref, vwt_ref, qh)}

        def scores(chunk):
            branch, j, nt, _ = chunk
            return _attn_scores(refs[branch][0][j * tk:(j + nt) * tk, :], refs[branch][2])

        state = {"sel": _attn_init(tq), "win": _attn_init(tq)}
        ss = scores(chunks[0])
        for idx, (branch, j, nt, mask) in enumerate(chunks):
            ss_next = scores(chunks[idx + 1]) if idx + 1 < len(chunks) else None
            vt_ref = refs[branch][1]
            vt = vt_ref[j] if nt == 1 else jnp.concatenate([vt_ref[j], vt_ref[j + 1]], axis=1)
            state[branch] = _attn_update(ss, vt, state[branch], mask)
            ss = ss_next

        outs = []
        for hh in range(HPG):
            o_sel, o_win = (acc[:HEAD_DIM] / acc[HEAD_DIM:HEAD_DIM + 1]
                            for _, acc in (state["sel"][hh], state["win"][hh]))
            head = g * HPG + hh
            outs.append(gate_row(head) * heads_cmp[hh] + gate_row(N_HEADS + head) * o_sel
                        + gate_row(2 * N_HEADS + head) * o_win)
        o_ref[...] = jnp.concatenate(outs, axis=0).T.astype(o_ref.dtype)

    for n in range(nq):
        pl.when(i == n)(functools.partial(run, n))


def _nsa_attn(q, kvcmp, kvcmp_t, ks, vs_t, kw, vw_t, gates, *, batch, seq):
    t = q.shape[0]
    tq, tk = TQ, TK
    assert tq == tk and WINDOW == 2 * tk
    nq = seq // tq
    n_sel = seq // SEL_BLOCK
    nb = kvcmp.shape[2]
    assert nb == LANES and n_sel == 32
    rowblk = lambda w: pl.BlockSpec((tq, w), lambda b, g, i: (b * nq + i, g))
    seqblk = pl.BlockSpec((seq, HPAD), lambda b, g, i: (b, g))
    vtblk = pl.BlockSpec((seq // tk, HPAD, tk), lambda b, g, i: (b, g, 0))
    return pl.pallas_call(
        functools.partial(_nsa_attn_kernel, tq=tq, tk=tk, n_sel=n_sel),
        grid=(batch, N_KV, nq),
        in_specs=[rowblk(HPG * HPAD),
                  pl.BlockSpec((1, 1, nb, HPAD), lambda b, g, i: (g, b, 0, 0)),
                  pl.BlockSpec((1, 1, HPAD, nb), lambda b, g, i: (N_KV + g, b, 0, 0)),
                  pl.BlockSpec((n_sel, LANES), lambda b, g, i: (0, 0)),
                  seqblk, vtblk, seqblk, vtblk,
                  pl.BlockSpec((tq, LANES), lambda b, g, i: (b * nq + i, 0))],
        out_specs=rowblk(HPG * HEAD_DIM),
        out_shape=jax.ShapeDtypeStruct((t, ATTN_W), BF16),
        compiler_params=_cparams(("arbitrary", "arbitrary", "arbitrary")),
        name="nsa_attn",
    )(q, kvcmp, kvcmp_t, _overlap_t(), ks, vs_t, kw, vw_t, gates)


def _out_proj_kernel(h_ref, oa_ref, ocv_ref, w_ref, g_ref, b_ref, o_ref, *, alpha):
    m = _dot(oa_ref[...], w_ref[:ATTN_W, :]) + _dot(ocv_ref[...], w_ref[ATTN_W:, :])
    o_ref[...] = _ln(alpha * h_ref[...] + m, g_ref[...], b_ref[...])


def _out_proj(h, o_attn, o_conv, w_o, ln_g, ln_b, *, alpha):
    t = h.shape[0]
    tm = TM_PROJ
    row = lambda w: pl.BlockSpec((tm, w), lambda i: (i, 0))
    return pl.pallas_call(
        functools.partial(_out_proj_kernel, alpha=alpha),
        grid=(t // tm,),
        in_specs=[row(D_MODEL), row(ATTN_W), row(CONV_CH),
                  _resident((D_MODEL, D_MODEL)), _resident((1, D_MODEL)), _resident((1, D_MODEL))],
        out_specs=row(D_MODEL),
        out_shape=jax.ShapeDtypeStruct((t, D_MODEL), F32),
        compiler_params=_cparams(("arbitrary",)),
        name="out_proj",
    )(h, o_attn, o_conv, w_o, ln_g, ln_b)


def _ffn_kernel(h_ref, wg_ref, wu_ref, wd_ref, g_ref, b_ref, o_ref, *, alpha):
    h = h_ref[...]
    hb = h.astype(BF16)
    a = (jax.nn.silu(_dot(hb, wg_ref[...])) * _dot(hb, wu_ref[...])).astype(BF16)
    o_ref[...] = _ln(alpha * h + _dot(a, wd_ref[...]), g_ref[...], b_ref[...])


def _ffn(h, wg, wu, wd, ln_g, ln_b, *, alpha):
    t = h.shape[0]
    tm = TM_PROJ
    row = pl.BlockSpec((tm, D_MODEL), lambda i: (i, 0))
    return pl.pallas_call(
        functools.partial(_ffn_kernel, alpha=alpha),
        grid=(t // tm,),
        in_specs=[row, _resident((D_MODEL, D_FF)), _resident((D_MODEL, D_FF)), _resident((D_FF, D_MODEL)),
                  _resident((1, D_MODEL)), _resident((1, D_MODEL))],
        out_specs=row,
        out_shape=jax.ShapeDtypeStruct((t, D_MODEL), F32),
        compiler_params=_cparams(("arbitrary",)),
        name="ffn",
    )(h, wg, wu, wd, ln_g, ln_b)


TMX = 512
CH = 16
LROWS = 2 * TMX + N_EXPERTS * CH
TMR = 512
XROW = D_MODEL + LANES


def _route_kernel(h_ref, wr_ref, ltri_ref, ustr_ref, meta_ref, metat_ref, cnt_ref, wp_ref, *, tm):
    h = h_ref[...]
    h_hi = h.astype(BF16)
    h_lo = (h - h_hi.astype(F32)).astype(BF16)
    logits = _dot(h_hi, wr_ref[0]) + _dot(h_lo, wr_ref[0]) + _dot(h_hi, wr_ref[1])
    lane = lax.broadcasted_iota(jnp.int32, logits.shape, 1)
    logits = jnp.where(lane < N_EXPERTS, logits, -jnp.inf)
    m1 = jnp.max(logits, axis=-1, keepdims=True)
    i1 = jnp.min(jnp.where(logits == m1, lane, LANES), axis=-1, keepdims=True)
    rest = jnp.where(lane == i1, -jnp.inf, logits)
    m2 = jnp.max(rest, axis=-1, keepdims=True)
    i2 = jnp.min(jnp.where(rest == m2, lane, LANES), axis=-1, keepdims=True)
    e2 = jnp.exp(m2 - m1)
    den = 1.0 + e2
    w1 = 1.0 / den
    w2 = e2 / den

    routed = (lane == i1) | (lane == i2)
    cnt = _dot(ltri_ref[...], routed.astype(BF16))
    n = cnt[tm - 1:tm, :]
    padded = jnp.floor((n + (CH - 1)) * (1.0 / CH)) * CH
    seg_off = _dot(jnp.broadcast_to(padded, (8, LANES)).astype(BF16), ustr_ref[...])[0:1]
    dest = seg_off + cnt - 1.0
    d1 = jnp.sum(jnp.where(lane == i1, dest, 0.0), axis=-1, keepdims=True)
    d2 = jnp.sum(jnp.where(lane == i2, dest, 0.0), axis=-1, keepdims=True)
    meta = jnp.where(lane == 0, d1, jnp.where(lane == 1, d2, jnp.where(lane == 2, w1, jnp.where(lane == 3, w2, 0.0))))
    meta_ref[...] = meta
    metat_ref[...] = meta.T[0:8, :]
    cnt_ref[0] = jnp.broadcast_to(n, (8, LANES))
    for k, w in enumerate((w1, w2)):
        hi = w.astype(BF16).astype(F32)
        mid = (w - hi).astype(BF16).astype(F32)
        lo = (w - hi - mid).astype(BF16).astype(F32)
        wp_ref[k] = jnp.where(lane == 0, hi, jnp.where(lane == 1, mid, jnp.where(lane == 2, lo, 0.0))).astype(BF16)


def _moe_route(h, w_router):
    t = h.shape[0]
    tm = TMX
    nt = t // tm
    wr = jnp.pad(w_router, ((0, 0), (0, LANES - N_EXPERTS)))
    wr_hi = wr.astype(BF16)
    wr_lo = (wr - wr_hi.astype(F32)).astype(BF16)
    wr2 = jnp.stack([wr_hi, wr_lo])
    ltri = jnp.asarray(np.tril(np.ones((tm, tm), np.float32)), dtype=BF16)
    ustr = jnp.asarray(np.triu(np.ones((LANES, LANES), np.float32), 1), dtype=BF16)
    return pl.pallas_call(
        functools.partial(_route_kernel, tm=tm),
        grid=(nt,),
        in_specs=[pl.BlockSpec((tm, D_MODEL), lambda i: (i, 0)), _resident((2, D_MODEL, LANES)),
                  _resident((tm, tm)), _resident((LANES, LANES))],
        out_specs=[pl.BlockSpec((tm, LANES), lambda i: (i, 0)),
                   pl.BlockSpec((8, tm), lambda i: (0, i)),
                   pl.BlockSpec((1, 8, LANES), lambda i: (i, 0, 0)),
                   pl.BlockSpec((2, tm, LANES), lambda i: (0, i, 0))],
        out_shape=[jax.ShapeDtypeStruct((t, LANES), F32), jax.ShapeDtypeStruct((8, t), F32),
                   jax.ShapeDtypeStruct((nt, 8, LANES), F32), jax.ShapeDtypeStruct((2, t, LANES), BF16)],
        compiler_params=_cparams(("arbitrary",)),
        name="moe_route",
    )(h, wr2, ltri, ustr)


def _moe_plan(cnt, n_row_tiles):
    n = cnt[:, 0, :N_EXPERTS].astype(jnp.int32)
    p = (n + CH - 1) // CH * CH
    tot = p.sum(0)
    tot_pad = (tot + TMR - 1) // TMR * TMR
    gend = jnp.cumsum(tot_pad)
    gstart = gend - tot_pad
    goff = gstart[None, :] + jnp.cumsum(p, 0) - p
    loff = jnp.cumsum(p, 1) - p
    n_used = gend[-1:] // TMR
    tile_start = jnp.arange(n_row_tiles, dtype=jnp.int32) * TMR
    texp = jnp.minimum(jnp.sum(tile_start[:, None] >= gend[None, :], axis=1), N_EXPERTS - 1).astype(jnp.int32)
    texp = jnp.where(jnp.arange(n_row_tiles) < n_used[0], texp, texp[jnp.maximum(n_used[0] - 1, 0)])
    flat = lambda a: a.reshape(-1).astype(jnp.int32)
    gap_start = jnp.concatenate([gstart + tot, gend[-1:]])
    gap_rows = jnp.concatenate([tot_pad - tot, n_row_tiles * TMR - gend[-1:]])
    return dict(loff=flat(loff), goff=flat(goff), nch=flat(p // CH), gap_start=flat(gap_start),
                gap_nch=flat(gap_rows // CH), texp=flat(texp), n_used=flat(n_used))


def _chunk_copies(src_ref, dst_ref, sem, src_off, dst_off, n, src_step=CH):
    def body(c, carry):
        s = pl.multiple_of(src_off + c * src_step, CH)
        d = pl.multiple_of(dst_off + c * CH, CH)
        pltpu.make_async_copy(src_ref.at[pl.ds(s, CH)], dst_ref.at[pl.ds(d, CH)], sem).start()
        return carry
    lax.fori_loop(0, n, body, 0)


def _wait_chunks(src_ref, dst_ref, sem, n):
    def body(c, carry):
        pltpu.make_async_copy(src_ref.at[pl.ds(0, CH)], dst_ref.at[pl.ds(0, CH)], sem).wait()
        return carry
    lax.fori_loop(0, n, body, 0)


def _tile_chunks(tile, nch):
    total = 0
    for e in range(N_EXPERTS):
        total = total + nch[tile * N_EXPERTS + e]
    return total


def _start_segments(tile, nch, src_off, dst_off, src_ref, dst_ref, sem):
    for e in range(N_EXPERTS):
        idx = tile * N_EXPERTS + e
        _chunk_copies(src_ref, dst_ref, sem, src_off[idx], dst_off[idx], nch[idx])


def _dispatch_kernel(loff, goff, nch, gap_start, gap_nch, h_ref, metat_ref, wp_ref, xs_ref, xc_ref, z_ref, sem,
                     *, tm):
    i = pl.program_id(0)
    last = pl.num_programs(0) - 1
    slot = i % 2
    r = lax.broadcasted_iota(jnp.int32, (LROWS, tm), 0).astype(F32)
    pick1 = (r == metat_ref[0:1, :]).astype(BF16)
    pick2 = (r == metat_ref[1:2, :]).astype(BF16)
    xc_ref[slot, :, :D_MODEL] = _dot(pick1 + pick2, h_ref[...].astype(BF16)).astype(BF16)
    xc_ref[slot, :, D_MODEL:] = (_dot(pick1, wp_ref[0]) + _dot(pick2, wp_ref[1])).astype(BF16)
    _start_segments(i, nch, loff, goff, xc_ref.at[slot], xs_ref, sem.at[slot])

    @pl.when(i > 0)
    def _():
        _wait_chunks(xc_ref.at[1 - slot], xs_ref, sem.at[1 - slot], _tile_chunks(i - 1, nch))

    @pl.when(i == last)
    def _():
        z_ref[...] = jnp.zeros_like(z_ref)
        gaps = 0
        for e in range(N_EXPERTS + 1):
            _chunk_copies(z_ref, xs_ref, sem.at[2], 0, gap_start[e], gap_nch[e], src_step=0)
            gaps = gaps + gap_nch[e]
        _wait_chunks(z_ref, xs_ref, sem.at[2], gaps)
        _wait_chunks(xc_ref.at[slot], xs_ref, sem.at[slot], _tile_chunks(i, nch))


def _moe_dispatch(h, metat, wparts, plan, n_rows):
    t = h.shape[0]
    tm = TMX
    grid_spec = pltpu.PrefetchScalarGridSpec(
        num_scalar_prefetch=5,
        grid=(t // tm,),
        in_specs=[pl.BlockSpec((tm, D_MODEL), lambda i, *_: (i, 0)),
                  pl.BlockSpec((8, tm), lambda i, *_: (0, i)),
                  pl.BlockSpec((2, tm, LANES), lambda i, *_: (0, i, 0))],
        out_specs=pl.BlockSpec(memory_space=pl.ANY),
        scratch_shapes=[pltpu.VMEM((2, LROWS, XROW), BF16), pltpu.VMEM((CH, XROW), BF16),
                        pltpu.SemaphoreType.DMA((3,))],
    )
    return pl.pallas_call(
        functools.partial(_dispatch_kernel, tm=tm),
        grid_spec=grid_spec,
        out_shape=jax.ShapeDtypeStruct((n_rows, XROW), BF16),
        compiler_params=_cparams(("arbitrary",)),
        name="moe_dispatch",
    )(plan["loff"], plan["goff"], plan["nch"], plan["gap_start"], plan["gap_nch"], h, metat, wparts)


def _experts_kernel(texp, n_used, x_ref, wg_ref, wu_ref, wd_ref, y_ref):
    used = pl.program_id(0) < n_used[0]

    @pl.when(used)
    def _():
        x = x_ref[:, :D_MODEL]
        wp = x_ref[:, D_MODEL:].astype(F32)
        gate = wp[:, 0:1] + wp[:, 1:2] + wp[:, 2:3]
        a = (jax.nn.silu(_dot(x, wg_ref[0])) * _dot(x, wu_ref[0])).astype(BF16)
        y_ref[...] = (gate * _dot(a, wd_ref[0])).astype(BF16)

    @pl.when(jnp.logical_not(used))
    def _():
        y_ref[...] = jnp.zeros_like(y_ref)


def _moe_experts(xs, wg, wu, wd, plan):
    n_rows = xs.shape[0]
    rows = lambda r, texp, n_used: (jnp.minimum(r, n_used[0] - 1), 0)
    wspec = lambda shape: pl.BlockSpec((1,) + shape, lambda r, texp, n_used: (texp[r], 0, 0))
    grid_spec = pltpu.PrefetchScalarGridSpec(
        num_scalar_prefetch=2,
        grid=(n_rows // TMR,),
        in_specs=[pl.BlockSpec((TMR, XROW), rows), wspec((D_MODEL, D_FF_EXPERT)),
                  wspec((D_MODEL, D_FF_EXPERT)), wspec((D_FF_EXPERT, D_MODEL))],
        out_specs=pl.BlockSpec((TMR, D_MODEL), lambda r, texp, n_used: (r, 0)),
    )
    return pl.pallas_call(
        _experts_kernel,
        grid_spec=grid_spec,
        out_shape=jax.ShapeDtypeStruct((n_rows, D_MODEL), BF16),
        compiler_params=_cparams(("arbitrary",)),
        name="moe_experts",
    )(plan["texp"], plan["n_used"], xs, wg, wu, wd)


def _combine_kernel(loff, goff, nch, h_ref, meta_ref, ys_ref, g_ref, b_ref, o_ref, yb_ref, sem, *, tm, alpha):
    i = pl.program_id(0)
    slot = i % 2

    @pl.when(i == 0)
    def _():
        yb_ref[...] = jnp.zeros_like(yb_ref)
        _start_segments(0, nch, goff, loff, ys_ref, yb_ref.at[0], sem.at[0])

    @pl.when(i + 1 < pl.num_programs(0))
    def _():
        _start_segments(i + 1, nch, goff, loff, ys_ref, yb_ref.at[1 - slot], sem.at[1 - slot])

    meta = meta_ref[...]
    r = lax.broadcasted_iota(jnp.int32, (tm, LROWS), 1).astype(F32)
    pick = ((r == meta[:, 0:1]) | (r == meta[:, 1:2])).astype(BF16)
    _wait_chunks(ys_ref, yb_ref.at[slot], sem.at[slot], _tile_chunks(i, nch))
    f = _dot(pick, yb_ref[slot])
    o_ref[...] = _ln(alpha * h_ref[...] + f, g_ref[...], b_ref[...])


def _moe_combine(h, meta, ys, plan, ln_g, ln_b, *, alpha):
    t = h.shape[0]
    tm = TMX
    grid_spec = pltpu.PrefetchScalarGridSpec(
        num_scalar_prefetch=3,
        grid=(t // tm,),
        in_specs=[pl.BlockSpec((tm, D_MODEL), lambda i, *_: (i, 0)),
                  pl.BlockSpec((tm, LANES), lambda i, *_: (i, 0)),
                  pl.BlockSpec(memory_space=pl.ANY),
                  pl.BlockSpec((1, D_MODEL), lambda i, *_: (0, 0)),
                  pl.BlockSpec((1, D_MODEL), lambda i, *_: (0, 0))],
        out_specs=pl.BlockSpec((tm, D_MODEL), lambda i, *_: (i, 0)),
        scratch_shapes=[pltpu.VMEM((2, LROWS, D_MODEL), BF16), pltpu.SemaphoreType.DMA((2,))],
    )
    return pl.pallas_call(
        functools.partial(_combine_kernel, tm=tm, alpha=alpha),
        grid_spec=grid_spec,
        out_shape=jax.ShapeDtypeStruct((t, D_MODEL), F32),
        compiler_params=_cparams(("arbitrary",)),
        name="moe_combine",
    )(plan["loff"], plan["goff"], plan["nch"], h, meta, ys, ln_g, ln_b)


def _moe(h, w_router, wg, wu, wd, ln_g, ln_b, *, alpha):
    t = h.shape[0]
    nt = t // TMX
    max_rows = 2 * t + nt * N_EXPERTS * (CH - 1) + N_EXPERTS * (TMR - CH)
    n_row_tiles = -(-max_rows // TMR)
    meta, metat, cnt, wparts = _moe_route(h, w_router)
    plan = _moe_plan(cnt, n_row_tiles)
    xs = _moe_dispatch(h, metat, wparts, plan, n_row_tiles * TMR)
    ys = _moe_experts(xs, wg, wu, wd, plan)
    return _moe_combine(h, meta, ys, plan, ln_g, ln_b, alpha=alpha)


def kernel(x, ln_in_g, ln_in_b, w_in, cmp_pos, cmp_w1, cmp_b1, cmp_w2, cmp_b2, conv_w, w_o, ln1_g, ln1_b, ln2_g, ln2_b, ffn_wg, ffn_wu, ffn_wd, moe_router, moe_wg, moe_wu, moe_wd):
    batch, seq, d = x.shape
    depth = w_in.shape[0]
    assert d == D_MODEL and seq % TM_PROJ == 0 and seq // CMP_STRIDE == LANES
    alpha = (2 * depth) ** 0.25
    t = batch * seq
    vec = lambda v: v.reshape(1, D_MODEL)
    h = x.reshape(t, d)
    for l in range(depth):
        w_r, w_vt = _prep_w_in(w_in[l])
        cw = jnp.pad(conv_w[l], ((0, 8 - CONV_W), (0, 0)))
        outs = _in_proj(h, vec(ln_in_g), vec(ln_in_b), w_r, w_vt, cw, seq=seq, pre_ln=(l == 0))
        if l == 0:
            h, outs = outs[0], outs[1:]
        q, kvc, ks, vs, kw, vw, gates, o_conv = outs
        kvcmp, kvcmp_t = _compress(kvc, cmp_pos[l], cmp_w1[l], cmp_b1[l], cmp_w2[l], cmp_b2[l],
                                   batch=batch, seq=seq)
        o_attn = _nsa_attn(q, kvcmp, kvcmp_t, ks, vs, kw, vw, gates, batch=batch, seq=seq)
        h = _out_proj(h, o_attn, o_conv, w_o[l].astype(BF16), vec(ln1_g[l]), vec(ln1_b[l]), alpha=alpha)
        if l % 2 == 0:
            h = _ffn(h, ffn_wg[l // 2].astype(BF16), ffn_wu[l // 2].astype(BF16), ffn_wd[l // 2].astype(BF16),
                     vec(ln2_g[l]), vec(ln2_b[l]), alpha=alpha)
        else:
            h = _moe(h, moe_router[l // 2], moe_wg[l // 2].astype(BF16), moe_wu[l // 2].astype(BF16),
                     moe_wd[l // 2].astype(BF16), vec(ln2_g[l]), vec(ln2_b[l]), alpha=alpha)
    return h.reshape(batch, seq, d)
```

```python
import functools

import numpy as np
import jax
import jax.numpy as jnp
from jax import lax
from jax.experimental import pallas as pl
from jax.experimental.pallas import tpu as pltpu

F32 = jnp.float32
BF16 = jnp.bfloat16

D_MODEL = 1024
HEAD_DIM = 64
N_HEADS = 8
N_KV = 2
HPG = N_HEADS // N_KV
ATTN_W = N_HEADS * HEAD_DIM
KV_W = N_KV * HEAD_DIM
CONV_CH = D_MODEL - ATTN_W
CONV_W = 3
CMP_BLOCK = 32
CMP_STRIDE = 16
CMP_HIDDEN = 256
SEL_BLOCK = 64
SEL_TOPN = 16
WINDOW = 512
D_FF = 2816
N_EXPERTS = 8
D_FF_EXPERT = 1408
LN_EPS = 1e-5
NEG = -1e30
FORCE = 1e9

LANES = 128
HPAD = LANES
Q_SCALE = HEAD_DIM ** -0.5 * np.log2(np.e)

C_Q = 0
C_KVC = C_Q + ATTN_W
C_KS = C_KVC + 2 * KV_W
C_KW = C_KS + KV_W
C_U = C_KW + KV_W
C_B = C_U + CONV_CH
C_C = C_B + CONV_CH
C_G = C_C + CONV_CH
C_END = C_G + LANES

TM_PROJ = 512
TQ = 256
TK = 256
VMEM_LIMIT = 56 * 1024 * 1024


def _cparams(sem):
    return pltpu.CompilerParams(dimension_semantics=sem, vmem_limit_bytes=VMEM_LIMIT)


def _ln(x, g, b):
    mu = jnp.mean(x, -1, keepdims=True)
    xc = x - mu
    var = jnp.mean(xc * xc, -1, keepdims=True)
    return xc * lax.rsqrt(var + LN_EPS) * g + b


def _dot(a, b):
    return jnp.dot(a, b, preferred_element_type=F32)


def _dot_nt(a, b):
    return lax.dot_general(a, b, (((1,), (1,)), ((), ())), preferred_element_type=F32)


def _resident(shape):
    nd = len(shape)
    return pl.BlockSpec(shape, lambda *_: (0,) * nd, pipeline_mode=pl.Buffered(1))


def _spread_heads(z):
    low = lax.broadcasted_iota(jnp.int32, (z.shape[0], LANES), 1) < HEAD_DIM
    tiles = []
    for p in range(z.shape[1] // LANES):
        pair = z[:, p * LANES:(p + 1) * LANES]
        tiles += [jnp.where(low, pair, 0.0), jnp.where(low, pltpu.roll(pair, HEAD_DIM, 1), 0.0)]
    return jnp.concatenate(tiles, axis=1)


def _project_inputs(h, w_refs, out_refs, scratch_refs, *, tiles_per_seq, tm):
    w_ref, wvt_ref, cw_ref = w_refs
    q_ref, kvc_ref, ks_ref, vs_ref, kw_ref, vw_ref, gate_ref, oc_ref = out_refs
    carry_ref, kvc_scr = scratch_refs
    hb = h.astype(BF16)

    def proj(lo, hi):
        return _dot(hb, w_ref[:, lo:hi])

    q_ref[...] = _spread_heads(proj(C_Q, C_KVC) * Q_SCALE).astype(BF16)

    kvc = proj(C_KVC, C_KS)
    low = lax.broadcasted_iota(jnp.int32, (tm // CMP_STRIDE, LANES), 1) < HEAD_DIM
    for kind in range(2):
        kvc_scr[kind] = kvc[:, kind * LANES:(kind + 1) * LANES]
        for p in range(CMP_STRIDE // 2):
            a = kvc_scr[kind, pl.ds(2 * p, tm // CMP_STRIDE, stride=CMP_STRIDE), :]
            b = kvc_scr[kind, pl.ds(2 * p + 1, tm // CMP_STRIDE, stride=CMP_STRIDE), :]
            kvc_ref[2 * kind, :, p * LANES:(p + 1) * LANES] = jnp.where(low, a, pltpu.roll(b, HEAD_DIM, 1))
            kvc_ref[2 * kind + 1, :, p * LANES:(p + 1) * LANES] = jnp.where(low, pltpu.roll(a, HEAD_DIM, 1), b)

    seq_tile = pl.program_id(0) % tiles_per_seq
    pos = lax.broadcasted_iota(jnp.int32, (tm, N_KV * HPAD), 0) + seq_tile * tm
    lane = lax.broadcasted_iota(jnp.int32, (tm, N_KV * HPAD), 1) % HPAD
    onehot = jnp.where(pos // SEL_BLOCK == lane - HEAD_DIM, 1.0, 0.0)
    in_tag = (lane >= HEAD_DIM) & (lane < HEAD_DIM + 32)
    ks_ref[...] = jnp.where(in_tag, onehot, _spread_heads(proj(C_KS, C_KW))).astype(BF16)
    kw_ref[...] = _spread_heads(proj(C_KW, C_U)).astype(BF16)

    vt = _dot_nt(wvt_ref[...], hb)
    tail = jnp.where(lax.broadcasted_iota(jnp.int32, (HPAD - HEAD_DIM, tm), 0) == 0, 1.0, 0.0)
    for n, ref in enumerate((vs_ref, vw_ref)):
        rows = []
        for grp in range(N_KV):
            lo = (n * N_KV + grp) * HEAD_DIM
            rows += [vt[lo:lo + HEAD_DIM], tail]
        full = jnp.concatenate(rows, axis=0).astype(BF16)
        for c in range(tm // TK):
            ref[c] = full[:, c * TK:(c + 1) * TK]

    cu = proj(C_C, C_G) * proj(C_U, C_B)

    @pl.when(seq_tile == 0)
    def _():
        carry_ref[...] = jnp.zeros_like(carry_ref)

    prev = carry_ref[...]
    row = lax.broadcasted_iota(jnp.int32, (tm, CONV_CH), 0)
    s1 = jnp.where(row == 0, prev[7:8], pltpu.roll(cu, 1, 0))
    s2 = jnp.where(row == 0, prev[6:7], jnp.where(row == 1, prev[7:8], pltpu.roll(cu, 2, 0)))
    y = s2 * cw_ref[0:1, :] + s1 * cw_ref[1:2, :] + cu * cw_ref[2:3, :]
    oc_ref[...] = (proj(C_B, C_C) * y).astype(BF16)
    carry_ref[...] = cu[tm - 8:tm]
    gate_ref[...] = jax.nn.sigmoid(proj(C_G, C_END))


N_PROJ_W = 3
N_PROJ_OUT = 8
N_PROJ_SCRATCH = 2


def _proj_specs(t, tm):
    row = lambda w: pl.BlockSpec((tm, w), lambda i: (i, 0))
    w_specs = [_resident((D_MODEL, C_END)), _resident((2 * KV_W, D_MODEL)), _resident((8, CONV_CH))]
    vt_shape = jax.ShapeDtypeStruct((t // TK, N_KV * HPAD, TK), BF16)
    vt_spec = pl.BlockSpec((tm // TK, N_KV * HPAD, TK), lambda i: (i, 0, 0))
    out_shape = [
        jax.ShapeDtypeStruct((t, N_HEADS * HPAD), BF16),
        jax.ShapeDtypeStruct((4, t // CMP_STRIDE, CMP_STRIDE * HEAD_DIM), F32),
        jax.ShapeDtypeStruct((t, N_KV * HPAD), BF16), vt_shape,
        jax.ShapeDtypeStruct((t, N_KV * HPAD), BF16), vt_shape,
        jax.ShapeDtypeStruct((t, LANES), F32),
        jax.ShapeDtypeStruct((t, CONV_CH), BF16),
    ]
    out_specs = [
        row(N_HEADS * HPAD),
        pl.BlockSpec((4, tm // CMP_STRIDE, CMP_STRIDE * HEAD_DIM), lambda i: (0, i, 0)),
        row(N_KV * HPAD), vt_spec, row(N_KV * HPAD), vt_spec,
        row(LANES), row(CONV_CH),
    ]
    scratch = [pltpu.VMEM((8, CONV_CH), F32), pltpu.VMEM((2, tm, LANES), F32)]
    return w_specs, out_shape, out_specs, scratch


def _in_proj_kernel(*refs, pre_ln, tiles_per_seq, tm):
    n_in = 3 if pre_ln else 1
    x_ref = refs[0]
    w_refs = refs[n_in:n_in + N_PROJ_W]
    outs = refs[n_in + N_PROJ_W:len(refs) - N_PROJ_SCRATCH]
    if pre_ln:
        h = _ln(x_ref[...], refs[1][...], refs[2][...])
        outs[0][...] = h
        outs = outs[1:]
    else:
        h = x_ref[...]
    _project_inputs(h, w_refs, outs, refs[len(refs) - N_PROJ_SCRATCH:], tiles_per_seq=tiles_per_seq, tm=tm)


def _in_proj(x, ln_g, ln_b, proj_w, *, seq, pre_ln):
    t = x.shape[0]
    tm = TM_PROJ
    row = pl.BlockSpec((tm, D_MODEL), lambda i: (i, 0))
    w_specs, out_shape, out_specs, scratch = _proj_specs(t, tm)
    in_specs, args = [row], [x]
    if pre_ln:
        in_specs += [_resident((1, D_MODEL)), _resident((1, D_MODEL))]
        args += [ln_g, ln_b]
        out_shape = [jax.ShapeDtypeStruct((t, D_MODEL), F32)] + out_shape
        out_specs = [row] + out_specs
    return pl.pallas_call(
        functools.partial(_in_proj_kernel, pre_ln=pre_ln, tiles_per_seq=seq // tm, tm=tm),
        grid=(t // tm,),
        in_specs=in_specs + w_specs,
        out_specs=out_specs,
        out_shape=out_shape,
        scratch_shapes=scratch,
        compiler_params=_cparams(("arbitrary",)),
        name="in_proj_ln" if pre_ln else "in_proj",
    )(*args, *proj_w)


def _prep_w_in(w_in):
    o = 0

    def take(n):
        nonlocal o
        s = w_in[:, o:o + n]
        o += n
        return s

    q, kc, vc, ks, vs, kw, vw = take(ATTN_W), *(take(KV_W) for _ in range(6))
    gates = jnp.pad(take(3 * N_HEADS), ((0, 0), (0, LANES - 3 * N_HEADS)))
    conv = take(3 * CONV_CH)
    w_r = jnp.concatenate([q, kc, vc, ks, kw, conv, gates], axis=1).astype(BF16)
    return w_r, jnp.concatenate([vs, vw], axis=1).T.astype(BF16)


def _compress_kernel(x_ref, pos_ref, w1_ref, b1_ref, w2_ref, b2_ref, w2t_ref, b2t_ref, o_ref, ot_ref):
    half = CMP_STRIDE * HEAD_DIM
    x = x_ref[0, 0]
    pos = pos_ref[0]
    xa = (x + pos[:, :half]).astype(BF16)
    xb = (x + pos[:, half:]).astype(BF16)
    a = _dot(xa, w1_ref[0, :half, :])
    b = _dot(xb, w1_ref[0, half:, :])
    n = x.shape[0]
    hid = a + pltpu.roll(b, n - 1, 0) + b1_ref[0]
    act = jax.nn.gelu(hid).astype(BF16)
    o_ref[0, 0] = (_dot(act, w2_ref[0]) + b2_ref[0]).astype(BF16)
    ot_ref[0, 0] = (_dot_nt(w2t_ref[0], act) + b2t_ref[0]).astype(BF16)


def _compress(kvc, cmp_pos, cmp_w1, cmp_b1, cmp_w2, cmp_b2, *, batch, seq):
    nb = seq // CMP_STRIDE
    half = CMP_STRIDE * HEAD_DIM
    x = kvc.reshape(4, batch, nb, half)
    pos = cmp_pos.reshape(2, 1, CMP_BLOCK * HEAD_DIM)
    w1 = cmp_w1.astype(BF16)
    b1 = cmp_b1.reshape(2, 1, CMP_HIDDEN)
    w2 = jnp.pad(cmp_w2, ((0, 0), (0, 0), (0, HPAD - HEAD_DIM))).astype(BF16)
    b2 = jnp.pad(cmp_b2, ((0, 0), (0, HPAD - HEAD_DIM))).reshape(2, 1, HPAD)
    w2t = jnp.swapaxes(w2, 1, 2)
    b2t = b2.reshape(2, HPAD, 1)
    kind = lambda j, b: (j // N_KV, 0, 0)
    return pl.pallas_call(
        _compress_kernel,
        grid=(4, batch),
        in_specs=[
            pl.BlockSpec((1, 1, nb, half), lambda j, b: (j, b, 0, 0)),
            pl.BlockSpec((1, 1, 2 * half), kind),
            pl.BlockSpec((1, 2 * half, CMP_HIDDEN), kind),
            pl.BlockSpec((1, 1, CMP_HIDDEN), kind),
            pl.BlockSpec((1, CMP_HIDDEN, HPAD), kind),
            pl.BlockSpec((1, 1, HPAD), kind),
            pl.BlockSpec((1, HPAD, CMP_HIDDEN), kind),
            pl.BlockSpec((1, HPAD, 1), kind),
        ],
        out_specs=[pl.BlockSpec((1, 1, nb, HPAD), lambda j, b: (j, b, 0, 0)),
                   pl.BlockSpec((1, 1, HPAD, nb), lambda j, b: (j, b, 0, 0))],
        out_shape=[jax.ShapeDtypeStruct((4, batch, nb, HPAD), BF16),
                   jax.ShapeDtypeStruct((4, batch, HPAD, nb), BF16)],
        compiler_params=_cparams(("arbitrary", "arbitrary")),
        name="compress",
    )(x, pos, w1, b1, w2, b2, w2t, b2t)


def _overlap_t():
    n = np.arange(LANES)
    j = np.arange(32)
    cs = n * CMP_STRIDE
    ss = j * SEL_BLOCK
    ov = (cs[None, :] < ss[:, None] + SEL_BLOCK) & (cs[None, :] + CMP_BLOCK > ss[:, None])
    return jnp.asarray(ov, dtype=BF16)


def _cmp_select(qh, kc, vct, ov, i, tq, n_sel):
    n = lax.broadcasted_iota(jnp.int32, (LANES, tq), 0)
    t = lax.broadcasted_iota(jnp.int32, (LANES, tq), 1) + i * tq
    mask = n * CMP_STRIDE + (CMP_BLOCK - 1) <= t
    maskf = mask.astype(F32)
    ss = [jnp.where(mask, _dot_nt(kc, q), NEG) for q in qh]
    es = [jnp.exp2(s - jnp.max(s, axis=0, keepdims=True)) for s in ss]
    ps = [e / jnp.sum(e, axis=0, keepdims=True) * maskf for e in es]
    heads_t = [_dot(vct, p.astype(BF16))[:HEAD_DIM] for p in ps]
    psum = functools.reduce(lambda a, b: a + b, ps)

    p_hi = psum.astype(BF16)
    r1 = psum - p_hi.astype(F32)
    p_mid = r1.astype(BF16)
    p_lo = (r1 - p_mid.astype(F32)).astype(BF16)
    imp = _dot(ov, p_hi) + _dot(ov, p_mid) + _dot(ov, p_lo)

    jt = lax.broadcasted_iota(jnp.int32, (n_sel, tq), 0)
    tt = lax.broadcasted_iota(jnp.int32, (n_sel, tq), 1) + i * tq
    cur = tt // SEL_BLOCK
    valid = jt * SEL_BLOCK <= tt
    forced = (jt == 0) | (jt == cur) | (jt == cur - 1)
    score = jnp.where(forced, FORCE, jnp.where(valid, imp, NEG))
    rank = jnp.zeros((n_sel, tq), jnp.int32)
    for k in range(n_sel):
        sk = score[k:k + 1, :]
        ahead = (sk > score) | ((sk == score) & (jt > k))
        rank = rank + ahead.astype(jnp.int32)
    bias = jnp.where(rank < SEL_TOPN, 0.0, NEG)
    full = jnp.concatenate(
        [jnp.zeros((HEAD_DIM, tq), F32), bias, jnp.zeros((HPAD - HEAD_DIM - n_sel, tq), F32)], axis=0)
    return heads_t, full.T.astype(BF16)


def _attn_scores(k, qa):
    return tuple(_dot_nt(k, qa[hh]) for hh in range(HPG))


def _attn_update(ss, vt, state, mask):
    if mask is not None:
        ss = [jnp.where(mask, s, NEG) for s in ss]
    m_new = [jnp.maximum(state[hh][0], jnp.max(ss[hh], axis=0, keepdims=True)) for hh in range(HPG)]
    ps = [jnp.exp2(ss[hh] - m_new[hh]).astype(BF16) for hh in range(HPG)]
    pv = [_dot(vt, ps[hh]) for hh in range(HPG)]
    return tuple((m_new[hh], jnp.exp2(state[hh][0] - m_new[hh]) * state[hh][1] + pv[hh]) for hh in range(HPG))


def _attn_init(tq):
    return tuple((jnp.full((1, tq), NEG, F32), jnp.zeros((HPAD, tq), F32)) for _ in range(HPG))


def _nsa_attn_kernel(q_ref, kc_ref, vct_ref, ov_ref, ks_ref, vst_ref, kw_ref, vwt_ref, gate_ref, o_ref,
                     *, tq, tk, n_sel):
    g = pl.program_id(1)
    i = pl.program_id(2)
    nq = ks_ref.shape[0] // tk
    qh = [q_ref[:, hh * HPAD:(hh + 1) * HPAD] for hh in range(HPG)]
    heads_cmp, selb = _cmp_select(qh, kc_ref[0, 0], vct_ref[0, 0], ov_ref[...], i, tq, n_sel)
    lane = lax.broadcasted_iota(jnp.int32, selb.shape, 1)
    qsel = [jnp.where(lane >= HEAD_DIM, selb, q) for q in qh]
    key = lax.broadcasted_iota(jnp.int32, (tk, tq), 0)
    qry = lax.broadcasted_iota(jnp.int32, (tk, tq), 1)
    causal = key <= qry
    gt = gate_ref[...].T
    sub = lax.broadcasted_iota(jnp.int32, gt.shape, 0)

    def gate_row(idx):
        return jnp.sum(jnp.where(sub == idx, gt, 0.0), axis=0, keepdims=True)

    key2 = lax.broadcasted_iota(jnp.int32, (2 * tk, tq), 0)
    pair_causal = key2 - tk <= lax.broadcasted_iota(jnp.int32, (2 * tk, tq), 1)

    def run(n):
        chunks = []
        for j in range(0, n, 2):
            chunks.append(("sel", j, 2, pair_causal if j + 1 == n else None))
        if n % 2 == 0:
            chunks.append(("sel", n, 1, causal))
        if n == 0:
            chunks.append(("win", 0, 1, causal))
        else:
            chunks.append(("win", n - 1, 2, pair_causal))
        if n >= 2:
            chunks.append(("win", n - 2, 1, key > qry))
        refs = {"sel": (ks_ref, vst_ref, qsel), "win": (kw_ref, vwt_ref, qh)}

        def scores(chunk):
            branch, j, nt, _ = chunk
            return _attn_scores(refs[branch][0][j * tk:(j + nt) * tk, :], refs[branch][2])

        state = {"sel": _attn_init(tq), "win": _attn_init(tq)}
        ss = scores(chunks[0])
        for idx, (branch, j, nt, mask) in enumerate(chunks):
            ss_next = scores(chunks[idx + 1]) if idx + 1 < len(chunks) else None
            vt_ref = refs[branch][1]
            vt = vt_ref[j] if nt == 1 else jnp.concatenate([vt_ref[j], vt_ref[j + 1]], axis=1)
            state[branch] = _attn_update(ss, vt, state[branch], mask)
            ss = ss_next

        outs = []
        for hh in range(HPG):
            o_sel, o_win = (acc[:HEAD_DIM] / acc[HEAD_DIM:HEAD_DIM + 1]
                            for _, acc in (state["sel"][hh], state["win"][hh]))
            head = g * HPG + hh
            outs.append(gate_row(head) * heads_cmp[hh] + gate_row(N_HEADS + head) * o_sel
                        + gate_row(2 * N_HEADS + head) * o_win)
        o_ref[...] = jnp.concatenate(outs, axis=0).T.astype(o_ref.dtype)

    for n in range(nq):
        pl.when(i == n)(functools.partial(run, n))


def _nsa_attn(q, kvcmp, kvcmp_t, ks, vs_t, kw, vw_t, gates, *, batch, seq):
    t = q.shape[0]
    tq, tk = TQ, TK
    assert tq == tk and WINDOW == 2 * tk
    nq = seq // tq
    n_sel = seq // SEL_BLOCK
    nb = kvcmp.shape[2]
    assert nb == LANES and n_sel == 32
    rowblk = lambda w: pl.BlockSpec((tq, w), lambda b, g, i: (b * nq + i, g))
    seqblk = pl.BlockSpec((seq, HPAD), lambda b, g, i: (b, g))
    vtblk = pl.BlockSpec((seq // tk, HPAD, tk), lambda b, g, i: (b, g, 0))
    return pl.pallas_call(
        functools.partial(_nsa_attn_kernel, tq=tq, tk=tk, n_sel=n_sel),
        grid=(batch, N_KV, nq),
        in_specs=[rowblk(HPG * HPAD),
                  pl.BlockSpec((1, 1, nb, HPAD), lambda b, g, i: (g, b, 0, 0)),
                  pl.BlockSpec((1, 1, HPAD, nb), lambda b, g, i: (N_KV + g, b, 0, 0)),
                  pl.BlockSpec((n_sel, LANES), lambda b, g, i: (0, 0)),
                  seqblk, vtblk, seqblk, vtblk,
                  pl.BlockSpec((tq, LANES), lambda b, g, i: (b * nq + i, 0))],
        out_specs=rowblk(HPG * HEAD_DIM),
        out_shape=jax.ShapeDtypeStruct((t, ATTN_W), BF16),
        compiler_params=_cparams(("arbitrary", "arbitrary", "arbitrary")),
        name="nsa_attn",
    )(q, kvcmp, kvcmp_t, _overlap_t(), ks, vs_t, kw, vw_t, gates)


def _dense_tail_kernel(*refs, alpha, with_proj, tiles_per_seq, tm):
    (h_ref, oa_ref, ocv_ref, wo_ref, g1_ref, b1_ref, wg_ref, wu_ref, wd_ref, g2_ref, b2_ref) = refs[:11]
    m = _dot(oa_ref[...], wo_ref[:ATTN_W, :]) + _dot(ocv_ref[...], wo_ref[ATTN_W:, :])
    h1 = _ln(alpha * h_ref[...] + m, g1_ref[...], b1_ref[...])
    hb = h1.astype(BF16)
    a = (jax.nn.silu(_dot(hb, wg_ref[...])) * _dot(hb, wu_ref[...])).astype(BF16)
    h2 = _ln(alpha * h1 + _dot(a, wd_ref[...]), g2_ref[...], b2_ref[...])
    if with_proj:
        w_refs = refs[11:11 + N_PROJ_W]
        outs = refs[11 + N_PROJ_W:len(refs) - N_PROJ_SCRATCH]
        outs[0][...] = h2
        _project_inputs(h2, w_refs, outs[1:], refs[len(refs) - N_PROJ_SCRATCH:], tiles_per_seq=tiles_per_seq, tm=tm)
    else:
        refs[11][...] = h2


def _dense_tail(h, o_attn, o_conv, w_o, ln1, ffn_w, ln2, proj_w, *, alpha, seq):
    t = h.shape[0]
    tm = TM_PROJ
    row = lambda w: pl.BlockSpec((tm, w), lambda i: (i, 0))
    vec = _resident((1, D_MODEL))
    in_specs = [row(D_MODEL), row(ATTN_W), row(CONV_CH), _resident((D_MODEL, D_MODEL)), vec, vec,
                _resident((D_MODEL, D_FF)), _resident((D_MODEL, D_FF)), _resident((D_FF, D_MODEL)), vec, vec]
    args = [h, o_attn, o_conv, w_o, *ln1, *ffn_w, *ln2]
    out_shape, out_specs, scratch = [jax.ShapeDtypeStruct((t, D_MODEL), F32)], [row(D_MODEL)], []
    if proj_w is not None:
        w_specs, p_shape, p_specs, scratch = _proj_specs(t, tm)
        in_specs += w_specs
        args += list(proj_w)
        out_shape += p_shape
        out_specs += p_specs
    return pl.pallas_call(
        functools.partial(_dense_tail_kernel, alpha=alpha, with_proj=proj_w is not None,
                          tiles_per_seq=seq // tm, tm=tm),
        grid=(t // tm,),
        in_specs=in_specs,
        out_specs=out_specs,
        out_shape=out_shape,
        scratch_shapes=scratch,
        compiler_params=_cparams(("arbitrary",)),
        name="dense_tail",
    )(*args)


TMX = 512
CH = 16
LROWS = 2 * TMX + N_EXPERTS * CH
TMR = 512
XROW = D_MODEL + LANES


def _route_kernel(h_ref, oa_ref, ocv_ref, wo_ref, g1_ref, b1_ref, wr_ref, ltri_ref, ustr_ref,
                  h1_ref, meta_ref, metat_ref, cnt_ref, wp_ref, *, tm, alpha):
    m = _dot(oa_ref[...], wo_ref[:ATTN_W, :]) + _dot(ocv_ref[...], wo_ref[ATTN_W:, :])
    h = _ln(alpha * h_ref[...] + m, g1_ref[...], b1_ref[...])
    h1_ref[...] = h
    h_hi = h.astype(BF16)
    h_lo = (h - h_hi.astype(F32)).astype(BF16)
    logits = _dot(h_hi, wr_ref[0]) + _dot(h_lo, wr_ref[0]) + _dot(h_hi, wr_ref[1])
    lane = lax.broadcasted_iota(jnp.int32, logits.shape, 1)
    logits = jnp.where(lane < N_EXPERTS, logits, -jnp.inf)
    m1 = jnp.max(logits, axis=-1, keepdims=True)
    i1 = jnp.min(jnp.where(logits == m1, lane, LANES), axis=-1, keepdims=True)
    rest = jnp.where(lane == i1, -jnp.inf, logits)
    m2 = jnp.max(rest, axis=-1, keepdims=True)
    i2 = jnp.min(jnp.where(rest == m2, lane, LANES), axis=-1, keepdims=True)
    e2 = jnp.exp(m2 - m1)
    den = 1.0 + e2
    w1 = 1.0 / den
    w2 = e2 / den

    routed = (lane == i1) | (lane == i2)
    cnt = _dot(ltri_ref[...], routed.astype(BF16))
    n = cnt[tm - 1:tm, :]
    padded = jnp.floor((n + (CH - 1)) * (1.0 / CH)) * CH
    seg_off = _dot(jnp.broadcast_to(padded, (8, LANES)).astype(BF16), ustr_ref[...])[0:1]
    dest = seg_off + cnt - 1.0
    d1 = jnp.sum(jnp.where(lane == i1, dest, 0.0), axis=-1, keepdims=True)
    d2 = jnp.sum(jnp.where(lane == i2, dest, 0.0), axis=-1, keepdims=True)
    meta = jnp.where(lane == 0, d1, jnp.where(lane == 1, d2, jnp.where(lane == 2, w1, jnp.where(lane == 3, w2, 0.0))))
    meta_ref[...] = meta
    metat_ref[...] = meta.T[0:8, :]
    cnt_ref[0] = jnp.broadcast_to(n, (8, LANES))
    for k, w in enumerate((w1, w2)):
        hi = w.astype(BF16).astype(F32)
        mid = (w - hi).astype(BF16).astype(F32)
        lo = (w - hi - mid).astype(BF16).astype(F32)
        wp_ref[k] = jnp.where(lane == 0, hi, jnp.where(lane == 1, mid, jnp.where(lane == 2, lo, 0.0))).astype(BF16)


def _moe_route(h, o_attn, o_conv, w_o, ln1, w_router, *, alpha):
    t = h.shape[0]
    tm = TMX
    nt = t // tm
    row = lambda w: pl.BlockSpec((tm, w), lambda i: (i, 0))
    wr = jnp.pad(w_router, ((0, 0), (0, LANES - N_EXPERTS)))
    wr_hi = wr.astype(BF16)
    wr_lo = (wr - wr_hi.astype(F32)).astype(BF16)
    wr2 = jnp.stack([wr_hi, wr_lo])
    ltri = jnp.asarray(np.tril(np.ones((tm, tm), np.float32)), dtype=BF16)
    ustr = jnp.asarray(np.triu(np.ones((LANES, LANES), np.float32), 1), dtype=BF16)
    return pl.pallas_call(
        functools.partial(_route_kernel, tm=tm, alpha=alpha),
        grid=(nt,),
        in_specs=[row(D_MODEL), row(ATTN_W), row(CONV_CH), _resident((D_MODEL, D_MODEL)),
                  _resident((1, D_MODEL)), _resident((1, D_MODEL)), _resident((2, D_MODEL, LANES)),
                  _resident((tm, tm)), _resident((LANES, LANES))],
        out_specs=[row(D_MODEL), row(LANES),
                   pl.BlockSpec((8, tm), lambda i: (0, i)),
                   pl.BlockSpec((1, 8, LANES), lambda i: (i, 0, 0)),
                   pl.BlockSpec((2, tm, LANES), lambda i: (0, i, 0))],
        out_shape=[jax.ShapeDtypeStruct((t, D_MODEL), F32),
                   jax.ShapeDtypeStruct((t, LANES), F32), jax.ShapeDtypeStruct((8, t), F32),
                   jax.ShapeDtypeStruct((nt, 8, LANES), F32), jax.ShapeDtypeStruct((2, t, LANES), BF16)],
        compiler_params=_cparams(("arbitrary",)),
        name="moe_route",
    )(h, o_attn, o_conv, w_o, *ln1, wr2, ltri, ustr)


def _moe_plan(cnt, n_row_tiles):
    n = cnt[:, 0, :N_EXPERTS].astype(jnp.int32)
    p = (n + CH - 1) // CH * CH
    tot = p.sum(0)
    tot_pad = (tot + TMR - 1) // TMR * TMR
    gend = jnp.cumsum(tot_pad)
    gstart = gend - tot_pad
    goff = gstart[None, :] + jnp.cumsum(p, 0) - p
    loff = jnp.cumsum(p, 1) - p
    n_used = gend[-1:] // TMR
    tile_start = jnp.arange(n_row_tiles, dtype=jnp.int32) * TMR
    texp = jnp.minimum(jnp.sum(tile_start[:, None] >= gend[None, :], axis=1), N_EXPERTS - 1).astype(jnp.int32)
    texp = jnp.where(jnp.arange(n_row_tiles) < n_used[0], texp, texp[jnp.maximum(n_used[0] - 1, 0)])
    flat = lambda a: a.reshape(-1).astype(jnp.int32)
    gap_start = jnp.concatenate([gstart + tot, gend[-1:]])
    gap_rows = jnp.concatenate([tot_pad - tot, n_row_tiles * TMR - gend[-1:]])
    return dict(loff=flat(loff), goff=flat(goff), nch=flat(p // CH), gap_start=flat(gap_start),
                gap_nch=flat(gap_rows // CH), texp=flat(texp), n_used=flat(n_used))


def _chunk_copies(src_ref, dst_ref, sem, src_off, dst_off, n, src_step=CH):
    def body(c, carry):
        s = pl.multiple_of(src_off + c * src_step, CH)
        d = pl.multiple_of(dst_off + c * CH, CH)
        pltpu.make_async_copy(src_ref.at[pl.ds(s, CH)], dst_ref.at[pl.ds(d, CH)], sem).start()
        return carry
    lax.fori_loop(0, n, body, 0)


def _wait_chunks(src_ref, dst_ref, sem, n):
    def body(c, carry):
        pltpu.make_async_copy(src_ref.at[pl.ds(0, CH)], dst_ref.at[pl.ds(0, CH)], sem).wait()
        return carry
    lax.fori_loop(0, n, body, 0)


def _tile_chunks(tile, nch):
    total = 0
    for e in range(N_EXPERTS):
        total = total + nch[tile * N_EXPERTS + e]
    return total


def _start_segments(tile, nch, src_off, dst_off, src_ref, dst_ref, sem):
    for e in range(N_EXPERTS):
        idx = tile * N_EXPERTS + e
        _chunk_copies(src_ref, dst_ref, sem, src_off[idx], dst_off[idx], nch[idx])


def _dispatch_kernel(loff, goff, nch, gap_start, gap_nch, h_ref, metat_ref, wp_ref, xs_ref, xc_ref, z_ref, sem,
                     *, tm):
    i = pl.program_id(0)
    last = pl.num_programs(0) - 1
    slot = i % 2
    r = lax.broadcasted_iota(jnp.int32, (LROWS, tm), 0).astype(F32)
    pick1 = (r == metat_ref[0:1, :]).astype(BF16)
    pick2 = (r == metat_ref[1:2, :]).astype(BF16)
    xc_ref[slot, :, :D_MODEL] = _dot(pick1 + pick2, h_ref[...].astype(BF16)).astype(BF16)
    xc_ref[slot, :, D_MODEL:] = (_dot(pick1, wp_ref[0]) + _dot(pick2, wp_ref[1])).astype(BF16)
    _start_segments(i, nch, loff, goff, xc_ref.at[slot], xs_ref, sem.at[slot])

    @pl.when(i > 0)
    def _():
        _wait_chunks(xc_ref.at[1 - slot], xs_ref, sem.at[1 - slot], _tile_chunks(i - 1, nch))

    @pl.when(i == last)
    def _():
        z_ref[...] = jnp.zeros_like(z_ref)
        gaps = 0
        for e in range(N_EXPERTS + 1):
            _chunk_copies(z_ref, xs_ref, sem.at[2], 0, gap_start[e], gap_nch[e], src_step=0)
            gaps = gaps + gap_nch[e]
        _wait_chunks(z_ref, xs_ref, sem.at[2], gaps)
        _wait_chunks(xc_ref.at[slot], xs_ref, sem.at[slot], _tile_chunks(i, nch))


def _moe_dispatch(h, metat, wparts, plan, n_rows):
    t = h.shape[0]
    tm = TMX
    grid_spec = pltpu.PrefetchScalarGridSpec(
        num_scalar_prefetch=5,
        grid=(t // tm,),
        in_specs=[pl.BlockSpec((tm, D_MODEL), lambda i, *_: (i, 0)),
                  pl.BlockSpec((8, tm), lambda i, *_: (0, i)),
                  pl.BlockSpec((2, tm, LANES), lambda i, *_: (0, i, 0))],
        out_specs=pl.BlockSpec(memory_space=pl.ANY),
        scratch_shapes=[pltpu.VMEM((2, LROWS, XROW), BF16), pltpu.VMEM((CH, XROW), BF16),
                        pltpu.SemaphoreType.DMA((3,))],
    )
    return pl.pallas_call(
        functools.partial(_dispatch_kernel, tm=tm),
        grid_spec=grid_spec,
        out_shape=jax.ShapeDtypeStruct((n_rows, XROW), BF16),
        compiler_params=_cparams(("arbitrary",)),
        name="moe_dispatch",
    )(plan["loff"], plan["goff"], plan["nch"], plan["gap_start"], plan["gap_nch"], h, metat, wparts)


def _experts_kernel(texp, n_used, x_ref, wg_ref, wu_ref, wd_ref, y_ref):
    used = pl.program_id(0) < n_used[0]

    @pl.when(used)
    def _():
        x = x_ref[:, :D_MODEL]
        wp = x_ref[:, D_MODEL:].astype(F32)
        gate = wp[:, 0:1] + wp[:, 1:2] + wp[:, 2:3]
        a = (jax.nn.silu(_dot(x, wg_ref[0])) * _dot(x, wu_ref[0])).astype(BF16)
        y_ref[...] = (gate * _dot(a, wd_ref[0])).astype(BF16)

    @pl.when(jnp.logical_not(used))
    def _():
        y_ref[...] = jnp.zeros_like(y_ref)


def _moe_experts(xs, wg, wu, wd, plan):
    n_rows = xs.shape[0]
    rows = lambda r, texp, n_used: (jnp.minimum(r, n_used[0] - 1), 0)
    wspec = lambda shape: pl.BlockSpec((1,) + shape, lambda r, texp, n_used: (texp[r], 0, 0))
    grid_spec = pltpu.PrefetchScalarGridSpec(
        num_scalar_prefetch=2,
        grid=(n_rows // TMR,),
        in_specs=[pl.BlockSpec((TMR, XROW), rows), wspec((D_MODEL, D_FF_EXPERT)),
                  wspec((D_MODEL, D_FF_EXPERT)), wspec((D_FF_EXPERT, D_MODEL))],
        out_specs=pl.BlockSpec((TMR, D_MODEL), lambda r, texp, n_used: (r, 0)),
    )
    return pl.pallas_call(
        _experts_kernel,
        grid_spec=grid_spec,
        out_shape=jax.ShapeDtypeStruct((n_rows, D_MODEL), BF16),
        compiler_params=_cparams(("arbitrary",)),
        name="moe_experts",
    )(plan["texp"], plan["n_used"], xs, wg, wu, wd)


def _combine_kernel(loff, goff, nch, h_ref, meta_ref, ys_ref, g_ref, b_ref, o_ref, yb_ref, sem, *, tm, alpha):
    i = pl.program_id(0)
    slot = i % 2

    @pl.when(i == 0)
    def _():
        yb_ref[...] = jnp.zeros_like(yb_ref)
        _start_segments(0, nch, goff, loff, ys_ref, yb_ref.at[0], sem.at[0])

    @pl.when(i + 1 < pl.num_programs(0))
    def _():
        _start_segments(i + 1, nch, goff, loff, ys_ref, yb_ref.at[1 - slot], sem.at[1 - slot])

    meta = meta_ref[...]
    r = lax.broadcasted_iota(jnp.int32, (tm, LROWS), 1).astype(F32)
    pick = ((r == meta[:, 0:1]) | (r == meta[:, 1:2])).astype(BF16)
    _wait_chunks(ys_ref, yb_ref.at[slot], sem.at[slot], _tile_chunks(i, nch))
    f = _dot(pick, yb_ref[slot])
    o_ref[...] = _ln(alpha * h_ref[...] + f, g_ref[...], b_ref[...])


def _moe_combine(h, meta, ys, plan, ln_g, ln_b, *, alpha):
    t = h.shape[0]
    tm = TMX
    grid_spec = pltpu.PrefetchScalarGridSpec(
        num_scalar_prefetch=3,
        grid=(t // tm,),
        in_specs=[pl.BlockSpec((tm, D_MODEL), lambda i, *_: (i, 0)),
                  pl.BlockSpec((tm, LANES), lambda i, *_: (i, 0)),
                  pl.BlockSpec(memory_space=pl.ANY),
                  pl.BlockSpec((1, D_MODEL), lambda i, *_: (0, 0)),
                  pl.BlockSpec((1, D_MODEL), lambda i, *_: (0, 0))],
        out_specs=pl.BlockSpec((tm, D_MODEL), lambda i, *_: (i, 0)),
        scratch_shapes=[pltpu.VMEM((2, LROWS, D_MODEL), BF16), pltpu.SemaphoreType.DMA((2,))],
    )
    return pl.pallas_call(
        functools.partial(_combine_kernel, tm=tm, alpha=alpha),
        grid_spec=grid_spec,
        out_shape=jax.ShapeDtypeStruct((t, D_MODEL), F32),
        compiler_params=_cparams(("arbitrary",)),
        name="moe_combine",
    )(plan["loff"], plan["goff"], plan["nch"], h, meta, ys, ln_g, ln_b)


def _moe_tail(h, o_attn, o_conv, w_o, ln1, w_router, wg, wu, wd, ln_g, ln_b, *, alpha):
    t = h.shape[0]
    nt = t // TMX
    max_rows = 2 * t + nt * N_EXPERTS * (CH - 1) + N_EXPERTS * (TMR - CH)
    n_row_tiles = -(-max_rows // TMR)
    h, meta, metat, cnt, wparts = _moe_route(h, o_attn, o_conv, w_o, ln1, w_router, alpha=alpha)
    plan = _moe_plan(cnt, n_row_tiles)
    xs = _moe_dispatch(h, metat, wparts, plan, n_row_tiles * TMR)
    ys = _moe_experts(xs, wg, wu, wd, plan)
    return _moe_combine(h, meta, ys, plan, ln_g, ln_b, alpha=alpha)


def kernel(x, ln_in_g, ln_in_b, w_in, cmp_pos, cmp_w1, cmp_b1, cmp_w2, cmp_b2, conv_w, w_o, ln1_g, ln1_b, ln2_g, ln2_b, ffn_wg, ffn_wu, ffn_wd, moe_router, moe_wg, moe_wu, moe_wd):
    batch, seq, d = x.shape
    depth = w_in.shape[0]
    assert d == D_MODEL and seq % TM_PROJ == 0 and seq // CMP_STRIDE == LANES
    alpha = (2 * depth) ** 0.25
    t = batch * seq
    vec = lambda v: v.reshape(1, D_MODEL)
    def proj_weights(l):
        return (*_prep_w_in(w_in[l]), jnp.pad(conv_w[l], ((0, 8 - CONV_W), (0, 0))))

    h, *proj = _in_proj(x.reshape(t, d), vec(ln_in_g), vec(ln_in_b), proj_weights(0), seq=seq, pre_ln=True)
    for l in range(depth):
        q, kvc, ks, vs, kw, vw, gates, o_conv = proj
        kvcmp, kvcmp_t = _compress(kvc, cmp_pos[l], cmp_w1[l], cmp_b1[l], cmp_w2[l], cmp_b2[l],
                                   batch=batch, seq=seq)
        o_attn = _nsa_attn(q, kvcmp, kvcmp_t, ks, vs, kw, vw, gates, batch=batch, seq=seq)
        ln1 = (vec(ln1_g[l]), vec(ln1_b[l]))
        ln2 = (vec(ln2_g[l]), vec(ln2_b[l]))
        next_w = proj_weights(l + 1) if l + 1 < depth else None
        if l % 2 == 0:
            ffn_w = (ffn_wg[l // 2].astype(BF16), ffn_wu[l // 2].astype(BF16), ffn_wd[l // 2].astype(BF16))
            h, *proj = _dense_tail(h, o_attn, o_conv, w_o[l].astype(BF16), ln1, ffn_w, ln2, next_w,
                                   alpha=alpha, seq=seq)
        else:
            h = _moe_tail(h, o_attn, o_conv, w_o[l].astype(BF16), ln1, moe_router[l // 2],
                          moe_wg[l // 2].astype(BF16), moe_wu[l // 2].astype(BF16), moe_wd[l // 2].astype(BF16),
                          *ln2, alpha=alpha)
            if next_w is not None:
                proj = _in_proj(h, None, None, next_w, seq=seq, pre_ln=False)
    return h.reshape(batch, seq, d)
```

```python
import functools

import numpy as np
import jax
import jax.numpy as jnp
from jax import lax
from jax.experimental import pallas as pl
from jax.experimental.pallas import tpu as pltpu

F32 = jnp.float32
BF16 = jnp.bfloat16

D_MODEL = 1024
HEAD_DIM = 64
N_HEADS = 8
N_KV = 2
HPG = N_HEADS // N_KV
ATTN_W = N_HEADS * HEAD_DIM
KV_W = N_KV * HEAD_DIM
CONV_CH = D_MODEL - ATTN_W
CONV_W = 3
CMP_BLOCK = 32
CMP_STRIDE = 16
CMP_HIDDEN = 256
SEL_BLOCK = 64
SEL_TOPN = 16
WINDOW = 512
D_FF = 2816
N_EXPERTS = 8
D_FF_EXPERT = 1408
LN_EPS = 1e-5
NEG = -1e30
FORCE = 1e9

LANES = 128
HPAD = LANES
Q_SCALE = HEAD_DIM ** -0.5 * np.log2(np.e)

C_Q = 0
C_KVC = C_Q + ATTN_W
C_KS = C_KVC + 2 * KV_W
C_KW = C_KS + KV_W
C_U = C_KW + KV_W
C_B = C_U + CONV_CH
C_C = C_B + CONV_CH
C_G = C_C + CONV_CH
C_END = C_G + N_KV * LANES

TM_PROJ = 512
TQ = 256
TK = 256
VMEM_LIMIT = 56 * 1024 * 1024


def _cparams(sem):
    return pltpu.CompilerParams(dimension_semantics=sem, vmem_limit_bytes=VMEM_LIMIT)


def _ln(x, g, b):
    mu = jnp.mean(x, -1, keepdims=True)
    xc = x - mu
    var = jnp.mean(xc * xc, -1, keepdims=True)
    return xc * lax.rsqrt(var + LN_EPS) * g + b


def _dot(a, b):
    return jnp.dot(a, b, preferred_element_type=F32)


def _dot_nt(a, b):
    return lax.dot_general(a, b, (((1,), (1,)), ((), ())), preferred_element_type=F32)


def _resident(shape):
    nd = len(shape)
    return pl.BlockSpec(shape, lambda *_: (0,) * nd, pipeline_mode=pl.Buffered(1))


def _spread_heads(z):
    low = lax.broadcasted_iota(jnp.int32, (z.shape[0], LANES), 1) < HEAD_DIM
    tiles = []
    for p in range(z.shape[1] // LANES):
        pair = z[:, p * LANES:(p + 1) * LANES]
        tiles += [jnp.where(low, pair, 0.0), jnp.where(low, pltpu.roll(pair, HEAD_DIM, 1), 0.0)]
    return jnp.concatenate(tiles, axis=1)


def _project_inputs(h, w_refs, out_refs, scratch_refs, *, tiles_per_seq, tm):
    w_ref, wvt_ref, cw_ref = w_refs
    q_ref, kvc_ref, ks_ref, vs_ref, kw_ref, vw_ref, gate_ref, oc_ref = out_refs
    carry_ref, kvc_scr = scratch_refs
    hb = h.astype(BF16)

    def proj(lo, hi):
        return _dot(hb, w_ref[:, lo:hi])

    q_ref[...] = _spread_heads(proj(C_Q, C_KVC) * Q_SCALE).astype(BF16)

    kvc = proj(C_KVC, C_KS)
    low = lax.broadcasted_iota(jnp.int32, (tm // CMP_STRIDE, LANES), 1) < HEAD_DIM
    for kind in range(2):
        kvc_scr[kind] = kvc[:, kind * LANES:(kind + 1) * LANES]
        for p in range(CMP_STRIDE // 2):
            a = kvc_scr[kind, pl.ds(2 * p, tm // CMP_STRIDE, stride=CMP_STRIDE), :]
            b = kvc_scr[kind, pl.ds(2 * p + 1, tm // CMP_STRIDE, stride=CMP_STRIDE), :]
            kvc_ref[2 * kind, :, p * LANES:(p + 1) * LANES] = jnp.where(low, a, pltpu.roll(b, HEAD_DIM, 1))
            kvc_ref[2 * kind + 1, :, p * LANES:(p + 1) * LANES] = jnp.where(low, pltpu.roll(a, HEAD_DIM, 1), b)

    seq_tile = pl.program_id(0) % tiles_per_seq
    pos = lax.broadcasted_iota(jnp.int32, (tm, N_KV * HPAD), 0) + seq_tile * tm
    lane = lax.broadcasted_iota(jnp.int32, (tm, N_KV * HPAD), 1) % HPAD
    onehot = jnp.where(pos // SEL_BLOCK == lane - HEAD_DIM, 1.0, 0.0)
    in_tag = (lane >= HEAD_DIM) & (lane < HEAD_DIM + 32)
    ks_ref[...] = jnp.where(in_tag, onehot, _spread_heads(proj(C_KS, C_KW))).astype(BF16)
    kw_ref[...] = _spread_heads(proj(C_KW, C_U)).astype(BF16)

    vt = _dot_nt(wvt_ref[...], hb)
    tail = jnp.where(lax.broadcasted_iota(jnp.int32, (HPAD - HEAD_DIM, tm), 0) == 0, 1.0, 0.0)
    for n, ref in enumerate((vs_ref, vw_ref)):
        rows = []
        for grp in range(N_KV):
            lo = (n * N_KV + grp) * HEAD_DIM
            rows += [vt[lo:lo + HEAD_DIM], tail]
        full = jnp.concatenate(rows, axis=0).astype(BF16)
        for c in range(tm // TK):
            ref[c] = full[:, c * TK:(c + 1) * TK]

    cu = proj(C_C, C_G) * proj(C_U, C_B)

    @pl.when(seq_tile == 0)
    def _():
        carry_ref[...] = jnp.zeros_like(carry_ref)

    prev = carry_ref[...]
    row = lax.broadcasted_iota(jnp.int32, (tm, CONV_CH), 0)
    s1 = jnp.where(row == 0, prev[7:8], pltpu.roll(cu, 1, 0))
    s2 = jnp.where(row == 0, prev[6:7], jnp.where(row == 1, prev[7:8], pltpu.roll(cu, 2, 0)))
    y = s2 * cw_ref[0:1, :] + s1 * cw_ref[1:2, :] + cu * cw_ref[2:3, :]
    oc_ref[...] = (proj(C_B, C_C) * y).astype(BF16)
    carry_ref[...] = cu[tm - 8:tm]
    gate_ref[...] = jax.nn.sigmoid(proj(C_G, C_END))


N_PROJ_W = 3
N_PROJ_OUT = 8
N_PROJ_SCRATCH = 2


def _proj_specs(t, tm):
    row = lambda w: pl.BlockSpec((tm, w), lambda i: (i, 0))
    w_specs = [_resident((D_MODEL, C_END)), _resident((2 * KV_W, D_MODEL)), _resident((8, CONV_CH))]
    vt_shape = jax.ShapeDtypeStruct((t // TK, N_KV * HPAD, TK), BF16)
    vt_spec = pl.BlockSpec((tm // TK, N_KV * HPAD, TK), lambda i: (i, 0, 0))
    out_shape = [
        jax.ShapeDtypeStruct((t, N_HEADS * HPAD), BF16),
        jax.ShapeDtypeStruct((4, t // CMP_STRIDE, CMP_STRIDE * HEAD_DIM), F32),
        jax.ShapeDtypeStruct((t, N_KV * HPAD), BF16), vt_shape,
        jax.ShapeDtypeStruct((t, N_KV * HPAD), BF16), vt_shape,
        jax.ShapeDtypeStruct((t, N_KV * LANES), F32),
        jax.ShapeDtypeStruct((t, CONV_CH), BF16),
    ]
    out_specs = [
        row(N_HEADS * HPAD),
        pl.BlockSpec((4, tm // CMP_STRIDE, CMP_STRIDE * HEAD_DIM), lambda i: (0, i, 0)),
        row(N_KV * HPAD), vt_spec, row(N_KV * HPAD), vt_spec,
        row(N_KV * LANES), row(CONV_CH),
    ]
    scratch = [pltpu.VMEM((8, CONV_CH), F32), pltpu.VMEM((2, tm, LANES), F32)]
    return w_specs, out_shape, out_specs, scratch


def _in_proj_kernel(*refs, pre_ln, tiles_per_seq, tm):
    n_in = 3 if pre_ln else 1
    x_ref = refs[0]
    w_refs = refs[n_in:n_in + N_PROJ_W]
    outs = refs[n_in + N_PROJ_W:len(refs) - N_PROJ_SCRATCH]
    if pre_ln:
        h = _ln(x_ref[...], refs[1][...], refs[2][...])
        outs[0][...] = h
        outs = outs[1:]
    else:
        h = x_ref[...]
    _project_inputs(h, w_refs, outs, refs[len(refs) - N_PROJ_SCRATCH:], tiles_per_seq=tiles_per_seq, tm=tm)


def _in_proj(x, ln_g, ln_b, proj_w, *, seq, pre_ln):
    t = x.shape[0]
    tm = TM_PROJ
    row = pl.BlockSpec((tm, D_MODEL), lambda i: (i, 0))
    w_specs, out_shape, out_specs, scratch = _proj_specs(t, tm)
    in_specs, args = [row], [x]
    if pre_ln:
        in_specs += [_resident((1, D_MODEL)), _resident((1, D_MODEL))]
        args += [ln_g, ln_b]
        out_shape = [jax.ShapeDtypeStruct((t, D_MODEL), F32)] + out_shape
        out_specs = [row] + out_specs
    return pl.pallas_call(
        functools.partial(_in_proj_kernel, pre_ln=pre_ln, tiles_per_seq=seq // tm, tm=tm),
        grid=(t // tm,),
        in_specs=in_specs + w_specs,
        out_specs=out_specs,
        out_shape=out_shape,
        scratch_shapes=scratch,
        compiler_params=_cparams(("arbitrary",)),
        name="in_proj_ln" if pre_ln else "in_proj",
    )(*args, *proj_w)


def _prep_w_in(w_in):
    o = 0

    def take(n):
        nonlocal o
        s = w_in[:, o:o + n]
        o += n
        return s

    q, kc, vc, ks, vs, kw, vw = take(ATTN_W), *(take(KV_W) for _ in range(6))
    gates = take(3 * N_HEADS).reshape(-1, 3, N_KV, HPG).transpose(0, 2, 1, 3).reshape(-1, N_KV, 3 * HPG)
    gates = jnp.pad(gates, ((0, 0), (0, 0), (0, LANES - 3 * HPG))).reshape(-1, N_KV * LANES)
    conv = take(3 * CONV_CH)
    w_r = jnp.concatenate([q, kc, vc, ks, kw, conv, gates], axis=1).astype(BF16)
    return w_r, jnp.concatenate([vs, vw], axis=1).T.astype(BF16)


def _compress_kernel(x_ref, pos_ref, w1_ref, b1_ref, w2_ref, b2_ref, w2t_ref, b2t_ref, o_ref, ot_ref):
    half = CMP_STRIDE * HEAD_DIM
    x = x_ref[0, 0]
    pos = pos_ref[0]
    xa = (x + pos[:, :half]).astype(BF16)
    xb = (x + pos[:, half:]).astype(BF16)
    a = _dot(xa, w1_ref[0, :half, :])
    b = _dot(xb, w1_ref[0, half:, :])
    n = x.shape[0]
    hid = a + pltpu.roll(b, n - 1, 0) + b1_ref[0]
    act = jax.nn.gelu(hid).astype(BF16)
    o_ref[0, 0] = (_dot(act, w2_ref[0]) + b2_ref[0]).astype(BF16)
    ot_ref[0, 0] = (_dot_nt(w2t_ref[0], act) + b2t_ref[0]).astype(BF16)


def _compress(kvc, cmp_pos, cmp_w1, cmp_b1, cmp_w2, cmp_b2, *, batch, seq):
    nb = seq // CMP_STRIDE
    half = CMP_STRIDE * HEAD_DIM
    x = kvc.reshape(4, batch, nb, half)
    pos = cmp_pos.reshape(2, 1, CMP_BLOCK * HEAD_DIM)
    w1 = cmp_w1.astype(BF16)
    b1 = cmp_b1.reshape(2, 1, CMP_HIDDEN)
    w2 = jnp.pad(cmp_w2, ((0, 0), (0, 0), (0, HPAD - HEAD_DIM))).astype(BF16)
    b2 = jnp.pad(cmp_b2, ((0, 0), (0, HPAD - HEAD_DIM))).reshape(2, 1, HPAD)
    w2t = jnp.swapaxes(w2, 1, 2)
    b2t = b2.reshape(2, HPAD, 1)
    kind = lambda j, b: (j // N_KV, 0, 0)
    return pl.pallas_call(
        _compress_kernel,
        grid=(4, batch),
        in_specs=[
            pl.BlockSpec((1, 1, nb, half), lambda j, b: (j, b, 0, 0)),
            pl.BlockSpec((1, 1, 2 * half), kind),
            pl.BlockSpec((1, 2 * half, CMP_HIDDEN), kind),
            pl.BlockSpec((1, 1, CMP_HIDDEN), kind),
            pl.BlockSpec((1, CMP_HIDDEN, HPAD), kind),
            pl.BlockSpec((1, 1, HPAD), kind),
            pl.BlockSpec((1, HPAD, CMP_HIDDEN), kind),
            pl.BlockSpec((1, HPAD, 1), kind),
        ],
        out_specs=[pl.BlockSpec((1, 1, nb, HPAD), lambda j, b: (j, b, 0, 0)),
                   pl.BlockSpec((1, 1, HPAD, nb), lambda j, b: (j, b, 0, 0))],
        out_shape=[jax.ShapeDtypeStruct((4, batch, nb, HPAD), BF16),
                   jax.ShapeDtypeStruct((4, batch, HPAD, nb), BF16)],
        compiler_params=_cparams(("arbitrary", "arbitrary")),
        name="compress",
    )(x, pos, w1, b1, w2, b2, w2t, b2t)


def _overlap_t():
    n = np.arange(LANES)
    j = np.arange(32)
    cs = n * CMP_STRIDE
    ss = j * SEL_BLOCK
    ov = (cs[None, :] < ss[:, None] + SEL_BLOCK) & (cs[None, :] + CMP_BLOCK > ss[:, None])
    return jnp.asarray(ov, dtype=BF16)


def _cmp_select(qh, kc, vct, ov, i, tq, n_sel):
    n = lax.broadcasted_iota(jnp.int32, (LANES, tq), 0)
    t = lax.broadcasted_iota(jnp.int32, (LANES, tq), 1) + i * tq
    mask = n * CMP_STRIDE + (CMP_BLOCK - 1) <= t
    maskf = mask.astype(F32)
    ss = [jnp.where(mask, _dot_nt(kc, q), NEG) for q in qh]
    es = [jnp.exp2(s - jnp.max(s, axis=0, keepdims=True)) for s in ss]
    ps = [e / jnp.sum(e, axis=0, keepdims=True) * maskf for e in es]
    heads_t = [_dot(vct, p.astype(BF16))[:HEAD_DIM] for p in ps]
    psum = functools.reduce(lambda a, b: a + b, ps)

    p_hi = psum.astype(BF16)
    r1 = psum - p_hi.astype(F32)
    p_mid = r1.astype(BF16)
    p_lo = (r1 - p_mid.astype(F32)).astype(BF16)
    imp = _dot(ov, p_hi) + _dot(ov, p_mid) + _dot(ov, p_lo)

    jt = lax.broadcasted_iota(jnp.int32, (n_sel, tq), 0)
    tt = lax.broadcasted_iota(jnp.int32, (n_sel, tq), 1) + i * tq
    cur = tt // SEL_BLOCK
    valid = jt * SEL_BLOCK <= tt
    forced = (jt == 0) | (jt == cur) | (jt == cur - 1)
    score = jnp.where(forced, FORCE, jnp.where(valid, imp, NEG))
    rank = jnp.zeros((n_sel, tq), jnp.int32)
    for k in range(n_sel):
        sk = score[k:k + 1, :]
        ahead = (sk > score) | ((sk == score) & (jt > k))
        rank = rank + ahead.astype(jnp.int32)
    bias = jnp.where(rank < SEL_TOPN, 0.0, NEG)
    full = jnp.concatenate(
        [jnp.zeros((HEAD_DIM, tq), F32), bias, jnp.zeros((HPAD - HEAD_DIM - n_sel, tq), F32)], axis=0)
    return heads_t, full.T.astype(BF16)


def _attn_scores(k, qa):
    return tuple(_dot_nt(k, qa[hh]) for hh in range(HPG))


def _attn_update(ss, vt, state, mask):
    if mask is not None:
        ss = [jnp.where(mask, s, NEG) for s in ss]
    m_new = [jnp.maximum(state[hh][0], jnp.max(ss[hh], axis=0, keepdims=True)) for hh in range(HPG)]
    ps = [jnp.exp2(ss[hh] - m_new[hh]).astype(BF16) for hh in range(HPG)]
    pv = [_dot(vt, ps[hh]) for hh in range(HPG)]
    return tuple((m_new[hh], jnp.exp2(state[hh][0] - m_new[hh]) * state[hh][1] + pv[hh]) for hh in range(HPG))


def _attn_init(tq):
    return tuple((jnp.full((1, tq), NEG, F32), jnp.zeros((HPAD, tq), F32)) for _ in range(HPG))


def _nsa_attn_kernel(q_ref, kc_ref, vct_ref, ov_ref, ks_ref, vst_ref, kw_ref, vwt_ref, gate_ref, o_ref,
                     *, tq, tk, n_sel):
    i = pl.program_id(2)
    nq = ks_ref.shape[0] // tk
    qh = [q_ref[:, hh * HPAD:(hh + 1) * HPAD] for hh in range(HPG)]
    heads_cmp, selb = _cmp_select(qh, kc_ref[0, 0], vct_ref[0, 0], ov_ref[...], i, tq, n_sel)
    lane = lax.broadcasted_iota(jnp.int32, selb.shape, 1)
    qsel = [jnp.where(lane >= HEAD_DIM, selb, q) for q in qh]
    key = lax.broadcasted_iota(jnp.int32, (tk, tq), 0)
    qry = lax.broadcasted_iota(jnp.int32, (tk, tq), 1)
    causal = key <= qry
    gt = gate_ref[...].T

    key2 = lax.broadcasted_iota(jnp.int32, (2 * tk, tq), 0)
    pair_causal = key2 - tk <= lax.broadcasted_iota(jnp.int32, (2 * tk, tq), 1)

    def run(n):
        chunks = []
        for j in range(0, n, 2):
            chunks.append(("sel", j, 2, pair_causal if j + 1 == n else None))
        if n % 2 == 0:
            chunks.append(("sel", n, 1, causal))
        if n == 0:
            chunks.append(("win", 0, 1, causal))
        else:
            chunks.append(("win", n - 1, 2, pair_causal))
        if n >= 2:
            chunks.append(("win", n - 2, 1, key > qry))
        refs = {"sel": (ks_ref, vst_ref, qsel), "win": (kw_ref, vwt_ref, qh)}

        def scores(chunk):
            branch, j, nt, _ = chunk
            return _attn_scores(refs[branch][0][j * tk:(j + nt) * tk, :], refs[branch][2])

        state = {"sel": _attn_init(tq), "win": _attn_init(tq)}
        ss = scores(chunks[0])
        for idx, (branch, j, nt, mask) in enumerate(chunks):
            ss_next = scores(chunks[idx + 1]) if idx + 1 < len(chunks) else None
            vt_ref = refs[branch][1]
            vt = vt_ref[j] if nt == 1 else jnp.concatenate([vt_ref[j], vt_ref[j + 1]], axis=1)
            state[branch] = _attn_update(ss, vt, state[branch], mask)
            ss = ss_next

        outs = []
        for hh in range(HPG):
            o_sel, o_win = (acc[:HEAD_DIM] / acc[HEAD_DIM:HEAD_DIM + 1]
                            for _, acc in (state["sel"][hh], state["win"][hh]))
            g_cmp, g_sel, g_win = (gt[c * HPG + hh:c * HPG + hh + 1] for c in range(3))
            outs.append(g_cmp * heads_cmp[hh] + g_sel * o_sel + g_win * o_win)
        o_ref[...] = jnp.concatenate(outs, axis=0).T.astype(o_ref.dtype)

    for n in range(nq):
        pl.when(i == n)(functools.partial(run, n))


def _nsa_attn(q, kvcmp, kvcmp_t, ks, vs_t, kw, vw_t, gates, *, batch, seq):
    t = q.shape[0]
    tq, tk = TQ, TK
    assert tq == tk and WINDOW == 2 * tk
    nq = seq // tq
    n_sel = seq // SEL_BLOCK
    nb = kvcmp.shape[2]
    assert nb == LANES and n_sel == 32
    rowblk = lambda w: pl.BlockSpec((tq, w), lambda b, g, i: (b * nq + i, g))
    seqblk = pl.BlockSpec((seq, HPAD), lambda b, g, i: (b, g))
    vtblk = pl.BlockSpec((seq // tk, HPAD, tk), lambda b, g, i: (b, g, 0))
    return pl.pallas_call(
        functools.partial(_nsa_attn_kernel, tq=tq, tk=tk, n_sel=n_sel),
        grid=(batch, N_KV, nq),
        in_specs=[rowblk(HPG * HPAD),
                  pl.BlockSpec((1, 1, nb, HPAD), lambda b, g, i: (g, b, 0, 0)),
                  pl.BlockSpec((1, 1, HPAD, nb), lambda b, g, i: (N_KV + g, b, 0, 0)),
                  pl.BlockSpec((n_sel, LANES), lambda b, g, i: (0, 0)),
                  seqblk, vtblk, seqblk, vtblk,
                  rowblk(LANES)],
        out_specs=rowblk(HPG * HEAD_DIM),
        out_shape=jax.ShapeDtypeStruct((t, ATTN_W), BF16),
        compiler_params=_cparams(("arbitrary", "arbitrary", "arbitrary")),
        name="nsa_attn",
    )(q, kvcmp, kvcmp_t, _overlap_t(), ks, vs_t, kw, vw_t, gates)


def _dense_tail_kernel(*refs, alpha, with_proj, tiles_per_seq, tm):
    (h_ref, oa_ref, ocv_ref, wo_ref, g1_ref, b1_ref, wg_ref, wu_ref, wd_ref, g2_ref, b2_ref) = refs[:11]
    m = _dot(oa_ref[...], wo_ref[:ATTN_W, :]) + _dot(ocv_ref[...], wo_ref[ATTN_W:, :])
    h1 = _ln(alpha * h_ref[...] + m, g1_ref[...], b1_ref[...])
    hb = h1.astype(BF16)
    a = (jax.nn.silu(_dot(hb, wg_ref[...])) * _dot(hb, wu_ref[...])).astype(BF16)
    h2 = _ln(alpha * h1 + _dot(a, wd_ref[...]), g2_ref[...], b2_ref[...])
    if with_proj:
        w_refs = refs[11:11 + N_PROJ_W]
        outs = refs[11 + N_PROJ_W:len(refs) - N_PROJ_SCRATCH]
        outs[0][...] = h2
        _project_inputs(h2, w_refs, outs[1:], refs[len(refs) - N_PROJ_SCRATCH:], tiles_per_seq=tiles_per_seq, tm=tm)
    else:
        refs[11][...] = h2


def _dense_tail(h, o_attn, o_conv, w_o, ln1, ffn_w, ln2, proj_w, *, alpha, seq):
    t = h.shape[0]
    tm = TM_PROJ
    row = lambda w: pl.BlockSpec((tm, w), lambda i: (i, 0))
    vec = _resident((1, D_MODEL))
    in_specs = [row(D_MODEL), row(ATTN_W), row(CONV_CH), _resident((D_MODEL, D_MODEL)), vec, vec,
                _resident((D_MODEL, D_FF)), _resident((D_MODEL, D_FF)), _resident((D_FF, D_MODEL)), vec, vec]
    args = [h, o_attn, o_conv, w_o, *ln1, *ffn_w, *ln2]
    out_shape, out_specs, scratch = [jax.ShapeDtypeStruct((t, D_MODEL), F32)], [row(D_MODEL)], []
    if proj_w is not None:
        w_specs, p_shape, p_specs, scratch = _proj_specs(t, tm)
        in_specs += w_specs
        args += list(proj_w)
        out_shape += p_shape
        out_specs += p_specs
    return pl.pallas_call(
        functools.partial(_dense_tail_kernel, alpha=alpha, with_proj=proj_w is not None,
                          tiles_per_seq=seq // tm, tm=tm),
        grid=(t // tm,),
        in_specs=in_specs,
        out_specs=out_specs,
        out_shape=out_shape,
        scratch_shapes=scratch,
        compiler_params=_cparams(("arbitrary",)),
        name="dense_tail",
    )(*args)


TMX = 512
CH = 16
LROWS = 2 * TMX + N_EXPERTS * CH
TMR = 512
XROW = D_MODEL + LANES


def _out_proj_kernel(h_ref, oa_ref, ocv_ref, w_ref, g_ref, b_ref, o_ref, *, alpha):
    m = _dot(oa_ref[...], w_ref[:ATTN_W, :]) + _dot(ocv_ref[...], w_ref[ATTN_W:, :])
    o_ref[...] = _ln(alpha * h_ref[...] + m, g_ref[...], b_ref[...])


def _out_proj(h, o_attn, o_conv, w_o, ln_g, ln_b, *, alpha):
    t = h.shape[0]
    tm = TM_PROJ
    row = lambda w: pl.BlockSpec((tm, w), lambda i: (i, 0))
    return pl.pallas_call(
        functools.partial(_out_proj_kernel, alpha=alpha),
        grid=(t // tm,),
        in_specs=[row(D_MODEL), row(ATTN_W), row(CONV_CH),
                  _resident((D_MODEL, D_MODEL)), _resident((1, D_MODEL)), _resident((1, D_MODEL))],
        out_specs=row(D_MODEL),
        out_shape=jax.ShapeDtypeStruct((t, D_MODEL), F32),
        compiler_params=_cparams(("arbitrary",)),
        name="out_proj",
    )(h, o_attn, o_conv, w_o, ln_g, ln_b)


def _route_kernel(h_ref, wr_ref, ltri_ref, ustr_ref, meta_ref, metat_ref, cnt_ref, wp_ref, *, tm):
    h = h_ref[...]
    h_hi = h.astype(BF16)
    h_lo = (h - h_hi.astype(F32)).astype(BF16)
    logits = _dot(h_hi, wr_ref[0]) + _dot(h_lo, wr_ref[0]) + _dot(h_hi, wr_ref[1])
    lane = lax.broadcasted_iota(jnp.int32, logits.shape, 1)
    logits = jnp.where(lane < N_EXPERTS, logits, -jnp.inf)
    m1 = jnp.max(logits, axis=-1, keepdims=True)
    i1 = jnp.min(jnp.where(logits == m1, lane, LANES), axis=-1, keepdims=True)
    rest = jnp.where(lane == i1, -jnp.inf, logits)
    m2 = jnp.max(rest, axis=-1, keepdims=True)
    i2 = jnp.min(jnp.where(rest == m2, lane, LANES), axis=-1, keepdims=True)
    e2 = jnp.exp(m2 - m1)
    den = 1.0 + e2
    w1 = 1.0 / den
    w2 = e2 / den

    routed = (lane == i1) | (lane == i2)
    cnt = _dot(ltri_ref[...], routed.astype(BF16))
    n = cnt[tm - 1:tm, :]
    padded = jnp.floor((n + (CH - 1)) * (1.0 / CH)) * CH
    seg_off = _dot(jnp.broadcast_to(padded, (8, LANES)).astype(BF16), ustr_ref[...])[0:1]
    dest = seg_off + cnt - 1.0
    d1 = jnp.sum(jnp.where(lane == i1, dest, 0.0), axis=-1, keepdims=True)
    d2 = jnp.sum(jnp.where(lane == i2, dest, 0.0), axis=-1, keepdims=True)
    meta = jnp.where(lane == 0, d1, jnp.where(lane == 1, d2, jnp.where(lane == 2, w1, jnp.where(lane == 3, w2, 0.0))))
    meta_ref[...] = meta
    metat_ref[...] = meta.T[0:8, :]
    cnt_ref[0] = jnp.broadcast_to(n, (8, LANES))
    for k, w in enumerate((w1, w2)):
        hi = w.astype(BF16).astype(F32)
        mid = (w - hi).astype(BF16).astype(F32)
        lo = (w - hi - mid).astype(BF16).astype(F32)
        wp_ref[k] = jnp.where(lane == 0, hi, jnp.where(lane == 1, mid, jnp.where(lane == 2, lo, 0.0))).astype(BF16)


def _moe_route(h, w_router):
    t = h.shape[0]
    tm = TMX
    nt = t // tm
    row = lambda w: pl.BlockSpec((tm, w), lambda i: (i, 0))
    wr = jnp.pad(w_router, ((0, 0), (0, LANES - N_EXPERTS)))
    wr_hi = wr.astype(BF16)
    wr_lo = (wr - wr_hi.astype(F32)).astype(BF16)
    wr2 = jnp.stack([wr_hi, wr_lo])
    ltri = jnp.asarray(np.tril(np.ones((tm, tm), np.float32)), dtype=BF16)
    ustr = jnp.asarray(np.triu(np.ones((LANES, LANES), np.float32), 1), dtype=BF16)
    return pl.pallas_call(
        functools.partial(_route_kernel, tm=tm),
        grid=(nt,),
        in_specs=[row(D_MODEL), _resident((2, D_MODEL, LANES)), _resident((tm, tm)), _resident((LANES, LANES))],
        out_specs=[row(LANES),
                   pl.BlockSpec((8, tm), lambda i: (0, i)),
                   pl.BlockSpec((1, 8, LANES), lambda i: (i, 0, 0)),
                   pl.BlockSpec((2, tm, LANES), lambda i: (0, i, 0))],
        out_shape=[jax.ShapeDtypeStruct((t, LANES), F32), jax.ShapeDtypeStruct((8, t), F32),
                   jax.ShapeDtypeStruct((nt, 8, LANES), F32), jax.ShapeDtypeStruct((2, t, LANES), BF16)],
        compiler_params=_cparams(("arbitrary",)),
        name="moe_route",
    )(h, wr2, ltri, ustr)


def _moe_plan(cnt, n_row_tiles):
    n = cnt[:, 0, :N_EXPERTS].astype(jnp.int32)
    p = (n + CH - 1) // CH * CH
    tot = p.sum(0)
    tot_pad = (tot + TMR - 1) // TMR * TMR
    gend = jnp.cumsum(tot_pad)
    gstart = gend - tot_pad
    goff = gstart[None, :] + jnp.cumsum(p, 0) - p
    loff = jnp.cumsum(p, 1) - p
    n_used = gend[-1:] // TMR
    tile_start = jnp.arange(n_row_tiles, dtype=jnp.int32) * TMR
    texp = jnp.minimum(jnp.sum(tile_start[:, None] >= gend[None, :], axis=1), N_EXPERTS - 1).astype(jnp.int32)
    texp = jnp.where(jnp.arange(n_row_tiles) < n_used[0], texp, texp[jnp.maximum(n_used[0] - 1, 0)])
    flat = lambda a: a.reshape(-1).astype(jnp.int32)
    gap_start = jnp.concatenate([gstart + tot, gend[-1:]])
    gap_rows = jnp.concatenate([tot_pad - tot, n_row_tiles * TMR - gend[-1:]])
    return dict(loff=flat(loff), goff=flat(goff), nch=flat(p // CH), gap_start=flat(gap_start),
                gap_nch=flat(gap_rows // CH), texp=flat(texp), n_used=flat(n_used))


def _chunk_copies(src_ref, dst_ref, sem, src_off, dst_off, n, src_step=CH):
    def body(c, carry):
        s = pl.multiple_of(src_off + c * src_step, CH)
        d = pl.multiple_of(dst_off + c * CH, CH)
        pltpu.make_async_copy(src_ref.at[pl.ds(s, CH)], dst_ref.at[pl.ds(d, CH)], sem).start()
        return carry
    lax.fori_loop(0, n, body, 0)


def _wait_chunks(src_ref, dst_ref, sem, n):
    def body(c, carry):
        pltpu.make_async_copy(src_ref.at[pl.ds(0, CH)], dst_ref.at[pl.ds(0, CH)], sem).wait()
        return carry
    lax.fori_loop(0, n, body, 0)


def _tile_chunks(tile, nch):
    total = 0
    for e in range(N_EXPERTS):
        total = total + nch[tile * N_EXPERTS + e]
    return total


def _start_segments(tile, nch, src_off, dst_off, src_ref, dst_ref, sem):
    for e in range(N_EXPERTS):
        idx = tile * N_EXPERTS + e
        _chunk_copies(src_ref, dst_ref, sem, src_off[idx], dst_off[idx], nch[idx])


def _dispatch_kernel(loff, goff, nch, gap_start, gap_nch, h_ref, metat_ref, wp_ref, xs_ref, xc_ref, z_ref, sem,
                     *, tm):
    i = pl.program_id(0)
    last = pl.num_programs(0) - 1
    slot = i % 2
    r = lax.broadcasted_iota(jnp.int32, (LROWS, tm), 0).astype(F32)
    pick1 = (r == metat_ref[0:1, :]).astype(BF16)
    pick2 = (r == metat_ref[1:2, :]).astype(BF16)
    xc_ref[slot, :, :D_MODEL] = _dot(pick1 + pick2, h_ref[...].astype(BF16)).astype(BF16)
    xc_ref[slot, :, D_MODEL:] = (_dot(pick1, wp_ref[0]) + _dot(pick2, wp_ref[1])).astype(BF16)
    _start_segments(i, nch, loff, goff, xc_ref.at[slot], xs_ref, sem.at[slot])

    @pl.when(i > 0)
    def _():
        _wait_chunks(xc_ref.at[1 - slot], xs_ref, sem.at[1 - slot], _tile_chunks(i - 1, nch))

    @pl.when(i == last)
    def _():
        z_ref[...] = jnp.zeros_like(z_ref)
        gaps = 0
        for e in range(N_EXPERTS + 1):
            _chunk_copies(z_ref, xs_ref, sem.at[2], 0, gap_start[e], gap_nch[e], src_step=0)
            gaps = gaps + gap_nch[e]
        _wait_chunks(z_ref, xs_ref, sem.at[2], gaps)
        _wait_chunks(xc_ref.at[slot], xs_ref, sem.at[slot], _tile_chunks(i, nch))


def _moe_dispatch(h, metat, wparts, plan, n_rows):
    t = h.shape[0]
    tm = TMX
    grid_spec = pltpu.PrefetchScalarGridSpec(
        num_scalar_prefetch=5,
        grid=(t // tm,),
        in_specs=[pl.BlockSpec((tm, D_MODEL), lambda i, *_: (i, 0)),
                  pl.BlockSpec((8, tm), lambda i, *_: (0, i)),
                  pl.BlockSpec((2, tm, LANES), lambda i, *_: (0, i, 0))],
        out_specs=pl.BlockSpec(memory_space=pl.ANY),
        scratch_shapes=[pltpu.VMEM((2, LROWS, XROW), BF16), pltpu.VMEM((CH, XROW), BF16),
                        pltpu.SemaphoreType.DMA((3,))],
    )
    return pl.pallas_call(
        functools.partial(_dispatch_kernel, tm=tm),
        grid_spec=grid_spec,
        out_shape=jax.ShapeDtypeStruct((n_rows, XROW), BF16),
        compiler_params=_cparams(("arbitrary",)),
        name="moe_dispatch",
    )(plan["loff"], plan["goff"], plan["nch"], plan["gap_start"], plan["gap_nch"], h, metat, wparts)


def _experts_kernel(texp, n_used, x_ref, wg_ref, wu_ref, wd_ref, y_ref):
    used = pl.program_id(0) < n_used[0]

    @pl.when(used)
    def _():
        x = x_ref[:, :D_MODEL]
        wp = x_ref[:, D_MODEL:].astype(F32)
        gate = wp[:, 0:1] + wp[:, 1:2] + wp[:, 2:3]
        a = (jax.nn.silu(_dot(x, wg_ref[0])) * _dot(x, wu_ref[0])).astype(BF16)
        y_ref[...] = (gate * _dot(a, wd_ref[0])).astype(BF16)

    @pl.when(jnp.logical_not(used))
    def _():
        y_ref[...] = jnp.zeros_like(y_ref)


def _moe_experts(xs, wg, wu, wd, plan):
    n_rows = xs.shape[0]
    rows = lambda r, texp, n_used: (jnp.minimum(r, n_used[0] - 1), 0)
    wspec = lambda shape: pl.BlockSpec((1,) + shape, lambda r, texp, n_used: (texp[r], 0, 0))
    grid_spec = pltpu.PrefetchScalarGridSpec(
        num_scalar_prefetch=2,
        grid=(n_rows // TMR,),
        in_specs=[pl.BlockSpec((TMR, XROW), rows), wspec((D_MODEL, D_FF_EXPERT)),
                  wspec((D_MODEL, D_FF_EXPERT)), wspec((D_FF_EXPERT, D_MODEL))],
        out_specs=pl.BlockSpec((TMR, D_MODEL), lambda r, texp, n_used: (r, 0)),
    )
    return pl.pallas_call(
        _experts_kernel,
        grid_spec=grid_spec,
        out_shape=jax.ShapeDtypeStruct((n_rows, D_MODEL), BF16),
        compiler_params=_cparams(("arbitrary",)),
        name="moe_experts",
    )(plan["texp"], plan["n_used"], xs, wg, wu, wd)


def _combine_kernel(loff, goff, nch, h_ref, meta_ref, ys_ref, g_ref, b_ref, o_ref, yb_ref, sem, *, tm, alpha):
    i = pl.program_id(0)
    slot = i % 2

    @pl.when(i == 0)
    def _():
        yb_ref[...] = jnp.zeros_like(yb_ref)
        _start_segments(0, nch, goff, loff, ys_ref, yb_ref.at[0], sem.at[0])

    @pl.when(i + 1 < pl.num_programs(0))
    def _():
        _start_segments(i + 1, nch, goff, loff, ys_ref, yb_ref.at[1 - slot], sem.at[1 - slot])

    meta = meta_ref[...]
    r = lax.broadcasted_iota(jnp.int32, (tm, LROWS), 1).astype(F32)
    pick = ((r == meta[:, 0:1]) | (r == meta[:, 1:2])).astype(BF16)
    _wait_chunks(ys_ref, yb_ref.at[slot], sem.at[slot], _tile_chunks(i, nch))
    f = _dot(pick, yb_ref[slot])
    o_ref[...] = _ln(alpha * h_ref[...] + f, g_ref[...], b_ref[...])


def _moe_combine(h, meta, ys, plan, ln_g, ln_b, *, alpha):
    t = h.shape[0]
    tm = TMX
    grid_spec = pltpu.PrefetchScalarGridSpec(
        num_scalar_prefetch=3,
        grid=(t // tm,),
        in_specs=[pl.BlockSpec((tm, D_MODEL), lambda i, *_: (i, 0)),
                  pl.BlockSpec((tm, LANES), lambda i, *_: (i, 0)),
                  pl.BlockSpec(memory_space=pl.ANY),
                  pl.BlockSpec((1, D_MODEL), lambda i, *_: (0, 0)),
                  pl.BlockSpec((1, D_MODEL), lambda i, *_: (0, 0))],
        out_specs=pl.BlockSpec((tm, D_MODEL), lambda i, *_: (i, 0)),
        scratch_shapes=[pltpu.VMEM((2, LROWS, D_MODEL), BF16), pltpu.SemaphoreType.DMA((2,))],
    )
    return pl.pallas_call(
        functools.partial(_combine_kernel, tm=tm, alpha=alpha),
        grid_spec=grid_spec,
        out_shape=jax.ShapeDtypeStruct((t, D_MODEL), F32),
        compiler_params=_cparams(("arbitrary",)),
        name="moe_combine",
    )(plan["loff"], plan["goff"], plan["nch"], h, meta, ys, ln_g, ln_b)


def _moe_tail(h, o_attn, o_conv, w_o, ln1, w_router, wg, wu, wd, ln_g, ln_b, *, alpha):
    t = h.shape[0]
    nt = t // TMX
    max_rows = 2 * t + nt * N_EXPERTS * (CH - 1) + N_EXPERTS * (TMR - CH)
    n_row_tiles = -(-max_rows // TMR)
    h = _out_proj(h, o_attn, o_conv, w_o, *ln1, alpha=alpha)
    meta, metat, cnt, wparts = _moe_route(h, w_router)
    plan = _moe_plan(cnt, n_row_tiles)
    xs = _moe_dispatch(h, metat, wparts, plan, n_row_tiles * TMR)
    ys = _moe_experts(xs, wg, wu, wd, plan)
    return _moe_combine(h, meta, ys, plan, ln_g, ln_b, alpha=alpha)


def kernel(x, ln_in_g, ln_in_b, w_in, cmp_pos, cmp_w1, cmp_b1, cmp_w2, cmp_b2, conv_w, w_o, ln1_g, ln1_b, ln2_g, ln2_b, ffn_wg, ffn_wu, ffn_wd, moe_router, moe_wg, moe_wu, moe_wd):
    batch, seq, d = x.shape
    depth = w_in.shape[0]
    assert d == D_MODEL and seq % TM_PROJ == 0 and seq // CMP_STRIDE == LANES
    alpha = (2 * depth) ** 0.25
    t = batch * seq
    vec = lambda v: v.reshape(1, D_MODEL)
    def proj_weights(l):
        return (*_prep_w_in(w_in[l]), jnp.pad(conv_w[l], ((0, 8 - CONV_W), (0, 0))))

    h, *proj = _in_proj(x.reshape(t, d), vec(ln_in_g), vec(ln_in_b), proj_weights(0), seq=seq, pre_ln=True)
    for l in range(depth):
        q, kvc, ks, vs, kw, vw, gates, o_conv = proj
        kvcmp, kvcmp_t = _compress(kvc, cmp_pos[l], cmp_w1[l], cmp_b1[l], cmp_w2[l], cmp_b2[l],
                                   batch=batch, seq=seq)
        o_attn = _nsa_attn(q, kvcmp, kvcmp_t, ks, vs, kw, vw, gates, batch=batch, seq=seq)
        ln1 = (vec(ln1_g[l]), vec(ln1_b[l]))
        ln2 = (vec(ln2_g[l]), vec(ln2_b[l]))
        next_w = proj_weights(l + 1) if l + 1 < depth else None
        if l % 2 == 0:
            ffn_w = (ffn_wg[l // 2].astype(BF16), ffn_wu[l // 2].astype(BF16), ffn_wd[l // 2].astype(BF16))
            h, *proj = _dense_tail(h, o_attn, o_conv, w_o[l].astype(BF16), ln1, ffn_w, ln2, next_w,
                                   alpha=alpha, seq=seq)
        else:
            h = _moe_tail(h, o_attn, o_conv, w_o[l].astype(BF16), ln1, moe_router[l // 2],
                          moe_wg[l // 2].astype(BF16), moe_wu[l // 2].astype(BF16), moe_wd[l // 2].astype(BF16),
                          *ln2, alpha=alpha)
            if next_w is not None:
                proj = _in_proj(h, None, None, next_w, seq=seq, pre_ln=False)
    return h.reshape(batch, seq, d)
```

```python
import functools

import numpy as np
import jax
import jax.numpy as jnp
from jax import lax
from jax.experimental import pallas as pl
from jax.experimental.pallas import tpu as pltpu

F32 = jnp.float32
BF16 = jnp.bfloat16

D_MODEL = 1024
HEAD_DIM = 64
N_HEADS = 8
N_KV = 2
HPG = N_HEADS // N_KV
ATTN_W = N_HEADS * HEAD_DIM
KV_W = N_KV * HEAD_DIM
CONV_CH = D_MODEL - ATTN_W
CONV_W = 3
CMP_BLOCK = 32
CMP_STRIDE = 16
CMP_HIDDEN = 256
SEL_BLOCK = 64
SEL_TOPN = 16
WINDOW = 512
D_FF = 2816
N_EXPERTS = 8
D_FF_EXPERT = 1408
LN_EPS = 1e-5
NEG = -1e30
FORCE = 1e9

LANES = 128
HPAD = LANES
BF16_SUBLANES = 16
VROWS = HEAD_DIM + BF16_SUBLANES
Q_SCALE = HEAD_DIM ** -0.5 * np.log2(np.e)

C_Q = 0
C_KVC = C_Q + ATTN_W
C_KS = C_KVC + 2 * KV_W
C_KW = C_KS + KV_W
C_U = C_KW + KV_W
C_B = C_U + CONV_CH
C_C = C_B + CONV_CH
C_G = C_C + CONV_CH
C_END = C_G + N_KV * LANES

TM_PROJ = 512
TQ = 256
TK = 256
VMEM_LIMIT = 56 * 1024 * 1024


def _cparams(sem):
    return pltpu.CompilerParams(dimension_semantics=sem, vmem_limit_bytes=VMEM_LIMIT)


def _ln(x, g, b):
    mu = jnp.mean(x, -1, keepdims=True)
    xc = x - mu
    var = jnp.mean(xc * xc, -1, keepdims=True)
    return xc * lax.rsqrt(var + LN_EPS) * g + b


def _dot(a, b):
    return jnp.dot(a, b, preferred_element_type=F32)


def _dot_nt(a, b):
    return lax.dot_general(a, b, (((1,), (1,)), ((), ())), preferred_element_type=F32)


def _resident(shape):
    nd = len(shape)
    return pl.BlockSpec(shape, lambda *_: (0,) * nd, pipeline_mode=pl.Buffered(1))


def _spread_heads(z):
    low = lax.broadcasted_iota(jnp.int32, (z.shape[0], LANES), 1) < HEAD_DIM
    tiles = []
    for p in range(z.shape[1] // LANES):
        pair = z[:, p * LANES:(p + 1) * LANES]
        tiles += [jnp.where(low, pair, 0.0), jnp.where(low, pltpu.roll(pair, HEAD_DIM, 1), 0.0)]
    return jnp.concatenate(tiles, axis=1)


def _project_inputs(h, w_refs, out_refs, scratch_refs, *, tiles_per_seq, tm):
    w_ref, wvt_ref, cw_ref = w_refs
    q_ref, kvc_ref, ks_ref, vs_ref, kw_ref, vw_ref, gate_ref, oc_ref = out_refs
    carry_ref, kvc_scr = scratch_refs
    hb = h.astype(BF16)

    def proj(lo, hi):
        return _dot(hb, w_ref[:, lo:hi])

    q_ref[...] = _spread_heads(proj(C_Q, C_KVC) * Q_SCALE).astype(BF16)

    kvc = proj(C_KVC, C_KS)
    low = lax.broadcasted_iota(jnp.int32, (tm // CMP_STRIDE, LANES), 1) < HEAD_DIM
    for kind in range(2):
        kvc_scr[kind] = kvc[:, kind * LANES:(kind + 1) * LANES]
        for p in range(CMP_STRIDE // 2):
            a = kvc_scr[kind, pl.ds(2 * p, tm // CMP_STRIDE, stride=CMP_STRIDE), :]
            b = kvc_scr[kind, pl.ds(2 * p + 1, tm // CMP_STRIDE, stride=CMP_STRIDE), :]
            kvc_ref[2 * kind, :, p * LANES:(p + 1) * LANES] = jnp.where(low, a, pltpu.roll(b, HEAD_DIM, 1))
            kvc_ref[2 * kind + 1, :, p * LANES:(p + 1) * LANES] = jnp.where(low, pltpu.roll(a, HEAD_DIM, 1), b)

    seq_tile = pl.program_id(0) % tiles_per_seq
    pos = lax.broadcasted_iota(jnp.int32, (tm, N_KV * HPAD), 0) + seq_tile * tm
    lane = lax.broadcasted_iota(jnp.int32, (tm, N_KV * HPAD), 1) % HPAD
    onehot = jnp.where(pos // SEL_BLOCK == lane - HEAD_DIM, 1.0, 0.0)
    in_tag = (lane >= HEAD_DIM) & (lane < HEAD_DIM + 32)
    ks_ref[...] = jnp.where(in_tag, onehot, _spread_heads(proj(C_KS, C_KW))).astype(BF16)
    kw_ref[...] = _spread_heads(proj(C_KW, C_U)).astype(BF16)

    vt = _dot_nt(wvt_ref[...], hb)
    tail = jnp.where(lax.broadcasted_iota(jnp.int32, (VROWS - HEAD_DIM, tm), 0) == 0, 1.0, 0.0)
    for n, ref in enumerate((vs_ref, vw_ref)):
        rows = []
        for grp in range(N_KV):
            lo = (n * N_KV + grp) * HEAD_DIM
            rows += [vt[lo:lo + HEAD_DIM], tail]
        full = jnp.concatenate(rows, axis=0).astype(BF16)
        for c in range(tm // TK):
            ref[c] = full[:, c * TK:(c + 1) * TK]

    cu = proj(C_C, C_G) * proj(C_U, C_B)

    @pl.when(seq_tile == 0)
    def _():
        carry_ref[...] = jnp.zeros_like(carry_ref)

    prev = carry_ref[...]
    row = lax.broadcasted_iota(jnp.int32, (tm, CONV_CH), 0)
    s1 = jnp.where(row == 0, prev[7:8], pltpu.roll(cu, 1, 0))
    s2 = jnp.where(row == 0, prev[6:7], jnp.where(row == 1, prev[7:8], pltpu.roll(cu, 2, 0)))
    y = s2 * cw_ref[0:1, :] + s1 * cw_ref[1:2, :] + cu * cw_ref[2:3, :]
    oc_ref[...] = (proj(C_B, C_C) * y).astype(BF16)
    carry_ref[...] = cu[tm - 8:tm]
    gate_ref[...] = jax.nn.sigmoid(proj(C_G, C_END))


N_PROJ_W = 3
N_PROJ_OUT = 8
N_PROJ_SCRATCH = 2


def _proj_specs(t, tm):
    row = lambda w: pl.BlockSpec((tm, w), lambda i: (i, 0))
    w_specs = [_resident((D_MODEL, C_END)), _resident((2 * KV_W, D_MODEL)), _resident((8, CONV_CH))]
    vt_shape = jax.ShapeDtypeStruct((t // TK, N_KV * VROWS, TK), BF16)
    vt_spec = pl.BlockSpec((tm // TK, N_KV * VROWS, TK), lambda i: (i, 0, 0))
    out_shape = [
        jax.ShapeDtypeStruct((t, N_HEADS * HPAD), BF16),
        jax.ShapeDtypeStruct((4, t // CMP_STRIDE, CMP_STRIDE * HEAD_DIM), F32),
        jax.ShapeDtypeStruct((t, N_KV * HPAD), BF16), vt_shape,
        jax.ShapeDtypeStruct((t, N_KV * HPAD), BF16), vt_shape,
        jax.ShapeDtypeStruct((t, N_KV * LANES), F32),
        jax.ShapeDtypeStruct((t, CONV_CH), BF16),
    ]
    out_specs = [
        row(N_HEADS * HPAD),
        pl.BlockSpec((4, tm // CMP_STRIDE, CMP_STRIDE * HEAD_DIM), lambda i: (0, i, 0)),
        row(N_KV * HPAD), vt_spec, row(N_KV * HPAD), vt_spec,
        row(N_KV * LANES), row(CONV_CH),
    ]
    scratch = [pltpu.VMEM((8, CONV_CH), F32), pltpu.VMEM((2, tm, LANES), F32)]
    return w_specs, out_shape, out_specs, scratch


def _in_proj_kernel(*refs, pre_ln, tiles_per_seq, tm):
    n_in = 3 if pre_ln else 1
    x_ref = refs[0]
    w_refs = refs[n_in:n_in + N_PROJ_W]
    outs = refs[n_in + N_PROJ_W:len(refs) - N_PROJ_SCRATCH]
    if pre_ln:
        h = _ln(x_ref[...], refs[1][...], refs[2][...])
        outs[0][...] = h
        outs = outs[1:]
    else:
        h = x_ref[...]
    _project_inputs(h, w_refs, outs, refs[len(refs) - N_PROJ_SCRATCH:], tiles_per_seq=tiles_per_seq, tm=tm)


def _in_proj(x, ln_g, ln_b, proj_w, *, seq, pre_ln):
    t = x.shape[0]
    tm = TM_PROJ
    row = pl.BlockSpec((tm, D_MODEL), lambda i: (i, 0))
    w_specs, out_shape, out_specs, scratch = _proj_specs(t, tm)
    in_specs, args = [row], [x]
    if pre_ln:
        in_specs += [_resident((1, D_MODEL)), _resident((1, D_MODEL))]
        args += [ln_g, ln_b]
        out_shape = [jax.ShapeDtypeStruct((t, D_MODEL), F32)] + out_shape
        out_specs = [row] + out_specs
    return pl.pallas_call(
        functools.partial(_in_proj_kernel, pre_ln=pre_ln, tiles_per_seq=seq // tm, tm=tm),
        grid=(t // tm,),
        in_specs=in_specs + w_specs,
        out_specs=out_specs,
        out_shape=out_shape,
        scratch_shapes=scratch,
        compiler_params=_cparams(("arbitrary",)),
        name="in_proj_ln" if pre_ln else "in_proj",
    )(*args, *proj_w)


def _prep_w_in(w_in):
    o = 0

    def take(n):
        nonlocal o
        s = w_in[:, o:o + n]
        o += n
        return s

    q, kc, vc, ks, vs, kw, vw = take(ATTN_W), *(take(KV_W) for _ in range(6))
    gates = take(3 * N_HEADS).reshape(-1, 3, N_KV, HPG).transpose(0, 2, 1, 3).reshape(-1, N_KV, 3 * HPG)
    gates = jnp.pad(gates, ((0, 0), (0, 0), (0, LANES - 3 * HPG))).reshape(-1, N_KV * LANES)
    conv = take(3 * CONV_CH)
    w_r = jnp.concatenate([q, kc, vc, ks, kw, conv, gates], axis=1).astype(BF16)
    return w_r, jnp.concatenate([vs, vw], axis=1).T.astype(BF16)


def _compress_kernel(x_ref, pos_ref, w1_ref, b1_ref, w2_ref, b2_ref, w2t_ref, b2t_ref, o_ref, ot_ref):
    half = CMP_STRIDE * HEAD_DIM
    x = x_ref[0, 0]
    pos = pos_ref[0]
    xa = (x + pos[:, :half]).astype(BF16)
    xb = (x + pos[:, half:]).astype(BF16)
    a = _dot(xa, w1_ref[0, :half, :])
    b = _dot(xb, w1_ref[0, half:, :])
    n = x.shape[0]
    hid = a + pltpu.roll(b, n - 1, 0) + b1_ref[0]
    act = jax.nn.gelu(hid).astype(BF16)
    o_ref[0, 0] = (_dot(act, w2_ref[0]) + b2_ref[0]).astype(BF16)
    ot_ref[0, 0] = (_dot_nt(w2t_ref[0], act) + b2t_ref[0]).astype(BF16)


def _compress(kvc, cmp_pos, cmp_w1, cmp_b1, cmp_w2, cmp_b2, *, batch, seq):
    nb = seq // CMP_STRIDE
    half = CMP_STRIDE * HEAD_DIM
    x = kvc.reshape(4, batch, nb, half)
    pos = cmp_pos.reshape(2, 1, CMP_BLOCK * HEAD_DIM)
    w1 = cmp_w1.astype(BF16)
    b1 = cmp_b1.reshape(2, 1, CMP_HIDDEN)
    w2 = jnp.pad(cmp_w2, ((0, 0), (0, 0), (0, HPAD - HEAD_DIM))).astype(BF16)
    b2 = jnp.pad(cmp_b2, ((0, 0), (0, HPAD - HEAD_DIM))).reshape(2, 1, HPAD)
    w2t = jnp.swapaxes(w2, 1, 2)
    b2t = b2.reshape(2, HPAD, 1)
    kind = lambda j, b: (j // N_KV, 0, 0)
    return pl.pallas_call(
        _compress_kernel,
        grid=(4, batch),
        in_specs=[
            pl.BlockSpec((1, 1, nb, half), lambda j, b: (j, b, 0, 0)),
            pl.BlockSpec((1, 1, 2 * half), kind),
            pl.BlockSpec((1, 2 * half, CMP_HIDDEN), kind),
            pl.BlockSpec((1, 1, CMP_HIDDEN), kind),
            pl.BlockSpec((1, CMP_HIDDEN, HPAD), kind),
            pl.BlockSpec((1, 1, HPAD), kind),
            pl.BlockSpec((1, HPAD, CMP_HIDDEN), kind),
            pl.BlockSpec((1, HPAD, 1), kind),
        ],
        out_specs=[pl.BlockSpec((1, 1, nb, HPAD), lambda j, b: (j, b, 0, 0)),
                   pl.BlockSpec((1, 1, HPAD, nb), lambda j, b: (j, b, 0, 0))],
        out_shape=[jax.ShapeDtypeStruct((4, batch, nb, HPAD), BF16),
                   jax.ShapeDtypeStruct((4, batch, HPAD, nb), BF16)],
        compiler_params=_cparams(("arbitrary", "arbitrary")),
        name="compress",
    )(x, pos, w1, b1, w2, b2, w2t, b2t)


def _overlap_t():
    n = np.arange(LANES)
    j = np.arange(32)
    cs = n * CMP_STRIDE
    ss = j * SEL_BLOCK
    ov = (cs[None, :] < ss[:, None] + SEL_BLOCK) & (cs[None, :] + CMP_BLOCK > ss[:, None])
    return jnp.asarray(ov, dtype=BF16)


def _cmp_select(qh, kc, vct, ov, i, tq, n_sel):
    n = lax.broadcasted_iota(jnp.int32, (LANES, tq), 0)
    t = lax.broadcasted_iota(jnp.int32, (LANES, tq), 1) + i * tq
    mask = n * CMP_STRIDE + (CMP_BLOCK - 1) <= t
    maskf = mask.astype(F32)
    ss = [jnp.where(mask, _dot_nt(kc, q), NEG) for q in qh]
    es = [jnp.exp2(s - jnp.max(s, axis=0, keepdims=True)) for s in ss]
    ps = [e / jnp.sum(e, axis=0, keepdims=True) * maskf for e in es]
    heads_t = [_dot(vct[:HEAD_DIM], p.astype(BF16)) for p in ps]
    psum = functools.reduce(lambda a, b: a + b, ps)

    p_hi = psum.astype(BF16)
    r1 = psum - p_hi.astype(F32)
    p_mid = r1.astype(BF16)
    p_lo = (r1 - p_mid.astype(F32)).astype(BF16)
    imp = _dot(ov, p_hi) + _dot(ov, p_mid) + _dot(ov, p_lo)

    jt = lax.broadcasted_iota(jnp.int32, (n_sel, tq), 0)
    tt = lax.broadcasted_iota(jnp.int32, (n_sel, tq), 1) + i * tq
    cur = tt // SEL_BLOCK
    valid = jt * SEL_BLOCK <= tt
    forced = (jt == 0) | (jt == cur) | (jt == cur - 1)
    score = jnp.where(forced, FORCE, jnp.where(valid, imp, NEG))
    rank = jnp.zeros((n_sel, tq), jnp.int32)
    for k in range(n_sel):
        sk = score[k:k + 1, :]
        ahead = (sk > score) | ((sk == score) & (jt > k))
        rank = rank + ahead.astype(jnp.int32)
    bias = jnp.where(rank < SEL_TOPN, 0.0, NEG)
    full = jnp.concatenate(
        [jnp.zeros((HEAD_DIM, tq), F32), bias, jnp.zeros((HPAD - HEAD_DIM - n_sel, tq), F32)], axis=0)
    return heads_t, full.T.astype(BF16)


def _attn_scores(k, qa):
    return tuple(_dot_nt(k, qa[hh]) for hh in range(HPG))


def _attn_update(ss, vt, state, mask):
    if mask is not None:
        ss = [jnp.where(mask, s, NEG) for s in ss]
    m_new = [jnp.maximum(state[hh][0], jnp.max(ss[hh], axis=0, keepdims=True)) for hh in range(HPG)]
    ps = [jnp.exp2(ss[hh] - m_new[hh]).astype(BF16) for hh in range(HPG)]
    pv = [_dot(vt, ps[hh]) for hh in range(HPG)]
    return tuple((m_new[hh], jnp.exp2(state[hh][0] - m_new[hh]) * state[hh][1] + pv[hh]) for hh in range(HPG))


def _attn_init(tq):
    return tuple((jnp.full((1, tq), NEG, F32), jnp.zeros((VROWS, tq), F32)) for _ in range(HPG))


def _nsa_attn_kernel(q_ref, kc_ref, vct_ref, ov_ref, ks_ref, vst_ref, kw_ref, vwt_ref, gate_ref, o_ref,
                     *, tq, tk, n_sel):
    i = pl.program_id(0)
    nq = ks_ref.shape[0] // tk
    qh = [q_ref[:, hh * HPAD:(hh + 1) * HPAD] for hh in range(HPG)]
    heads_cmp, selb = _cmp_select(qh, kc_ref[0, 0], vct_ref[0, 0], ov_ref[...], i, tq, n_sel)
    lane = lax.broadcasted_iota(jnp.int32, selb.shape, 1)
    qsel = [jnp.where(lane >= HEAD_DIM, selb, q) for q in qh]
    key = lax.broadcasted_iota(jnp.int32, (tk, tq), 0)
    qry = lax.broadcasted_iota(jnp.int32, (tk, tq), 1)
    causal = key <= qry
    gt = gate_ref[...].T

    key2 = lax.broadcasted_iota(jnp.int32, (2 * tk, tq), 0)
    pair_causal = key2 - tk <= lax.broadcasted_iota(jnp.int32, (2 * tk, tq), 1)

    def run(n):
        chunks = []
        for j in range(0, n, 2):
            chunks.append(("sel", j, 2, pair_causal if j + 1 == n else None))
        if n % 2 == 0:
            chunks.append(("sel", n, 1, causal))
        if n == 0:
            chunks.append(("win", 0, 1, causal))
        else:
            chunks.append(("win", n - 1, 2, pair_causal))
        if n >= 2:
            chunks.append(("win", n - 2, 1, key > qry))
        refs = {"sel": (ks_ref, vst_ref, qsel), "win": (kw_ref, vwt_ref, qh)}

        def scores(chunk):
            branch, j, nt, _ = chunk
            return _attn_scores(refs[branch][0][j * tk:(j + nt) * tk, :], refs[branch][2])

        state = {"sel": _attn_init(tq), "win": _attn_init(tq)}
        ss = scores(chunks[0])
        for idx, (branch, j, nt, mask) in enumerate(chunks):
            ss_next = scores(chunks[idx + 1]) if idx + 1 < len(chunks) else None
            vt_ref = refs[branch][1]
            vt = vt_ref[j] if nt == 1 else jnp.concatenate([vt_ref[j], vt_ref[j + 1]], axis=1)
            state[branch] = _attn_update(ss, vt, state[branch], mask)
            ss = ss_next

        outs = []
        for hh in range(HPG):
            o_sel, o_win = (acc[:HEAD_DIM] / acc[HEAD_DIM:HEAD_DIM + 1]
                            for _, acc in (state["sel"][hh], state["win"][hh]))
            g_cmp, g_sel, g_win = (gt[c * HPG + hh:c * HPG + hh + 1] for c in range(3))
            outs.append(g_cmp * heads_cmp[hh] + g_sel * o_sel + g_win * o_win)
        o_ref[...] = jnp.concatenate(outs, axis=0).T.astype(o_ref.dtype)

    for n in range(nq):
        pl.when(i == n)(functools.partial(run, n))


def _nsa_attn(q, kvcmp, kvcmp_t, ks, vs_t, kw, vw_t, gates, *, batch, seq):
    t = q.shape[0]
    tq, tk = TQ, TK
    assert tq == tk and WINDOW == 2 * tk
    nq = seq // tq
    n_sel = seq // SEL_BLOCK
    nb = kvcmp.shape[2]
    assert nb == LANES and n_sel == 32
    rowblk = lambda w: pl.BlockSpec((tq, w), lambda i, b, g: (b * nq + i, g))
    seqblk = pl.BlockSpec((seq, HPAD), lambda i, b, g: (b, g))
    vtblk = pl.BlockSpec((seq // tk, VROWS, tk), lambda i, b, g: (b, g, 0))
    return pl.pallas_call(
        functools.partial(_nsa_attn_kernel, tq=tq, tk=tk, n_sel=n_sel),
        grid=(nq, batch, N_KV),
        in_specs=[rowblk(HPG * HPAD),
                  pl.BlockSpec((1, 1, nb, HPAD), lambda i, b, g: (g, b, 0, 0)),
                  pl.BlockSpec((1, 1, HPAD, nb), lambda i, b, g: (N_KV + g, b, 0, 0)),
                  pl.BlockSpec((n_sel, LANES), lambda i, b, g: (0, 0)),
                  seqblk, vtblk, seqblk, vtblk,
                  rowblk(LANES)],
        out_specs=rowblk(HPG * HEAD_DIM),
        out_shape=jax.ShapeDtypeStruct((t, ATTN_W), BF16),
        compiler_params=_cparams(("arbitrary", "arbitrary", "arbitrary")),
        name="nsa_attn",
    )(q, kvcmp, kvcmp_t, _overlap_t(), ks, vs_t, kw, vw_t, gates)


def _dense_tail_kernel(*refs, alpha, with_proj, tiles_per_seq, tm):
    (h_ref, oa_ref, ocv_ref, wo_ref, g1_ref, b1_ref, wg_ref, wu_ref, wd_ref, g2_ref, b2_ref) = refs[:11]
    m = _dot(oa_ref[...], wo_ref[:ATTN_W, :]) + _dot(ocv_ref[...], wo_ref[ATTN_W:, :])
    h1 = _ln(alpha * h_ref[...] + m, g1_ref[...], b1_ref[...])
    hb = h1.astype(BF16)
    a = (jax.nn.silu(_dot(hb, wg_ref[...])) * _dot(hb, wu_ref[...])).astype(BF16)
    h2 = _ln(alpha * h1 + _dot(a, wd_ref[...]), g2_ref[...], b2_ref[...])
    if with_proj:
        w_refs = refs[11:11 + N_PROJ_W]
        outs = refs[11 + N_PROJ_W:len(refs) - N_PROJ_SCRATCH]
        outs[0][...] = h2
        _project_inputs(h2, w_refs, outs[1:], refs[len(refs) - N_PROJ_SCRATCH:], tiles_per_seq=tiles_per_seq, tm=tm)
    else:
        refs[11][...] = h2


def _dense_tail(h, o_attn, o_conv, w_o, ln1, ffn_w, ln2, proj_w, *, alpha, seq):
    t = h.shape[0]
    tm = TM_PROJ
    row = lambda w: pl.BlockSpec((tm, w), lambda i: (i, 0))
    vec = _resident((1, D_MODEL))
    in_specs = [row(D_MODEL), row(ATTN_W), row(CONV_CH), _resident((D_MODEL, D_MODEL)), vec, vec,
                _resident((D_MODEL, D_FF)), _resident((D_MODEL, D_FF)), _resident((D_FF, D_MODEL)), vec, vec]
    args = [h, o_attn, o_conv, w_o, *ln1, *ffn_w, *ln2]
    out_shape, out_specs, scratch = [jax.ShapeDtypeStruct((t, D_MODEL), F32)], [row(D_MODEL)], []
    if proj_w is not None:
        w_specs, p_shape, p_specs, scratch = _proj_specs(t, tm)
        in_specs += w_specs
        args += list(proj_w)
        out_shape += p_shape
        out_specs += p_specs
    return pl.pallas_call(
        functools.partial(_dense_tail_kernel, alpha=alpha, with_proj=proj_w is not None,
                          tiles_per_seq=seq // tm, tm=tm),
        grid=(t // tm,),
        in_specs=in_specs,
        out_specs=out_specs,
        out_shape=out_shape,
        scratch_shapes=scratch,
        compiler_params=_cparams(("arbitrary",)),
        name="dense_tail",
    )(*args)


TMX = 512
CH = 16
LROWS = 2 * TMX + N_EXPERTS * CH
TMR = 512
XROW = D_MODEL + LANES


def _out_proj_kernel(h_ref, oa_ref, ocv_ref, w_ref, g_ref, b_ref, o_ref, *, alpha):
    m = _dot(oa_ref[...], w_ref[:ATTN_W, :]) + _dot(ocv_ref[...], w_ref[ATTN_W:, :])
    o_ref[...] = _ln(alpha * h_ref[...] + m, g_ref[...], b_ref[...])


def _out_proj(h, o_attn, o_conv, w_o, ln_g, ln_b, *, alpha):
    t = h.shape[0]
    tm = TM_PROJ
    row = lambda w: pl.BlockSpec((tm, w), lambda i: (i, 0))
    return pl.pallas_call(
        functools.partial(_out_proj_kernel, alpha=alpha),
        grid=(t // tm,),
        in_specs=[row(D_MODEL), row(ATTN_W), row(CONV_CH),
                  _resident((D_MODEL, D_MODEL)), _resident((1, D_MODEL)), _resident((1, D_MODEL))],
        out_specs=row(D_MODEL),
        out_shape=jax.ShapeDtypeStruct((t, D_MODEL), F32),
        compiler_params=_cparams(("arbitrary",)),
        name="out_proj",
    )(h, o_attn, o_conv, w_o, ln_g, ln_b)


def _route_kernel(h_ref, wr_ref, ltri_ref, ustr_ref, meta_ref, metat_ref, cnt_ref, wp_ref, *, tm):
    h = h_ref[...]
    h_hi = h.astype(BF16)
    h_lo = (h - h_hi.astype(F32)).astype(BF16)
    logits = _dot(h_hi, wr_ref[0]) + _dot(h_lo, wr_ref[0]) + _dot(h_hi, wr_ref[1])
    lane = lax.broadcasted_iota(jnp.int32, logits.shape, 1)
    logits = jnp.where(lane < N_EXPERTS, logits, -jnp.inf)
    m1 = jnp.max(logits, axis=-1, keepdims=True)
    i1 = jnp.min(jnp.where(logits == m1, lane, LANES), axis=-1, keepdims=True)
    rest = jnp.where(lane == i1, -jnp.inf, logits)
    m2 = jnp.max(rest, axis=-1, keepdims=True)
    i2 = jnp.min(jnp.where(rest == m2, lane, LANES), axis=-1, keepdims=True)
    e2 = jnp.exp(m2 - m1)
    den = 1.0 + e2
    w1 = 1.0 / den
    w2 = e2 / den

    routed = (lane == i1) | (lane == i2)
    cnt = _dot(ltri_ref[...], routed.astype(BF16))
    n = cnt[tm - 1:tm, :]
    padded = jnp.floor((n + (CH - 1)) * (1.0 / CH)) * CH
    seg_off = _dot(jnp.broadcast_to(padded, (8, LANES)).astype(BF16), ustr_ref[...])[0:1]
    dest = seg_off + cnt - 1.0
    d1 = jnp.sum(jnp.where(lane == i1, dest, 0.0), axis=-1, keepdims=True)
    d2 = jnp.sum(jnp.where(lane == i2, dest, 0.0), axis=-1, keepdims=True)
    meta = jnp.where(lane == 0, d1, jnp.where(lane == 1, d2, jnp.where(lane == 2, w1, jnp.where(lane == 3, w2, 0.0))))
    meta_ref[...] = meta
    metat_ref[...] = meta.T[0:8, :]
    cnt_ref[0] = jnp.broadcast_to(n, (8, LANES))
    for k, w in enumerate((w1, w2)):
        hi = w.astype(BF16).astype(F32)
        mid = (w - hi).astype(BF16).astype(F32)
        lo = (w - hi - mid).astype(BF16).astype(F32)
        wp_ref[k] = jnp.where(lane == 0, hi, jnp.where(lane == 1, mid, jnp.where(lane == 2, lo, 0.0))).astype(BF16)


def _moe_route(h, w_router):
    t = h.shape[0]
    tm = TMX
    nt = t // tm
    row = lambda w: pl.BlockSpec((tm, w), lambda i: (i, 0))
    wr = jnp.pad(w_router, ((0, 0), (0, LANES - N_EXPERTS)))
    wr_hi = wr.astype(BF16)
    wr_lo = (wr - wr_hi.astype(F32)).astype(BF16)
    wr2 = jnp.stack([wr_hi, wr_lo])
    ltri = jnp.asarray(np.tril(np.ones((tm, tm), np.float32)), dtype=BF16)
    ustr = jnp.asarray(np.triu(np.ones((LANES, LANES), np.float32), 1), dtype=BF16)
    return pl.pallas_call(
        functools.partial(_route_kernel, tm=tm),
        grid=(nt,),
        in_specs=[row(D_MODEL), _resident((2, D_MODEL, LANES)), _resident((tm, tm)), _resident((LANES, LANES))],
        out_specs=[row(LANES),
                   pl.BlockSpec((8, tm), lambda i: (0, i)),
                   pl.BlockSpec((1, 8, LANES), lambda i: (i, 0, 0)),
                   pl.BlockSpec((2, tm, LANES), lambda i: (0, i, 0))],
        out_shape=[jax.ShapeDtypeStruct((t, LANES), F32), jax.ShapeDtypeStruct((8, t), F32),
                   jax.ShapeDtypeStruct((nt, 8, LANES), F32), jax.ShapeDtypeStruct((2, t, LANES), BF16)],
        compiler_params=_cparams(("arbitrary",)),
        name="moe_route",
    )(h, wr2, ltri, ustr)


def _moe_plan(cnt, n_row_tiles):
    n = cnt[:, 0, :N_EXPERTS].astype(jnp.int32)
    p = (n + CH - 1) // CH * CH
    tot = p.sum(0)
    tot_pad = (tot + TMR - 1) // TMR * TMR
    gend = jnp.cumsum(tot_pad)
    gstart = gend - tot_pad
    goff = gstart[None, :] + jnp.cumsum(p, 0) - p
    loff = jnp.cumsum(p, 1) - p
    n_used = gend[-1:] // TMR
    tile_start = jnp.arange(n_row_tiles, dtype=jnp.int32) * TMR
    texp = jnp.minimum(jnp.sum(tile_start[:, None] >= gend[None, :], axis=1), N_EXPERTS - 1).astype(jnp.int32)
    texp = jnp.where(jnp.arange(n_row_tiles) < n_used[0], texp, texp[jnp.maximum(n_used[0] - 1, 0)])
    flat = lambda a: a.reshape(-1).astype(jnp.int32)
    gap_start = jnp.concatenate([gstart + tot, gend[-1:]])
    gap_rows = jnp.concatenate([tot_pad - tot, n_row_tiles * TMR - gend[-1:]])
    return dict(loff=flat(loff), goff=flat(goff), nch=flat(p // CH), gap_start=flat(gap_start),
                gap_nch=flat(gap_rows // CH), texp=flat(texp), n_used=flat(n_used))


def _chunk_copies(src_ref, dst_ref, sem, src_off, dst_off, n, src_step=CH):
    def body(c, carry):
        s = pl.multiple_of(src_off + c * src_step, CH)
        d = pl.multiple_of(dst_off + c * CH, CH)
        pltpu.make_async_copy(src_ref.at[pl.ds(s, CH)], dst_ref.at[pl.ds(d, CH)], sem).start()
        return carry
    lax.fori_loop(0, n, body, 0)


def _wait_chunks(src_ref, dst_ref, sem, n):
    def body(c, carry):
        pltpu.make_async_copy(src_ref.at[pl.ds(0, CH)], dst_ref.at[pl.ds(0, CH)], sem).wait()
        return carry
    lax.fori_loop(0, n, body, 0)


def _tile_chunks(tile, nch):
    total = 0
    for e in range(N_EXPERTS):
        total = total + nch[tile * N_EXPERTS + e]
    return total


def _start_segments(tile, nch, src_off, dst_off, src_ref, dst_ref, sem):
    for e in range(N_EXPERTS):
        idx = tile * N_EXPERTS + e
        _chunk_copies(src_ref, dst_ref, sem, src_off[idx], dst_off[idx], nch[idx])


def _dispatch_kernel(loff, goff, nch, gap_start, gap_nch, h_ref, metat_ref, wp_ref, xs_ref, xc_ref, z_ref, sem,
                     *, tm):
    i = pl.program_id(0)
    last = pl.num_programs(0) - 1
    slot = i % 2
    r = lax.broadcasted_iota(jnp.int32, (LROWS, tm), 0).astype(F32)
    pick1 = (r == metat_ref[0:1, :]).astype(BF16)
    pick2 = (r == metat_ref[1:2, :]).astype(BF16)
    xc_ref[slot, :, :D_MODEL] = _dot(pick1 + pick2, h_ref[...].astype(BF16)).astype(BF16)
    xc_ref[slot, :, D_MODEL:] = (_dot(pick1, wp_ref[0]) + _dot(pick2, wp_ref[1])).astype(BF16)
    _start_segments(i, nch, loff, goff, xc_ref.at[slot], xs_ref, sem.at[slot])

    @pl.when(i > 0)
    def _():
        _wait_chunks(xc_ref.at[1 - slot], xs_ref, sem.at[1 - slot], _tile_chunks(i - 1, nch))

    @pl.when(i == last)
    def _():
        z_ref[...] = jnp.zeros_like(z_ref)
        gaps = 0
        for e in range(N_EXPERTS + 1):
            _chunk_copies(z_ref, xs_ref, sem.at[2], 0, gap_start[e], gap_nch[e], src_step=0)
            gaps = gaps + gap_nch[e]
        _wait_chunks(z_ref, xs_ref, sem.at[2], gaps)
        _wait_chunks(xc_ref.at[slot], xs_ref, sem.at[slot], _tile_chunks(i, nch))


def _moe_dispatch(h, metat, wparts, plan, n_rows):
    t = h.shape[0]
    tm = TMX
    grid_spec = pltpu.PrefetchScalarGridSpec(
        num_scalar_prefetch=5,
        grid=(t // tm,),
        in_specs=[pl.BlockSpec((tm, D_MODEL), lambda i, *_: (i, 0)),
                  pl.BlockSpec((8, tm), lambda i, *_: (0, i)),
                  pl.BlockSpec((2, tm, LANES), lambda i, *_: (0, i, 0))],
        out_specs=pl.BlockSpec(memory_space=pl.ANY),
        scratch_shapes=[pltpu.VMEM((2, LROWS, XROW), BF16), pltpu.VMEM((CH, XROW), BF16),
                        pltpu.SemaphoreType.DMA((3,))],
    )
    return pl.pallas_call(
        functools.partial(_dispatch_kernel, tm=tm),
        grid_spec=grid_spec,
        out_shape=jax.ShapeDtypeStruct((n_rows, XROW), BF16),
        compiler_params=_cparams(("arbitrary",)),
        name="moe_dispatch",
    )(plan["loff"], plan["goff"], plan["nch"], plan["gap_start"], plan["gap_nch"], h, metat, wparts)


def _experts_kernel(texp, n_used, x_ref, wg_ref, wu_ref, wd_ref, y_ref):
    used = pl.program_id(0) < n_used[0]

    @pl.when(used)
    def _():
        x = x_ref[:, :D_MODEL]
        wp = x_ref[:, D_MODEL:].astype(F32)
        gate = wp[:, 0:1] + wp[:, 1:2] + wp[:, 2:3]
        a = (jax.nn.silu(_dot(x, wg_ref[0])) * _dot(x, wu_ref[0])).astype(BF16)
        y_ref[...] = (gate * _dot(a, wd_ref[0])).astype(BF16)

    @pl.when(jnp.logical_not(used))
    def _():
        y_ref[...] = jnp.zeros_like(y_ref)


def _moe_experts(xs, wg, wu, wd, plan):
    n_rows = xs.shape[0]
    rows = lambda r, texp, n_used: (jnp.minimum(r, n_used[0] - 1), 0)
    wspec = lambda shape: pl.BlockSpec((1,) + shape, lambda r, texp, n_used: (texp[r], 0, 0))
    grid_spec = pltpu.PrefetchScalarGridSpec(
        num_scalar_prefetch=2,
        grid=(n_rows // TMR,),
        in_specs=[pl.BlockSpec((TMR, XROW), rows), wspec((D_MODEL, D_FF_EXPERT)),
                  wspec((D_MODEL, D_FF_EXPERT)), wspec((D_FF_EXPERT, D_MODEL))],
        out_specs=pl.BlockSpec((TMR, D_MODEL), lambda r, texp, n_used: (r, 0)),
    )
    return pl.pallas_call(
        _experts_kernel,
        grid_spec=grid_spec,
        out_shape=jax.ShapeDtypeStruct((n_rows, D_MODEL), BF16),
        compiler_params=_cparams(("arbitrary",)),
        name="moe_experts",
    )(plan["texp"], plan["n_used"], xs, wg, wu, wd)


def _combine_kernel(loff, goff, nch, h_ref, meta_ref, ys_ref, g_ref, b_ref, o_ref, yb_ref, sem, *, tm, alpha):
    i = pl.program_id(0)
    slot = i % 2

    @pl.when(i == 0)
    def _():
        yb_ref[...] = jnp.zeros_like(yb_ref)
        _start_segments(0, nch, goff, loff, ys_ref, yb_ref.at[0], sem.at[0])

    @pl.when(i + 1 < pl.num_programs(0))
    def _():
        _start_segments(i + 1, nch, goff, loff, ys_ref, yb_ref.at[1 - slot], sem.at[1 - slot])

    meta = meta_ref[...]
    r = lax.broadcasted_iota(jnp.int32, (tm, LROWS), 1).astype(F32)
    pick = ((r == meta[:, 0:1]) | (r == meta[:, 1:2])).astype(BF16)
    _wait_chunks(ys_ref, yb_ref.at[slot], sem.at[slot], _tile_chunks(i, nch))
    f = _dot(pick, yb_ref[slot])
    o_ref[...] = _ln(alpha * h_ref[...] + f, g_ref[...], b_ref[...])


def _moe_combine(h, meta, ys, plan, ln_g, ln_b, *, alpha):
    t = h.shape[0]
    tm = TMX
    grid_spec = pltpu.PrefetchScalarGridSpec(
        num_scalar_prefetch=3,
        grid=(t // tm,),
        in_specs=[pl.BlockSpec((tm, D_MODEL), lambda i, *_: (i, 0)),
                  pl.BlockSpec((tm, LANES), lambda i, *_: (i, 0)),
                  pl.BlockSpec(memory_space=pl.ANY),
                  pl.BlockSpec((1, D_MODEL), lambda i, *_: (0, 0)),
                  pl.BlockSpec((1, D_MODEL), lambda i, *_: (0, 0))],
        out_specs=pl.BlockSpec((tm, D_MODEL), lambda i, *_: (i, 0)),
        scratch_shapes=[pltpu.VMEM((2, LROWS, D_MODEL), BF16), pltpu.SemaphoreType.DMA((2,))],
    )
    return pl.pallas_call(
        functools.partial(_combine_kernel, tm=tm, alpha=alpha),
        grid_spec=grid_spec,
        out_shape=jax.ShapeDtypeStruct((t, D_MODEL), F32),
        compiler_params=_cparams(("arbitrary",)),
        name="moe_combine",
    )(plan["loff"], plan["goff"], plan["nch"], h, meta, ys, ln_g, ln_b)


def _moe_tail(h, o_attn, o_conv, w_o, ln1, w_router, wg, wu, wd, ln_g, ln_b, *, alpha):
    t = h.shape[0]
    nt = t // TMX
    max_rows = 2 * t + nt * N_EXPERTS * (CH - 1) + N_EXPERTS * (TMR - CH)
    n_row_tiles = -(-max_rows // TMR)
    h = _out_proj(h, o_attn, o_conv, w_o, *ln1, alpha=alpha)
    meta, metat, cnt, wparts = _moe_route(h, w_router)
    plan = _moe_plan(cnt, n_row_tiles)
    xs = _moe_dispatch(h, metat, wparts, plan, n_row_tiles * TMR)
    ys = _moe_experts(xs, wg, wu, wd, plan)
    return _moe_combine(h, meta, ys, plan, ln_g, ln_b, alpha=alpha)


def kernel(x, ln_in_g, ln_in_b, w_in, cmp_pos, cmp_w1, cmp_b1, cmp_w2, cmp_b2, conv_w, w_o, ln1_g, ln1_b, ln2_g, ln2_b, ffn_wg, ffn_wu, ffn_wd, moe_router, moe_wg, moe_wu, moe_wd):
    batch, seq, d = x.shape
    depth = w_in.shape[0]
    assert d == D_MODEL and seq % TM_PROJ == 0 and seq // CMP_STRIDE == LANES
    alpha = (2 * depth) ** 0.25
    t = batch * seq
    vec = lambda v: v.reshape(1, D_MODEL)
    def proj_weights(l):
        return (*_prep_w_in(w_in[l]), jnp.pad(conv_w[l], ((0, 8 - CONV_W), (0, 0))))

    h, *proj = _in_proj(x.reshape(t, d), vec(ln_in_g), vec(ln_in_b), proj_weights(0), seq=seq, pre_ln=True)
    for l in range(depth):
        q, kvc, ks, vs, kw, vw, gates, o_conv = proj
        kvcmp, kvcmp_t = _compress(kvc, cmp_pos[l], cmp_w1[l], cmp_b1[l], cmp_w2[l], cmp_b2[l],
                                   batch=batch, seq=seq)
        o_attn = _nsa_attn(q, kvcmp, kvcmp_t, ks, vs, kw, vw, gates, batch=batch, seq=seq)
        ln1 = (vec(ln1_g[l]), vec(ln1_b[l]))
        ln2 = (vec(ln2_g[l]), vec(ln2_b[l]))
        next_w = proj_weights(l + 1) if l + 1 < depth else None
        if l % 2 == 0:
            ffn_w = (ffn_wg[l // 2].astype(BF16), ffn_wu[l // 2].astype(BF16), ffn_wd[l // 2].astype(BF16))
            h, *proj = _dense_tail(h, o_attn, o_conv, w_o[l].astype(BF16), ln1, ffn_w, ln2, next_w,
                                   alpha=alpha, seq=seq)
        else:
            h = _moe_tail(h, o_attn, o_conv, w_o[l].astype(BF16), ln1, moe_router[l // 2],
                          moe_wg[l // 2].astype(BF16), moe_wu[l // 2].astype(BF16), moe_wd[l // 2].astype(BF16),
                          *ln2, alpha=alpha)
            if next_w is not None:
                proj = _in_proj(h, None, None, next_w, seq=seq, pre_ln=False)
    return h.reshape(batch, seq, d)
```

```python
import functools

import numpy as np
import jax
import jax.numpy as jnp
from jax import lax
from jax.experimental import pallas as pl
from jax.experimental.pallas import tpu as pltpu

F32 = jnp.float32
BF16 = jnp.bfloat16

D_MODEL = 1024
HEAD_DIM = 64
N_HEADS = 8
N_KV = 2
HPG = N_HEADS // N_KV
ATTN_W = N_HEADS * HEAD_DIM
KV_W = N_KV * HEAD_DIM
CONV_CH = D_MODEL - ATTN_W
CONV_W = 3
CMP_BLOCK = 32
CMP_STRIDE = 16
CMP_HIDDEN = 256
SEL_BLOCK = 64
SEL_TOPN = 16
WINDOW = 512
D_FF = 2816
N_EXPERTS = 8
D_FF_EXPERT = 1408
LN_EPS = 1e-5
NEG = -1e30
FORCE = 1e9

LANES = 128
HPAD = LANES
BF16_SUBLANES = 16
VROWS = HEAD_DIM + BF16_SUBLANES
Q_SCALE = HEAD_DIM ** -0.5 * np.log2(np.e)

C_Q = 0
C_KVC = C_Q + ATTN_W
C_KS = C_KVC + 2 * KV_W
C_KW = C_KS + KV_W
C_U = C_KW + KV_W
C_B = C_U + CONV_CH
C_C = C_B + CONV_CH
C_G = C_C + CONV_CH
C_END = C_G + N_KV * LANES

TM_PROJ = 512
TQ = 256
TK = 256
VMEM_LIMIT = 56 * 1024 * 1024


def _cparams(sem):
    return pltpu.CompilerParams(dimension_semantics=sem, vmem_limit_bytes=VMEM_LIMIT)


def _ln(x, g, b):
    mu = jnp.mean(x, -1, keepdims=True)
    xc = x - mu
    var = jnp.mean(xc * xc, -1, keepdims=True)
    return xc * lax.rsqrt(var + LN_EPS) * g + b


def _dot(a, b):
    return jnp.dot(a, b, preferred_element_type=F32)


def _dot_nt(a, b):
    return lax.dot_general(a, b, (((1,), (1,)), ((), ())), preferred_element_type=F32)


def _resident(shape):
    nd = len(shape)
    return pl.BlockSpec(shape, lambda *_: (0,) * nd, pipeline_mode=pl.Buffered(1))


def _spread_heads(z):
    low = lax.broadcasted_iota(jnp.int32, (z.shape[0], LANES), 1) < HEAD_DIM
    tiles = []
    for p in range(z.shape[1] // LANES):
        pair = z[:, p * LANES:(p + 1) * LANES]
        tiles += [jnp.where(low, pair, 0.0), jnp.where(low, pltpu.roll(pair, HEAD_DIM, 1), 0.0)]
    return jnp.concatenate(tiles, axis=1)


def _project_inputs(h, w_refs, out_refs, scratch_refs, *, tiles_per_seq, tm):
    w_ref, wvt_ref, cw_ref = w_refs
    q_ref, kvc_ref, ks_ref, vs_ref, kw_ref, vw_ref, gate_ref, oc_ref = out_refs
    carry_ref, kvc_scr = scratch_refs
    hb = h.astype(BF16)

    def proj(lo, hi):
        return _dot(hb, w_ref[:, lo:hi])

    q_ref[...] = _spread_heads(proj(C_Q, C_KVC) * Q_SCALE).astype(BF16)

    kvc = proj(C_KVC, C_KS)
    low = lax.broadcasted_iota(jnp.int32, (tm // CMP_STRIDE, LANES), 1) < HEAD_DIM
    for kind in range(2):
        kvc_scr[kind] = kvc[:, kind * LANES:(kind + 1) * LANES]
        for p in range(CMP_STRIDE // 2):
            a = kvc_scr[kind, pl.ds(2 * p, tm // CMP_STRIDE, stride=CMP_STRIDE), :]
            b = kvc_scr[kind, pl.ds(2 * p + 1, tm // CMP_STRIDE, stride=CMP_STRIDE), :]
            kvc_ref[2 * kind, :, p * LANES:(p + 1) * LANES] = jnp.where(low, a, pltpu.roll(b, HEAD_DIM, 1))
            kvc_ref[2 * kind + 1, :, p * LANES:(p + 1) * LANES] = jnp.where(low, pltpu.roll(a, HEAD_DIM, 1), b)

    seq_tile = pl.program_id(0) % tiles_per_seq
    pos = lax.broadcasted_iota(jnp.int32, (tm, N_KV * HPAD), 0) + seq_tile * tm
    lane = lax.broadcasted_iota(jnp.int32, (tm, N_KV * HPAD), 1) % HPAD
    onehot = jnp.where(pos // SEL_BLOCK == lane - HEAD_DIM, 1.0, 0.0)
    in_tag = (lane >= HEAD_DIM) & (lane < HEAD_DIM + 32)
    kk = proj(C_KS, C_U)
    ks_ref[...] = jnp.where(in_tag, onehot, _spread_heads(kk[:, :KV_W])).astype(BF16)
    kw_ref[...] = _spread_heads(kk[:, KV_W:]).astype(BF16)

    vt = _dot_nt(wvt_ref[...], hb)
    tail = jnp.where(lax.broadcasted_iota(jnp.int32, (VROWS - HEAD_DIM, tm), 0) == 0, 1.0, 0.0)
    for n, ref in enumerate((vs_ref, vw_ref)):
        rows = []
        for grp in range(N_KV):
            lo = (n * N_KV + grp) * HEAD_DIM
            rows += [vt[lo:lo + HEAD_DIM], tail]
        full = jnp.concatenate(rows, axis=0).astype(BF16)
        for c in range(tm // TK):
            ref[c] = full[:, c * TK:(c + 1) * TK]

    cu = proj(C_C, C_G) * proj(C_U, C_B)

    @pl.when(seq_tile == 0)
    def _():
        carry_ref[...] = jnp.zeros_like(carry_ref)

    prev = carry_ref[...]
    row = lax.broadcasted_iota(jnp.int32, (tm, CONV_CH), 0)
    s1 = jnp.where(row == 0, prev[7:8], pltpu.roll(cu, 1, 0))
    s2 = jnp.where(row == 0, prev[6:7], jnp.where(row == 1, prev[7:8], pltpu.roll(cu, 2, 0)))
    y = s2 * cw_ref[0:1, :] + s1 * cw_ref[1:2, :] + cu * cw_ref[2:3, :]
    oc_ref[...] = (proj(C_B, C_C) * y).astype(BF16)
    carry_ref[...] = cu[tm - 8:tm]
    gate_ref[...] = jax.nn.sigmoid(proj(C_G, C_END))


N_PROJ_W = 3
N_PROJ_OUT = 8
N_PROJ_SCRATCH = 2


def _proj_specs(t, tm):
    row = lambda w: pl.BlockSpec((tm, w), lambda i: (i, 0))
    w_specs = [_resident((D_MODEL, C_END)), _resident((2 * KV_W, D_MODEL)), _resident((8, CONV_CH))]
    vt_shape = jax.ShapeDtypeStruct((t // TK, N_KV * VROWS, TK), BF16)
    vt_spec = pl.BlockSpec((tm // TK, N_KV * VROWS, TK), lambda i: (i, 0, 0))
    out_shape = [
        jax.ShapeDtypeStruct((t, N_HEADS * HPAD), BF16),
        jax.ShapeDtypeStruct((4, t // CMP_STRIDE, CMP_STRIDE * HEAD_DIM), F32),
        jax.ShapeDtypeStruct((t, N_KV * HPAD), BF16), vt_shape,
        jax.ShapeDtypeStruct((t, N_KV * HPAD), BF16), vt_shape,
        jax.ShapeDtypeStruct((t, N_KV * LANES), F32),
        jax.ShapeDtypeStruct((t, CONV_CH), BF16),
    ]
    out_specs = [
        row(N_HEADS * HPAD),
        pl.BlockSpec((4, tm // CMP_STRIDE, CMP_STRIDE * HEAD_DIM), lambda i: (0, i, 0)),
        row(N_KV * HPAD), vt_spec, row(N_KV * HPAD), vt_spec,
        row(N_KV * LANES), row(CONV_CH),
    ]
    scratch = [pltpu.VMEM((8, CONV_CH), F32), pltpu.VMEM((2, tm, LANES), F32)]
    return w_specs, out_shape, out_specs, scratch


def _in_proj_kernel(*refs, pre_ln, tiles_per_seq, tm):
    n_in = 3 if pre_ln else 1
    x_ref = refs[0]
    w_refs = refs[n_in:n_in + N_PROJ_W]
    outs = refs[n_in + N_PROJ_W:len(refs) - N_PROJ_SCRATCH]
    if pre_ln:
        h = _ln(x_ref[...], refs[1][...], refs[2][...])
        outs[0][...] = h
        outs = outs[1:]
    else:
        h = x_ref[...]
    _project_inputs(h, w_refs, outs, refs[len(refs) - N_PROJ_SCRATCH:], tiles_per_seq=tiles_per_seq, tm=tm)


def _in_proj(x, ln_g, ln_b, proj_w, *, seq, pre_ln):
    t = x.shape[0]
    tm = TM_PROJ
    row = pl.BlockSpec((tm, D_MODEL), lambda i: (i, 0))
    w_specs, out_shape, out_specs, scratch = _proj_specs(t, tm)
    in_specs, args = [row], [x]
    if pre_ln:
        in_specs += [_resident((1, D_MODEL)), _resident((1, D_MODEL))]
        args += [ln_g, ln_b]
        out_shape = [jax.ShapeDtypeStruct((t, D_MODEL), F32)] + out_shape
        out_specs = [row] + out_specs
    return pl.pallas_call(
        functools.partial(_in_proj_kernel, pre_ln=pre_ln, tiles_per_seq=seq // tm, tm=tm),
        grid=(t // tm,),
        in_specs=in_specs + w_specs,
        out_specs=out_specs,
        out_shape=out_shape,
        scratch_shapes=scratch,
        compiler_params=_cparams(("arbitrary",)),
        name="in_proj_ln" if pre_ln else "in_proj",
    )(*args, *proj_w)


def _prep_w_in(w_in):
    o = 0

    def take(n):
        nonlocal o
        s = w_in[:, o:o + n]
        o += n
        return s

    q, kc, vc, ks, vs, kw, vw = take(ATTN_W), *(take(KV_W) for _ in range(6))
    gates = take(3 * N_HEADS).reshape(-1, 3, N_KV, HPG).transpose(0, 2, 1, 3).reshape(-1, N_KV, 3 * HPG)
    gates = jnp.pad(gates, ((0, 0), (0, 0), (0, LANES - 3 * HPG))).reshape(-1, N_KV * LANES)
    conv = take(3 * CONV_CH)
    w_r = jnp.concatenate([q, kc, vc, ks, kw, conv, gates], axis=1).astype(BF16)
    return w_r, jnp.concatenate([vs, vw], axis=1).T.astype(BF16)


def _compress_kernel(x_ref, pos_ref, w1_ref, b1_ref, w2_ref, b2_ref, w2t_ref, b2t_ref, o_ref, ot_ref):
    half = CMP_STRIDE * HEAD_DIM
    x = x_ref[0, 0]
    pos = pos_ref[0]
    xa = (x + pos[:, :half]).astype(BF16)
    xb = (x + pos[:, half:]).astype(BF16)
    a = _dot(xa, w1_ref[0, :half, :])
    b = _dot(xb, w1_ref[0, half:, :])
    n = x.shape[0]
    hid = a + pltpu.roll(b, n - 1, 0) + b1_ref[0]
    act = jax.nn.gelu(hid).astype(BF16)
    o_ref[0, 0] = (_dot(act, w2_ref[0]) + b2_ref[0]).astype(BF16)
    ot_ref[0, 0] = (_dot_nt(w2t_ref[0], act) + b2t_ref[0]).astype(BF16)


def _compress(kvc, cmp_pos, cmp_w1, cmp_b1, cmp_w2, cmp_b2, *, batch, seq):
    nb = seq // CMP_STRIDE
    half = CMP_STRIDE * HEAD_DIM
    x = kvc.reshape(4, batch, nb, half)
    pos = cmp_pos.reshape(2, 1, CMP_BLOCK * HEAD_DIM)
    w1 = cmp_w1.astype(BF16)
    b1 = cmp_b1.reshape(2, 1, CMP_HIDDEN)
    w2 = jnp.pad(cmp_w2, ((0, 0), (0, 0), (0, HPAD - HEAD_DIM))).astype(BF16)
    b2 = jnp.pad(cmp_b2, ((0, 0), (0, HPAD - HEAD_DIM))).reshape(2, 1, HPAD)
    w2t = jnp.swapaxes(w2, 1, 2)
    b2t = b2.reshape(2, HPAD, 1)
    kind = lambda j, b: (j // N_KV, 0, 0)
    return pl.pallas_call(
        _compress_kernel,
        grid=(4, batch),
        in_specs=[
            pl.BlockSpec((1, 1, nb, half), lambda j, b: (j, b, 0, 0)),
            pl.BlockSpec((1, 1, 2 * half), kind),
            pl.BlockSpec((1, 2 * half, CMP_HIDDEN), kind),
            pl.BlockSpec((1, 1, CMP_HIDDEN), kind),
            pl.BlockSpec((1, CMP_HIDDEN, HPAD), kind),
            pl.BlockSpec((1, 1, HPAD), kind),
            pl.BlockSpec((1, HPAD, CMP_HIDDEN), kind),
            pl.BlockSpec((1, HPAD, 1), kind),
        ],
        out_specs=[pl.BlockSpec((1, 1, nb, HPAD), lambda j, b: (j, b, 0, 0)),
                   pl.BlockSpec((1, 1, HPAD, nb), lambda j, b: (j, b, 0, 0))],
        out_shape=[jax.ShapeDtypeStruct((4, batch, nb, HPAD), BF16),
                   jax.ShapeDtypeStruct((4, batch, HPAD, nb), BF16)],
        compiler_params=_cparams(("arbitrary", "arbitrary")),
        name="compress",
    )(x, pos, w1, b1, w2, b2, w2t, b2t)


def _overlap_t():
    n = np.arange(LANES)
    j = np.arange(32)
    cs = n * CMP_STRIDE
    ss = j * SEL_BLOCK
    ov = (cs[None, :] < ss[:, None] + SEL_BLOCK) & (cs[None, :] + CMP_BLOCK > ss[:, None])
    return jnp.asarray(ov, dtype=BF16)


def _cmp_select(qh, kc, vct, ov, i, tq, n_sel):
    n = lax.broadcasted_iota(jnp.int32, (LANES, tq), 0)
    t = lax.broadcasted_iota(jnp.int32, (LANES, tq), 1) + i * tq
    mask = n * CMP_STRIDE + (CMP_BLOCK - 1) <= t
    maskf = mask.astype(F32)
    ss = [jnp.where(mask, _dot_nt(kc, q), NEG) for q in qh]
    es = [jnp.exp2(s - jnp.max(s, axis=0, keepdims=True)) for s in ss]
    ps = [e / jnp.sum(e, axis=0, keepdims=True) * maskf for e in es]
    heads_t = [_dot(vct[:HEAD_DIM], p.astype(BF16)) for p in ps]
    psum = functools.reduce(lambda a, b: a + b, ps)

    p_hi = psum.astype(BF16)
    r1 = psum - p_hi.astype(F32)
    p_mid = r1.astype(BF16)
    p_lo = (r1 - p_mid.astype(F32)).astype(BF16)
    imp = _dot(ov, p_hi) + _dot(ov, p_mid) + _dot(ov, p_lo)

    jt = lax.broadcasted_iota(jnp.int32, (n_sel, tq), 0)
    tt = lax.broadcasted_iota(jnp.int32, (n_sel, tq), 1) + i * tq
    cur = tt // SEL_BLOCK
    valid = jt * SEL_BLOCK <= tt
    forced = (jt == 0) | (jt == cur) | (jt == cur - 1)
    score = jnp.where(forced, FORCE, jnp.where(valid, imp, NEG))
    rank = jnp.zeros((n_sel, tq), jnp.int32)
    for k in range(n_sel):
        sk = score[k:k + 1, :]
        ahead = (sk > score) | ((sk == score) & (jt > k))
        rank = rank + ahead.astype(jnp.int32)
    bias = jnp.where(rank < SEL_TOPN, 0.0, NEG)
    full = jnp.concatenate(
        [jnp.zeros((HEAD_DIM, tq), F32), bias, jnp.zeros((HPAD - HEAD_DIM - n_sel, tq), F32)], axis=0)
    return heads_t, full.T.astype(BF16)


def _attn_scores(k, qa):
    return tuple(_dot_nt(k, qa[hh]) for hh in range(HPG))


def _attn_update(ss, vt, state, mask):
    if mask is not None:
        ss = [jnp.where(mask, s, NEG) for s in ss]
    m_new = [jnp.maximum(state[hh][0], jnp.max(ss[hh], axis=0, keepdims=True)) for hh in range(HPG)]
    ps = [jnp.exp2(ss[hh] - m_new[hh]).astype(BF16) for hh in range(HPG)]
    pv = [_dot(vt, ps[hh]) for hh in range(HPG)]
    return tuple((m_new[hh], jnp.exp2(state[hh][0] - m_new[hh]) * state[hh][1] + pv[hh]) for hh in range(HPG))


def _attn_init(tq):
    return tuple((jnp.full((1, tq), NEG, F32), jnp.zeros((VROWS, tq), F32)) for _ in range(HPG))


def _nsa_attn_kernel(q_ref, kc_ref, vct_ref, ov_ref, ks_ref, vst_ref, kw_ref, vwt_ref, gate_ref, o_ref,
                     *, tq, tk, n_sel):
    i = pl.program_id(2)
    nq = ks_ref.shape[0] // tk
    qh = [q_ref[:, hh * HPAD:(hh + 1) * HPAD] for hh in range(HPG)]
    heads_cmp, selb = _cmp_select(qh, kc_ref[0, 0], vct_ref[0, 0], ov_ref[...], i, tq, n_sel)
    lane = lax.broadcasted_iota(jnp.int32, selb.shape, 1)
    qsel = [jnp.where(lane >= HEAD_DIM, selb, q) for q in qh]
    key = lax.broadcasted_iota(jnp.int32, (tk, tq), 0)
    qry = lax.broadcasted_iota(jnp.int32, (tk, tq), 1)
    causal = key <= qry
    gt = gate_ref[...].T

    key2 = lax.broadcasted_iota(jnp.int32, (2 * tk, tq), 0)
    pair_causal = key2 - tk <= lax.broadcasted_iota(jnp.int32, (2 * tk, tq), 1)

    def run(n):
        chunks = []
        for j in range(0, n, 2):
            chunks.append(("sel", j, 2, pair_causal if j + 1 == n else None))
        if n % 2 == 0:
            chunks.append(("sel", n, 1, causal))
        if n == 0:
            chunks.append(("win", 0, 1, causal))
        else:
            chunks.append(("win", n - 1, 2, pair_causal))
        if n >= 2:
            chunks.append(("win", n - 2, 1, key > qry))
        refs = {"sel": (ks_ref, vst_ref, qsel), "win": (kw_ref, vwt_ref, qh)}

        def scores(chunk):
            branch, j, nt, _ = chunk
            return _attn_scores(refs[branch][0][j * tk:(j + nt) * tk, :], refs[branch][2])

        state = {"sel": _attn_init(tq), "win": _attn_init(tq)}
        ss = scores(chunks[0])
        for idx, (branch, j, nt, mask) in enumerate(chunks):
            ss_next = scores(chunks[idx + 1]) if idx + 1 < len(chunks) else None
            vt_ref = refs[branch][1]
            vt = vt_ref[j] if nt == 1 else jnp.concatenate([vt_ref[j], vt_ref[j + 1]], axis=1)
            state[branch] = _attn_update(ss, vt, state[branch], mask)
            ss = ss_next

        outs = []
        for hh in range(HPG):
            o_sel, o_win = (acc[:HEAD_DIM] / acc[HEAD_DIM:HEAD_DIM + 1]
                            for _, acc in (state["sel"][hh], state["win"][hh]))
            g_cmp, g_sel, g_win = (gt[c * HPG + hh:c * HPG + hh + 1] for c in range(3))
            outs.append(g_cmp * heads_cmp[hh] + g_sel * o_sel + g_win * o_win)
        o_ref[...] = jnp.concatenate(outs, axis=0).T.astype(o_ref.dtype)

    for n in range(nq):
        pl.when(i == n)(functools.partial(run, n))


def _nsa_attn(q, kvcmp, kvcmp_t, ks, vs_t, kw, vw_t, gates, *, batch, seq):
    t = q.shape[0]
    tq, tk = TQ, TK
    assert tq == tk and WINDOW == 2 * tk
    nq = seq // tq
    n_sel = seq // SEL_BLOCK
    nb = kvcmp.shape[2]
    assert nb == LANES and n_sel == 32
    rowblk = lambda w: pl.BlockSpec((tq, w), lambda b, g, i: (b * nq + i, g))
    seqblk = pl.BlockSpec((seq, HPAD), lambda b, g, i: (b, g))
    vtblk = pl.BlockSpec((seq // tk, VROWS, tk), lambda b, g, i: (b, g, 0))
    return pl.pallas_call(
        functools.partial(_nsa_attn_kernel, tq=tq, tk=tk, n_sel=n_sel),
        grid=(batch, N_KV, nq),
        in_specs=[rowblk(HPG * HPAD),
                  pl.BlockSpec((1, 1, nb, HPAD), lambda b, g, i: (g, b, 0, 0)),
                  pl.BlockSpec((1, 1, HPAD, nb), lambda b, g, i: (N_KV + g, b, 0, 0)),
                  pl.BlockSpec((n_sel, LANES), lambda b, g, i: (0, 0)),
                  seqblk, vtblk, seqblk, vtblk,
                  rowblk(LANES)],
        out_specs=rowblk(HPG * HEAD_DIM),
        out_shape=jax.ShapeDtypeStruct((t, ATTN_W), BF16),
        compiler_params=_cparams(("arbitrary", "arbitrary", "arbitrary")),
        name="nsa_attn",
    )(q, kvcmp, kvcmp_t, _overlap_t(), ks, vs_t, kw, vw_t, gates)


def _dense_tail_kernel(*refs, alpha, with_proj, tiles_per_seq, tm):
    (h_ref, oa_ref, ocv_ref, wo_ref, g1_ref, b1_ref, wg_ref, wu_ref, wd_ref, g2_ref, b2_ref) = refs[:11]
    m = _dot(oa_ref[...], wo_ref[:ATTN_W, :]) + _dot(ocv_ref[...], wo_ref[ATTN_W:, :])
    h1 = _ln(alpha * h_ref[...] + m, g1_ref[...], b1_ref[...])
    hb = h1.astype(BF16)
    a = (jax.nn.silu(_dot(hb, wg_ref[...])) * _dot(hb, wu_ref[...])).astype(BF16)
    h2 = _ln(alpha * h1 + _dot(a, wd_ref[...]), g2_ref[...], b2_ref[...])
    if with_proj:
        w_refs = refs[11:11 + N_PROJ_W]
        outs = refs[11 + N_PROJ_W:len(refs) - N_PROJ_SCRATCH]
        outs[0][...] = h2
        _project_inputs(h2, w_refs, outs[1:], refs[len(refs) - N_PROJ_SCRATCH:], tiles_per_seq=tiles_per_seq, tm=tm)
    else:
        refs[11][...] = h2


def _dense_tail(h, o_attn, o_conv, w_o, ln1, ffn_w, ln2, proj_w, *, alpha, seq):
    t = h.shape[0]
    tm = TM_PROJ
    row = lambda w: pl.BlockSpec((tm, w), lambda i: (i, 0))
    vec = _resident((1, D_MODEL))
    in_specs = [row(D_MODEL), row(ATTN_W), row(CONV_CH), _resident((D_MODEL, D_MODEL)), vec, vec,
                _resident((D_MODEL, D_FF)), _resident((D_MODEL, D_FF)), _resident((D_FF, D_MODEL)), vec, vec]
    args = [h, o_attn, o_conv, w_o, *ln1, *ffn_w, *ln2]
    out_shape, out_specs, scratch = [jax.ShapeDtypeStruct((t, D_MODEL), F32)], [row(D_MODEL)], []
    if proj_w is not None:
        w_specs, p_shape, p_specs, scratch = _proj_specs(t, tm)
        in_specs += w_specs
        args += list(proj_w)
        out_shape += p_shape
        out_specs += p_specs
    return pl.pallas_call(
        functools.partial(_dense_tail_kernel, alpha=alpha, with_proj=proj_w is not None,
                          tiles_per_seq=seq // tm, tm=tm),
        grid=(t // tm,),
        in_specs=in_specs,
        out_specs=out_specs,
        out_shape=out_shape,
        scratch_shapes=scratch,
        compiler_params=_cparams(("arbitrary",)),
        name="dense_tail",
    )(*args)


TMX = 512
CH = 16
LROWS = 2 * TMX + N_EXPERTS * CH
TMR = 512
XROW = D_MODEL + LANES


def _out_proj_kernel(h_ref, oa_ref, ocv_ref, w_ref, g_ref, b_ref, o_ref, *, alpha):
    m = _dot(oa_ref[...], w_ref[:ATTN_W, :]) + _dot(ocv_ref[...], w_ref[ATTN_W:, :])
    o_ref[...] = _ln(alpha * h_ref[...] + m, g_ref[...], b_ref[...])


def _out_proj(h, o_attn, o_conv, w_o, ln_g, ln_b, *, alpha):
    t = h.shape[0]
    tm = TM_PROJ
    row = lambda w: pl.BlockSpec((tm, w), lambda i: (i, 0))
    return pl.pallas_call(
        functools.partial(_out_proj_kernel, alpha=alpha),
        grid=(t // tm,),
        in_specs=[row(D_MODEL), row(ATTN_W), row(CONV_CH),
                  _resident((D_MODEL, D_MODEL)), _resident((1, D_MODEL)), _resident((1, D_MODEL))],
        out_specs=row(D_MODEL),
        out_shape=jax.ShapeDtypeStruct((t, D_MODEL), F32),
        compiler_params=_cparams(("arbitrary",)),
        name="out_proj",
    )(h, o_attn, o_conv, w_o, ln_g, ln_b)


def _route_kernel(h_ref, wr_ref, ltri_ref, ustr_ref, meta_ref, metat_ref, cnt_ref, wp_ref, *, tm):
    h = h_ref[...]
    h_hi = h.astype(BF16)
    h_lo = (h - h_hi.astype(F32)).astype(BF16)
    hw = _dot(h_hi, wr_ref[...])
    logits = hw[:, :LANES] + hw[:, LANES:] + _dot(h_lo, wr_ref[:, :LANES])
    lane = lax.broadcasted_iota(jnp.int32, logits.shape, 1)
    logits = jnp.where(lane < N_EXPERTS, logits, -jnp.inf)
    m1 = jnp.max(logits, axis=-1, keepdims=True)
    i1 = jnp.min(jnp.where(logits == m1, lane, LANES), axis=-1, keepdims=True)
    rest = jnp.where(lane == i1, -jnp.inf, logits)
    m2 = jnp.max(rest, axis=-1, keepdims=True)
    i2 = jnp.min(jnp.where(rest == m2, lane, LANES), axis=-1, keepdims=True)
    e2 = jnp.exp(m2 - m1)
    den = 1.0 + e2
    w1 = 1.0 / den
    w2 = e2 / den

    routed = (lane == i1) | (lane == i2)
    cnt = _dot(ltri_ref[...], routed.astype(BF16))
    n = cnt[tm - 1:tm, :]
    padded = jnp.floor((n + (CH - 1)) * (1.0 / CH)) * CH
    seg_off = _dot(jnp.broadcast_to(padded, (8, LANES)).astype(BF16), ustr_ref[...])[0:1]
    dest = seg_off + cnt - 1.0
    d1 = jnp.sum(jnp.where(lane == i1, dest, 0.0), axis=-1, keepdims=True)
    d2 = jnp.sum(jnp.where(lane == i2, dest, 0.0), axis=-1, keepdims=True)
    meta = jnp.where(lane == 0, d1, jnp.where(lane == 1, d2, jnp.where(lane == 2, w1, jnp.where(lane == 3, w2, 0.0))))
    meta_ref[...] = meta
    metat_ref[...] = meta.T[0:8, :]
    cnt_ref[0] = jnp.broadcast_to(n, (8, LANES))
    for k, w in enumerate((w1, w2)):
        hi = w.astype(BF16).astype(F32)
        mid = (w - hi).astype(BF16).astype(F32)
        lo = (w - hi - mid).astype(BF16).astype(F32)
        wp_ref[k] = jnp.where(lane == 0, hi, jnp.where(lane == 1, mid, jnp.where(lane == 2, lo, 0.0))).astype(BF16)


def _moe_route(h, w_router):
    t = h.shape[0]
    tm = TMX
    nt = t // tm
    row = lambda w: pl.BlockSpec((tm, w), lambda i: (i, 0))
    wr = jnp.pad(w_router, ((0, 0), (0, LANES - N_EXPERTS)))
    wr_hi = wr.astype(BF16)
    wr_lo = (wr - wr_hi.astype(F32)).astype(BF16)
    wr2 = jnp.concatenate([wr_hi, wr_lo], axis=1)
    ltri = jnp.asarray(np.tril(np.ones((tm, tm), np.float32)), dtype=BF16)
    ustr = jnp.asarray(np.triu(np.ones((LANES, LANES), np.float32), 1), dtype=BF16)
    return pl.pallas_call(
        functools.partial(_route_kernel, tm=tm),
        grid=(nt,),
        in_specs=[row(D_MODEL), _resident((D_MODEL, 2 * LANES)), _resident((tm, tm)), _resident((LANES, LANES))],
        out_specs=[row(LANES),
                   pl.BlockSpec((8, tm), lambda i: (0, i)),
                   pl.BlockSpec((1, 8, LANES), lambda i: (i, 0, 0)),
                   pl.BlockSpec((2, tm, LANES), lambda i: (0, i, 0))],
        out_shape=[jax.ShapeDtypeStruct((t, LANES), F32), jax.ShapeDtypeStruct((8, t), F32),
                   jax.ShapeDtypeStruct((nt, 8, LANES), F32), jax.ShapeDtypeStruct((2, t, LANES), BF16)],
        compiler_params=_cparams(("arbitrary",)),
        name="moe_route",
    )(h, wr2, ltri, ustr)


def _moe_plan(cnt, n_row_tiles):
    n = cnt[:, 0, :N_EXPERTS].astype(jnp.int32)
    p = (n + CH - 1) // CH * CH
    tot = p.sum(0)
    tot_pad = (tot + TMR - 1) // TMR * TMR
    gend = jnp.cumsum(tot_pad)
    gstart = gend - tot_pad
    goff = gstart[None, :] + jnp.cumsum(p, 0) - p
    loff = jnp.cumsum(p, 1) - p
    n_used = gend[-1:] // TMR
    tile_start = jnp.arange(n_row_tiles, dtype=jnp.int32) * TMR
    texp = jnp.minimum(jnp.sum(tile_start[:, None] >= gend[None, :], axis=1), N_EXPERTS - 1).astype(jnp.int32)
    texp = jnp.where(jnp.arange(n_row_tiles) < n_used[0], texp, texp[jnp.maximum(n_used[0] - 1, 0)])
    flat = lambda a: a.reshape(-1).astype(jnp.int32)
    gap_start = jnp.concatenate([gstart + tot, gend[-1:]])
    gap_rows = jnp.concatenate([tot_pad - tot, n_row_tiles * TMR - gend[-1:]])
    return dict(loff=flat(loff), goff=flat(goff), nch=flat(p // CH), gap_start=flat(gap_start),
                gap_nch=flat(gap_rows // CH), texp=flat(texp), n_used=flat(n_used))


def _chunk_copies(src_ref, dst_ref, sem, src_off, dst_off, n, src_step=CH):
    def body(c, carry):
        s = pl.multiple_of(src_off + c * src_step, CH)
        d = pl.multiple_of(dst_off + c * CH, CH)
        pltpu.make_async_copy(src_ref.at[pl.ds(s, CH)], dst_ref.at[pl.ds(d, CH)], sem).start()
        return carry
    lax.fori_loop(0, n, body, 0)


def _wait_chunks(src_ref, dst_ref, sem, n):
    def body(c, carry):
        pltpu.make_async_copy(src_ref.at[pl.ds(0, CH)], dst_ref.at[pl.ds(0, CH)], sem).wait()
        return carry
    lax.fori_loop(0, n, body, 0)


def _tile_chunks(tile, nch):
    total = 0
    for e in range(N_EXPERTS):
        total = total + nch[tile * N_EXPERTS + e]
    return total


def _start_segments(tile, nch, src_off, dst_off, src_ref, dst_ref, sem):
    for e in range(N_EXPERTS):
        idx = tile * N_EXPERTS + e
        _chunk_copies(src_ref, dst_ref, sem, src_off[idx], dst_off[idx], nch[idx])


def _dispatch_kernel(loff, goff, nch, gap_start, gap_nch, h_ref, metat_ref, wp_ref, xs_ref, xc_ref, z_ref, sem,
                     *, tm):
    i = pl.program_id(0)
    last = pl.num_programs(0) - 1
    slot = i % 2
    r = lax.broadcasted_iota(jnp.int32, (LROWS, tm), 0).astype(F32)
    pick1 = (r == metat_ref[0:1, :]).astype(BF16)
    pick2 = (r == metat_ref[1:2, :]).astype(BF16)
    xc_ref[slot, :, :D_MODEL] = _dot(pick1 + pick2, h_ref[...].astype(BF16)).astype(BF16)
    xc_ref[slot, :, D_MODEL:] = (_dot(pick1, wp_ref[0]) + _dot(pick2, wp_ref[1])).astype(BF16)
    _start_segments(i, nch, loff, goff, xc_ref.at[slot], xs_ref, sem.at[slot])

    @pl.when(i > 0)
    def _():
        _wait_chunks(xc_ref.at[1 - slot], xs_ref, sem.at[1 - slot], _tile_chunks(i - 1, nch))

    @pl.when(i == last)
    def _():
        z_ref[...] = jnp.zeros_like(z_ref)
        gaps = 0
        for e in range(N_EXPERTS + 1):
            _chunk_copies(z_ref, xs_ref, sem.at[2], 0, gap_start[e], gap_nch[e], src_step=0)
            gaps = gaps + gap_nch[e]
        _wait_chunks(z_ref, xs_ref, sem.at[2], gaps)
        _wait_chunks(xc_ref.at[slot], xs_ref, sem.at[slot], _tile_chunks(i, nch))


def _moe_dispatch(h, metat, wparts, plan, n_rows):
    t = h.shape[0]
    tm = TMX
    grid_spec = pltpu.PrefetchScalarGridSpec(
        num_scalar_prefetch=5,
        grid=(t // tm,),
        in_specs=[pl.BlockSpec((tm, D_MODEL), lambda i, *_: (i, 0)),
                  pl.BlockSpec((8, tm), lambda i, *_: (0, i)),
                  pl.BlockSpec((2, tm, LANES), lambda i, *_: (0, i, 0))],
        out_specs=pl.BlockSpec(memory_space=pl.ANY),
        scratch_shapes=[pltpu.VMEM((2, LROWS, XROW), BF16), pltpu.VMEM((CH, XROW), BF16),
                        pltpu.SemaphoreType.DMA((3,))],
    )
    return pl.pallas_call(
        functools.partial(_dispatch_kernel, tm=tm),
        grid_spec=grid_spec,
        out_shape=jax.ShapeDtypeStruct((n_rows, XROW), BF16),
        compiler_params=_cparams(("arbitrary",)),
        name="moe_dispatch",
    )(plan["loff"], plan["goff"], plan["nch"], plan["gap_start"], plan["gap_nch"], h, metat, wparts)


def _experts_kernel(texp, n_used, x_ref, wgu_ref, wd_ref, y_ref):
    used = pl.program_id(0) < n_used[0]

    @pl.when(used)
    def _():
        x = x_ref[:, :D_MODEL]
        wp = x_ref[:, D_MODEL:].astype(F32)
        gate = wp[:, 0:1] + wp[:, 1:2] + wp[:, 2:3]
        gu = _dot(x, wgu_ref[0])
        a = (jax.nn.silu(gu[:, :D_FF_EXPERT]) * gu[:, D_FF_EXPERT:]).astype(BF16)
        y_ref[...] = (gate * _dot(a, wd_ref[0])).astype(BF16)

    @pl.when(jnp.logical_not(used))
    def _():
        y_ref[...] = jnp.zeros_like(y_ref)


def _moe_experts(xs, wgu, wd, plan):
    n_rows = xs.shape[0]
    rows = lambda r, texp, n_used: (jnp.minimum(r, n_used[0] - 1), 0)
    wspec = lambda shape: pl.BlockSpec((1,) + shape, lambda r, texp, n_used: (texp[r], 0, 0))
    grid_spec = pltpu.PrefetchScalarGridSpec(
        num_scalar_prefetch=2,
        grid=(n_rows // TMR,),
        in_specs=[pl.BlockSpec((TMR, XROW), rows), wspec((D_MODEL, 2 * D_FF_EXPERT)),
                  wspec((D_FF_EXPERT, D_MODEL))],
        out_specs=pl.BlockSpec((TMR, D_MODEL), lambda r, texp, n_used: (r, 0)),
    )
    return pl.pallas_call(
        _experts_kernel,
        grid_spec=grid_spec,
        out_shape=jax.ShapeDtypeStruct((n_rows, D_MODEL), BF16),
        compiler_params=_cparams(("arbitrary",)),
        name="moe_experts",
    )(plan["texp"], plan["n_used"], xs, wgu, wd)


def _combine_kernel(loff, goff, nch, h_ref, meta_ref, ys_ref, g_ref, b_ref, o_ref, yb_ref, sem, *, tm, alpha):
    i = pl.program_id(0)
    slot = i % 2

    @pl.when(i == 0)
    def _():
        yb_ref[...] = jnp.zeros_like(yb_ref)
        _start_segments(0, nch, goff, loff, ys_ref, yb_ref.at[0], sem.at[0])

    @pl.when(i + 1 < pl.num_programs(0))
    def _():
        _start_segments(i + 1, nch, goff, loff, ys_ref, yb_ref.at[1 - slot], sem.at[1 - slot])

    meta = meta_ref[...]
    r = lax.broadcasted_iota(jnp.int32, (tm, LROWS), 1).astype(F32)
    pick = ((r == meta[:, 0:1]) | (r == meta[:, 1:2])).astype(BF16)
    _wait_chunks(ys_ref, yb_ref.at[slot], sem.at[slot], _tile_chunks(i, nch))
    f = _dot(pick, yb_ref[slot])
    o_ref[...] = _ln(alpha * h_ref[...] + f, g_ref[...], b_ref[...])


def _moe_combine(h, meta, ys, plan, ln_g, ln_b, *, alpha):
    t = h.shape[0]
    tm = TMX
    grid_spec = pltpu.PrefetchScalarGridSpec(
        num_scalar_prefetch=3,
        grid=(t // tm,),
        in_specs=[pl.BlockSpec((tm, D_MODEL), lambda i, *_: (i, 0)),
                  pl.BlockSpec((tm, LANES), lambda i, *_: (i, 0)),
                  pl.BlockSpec(memory_space=pl.ANY),
                  pl.BlockSpec((1, D_MODEL), lambda i, *_: (0, 0)),
                  pl.BlockSpec((1, D_MODEL), lambda i, *_: (0, 0))],
        out_specs=pl.BlockSpec((tm, D_MODEL), lambda i, *_: (i, 0)),
        scratch_shapes=[pltpu.VMEM((2, LROWS, D_MODEL), BF16), pltpu.SemaphoreType.DMA((2,))],
    )
    return pl.pallas_call(
        functools.partial(_combine_kernel, tm=tm, alpha=alpha),
        grid_spec=grid_spec,
        out_shape=jax.ShapeDtypeStruct((t, D_MODEL), F32),
        compiler_params=_cparams(("arbitrary",)),
        name="moe_combine",
    )(plan["loff"], plan["goff"], plan["nch"], h, meta, ys, ln_g, ln_b)


def _moe_tail(h, o_attn, o_conv, w_o, ln1, w_router, wgu, wd, ln_g, ln_b, *, alpha):
    t = h.shape[0]
    nt = t // TMX
    max_rows = 2 * t + nt * N_EXPERTS * (CH - 1) + N_EXPERTS * (TMR - CH)
    n_row_tiles = -(-max_rows // TMR)
    h = _out_proj(h, o_attn, o_conv, w_o, *ln1, alpha=alpha)
    meta, metat, cnt, wparts = _moe_route(h, w_router)
    plan = _moe_plan(cnt, n_row_tiles)
    xs = _moe_dispatch(h, metat, wparts, plan, n_row_tiles * TMR)
    ys = _moe_experts(xs, wgu, wd, plan)
    return _moe_combine(h, meta, ys, plan, ln_g, ln_b, alpha=alpha)


def kernel(x, ln_in_g, ln_in_b, w_in, cmp_pos, cmp_w1, cmp_b1, cmp_w2, cmp_b2, conv_w, w_o, ln1_g, ln1_b, ln2_g, ln2_b, ffn_wg, ffn_wu, ffn_wd, moe_router, moe_wg, moe_wu, moe_wd):
    batch, seq, d = x.shape
    depth = w_in.shape[0]
    assert d == D_MODEL and seq % TM_PROJ == 0 and seq // CMP_STRIDE == LANES
    alpha = (2 * depth) ** 0.25
    t = batch * seq
    vec = lambda v: v.reshape(1, D_MODEL)
    def proj_weights(l):
        return (*_prep_w_in(w_in[l]), jnp.pad(conv_w[l], ((0, 8 - CONV_W), (0, 0))))

    h, *proj = _in_proj(x.reshape(t, d), vec(ln_in_g), vec(ln_in_b), proj_weights(0), seq=seq, pre_ln=True)
    for l in range(depth):
        q, kvc, ks, vs, kw, vw, gates, o_conv = proj
        kvcmp, kvcmp_t = _compress(kvc, cmp_pos[l], cmp_w1[l], cmp_b1[l], cmp_w2[l], cmp_b2[l],
                                   batch=batch, seq=seq)
        o_attn = _nsa_attn(q, kvcmp, kvcmp_t, ks, vs, kw, vw, gates, batch=batch, seq=seq)
        ln1 = (vec(ln1_g[l]), vec(ln1_b[l]))
        ln2 = (vec(ln2_g[l]), vec(ln2_b[l]))
        next_w = proj_weights(l + 1) if l + 1 < depth else None
        if l % 2 == 0:
            ffn_w = (ffn_wg[l // 2].astype(BF16), ffn_wu[l // 2].astype(BF16), ffn_wd[l // 2].astype(BF16))
            h, *proj = _dense_tail(h, o_attn, o_conv, w_o[l].astype(BF16), ln1, ffn_w, ln2, next_w,
                                   alpha=alpha, seq=seq)
        else:
            wgu = jnp.concatenate([moe_wg[l // 2], moe_wu[l // 2]], axis=-1).astype(BF16)
            h = _moe_tail(h, o_attn, o_conv, w_o[l].astype(BF16), ln1, moe_router[l // 2],
                          wgu, moe_wd[l // 2].astype(BF16), *ln2, alpha=alpha)
            if next_w is not None:
                proj = _in_proj(h, None, None, next_w, seq=seq, pre_ln=False)
    return h.reshape(batch, seq, d)
```

```python
import functools

import numpy as np
import jax
import jax.numpy as jnp
from jax import lax
from jax.experimental import pallas as pl
from jax.experimental.pallas import tpu as pltpu

F32 = jnp.float32
BF16 = jnp.bfloat16

D_MODEL = 1024
HEAD_DIM = 64
N_HEADS = 8
N_KV = 2
HPG = N_HEADS // N_KV
ATTN_W = N_HEADS * HEAD_DIM
KV_W = N_KV * HEAD_DIM
CONV_CH = D_MODEL - ATTN_W
CONV_W = 3
CMP_BLOCK = 32
CMP_STRIDE = 16
CMP_HIDDEN = 256
SEL_BLOCK = 64
SEL_TOPN = 16
WINDOW = 512
D_FF = 2816
N_EXPERTS = 8
D_FF_EXPERT = 1408
LN_EPS = 1e-5
NEG = -1e30
FORCE = 1e9

LANES = 128
HPAD = LANES
VROWS = HPAD
Q_SCALE = HEAD_DIM ** -0.5 * np.log2(np.e)

C_Q = 0
C_KVC = C_Q + ATTN_W
C_KS = C_KVC + 2 * KV_W
C_KW = C_KS + KV_W
C_U = C_KW + KV_W
C_B = C_U + CONV_CH
C_C = C_B + CONV_CH
C_G = C_C + CONV_CH
C_END = C_G + N_KV * LANES

TM_PROJ = 512
TQ = 256
TK = 256
VMEM_LIMIT = 56 * 1024 * 1024


def _cparams(sem):
    return pltpu.CompilerParams(dimension_semantics=sem, vmem_limit_bytes=VMEM_LIMIT)


def _ln(x, g, b):
    mu = jnp.mean(x, -1, keepdims=True)
    xc = x - mu
    var = jnp.mean(xc * xc, -1, keepdims=True)
    return xc * lax.rsqrt(var + LN_EPS) * g + b


def _dot(a, b):
    return jnp.dot(a, b, preferred_element_type=F32)


def _dot_nt(a, b):
    return lax.dot_general(a, b, (((1,), (1,)), ((), ())), preferred_element_type=F32)


def _resident(shape):
    nd = len(shape)
    return pl.BlockSpec(shape, lambda *_: (0,) * nd, pipeline_mode=pl.Buffered(1))


def _spread_heads(z):
    low = lax.broadcasted_iota(jnp.int32, (z.shape[0], LANES), 1) < HEAD_DIM
    tiles = []
    for p in range(z.shape[1] // LANES):
        pair = z[:, p * LANES:(p + 1) * LANES]
        tiles += [jnp.where(low, pair, 0.0), jnp.where(low, pltpu.roll(pair, HEAD_DIM, 1), 0.0)]
    return jnp.concatenate(tiles, axis=1)


def _project_inputs(h, w_refs, out_refs, scratch_refs, *, tiles_per_seq, tm):
    w_ref, wvt_ref, cw_ref = w_refs
    q_ref, kvc_ref, ks_ref, vs_ref, kw_ref, vw_ref, gate_ref, oc_ref = out_refs
    carry_ref, kvc_scr = scratch_refs
    hb = h.astype(BF16)

    def proj(lo, hi):
        return _dot(hb, w_ref[:, lo:hi])

    q_ref[...] = _spread_heads(proj(C_Q, C_KVC) * Q_SCALE).astype(BF16)

    kvc = proj(C_KVC, C_KS)
    low = lax.broadcasted_iota(jnp.int32, (tm // CMP_STRIDE, LANES), 1) < HEAD_DIM
    for kind in range(2):
        kvc_scr[kind] = kvc[:, kind * LANES:(kind + 1) * LANES]
        for p in range(CMP_STRIDE // 2):
            a = kvc_scr[kind, pl.ds(2 * p, tm // CMP_STRIDE, stride=CMP_STRIDE), :]
            b = kvc_scr[kind, pl.ds(2 * p + 1, tm // CMP_STRIDE, stride=CMP_STRIDE), :]
            kvc_ref[2 * kind, :, p * LANES:(p + 1) * LANES] = jnp.where(low, a, pltpu.roll(b, HEAD_DIM, 1))
            kvc_ref[2 * kind + 1, :, p * LANES:(p + 1) * LANES] = jnp.where(low, pltpu.roll(a, HEAD_DIM, 1), b)

    seq_tile = pl.program_id(0) % tiles_per_seq
    pos = lax.broadcasted_iota(jnp.int32, (tm, N_KV * HPAD), 0) + seq_tile * tm
    lane = lax.broadcasted_iota(jnp.int32, (tm, N_KV * HPAD), 1) % HPAD
    onehot = jnp.where(pos // SEL_BLOCK == lane - HEAD_DIM, 1.0, 0.0)
    in_tag = (lane >= HEAD_DIM) & (lane < HEAD_DIM + 32)
    kk = proj(C_KS, C_U)
    ks_ref[...] = jnp.where(in_tag, onehot, _spread_heads(kk[:, :KV_W])).astype(BF16)
    kw_ref[...] = _spread_heads(kk[:, KV_W:]).astype(BF16)

    vt = _dot_nt(wvt_ref[...], hb)
    tail = jnp.where(lax.broadcasted_iota(jnp.int32, (VROWS - HEAD_DIM, tm), 0) == 0, 1.0, 0.0)
    for n, ref in enumerate((vs_ref, vw_ref)):
        rows = []
        for grp in range(N_KV):
            lo = (n * N_KV + grp) * HEAD_DIM
            rows += [vt[lo:lo + HEAD_DIM], tail]
        full = jnp.concatenate(rows, axis=0).astype(BF16)
        for c in range(tm // TK):
            ref[c] = full[:, c * TK:(c + 1) * TK]

    cu = proj(C_C, C_G) * proj(C_U, C_B)

    @pl.when(seq_tile == 0)
    def _():
        carry_ref[...] = jnp.zeros_like(carry_ref)

    prev = carry_ref[...]
    row = lax.broadcasted_iota(jnp.int32, (tm, CONV_CH), 0)
    s1 = jnp.where(row == 0, prev[7:8], pltpu.roll(cu, 1, 0))
    s2 = jnp.where(row == 0, prev[6:7], jnp.where(row == 1, prev[7:8], pltpu.roll(cu, 2, 0)))
    y = s2 * cw_ref[0:1, :] + s1 * cw_ref[1:2, :] + cu * cw_ref[2:3, :]
    oc_ref[...] = (proj(C_B, C_C) * y).astype(BF16)
    carry_ref[...] = cu[tm - 8:tm]
    gate_ref[...] = jax.nn.sigmoid(proj(C_G, C_END))


N_PROJ_W = 3
N_PROJ_OUT = 8
N_PROJ_SCRATCH = 2


def _proj_specs(t, tm):
    row = lambda w: pl.BlockSpec((tm, w), lambda i: (i, 0))
    w_specs = [_resident((D_MODEL, C_END)), _resident((2 * KV_W, D_MODEL)), _resident((8, CONV_CH))]
    vt_shape = jax.ShapeDtypeStruct((t // TK, N_KV * VROWS, TK), BF16)
    vt_spec = pl.BlockSpec((tm // TK, N_KV * VROWS, TK), lambda i: (i, 0, 0))
    out_shape = [
        jax.ShapeDtypeStruct((t, N_HEADS * HPAD), BF16),
        jax.ShapeDtypeStruct((4, t // CMP_STRIDE, CMP_STRIDE * HEAD_DIM), F32),
        jax.ShapeDtypeStruct((t, N_KV * HPAD), BF16), vt_shape,
        jax.ShapeDtypeStruct((t, N_KV * HPAD), BF16), vt_shape,
        jax.ShapeDtypeStruct((t, N_KV * LANES), F32),
        jax.ShapeDtypeStruct((t, CONV_CH), BF16),
    ]
    out_specs = [
        row(N_HEADS * HPAD),
        pl.BlockSpec((4, tm // CMP_STRIDE, CMP_STRIDE * HEAD_DIM), lambda i: (0, i, 0)),
        row(N_KV * HPAD), vt_spec, row(N_KV * HPAD), vt_spec,
        row(N_KV * LANES), row(CONV_CH),
    ]
    scratch = [pltpu.VMEM((8, CONV_CH), F32), pltpu.VMEM((2, tm, LANES), F32)]
    return w_specs, out_shape, out_specs, scratch


def _in_proj_kernel(*refs, pre_ln, tiles_per_seq, tm):
    n_in = 3 if pre_ln else 1
    x_ref = refs[0]
    w_refs = refs[n_in:n_in + N_PROJ_W]
    outs = refs[n_in + N_PROJ_W:len(refs) - N_PROJ_SCRATCH]
    if pre_ln:
        h = _ln(x_ref[...], refs[1][...], refs[2][...])
        outs[0][...] = h
        outs = outs[1:]
    else:
        h = x_ref[...]
    _project_inputs(h, w_refs, outs, refs[len(refs) - N_PROJ_SCRATCH:], tiles_per_seq=tiles_per_seq, tm=tm)


def _in_proj(x, ln_g, ln_b, proj_w, *, seq, pre_ln):
    t = x.shape[0]
    tm = TM_PROJ
    row = pl.BlockSpec((tm, D_MODEL), lambda i: (i, 0))
    w_specs, out_shape, out_specs, scratch = _proj_specs(t, tm)
    in_specs, args = [row], [x]
    if pre_ln:
        in_specs += [_resident((1, D_MODEL)), _resident((1, D_MODEL))]
        args += [ln_g, ln_b]
        out_shape = [jax.ShapeDtypeStruct((t, D_MODEL), F32)] + out_shape
        out_specs = [row] + out_specs
    return pl.pallas_call(
        functools.partial(_in_proj_kernel, pre_ln=pre_ln, tiles_per_seq=seq // tm, tm=tm),
        grid=(t // tm,),
        in_specs=in_specs + w_specs,
        out_specs=out_specs,
        out_shape=out_shape,
        scratch_shapes=scratch,
        compiler_params=_cparams(("arbitrary",)),
        name="in_proj_ln" if pre_ln else "in_proj",
    )(*args, *proj_w)


def _prep_w_in(w_in):
    o = 0

    def take(n):
        nonlocal o
        s = w_in[:, o:o + n]
        o += n
        return s

    q, kc, vc, ks, vs, kw, vw = take(ATTN_W), *(take(KV_W) for _ in range(6))
    gates = take(3 * N_HEADS).reshape(-1, 3, N_KV, HPG).transpose(0, 2, 1, 3).reshape(-1, N_KV, 3 * HPG)
    gates = jnp.pad(gates, ((0, 0), (0, 0), (0, LANES - 3 * HPG))).reshape(-1, N_KV * LANES)
    conv = take(3 * CONV_CH)
    w_r = jnp.concatenate([q, kc, vc, ks, kw, conv, gates], axis=1).astype(BF16)
    return w_r, jnp.concatenate([vs, vw], axis=1).T.astype(BF16)


def _compress_kernel(x_ref, pos_ref, w1_ref, b1_ref, w2_ref, b2_ref, w2t_ref, b2t_ref, o_ref, ot_ref):
    half = CMP_STRIDE * HEAD_DIM
    nb = x_ref.shape[2]
    x = jnp.concatenate([x_ref[g, 0] for g in range(N_KV)], axis=0)
    pos = pos_ref[0]
    xa = (x + pos[:, :half]).astype(BF16)
    xb = (x + pos[:, half:]).astype(BF16)
    a = _dot(xa, w1_ref[0, :half, :])
    b = _dot(xb, w1_ref[0, half:, :])
    hid = a + pltpu.roll(b, N_KV * nb - 1, 0) + b1_ref[0]
    act = jax.nn.gelu(hid).astype(BF16)
    out = (_dot(act, w2_ref[0]) + b2_ref[0]).astype(BF16)
    out_t = (_dot_nt(w2t_ref[0], act) + b2t_ref[0]).astype(BF16)
    for g in range(N_KV):
        o_ref[g, 0] = out[g * nb:(g + 1) * nb]
        ot_ref[g, 0] = out_t[:, g * nb:(g + 1) * nb]


def _compress(kvc, cmp_pos, cmp_w1, cmp_b1, cmp_w2, cmp_b2, *, batch, seq):
    nb = seq // CMP_STRIDE
    half = CMP_STRIDE * HEAD_DIM
    x = kvc.reshape(4, batch, nb, half)
    pos = cmp_pos.reshape(2, 1, CMP_BLOCK * HEAD_DIM)
    w1 = cmp_w1.astype(BF16)
    b1 = cmp_b1.reshape(2, 1, CMP_HIDDEN)
    w2 = jnp.pad(cmp_w2, ((0, 0), (0, 0), (0, HPAD - HEAD_DIM))).astype(BF16)
    b2 = jnp.pad(cmp_b2, ((0, 0), (0, HPAD - HEAD_DIM))).reshape(2, 1, HPAD)
    w2t = jnp.swapaxes(w2, 1, 2)
    b2t = b2.reshape(2, HPAD, 1)
    kind = lambda j, b: (j, 0, 0)
    return pl.pallas_call(
        _compress_kernel,
        grid=(2, batch),
        in_specs=[
            pl.BlockSpec((N_KV, 1, nb, half), lambda j, b: (j, b, 0, 0)),
            pl.BlockSpec((1, 1, 2 * half), kind),
            pl.BlockSpec((1, 2 * half, CMP_HIDDEN), kind),
            pl.BlockSpec((1, 1, CMP_HIDDEN), kind),
            pl.BlockSpec((1, CMP_HIDDEN, HPAD), kind),
            pl.BlockSpec((1, 1, HPAD), kind),
            pl.BlockSpec((1, HPAD, CMP_HIDDEN), kind),
            pl.BlockSpec((1, HPAD, 1), kind),
        ],
        out_specs=[pl.BlockSpec((N_KV, 1, nb, HPAD), lambda j, b: (j, b, 0, 0)),
                   pl.BlockSpec((N_KV, 1, HPAD, nb), lambda j, b: (j, b, 0, 0))],
        out_shape=[jax.ShapeDtypeStruct((4, batch, nb, HPAD), BF16),
                   jax.ShapeDtypeStruct((4, batch, HPAD, nb), BF16)],
        compiler_params=_cparams(("arbitrary", "arbitrary")),
        name="compress",
    )(x, pos, w1, b1, w2, b2, w2t, b2t)


def _overlap_t():
    n = np.arange(LANES)
    j = np.arange(32)
    cs = n * CMP_STRIDE
    ss = j * SEL_BLOCK
    ov = (cs[None, :] < ss[:, None] + SEL_BLOCK) & (cs[None, :] + CMP_BLOCK > ss[:, None])
    return jnp.asarray(ov, dtype=BF16)


def _cmp_select(qh, kc, vct, ov, i, tq, n_sel):
    n = lax.broadcasted_iota(jnp.int32, (LANES, tq), 0)
    t = lax.broadcasted_iota(jnp.int32, (LANES, tq), 1) + i * tq
    mask = n * CMP_STRIDE + (CMP_BLOCK - 1) <= t
    maskf = mask.astype(F32)
    ss = [jnp.where(mask, _dot_nt(kc, q), NEG) for q in qh]
    es = [jnp.exp2(s - jnp.max(s, axis=0, keepdims=True)) for s in ss]
    ps = [e / jnp.sum(e, axis=0, keepdims=True) * maskf for e in es]
    heads_t = [_dot(vct, p.astype(BF16))[:HEAD_DIM] for p in ps]
    psum = functools.reduce(lambda a, b: a + b, ps)

    p_hi = psum.astype(BF16)
    r1 = psum - p_hi.astype(F32)
    p_mid = r1.astype(BF16)
    p_lo = (r1 - p_mid.astype(F32)).astype(BF16)
    imp = _dot(ov, p_hi) + _dot(ov, p_mid) + _dot(ov, p_lo)

    jt = lax.broadcasted_iota(jnp.int32, (n_sel, tq), 0)
    tt = lax.broadcasted_iota(jnp.int32, (n_sel, tq), 1) + i * tq
    cur = tt // SEL_BLOCK
    valid = jt * SEL_BLOCK <= tt
    forced = (jt == 0) | (jt == cur) | (jt == cur - 1)
    score = jnp.where(forced, FORCE, jnp.where(valid, imp, NEG))
    rank = jnp.zeros((n_sel, tq), jnp.int32)
    for k in range(n_sel):
        sk = score[k:k + 1, :]
        ahead = (sk > score) | ((sk == score) & (jt > k))
        rank = rank + ahead.astype(jnp.int32)
    bias = jnp.where(rank < SEL_TOPN, 0.0, NEG)
    full = jnp.concatenate(
        [jnp.zeros((HEAD_DIM, tq), F32), bias, jnp.zeros((HPAD - HEAD_DIM - n_sel, tq), F32)], axis=0)
    return heads_t, full.T.astype(BF16)


def _attn_scores(k, qa):
    return tuple(_dot_nt(k, qa[hh]) for hh in range(HPG))


def _attn_update(ss, vt, state, mask):
    if mask is not None:
        ss = [jnp.where(mask, s, NEG) for s in ss]
    m_new = [jnp.maximum(state[hh][0], jnp.max(ss[hh], axis=0, keepdims=True)) for hh in range(HPG)]
    ps = [jnp.exp2(ss[hh] - m_new[hh]).astype(BF16) for hh in range(HPG)]
    pv = [_dot(vt, ps[hh]) for hh in range(HPG)]
    return tuple((m_new[hh], jnp.exp2(state[hh][0] - m_new[hh]) * state[hh][1] + pv[hh]) for hh in range(HPG))


def _attn_init(tq):
    return tuple((jnp.full((1, tq), NEG, F32), jnp.zeros((VROWS, tq), F32)) for _ in range(HPG))


def _nsa_attn_kernel(q_ref, kc_ref, vct_ref, ov_ref, ks_ref, vst_ref, kw_ref, vwt_ref, gate_ref, o_ref,
                     *, tq, tk, n_sel):
    i = pl.program_id(2)
    nq = ks_ref.shape[0] // tk
    qh = [q_ref[:, hh * HPAD:(hh + 1) * HPAD] for hh in range(HPG)]
    heads_cmp, selb = _cmp_select(qh, kc_ref[0, 0], vct_ref[0, 0], ov_ref[...], i, tq, n_sel)
    lane = lax.broadcasted_iota(jnp.int32, selb.shape, 1)
    qsel = [jnp.where(lane >= HEAD_DIM, selb, q) for q in qh]
    key = lax.broadcasted_iota(jnp.int32, (tk, tq), 0)
    qry = lax.broadcasted_iota(jnp.int32, (tk, tq), 1)
    causal = key <= qry
    gt = gate_ref[...].T

    key2 = lax.broadcasted_iota(jnp.int32, (2 * tk, tq), 0)
    pair_causal = key2 - tk <= lax.broadcasted_iota(jnp.int32, (2 * tk, tq), 1)

    def run(n):
        chunks = []
        for j in range(0, n, 2):
            chunks.append(("sel", j, 2, pair_causal if j + 1 == n else None))
        if n % 2 == 0:
            chunks.append(("sel", n, 1, causal))
        if n == 0:
            chunks.append(("win", 0, 1, causal))
        else:
            chunks.append(("win", n - 1, 2, pair_causal))
        if n >= 2:
            chunks.append(("win", n - 2, 1, key > qry))
        refs = {"sel": (ks_ref, vst_ref, qsel), "win": (kw_ref, vwt_ref, qh)}

        def scores(chunk):
            branch, j, nt, _ = chunk
            return _attn_scores(refs[branch][0][j * tk:(j + nt) * tk, :], refs[branch][2])

        state = {"sel": _attn_init(tq), "win": _attn_init(tq)}
        ss = scores(chunks[0])
        for idx, (branch, j, nt, mask) in enumerate(chunks):
            ss_next = scores(chunks[idx + 1]) if idx + 1 < len(chunks) else None
            vt_ref = refs[branch][1]
            vt = vt_ref[j] if nt == 1 else jnp.concatenate([vt_ref[j], vt_ref[j + 1]], axis=1)
            state[branch] = _attn_update(ss, vt, state[branch], mask)
            ss = ss_next

        outs = []
        for hh in range(HPG):
            o_sel, o_win = (acc[:HEAD_DIM] / acc[HEAD_DIM:HEAD_DIM + 1]
                            for _, acc in (state["sel"][hh], state["win"][hh]))
            g_cmp, g_sel, g_win = (gt[c * HPG + hh:c * HPG + hh + 1] for c in range(3))
            outs.append(g_cmp * heads_cmp[hh] + g_sel * o_sel + g_win * o_win)
        o_ref[...] = jnp.concatenate(outs, axis=0).T.astype(o_ref.dtype)

    for n in range(nq):
        pl.when(i == n)(functools.partial(run, n))


def _nsa_attn(q, kvcmp, kvcmp_t, ks, vs_t, kw, vw_t, gates, *, batch, seq):
    t = q.shape[0]
    tq, tk = TQ, TK
    assert tq == tk and WINDOW == 2 * tk
    nq = seq // tq
    n_sel = seq // SEL_BLOCK
    nb = kvcmp.shape[2]
    assert nb == LANES and n_sel == 32
    rowblk = lambda w: pl.BlockSpec((tq, w), lambda b, g, i: (b * nq + i, g))
    seqblk = pl.BlockSpec((seq, HPAD), lambda b, g, i: (b, g))
    vtblk = pl.BlockSpec((seq // tk, VROWS, tk), lambda b, g, i: (b, g, 0))
    return pl.pallas_call(
        functools.partial(_nsa_attn_kernel, tq=tq, tk=tk, n_sel=n_sel),
        grid=(batch, N_KV, nq),
        in_specs=[rowblk(HPG * HPAD),
                  pl.BlockSpec((1, 1, nb, HPAD), lambda b, g, i: (g, b, 0, 0)),
                  pl.BlockSpec((1, 1, HPAD, nb), lambda b, g, i: (N_KV + g, b, 0, 0)),
                  pl.BlockSpec((n_sel, LANES), lambda b, g, i: (0, 0)),
                  seqblk, vtblk, seqblk, vtblk,
                  rowblk(LANES)],
        out_specs=rowblk(HPG * HEAD_DIM),
        out_shape=jax.ShapeDtypeStruct((t, ATTN_W), BF16),
        compiler_params=_cparams(("arbitrary", "arbitrary", "arbitrary")),
        name="nsa_attn",
    )(q, kvcmp, kvcmp_t, _overlap_t(), ks, vs_t, kw, vw_t, gates)


def _dense_tail_kernel(*refs, alpha, with_proj, tiles_per_seq, tm):
    (h_ref, oa_ref, ocv_ref, wo_ref, g1_ref, b1_ref, wg_ref, wu_ref, wd_ref, g2_ref, b2_ref) = refs[:11]
    m = _dot(oa_ref[...], wo_ref[:ATTN_W, :]) + _dot(ocv_ref[...], wo_ref[ATTN_W:, :])
    h1 = _ln(alpha * h_ref[...] + m, g1_ref[...], b1_ref[...])
    hb = h1.astype(BF16)
    a = (jax.nn.silu(_dot(hb, wg_ref[...])) * _dot(hb, wu_ref[...])).astype(BF16)
    h2 = _ln(alpha * h1 + _dot(a, wd_ref[...]), g2_ref[...], b2_ref[...])
    if with_proj:
        w_refs = refs[11:11 + N_PROJ_W]
        outs = refs[11 + N_PROJ_W:len(refs) - N_PROJ_SCRATCH]
        outs[0][...] = h2
        _project_inputs(h2, w_refs, outs[1:], refs[len(refs) - N_PROJ_SCRATCH:], tiles_per_seq=tiles_per_seq, tm=tm)
    else:
        refs[11][...] = h2


def _dense_tail(h, o_attn, o_conv, w_o, ln1, ffn_w, ln2, proj_w, *, alpha, seq):
    t = h.shape[0]
    tm = TM_PROJ
    row = lambda w: pl.BlockSpec((tm, w), lambda i: (i, 0))
    vec = _resident((1, D_MODEL))
    in_specs = [row(D_MODEL), row(ATTN_W), row(CONV_CH), _resident((D_MODEL, D_MODEL)), vec, vec,
                _resident((D_MODEL, D_FF)), _resident((D_MODEL, D_FF)), _resident((D_FF, D_MODEL)), vec, vec]
    args = [h, o_attn, o_conv, w_o, *ln1, *ffn_w, *ln2]
    out_shape, out_specs, scratch = [jax.ShapeDtypeStruct((t, D_MODEL), F32)], [row(D_MODEL)], []
    if proj_w is not None:
        w_specs, p_shape, p_specs, scratch = _proj_specs(t, tm)
        in_specs += w_specs
        args += list(proj_w)
        out_shape += p_shape
        out_specs += p_specs
    return pl.pallas_call(
        functools.partial(_dense_tail_kernel, alpha=alpha, with_proj=proj_w is not None,
                          tiles_per_seq=seq // tm, tm=tm),
        grid=(t // tm,),
        in_specs=in_specs,
        out_specs=out_specs,
        out_shape=out_shape,
        scratch_shapes=scratch,
        compiler_params=_cparams(("arbitrary",)),
        name="dense_tail",
    )(*args)


TMX = 512
CH = 16
LROWS = 2 * TMX + N_EXPERTS * CH
TMR = 512
XROW = D_MODEL + LANES


def _out_proj_kernel(h_ref, oa_ref, ocv_ref, w_ref, g_ref, b_ref, o_ref, *, alpha):
    m = _dot(oa_ref[...], w_ref[:ATTN_W, :]) + _dot(ocv_ref[...], w_ref[ATTN_W:, :])
    o_ref[...] = _ln(alpha * h_ref[...] + m, g_ref[...], b_ref[...])


def _out_proj(h, o_attn, o_conv, w_o, ln_g, ln_b, *, alpha):
    t = h.shape[0]
    tm = TM_PROJ
    row = lambda w: pl.BlockSpec((tm, w), lambda i: (i, 0))
    return pl.pallas_call(
        functools.partial(_out_proj_kernel, alpha=alpha),
        grid=(t // tm,),
        in_specs=[row(D_MODEL), row(ATTN_W), row(CONV_CH),
                  _resident((D_MODEL, D_MODEL)), _resident((1, D_MODEL)), _resident((1, D_MODEL))],
        out_specs=row(D_MODEL),
        out_shape=jax.ShapeDtypeStruct((t, D_MODEL), F32),
        compiler_params=_cparams(("arbitrary",)),
        name="out_proj",
    )(h, o_attn, o_conv, w_o, ln_g, ln_b)


def _route_kernel(h_ref, wr_ref, ltri_ref, ustr_ref, meta_ref, metat_ref, cnt_ref, wp_ref, *, tm):
    h = h_ref[...]
    h_hi = h.astype(BF16)
    h_lo = (h - h_hi.astype(F32)).astype(BF16)
    hw = _dot(h_hi, wr_ref[...])
    logits = hw[:, :LANES] + hw[:, LANES:] + _dot(h_lo, wr_ref[:, :LANES])
    lane = lax.broadcasted_iota(jnp.int32, logits.shape, 1)
    logits = jnp.where(lane < N_EXPERTS, logits, -jnp.inf)
    m1 = jnp.max(logits, axis=-1, keepdims=True)
    i1 = jnp.min(jnp.where(logits == m1, lane, LANES), axis=-1, keepdims=True)
    rest = jnp.where(lane == i1, -jnp.inf, logits)
    m2 = jnp.max(rest, axis=-1, keepdims=True)
    i2 = jnp.min(jnp.where(rest == m2, lane, LANES), axis=-1, keepdims=True)
    e2 = jnp.exp(m2 - m1)
    den = 1.0 + e2
    w1 = 1.0 / den
    w2 = e2 / den

    routed = (lane == i1) | (lane == i2)
    cnt = _dot(ltri_ref[...], routed.astype(BF16))
    n = cnt[tm - 1:tm, :]
    padded = jnp.floor((n + (CH - 1)) * (1.0 / CH)) * CH
    seg_off = _dot(jnp.broadcast_to(padded, (8, LANES)).astype(BF16), ustr_ref[...])[0:1]
    dest = seg_off + cnt - 1.0
    d1 = jnp.sum(jnp.where(lane == i1, dest, 0.0), axis=-1, keepdims=True)
    d2 = jnp.sum(jnp.where(lane == i2, dest, 0.0), axis=-1, keepdims=True)
    meta = jnp.where(lane == 0, d1, jnp.where(lane == 1, d2, jnp.where(lane == 2, w1, jnp.where(lane == 3, w2, 0.0))))
    meta_ref[...] = meta
    metat_ref[...] = meta.T[0:8, :]
    cnt_ref[0] = jnp.broadcast_to(n, (8, LANES))
    for k, w in enumerate((w1, w2)):
        hi = w.astype(BF16).astype(F32)
        mid = (w - hi).astype(BF16).astype(F32)
        lo = (w - hi - mid).astype(BF16).astype(F32)
        wp_ref[k] = jnp.where(lane == 0, hi, jnp.where(lane == 1, mid, jnp.where(lane == 2, lo, 0.0))).astype(BF16)


def _moe_route(h, w_router):
    t = h.shape[0]
    tm = TMX
    nt = t // tm
    row = lambda w: pl.BlockSpec((tm, w), lambda i: (i, 0))
    wr = jnp.pad(w_router, ((0, 0), (0, LANES - N_EXPERTS)))
    wr_hi = wr.astype(BF16)
    wr_lo = (wr - wr_hi.astype(F32)).astype(BF16)
    wr2 = jnp.concatenate([wr_hi, wr_lo], axis=1)
    ltri = jnp.asarray(np.tril(np.ones((tm, tm), np.float32)), dtype=BF16)
    ustr = jnp.asarray(np.triu(np.ones((LANES, LANES), np.float32), 1), dtype=BF16)
    return pl.pallas_call(
        functools.partial(_route_kernel, tm=tm),
        grid=(nt,),
        in_specs=[row(D_MODEL), _resident((D_MODEL, 2 * LANES)), _resident((tm, tm)), _resident((LANES, LANES))],
        out_specs=[row(LANES),
                   pl.BlockSpec((8, tm), lambda i: (0, i)),
                   pl.BlockSpec((1, 8, LANES), lambda i: (i, 0, 0)),
                   pl.BlockSpec((2, tm, LANES), lambda i: (0, i, 0))],
        out_shape=[jax.ShapeDtypeStruct((t, LANES), F32), jax.ShapeDtypeStruct((8, t), F32),
                   jax.ShapeDtypeStruct((nt, 8, LANES), F32), jax.ShapeDtypeStruct((2, t, LANES), BF16)],
        compiler_params=_cparams(("arbitrary",)),
        name="moe_route",
    )(h, wr2, ltri, ustr)


def _moe_plan(cnt, n_row_tiles):
    n = cnt[:, 0, :N_EXPERTS].astype(jnp.int32)
    p = (n + CH - 1) // CH * CH
    tot = p.sum(0)
    tot_pad = (tot + TMR - 1) // TMR * TMR
    gend = jnp.cumsum(tot_pad)
    gstart = gend - tot_pad
    goff = gstart[None, :] + jnp.cumsum(p, 0) - p
    loff = jnp.cumsum(p, 1) - p
    n_used = gend[-1:] // TMR
    tile_start = jnp.arange(n_row_tiles, dtype=jnp.int32) * TMR
    texp = jnp.minimum(jnp.sum(tile_start[:, None] >= gend[None, :], axis=1), N_EXPERTS - 1).astype(jnp.int32)
    texp = jnp.where(jnp.arange(n_row_tiles) < n_used[0], texp, texp[jnp.maximum(n_used[0] - 1, 0)])
    flat = lambda a: a.reshape(-1).astype(jnp.int32)
    gap_start = jnp.concatenate([gstart + tot, gend[-1:]])
    gap_rows = jnp.concatenate([tot_pad - tot, n_row_tiles * TMR - gend[-1:]])
    return dict(loff=flat(loff), goff=flat(goff), nch=flat(p // CH), gap_start=flat(gap_start),
                gap_nch=flat(gap_rows // CH), texp=flat(texp), n_used=flat(n_used))


def _chunk_copies(src_ref, dst_ref, sem, src_off, dst_off, n, src_step=CH):
    def body(c, carry):
        s = pl.multiple_of(src_off + c * src_step, CH)
        d = pl.multiple_of(dst_off + c * CH, CH)
        pltpu.make_async_copy(src_ref.at[pl.ds(s, CH)], dst_ref.at[pl.ds(d, CH)], sem).start()
        return carry
    lax.fori_loop(0, n, body, 0)


def _wait_chunks(src_ref, dst_ref, sem, n):
    def body(c, carry):
        pltpu.make_async_copy(src_ref.at[pl.ds(0, CH)], dst_ref.at[pl.ds(0, CH)], sem).wait()
        return carry
    lax.fori_loop(0, n, body, 0)


def _tile_chunks(tile, nch):
    total = 0
    for e in range(N_EXPERTS):
        total = total + nch[tile * N_EXPERTS + e]
    return total


def _start_segments(tile, nch, src_off, dst_off, src_ref, dst_ref, sem):
    for e in range(N_EXPERTS):
        idx = tile * N_EXPERTS + e
        _chunk_copies(src_ref, dst_ref, sem, src_off[idx], dst_off[idx], nch[idx])


def _dispatch_kernel(loff, goff, nch, gap_start, gap_nch, h_ref, metat_ref, wp_ref, xs_ref, xc_ref, z_ref, sem,
                     *, tm):
    i = pl.program_id(0)
    last = pl.num_programs(0) - 1
    slot = i % 2
    r = lax.broadcasted_iota(jnp.int32, (LROWS, tm), 0).astype(F32)
    pick1 = (r == metat_ref[0:1, :]).astype(BF16)
    pick2 = (r == metat_ref[1:2, :]).astype(BF16)
    xc_ref[slot, :, :D_MODEL] = _dot(pick1 + pick2, h_ref[...].astype(BF16)).astype(BF16)
    xc_ref[slot, :, D_MODEL:] = (_dot(pick1, wp_ref[0]) + _dot(pick2, wp_ref[1])).astype(BF16)
    _start_segments(i, nch, loff, goff, xc_ref.at[slot], xs_ref, sem.at[slot])

    @pl.when(i > 0)
    def _():
        _wait_chunks(xc_ref.at[1 - slot], xs_ref, sem.at[1 - slot], _tile_chunks(i - 1, nch))

    @pl.when(i == last)
    def _():
        z_ref[...] = jnp.zeros_like(z_ref)
        gaps = 0
        for e in range(N_EXPERTS + 1):
            _chunk_copies(z_ref, xs_ref, sem.at[2], 0, gap_start[e], gap_nch[e], src_step=0)
            gaps = gaps + gap_nch[e]
        _wait_chunks(z_ref, xs_ref, sem.at[2], gaps)
        _wait_chunks(xc_ref.at[slot], xs_ref, sem.at[slot], _tile_chunks(i, nch))


def _moe_dispatch(h, metat, wparts, plan, n_rows):
    t = h.shape[0]
    tm = TMX
    grid_spec = pltpu.PrefetchScalarGridSpec(
        num_scalar_prefetch=5,
        grid=(t // tm,),
        in_specs=[pl.BlockSpec((tm, D_MODEL), lambda i, *_: (i, 0)),
                  pl.BlockSpec((8, tm), lambda i, *_: (0, i)),
                  pl.BlockSpec((2, tm, LANES), lambda i, *_: (0, i, 0))],
        out_specs=pl.BlockSpec(memory_space=pl.ANY),
        scratch_shapes=[pltpu.VMEM((2, LROWS, XROW), BF16), pltpu.VMEM((CH, XROW), BF16),
                        pltpu.SemaphoreType.DMA((3,))],
    )
    return pl.pallas_call(
        functools.partial(_dispatch_kernel, tm=tm),
        grid_spec=grid_spec,
        out_shape=jax.ShapeDtypeStruct((n_rows, XROW), BF16),
        compiler_params=_cparams(("arbitrary",)),
        name="moe_dispatch",
    )(plan["loff"], plan["goff"], plan["nch"], plan["gap_start"], plan["gap_nch"], h, metat, wparts)


def _experts_kernel(texp, n_used, x_ref, wgu_ref, wd_ref, y_ref):
    used = pl.program_id(0) < n_used[0]

    @pl.when(used)
    def _():
        x = x_ref[:, :D_MODEL]
        wp = x_ref[:, D_MODEL:].astype(F32)
        gate = wp[:, 0:1] + wp[:, 1:2] + wp[:, 2:3]
        gu = _dot(x, wgu_ref[0])
        a = (jax.nn.silu(gu[:, :D_FF_EXPERT]) * gu[:, D_FF_EXPERT:]).astype(BF16)
        y_ref[...] = (gate * _dot(a, wd_ref[0])).astype(BF16)

    @pl.when(jnp.logical_not(used))
    def _():
        y_ref[...] = jnp.zeros_like(y_ref)


def _moe_experts(xs, wgu, wd, plan):
    n_rows = xs.shape[0]
    rows = lambda r, texp, n_used: (jnp.minimum(r, n_used[0] - 1), 0)
    wspec = lambda shape: pl.BlockSpec((1,) + shape, lambda r, texp, n_used: (texp[r], 0, 0))
    grid_spec = pltpu.PrefetchScalarGridSpec(
        num_scalar_prefetch=2,
        grid=(n_rows // TMR,),
        in_specs=[pl.BlockSpec((TMR, XROW), rows), wspec((D_MODEL, 2 * D_FF_EXPERT)),
                  wspec((D_FF_EXPERT, D_MODEL))],
        out_specs=pl.BlockSpec((TMR, D_MODEL), lambda r, texp, n_used: (r, 0)),
    )
    return pl.pallas_call(
        _experts_kernel,
        grid_spec=grid_spec,
        out_shape=jax.ShapeDtypeStruct((n_rows, D_MODEL), BF16),
        compiler_params=_cparams(("arbitrary",)),
        name="moe_experts",
    )(plan["texp"], plan["n_used"], xs, wgu, wd)


def _combine_kernel(loff, goff, nch, h_ref, meta_ref, ys_ref, g_ref, b_ref, o_ref, yb_ref, sem, *, tm, alpha):
    i = pl.program_id(0)
    slot = i % 2

    @pl.when(i == 0)
    def _():
        yb_ref[...] = jnp.zeros_like(yb_ref)
        _start_segments(0, nch, goff, loff, ys_ref, yb_ref.at[0], sem.at[0])

    @pl.when(i + 1 < pl.num_programs(0))
    def _():
        _start_segments(i + 1, nch, goff, loff, ys_ref, yb_ref.at[1 - slot], sem.at[1 - slot])

    meta = meta_ref[...]
    r = lax.broadcasted_iota(jnp.int32, (tm, LROWS), 1).astype(F32)
    pick = ((r == meta[:, 0:1]) | (r == meta[:, 1:2])).astype(BF16)
    _wait_chunks(ys_ref, yb_ref.at[slot], sem.at[slot], _tile_chunks(i, nch))
    f = _dot(pick, yb_ref[slot])
    o_ref[...] = _ln(alpha * h_ref[...] + f, g_ref[...], b_ref[...])


def _moe_combine(h, meta, ys, plan, ln_g, ln_b, *, alpha):
    t = h.shape[0]
    tm = TMX
    grid_spec = pltpu.PrefetchScalarGridSpec(
        num_scalar_prefetch=3,
        grid=(t // tm,),
        in_specs=[pl.BlockSpec((tm, D_MODEL), lambda i, *_: (i, 0)),
                  pl.BlockSpec((tm, LANES), lambda i, *_: (i, 0)),
                  pl.BlockSpec(memory_space=pl.ANY),
                  pl.BlockSpec((1, D_MODEL), lambda i, *_: (0, 0)),
                  pl.BlockSpec((1, D_MODEL), lambda i, *_: (0, 0))],
        out_specs=pl.BlockSpec((tm, D_MODEL), lambda i, *_: (i, 0)),
        scratch_shapes=[pltpu.VMEM((2, LROWS, D_MODEL), BF16), pltpu.SemaphoreType.DMA((2,))],
    )
    return pl.pallas_call(
        functools.partial(_combine_kernel, tm=tm, alpha=alpha),
        grid_spec=grid_spec,
        out_shape=jax.ShapeDtypeStruct((t, D_MODEL), F32),
        compiler_params=_cparams(("arbitrary",)),
        name="moe_combine",
    )(plan["loff"], plan["goff"], plan["nch"], h, meta, ys, ln_g, ln_b)


def _moe_tail(h, o_attn, o_conv, w_o, ln1, w_router, wgu, wd, ln_g, ln_b, *, alpha):
    t = h.shape[0]
    nt = t // TMX
    max_rows = 2 * t + nt * N_EXPERTS * (CH - 1) + N_EXPERTS * (TMR - CH)
    n_row_tiles = -(-max_rows // TMR)
    h = _out_proj(h, o_attn, o_conv, w_o, *ln1, alpha=alpha)
    meta, metat, cnt, wparts = _moe_route(h, w_router)
    plan = _moe_plan(cnt, n_row_tiles)
    xs = _moe_dispatch(h, metat, wparts, plan, n_row_tiles * TMR)
    ys = _moe_experts(xs, wgu, wd, plan)
    return _moe_combine(h, meta, ys, plan, ln_g, ln_b, alpha=alpha)


def kernel(x, ln_in_g, ln_in_b, w_in, cmp_pos, cmp_w1, cmp_b1, cmp_w2, cmp_b2, conv_w, w_o, ln1_g, ln1_b, ln2_g, ln2_b, ffn_wg, ffn_wu, ffn_wd, moe_router, moe_wg, moe_wu, moe_wd):
    batch, seq, d = x.shape
    depth = w_in.shape[0]
    assert d == D_MODEL and seq % TM_PROJ == 0 and seq // CMP_STRIDE == LANES
    alpha = (2 * depth) ** 0.25
    t = batch * seq
    vec = lambda v: v.reshape(1, D_MODEL)
    def proj_weights(l):
        return (*_prep_w_in(w_in[l]), jnp.pad(conv_w[l], ((0, 8 - CONV_W), (0, 0))))

    h, *proj = _in_proj(x.reshape(t, d), vec(ln_in_g), vec(ln_in_b), proj_weights(0), seq=seq, pre_ln=True)
    for l in range(depth):
        q, kvc, ks, vs, kw, vw, gates, o_conv = proj
        kvcmp, kvcmp_t = _compress(kvc, cmp_pos[l], cmp_w1[l], cmp_b1[l], cmp_w2[l], cmp_b2[l],
                                   batch=batch, seq=seq)
        o_attn = _nsa_attn(q, kvcmp, kvcmp_t, ks, vs, kw, vw, gates, batch=batch, seq=seq)
        ln1 = (vec(ln1_g[l]), vec(ln1_b[l]))
        ln2 = (vec(ln2_g[l]), vec(ln2_b[l]))
        next_w = proj_weights(l + 1) if l + 1 < depth else None
        if l % 2 == 0:
            ffn_w = (ffn_wg[l // 2].astype(BF16), ffn_wu[l // 2].astype(BF16), ffn_wd[l // 2].astype(BF16))
            h, *proj = _dense_tail(h, o_attn, o_conv, w_o[l].astype(BF16), ln1, ffn_w, ln2, next_w,
                                   alpha=alpha, seq=seq)
        else:
            wgu = jnp.concatenate([moe_wg[l // 2], moe_wu[l // 2]], axis=-1).astype(BF16)
            h = _moe_tail(h, o_attn, o_conv, w_o[l].astype(BF16), ln1, moe_router[l // 2],
                          wgu, moe_wd[l // 2].astype(BF16), *ln2, alpha=alpha)
            if next_w is not None:
                proj = _in_proj(h, None, None, next_w, seq=seq, pre_ln=False)
    return h.reshape(batch, seq, d)
```

```python
import functools

import numpy as np
import jax
import jax.numpy as jnp
from jax import lax
from jax.experimental import pallas as pl
from jax.experimental.pallas import tpu as pltpu

F32 = jnp.float32
BF16 = jnp.bfloat16

D_MODEL = 1024
HEAD_DIM = 64
N_HEADS = 8
N_KV = 2
HPG = N_HEADS // N_KV
ATTN_W = N_HEADS * HEAD_DIM
KV_W = N_KV * HEAD_DIM
CONV_CH = D_MODEL - ATTN_W
CONV_W = 3
CMP_BLOCK = 32
CMP_STRIDE = 16
CMP_HIDDEN = 256
SEL_BLOCK = 64
SEL_TOPN = 16
WINDOW = 512
D_FF = 2816
N_EXPERTS = 8
D_FF_EXPERT = 1408
LN_EPS = 1e-5
NEG = -1e30
FORCE = 1e9

LANES = 128
HPAD = LANES
VROWS = HPAD
Q_SCALE = HEAD_DIM ** -0.5 * np.log2(np.e)

C_Q = 0
C_KVC = C_Q + ATTN_W
C_KS = C_KVC + 2 * KV_W
C_KW = C_KS + KV_W
C_U = C_KW + KV_W
C_B = C_U + CONV_CH
C_C = C_B + CONV_CH
C_G = C_C + CONV_CH
C_END = C_G + N_KV * LANES

TM_PROJ = 512
TQ = 256
TK = 256
VMEM_LIMIT = 56 * 1024 * 1024


def _cparams(sem):
    return pltpu.CompilerParams(dimension_semantics=sem, vmem_limit_bytes=VMEM_LIMIT)


def _ln(x, g, b):
    mu = jnp.mean(x, -1, keepdims=True)
    xc = x - mu
    var = jnp.mean(xc * xc, -1, keepdims=True)
    return xc * lax.rsqrt(var + LN_EPS) * g + b


def _dot(a, b):
    return jnp.dot(a, b, preferred_element_type=F32)


def _dot_nt(a, b):
    return lax.dot_general(a, b, (((1,), (1,)), ((), ())), preferred_element_type=F32)


def _resident(shape):
    nd = len(shape)
    return pl.BlockSpec(shape, lambda *_: (0,) * nd, pipeline_mode=pl.Buffered(1))


def _spread_heads(z):
    low = lax.broadcasted_iota(jnp.int32, (z.shape[0], LANES), 1) < HEAD_DIM
    tiles = []
    for p in range(z.shape[1] // LANES):
        pair = z[:, p * LANES:(p + 1) * LANES]
        tiles += [jnp.where(low, pair, 0.0), jnp.where(low, pltpu.roll(pair, HEAD_DIM, 1), 0.0)]
    return jnp.concatenate(tiles, axis=1)


def _project_inputs(h, w_refs, out_refs, scratch_refs, *, tiles_per_seq, tm):
    w_ref, wvt_ref, cw_ref = w_refs
    q_ref, kvc_ref, ks_ref, vs_ref, kw_ref, vw_ref, gate_ref, oc_ref = out_refs
    carry_ref, kvc_scr = scratch_refs
    hb = h.astype(BF16)

    def proj(lo, hi):
        return _dot(hb, w_ref[:, lo:hi])

    q_ref[...] = _spread_heads(proj(C_Q, C_KVC) * Q_SCALE).astype(BF16)

    kvc = proj(C_KVC, C_KS)
    low = lax.broadcasted_iota(jnp.int32, (tm // CMP_STRIDE, LANES), 1) < HEAD_DIM
    for kind in range(2):
        kvc_scr[kind] = kvc[:, kind * LANES:(kind + 1) * LANES]
        for p in range(CMP_STRIDE // 2):
            a = kvc_scr[kind, pl.ds(2 * p, tm // CMP_STRIDE, stride=CMP_STRIDE), :]
            b = kvc_scr[kind, pl.ds(2 * p + 1, tm // CMP_STRIDE, stride=CMP_STRIDE), :]
            kvc_ref[2 * kind, :, p * LANES:(p + 1) * LANES] = jnp.where(low, a, pltpu.roll(b, HEAD_DIM, 1))
            kvc_ref[2 * kind + 1, :, p * LANES:(p + 1) * LANES] = jnp.where(low, pltpu.roll(a, HEAD_DIM, 1), b)

    seq_tile = pl.program_id(0) % tiles_per_seq
    pos = lax.broadcasted_iota(jnp.int32, (tm, N_KV * HPAD), 0) + seq_tile * tm
    lane = lax.broadcasted_iota(jnp.int32, (tm, N_KV * HPAD), 1) % HPAD
    onehot = jnp.where(pos // SEL_BLOCK == lane - HEAD_DIM, 1.0, 0.0)
    in_tag = (lane >= HEAD_DIM) & (lane < HEAD_DIM + 32)
    kk = proj(C_KS, C_U)
    ks_ref[...] = jnp.where(in_tag, onehot, _spread_heads(kk[:, :KV_W])).astype(BF16)
    kw_ref[...] = _spread_heads(kk[:, KV_W:]).astype(BF16)

    vt = _dot_nt(wvt_ref[...], hb)
    tail = jnp.where(lax.broadcasted_iota(jnp.int32, (VROWS - HEAD_DIM, tm), 0) == 0, 1.0, 0.0)
    for n, ref in enumerate((vs_ref, vw_ref)):
        rows = []
        for grp in range(N_KV):
            lo = (n * N_KV + grp) * HEAD_DIM
            rows += [vt[lo:lo + HEAD_DIM], tail]
        full = jnp.concatenate(rows, axis=0).astype(BF16)
        for c in range(tm // TK):
            ref[c] = full[:, c * TK:(c + 1) * TK]

    cu = proj(C_C, C_G) * proj(C_U, C_B)

    @pl.when(seq_tile == 0)
    def _():
        carry_ref[...] = jnp.zeros_like(carry_ref)

    prev = carry_ref[...]
    row = lax.broadcasted_iota(jnp.int32, (tm, CONV_CH), 0)
    s1 = jnp.where(row == 0, prev[7:8], pltpu.roll(cu, 1, 0))
    s2 = jnp.where(row == 0, prev[6:7], jnp.where(row == 1, prev[7:8], pltpu.roll(cu, 2, 0)))
    y = s2 * cw_ref[0:1, :] + s1 * cw_ref[1:2, :] + cu * cw_ref[2:3, :]
    oc_ref[...] = (proj(C_B, C_C) * y).astype(BF16)
    carry_ref[...] = cu[tm - 8:tm]
    gate_ref[...] = jax.nn.sigmoid(proj(C_G, C_END))


N_PROJ_W = 3
N_PROJ_OUT = 8
N_PROJ_SCRATCH = 2


def _proj_specs(t, tm):
    row = lambda w: pl.BlockSpec((tm, w), lambda i: (i, 0))
    w_specs = [_resident((D_MODEL, C_END)), _resident((2 * KV_W, D_MODEL)), _resident((8, CONV_CH))]
    vt_shape = jax.ShapeDtypeStruct((t // TK, N_KV * VROWS, TK), BF16)
    vt_spec = pl.BlockSpec((tm // TK, N_KV * VROWS, TK), lambda i: (i, 0, 0))
    out_shape = [
        jax.ShapeDtypeStruct((t, N_HEADS * HPAD), BF16),
        jax.ShapeDtypeStruct((4, t // CMP_STRIDE, CMP_STRIDE * HEAD_DIM), F32),
        jax.ShapeDtypeStruct((t, N_KV * HPAD), BF16), vt_shape,
        jax.ShapeDtypeStruct((t, N_KV * HPAD), BF16), vt_shape,
        jax.ShapeDtypeStruct((t, N_KV * LANES), F32),
        jax.ShapeDtypeStruct((t, CONV_CH), BF16),
    ]
    out_specs = [
        row(N_HEADS * HPAD),
        pl.BlockSpec((4, tm // CMP_STRIDE, CMP_STRIDE * HEAD_DIM), lambda i: (0, i, 0)),
        row(N_KV * HPAD), vt_spec, row(N_KV * HPAD), vt_spec,
        row(N_KV * LANES), row(CONV_CH),
    ]
    scratch = [pltpu.VMEM((8, CONV_CH), F32), pltpu.VMEM((2, tm, LANES), F32)]
    return w_specs, out_shape, out_specs, scratch


def _in_proj_kernel(*refs, pre_ln, tiles_per_seq, tm):
    n_in = 3 if pre_ln else 1
    x_ref = refs[0]
    w_refs = refs[n_in:n_in + N_PROJ_W]
    outs = refs[n_in + N_PROJ_W:len(refs) - N_PROJ_SCRATCH]
    if pre_ln:
        h = _ln(x_ref[...], refs[1][...], refs[2][...])
        outs[0][...] = h
        outs = outs[1:]
    else:
        h = x_ref[...]
    _project_inputs(h, w_refs, outs, refs[len(refs) - N_PROJ_SCRATCH:], tiles_per_seq=tiles_per_seq, tm=tm)


def _in_proj(x, ln_g, ln_b, proj_w, *, seq, pre_ln):
    t = x.shape[0]
    tm = TM_PROJ
    row = pl.BlockSpec((tm, D_MODEL), lambda i: (i, 0))
    w_specs, out_shape, out_specs, scratch = _proj_specs(t, tm)
    in_specs, args = [row], [x]
    if pre_ln:
        in_specs += [_resident((1, D_MODEL)), _resident((1, D_MODEL))]
        args += [ln_g, ln_b]
        out_shape = [jax.ShapeDtypeStruct((t, D_MODEL), F32)] + out_shape
        out_specs = [row] + out_specs
    return pl.pallas_call(
        functools.partial(_in_proj_kernel, pre_ln=pre_ln, tiles_per_seq=seq // tm, tm=tm),
        grid=(t // tm,),
        in_specs=in_specs + w_specs,
        out_specs=out_specs,
        out_shape=out_shape,
        scratch_shapes=scratch,
        compiler_params=_cparams(("arbitrary",)),
        name="in_proj_ln" if pre_ln else "in_proj",
    )(*args, *proj_w)


def _prep_w_in(w_in):
    o = 0

    def take(n):
        nonlocal o
        s = w_in[:, o:o + n]
        o += n
        return s

    q, kc, vc, ks, vs, kw, vw = take(ATTN_W), *(take(KV_W) for _ in range(6))
    gates = take(3 * N_HEADS).reshape(-1, 3, N_KV, HPG).transpose(0, 2, 1, 3).reshape(-1, N_KV, 3 * HPG)
    gates = jnp.pad(gates, ((0, 0), (0, 0), (0, LANES - 3 * HPG))).reshape(-1, N_KV * LANES)
    conv = take(3 * CONV_CH)
    w_r = jnp.concatenate([q, kc, vc, ks, kw, conv, gates], axis=1).astype(BF16)
    return w_r, jnp.concatenate([vs, vw], axis=1).T.astype(BF16)


def _compress_kernel(x_ref, pos_ref, w1_ref, b1_ref, w2_ref, b2_ref, w2t_ref, b2t_ref, o_ref, ot_ref):
    half = CMP_STRIDE * HEAD_DIM
    nb = x_ref.shape[2]
    x = jnp.concatenate([x_ref[g, 0] for g in range(N_KV)], axis=0)
    pos = pos_ref[0]
    xa = (x + pos[:, :half]).astype(BF16)
    xb = (x + pos[:, half:]).astype(BF16)
    a = _dot(xa, w1_ref[0, :half, :])
    b = _dot(xb, w1_ref[0, half:, :])
    hid = a + pltpu.roll(b, N_KV * nb - 1, 0) + b1_ref[0]
    act = jax.nn.gelu(hid).astype(BF16)
    out = (_dot(act, w2_ref[0]) + b2_ref[0]).astype(BF16)
    out_t = (_dot_nt(w2t_ref[0], act) + b2t_ref[0]).astype(BF16)
    for g in range(N_KV):
        o_ref[g, 0] = out[g * nb:(g + 1) * nb]
        ot_ref[g, 0] = out_t[:, g * nb:(g + 1) * nb]


def _compress(kvc, cmp_pos, cmp_w1, cmp_b1, cmp_w2, cmp_b2, *, batch, seq):
    nb = seq // CMP_STRIDE
    half = CMP_STRIDE * HEAD_DIM
    x = kvc.reshape(4, batch, nb, half)
    pos = cmp_pos.reshape(2, 1, CMP_BLOCK * HEAD_DIM)
    w1 = cmp_w1.astype(BF16)
    b1 = cmp_b1.reshape(2, 1, CMP_HIDDEN)
    w2 = jnp.pad(cmp_w2, ((0, 0), (0, 0), (0, HPAD - HEAD_DIM))).astype(BF16)
    b2 = jnp.pad(cmp_b2, ((0, 0), (0, HPAD - HEAD_DIM))).reshape(2, 1, HPAD)
    w2t = jnp.swapaxes(w2, 1, 2)
    b2t = b2.reshape(2, HPAD, 1)
    kind = lambda j, b: (j, 0, 0)
    return pl.pallas_call(
        _compress_kernel,
        grid=(2, batch),
        in_specs=[
            pl.BlockSpec((N_KV, 1, nb, half), lambda j, b: (j, b, 0, 0)),
            pl.BlockSpec((1, 1, 2 * half), kind),
            pl.BlockSpec((1, 2 * half, CMP_HIDDEN), kind),
            pl.BlockSpec((1, 1, CMP_HIDDEN), kind),
            pl.BlockSpec((1, CMP_HIDDEN, HPAD), kind),
            pl.BlockSpec((1, 1, HPAD), kind),
            pl.BlockSpec((1, HPAD, CMP_HIDDEN), kind),
            pl.BlockSpec((1, HPAD, 1), kind),
        ],
        out_specs=[pl.BlockSpec((N_KV, 1, nb, HPAD), lambda j, b: (j, b, 0, 0)),
                   pl.BlockSpec((N_KV, 1, HPAD, nb), lambda j, b: (j, b, 0, 0))],
        out_shape=[jax.ShapeDtypeStruct((4, batch, nb, HPAD), BF16),
                   jax.ShapeDtypeStruct((4, batch, HPAD, nb), BF16)],
        compiler_params=_cparams(("arbitrary", "arbitrary")),
        name="compress",
    )(x, pos, w1, b1, w2, b2, w2t, b2t)


def _overlap_t():
    n = np.arange(LANES)
    j = np.arange(32)
    cs = n * CMP_STRIDE
    ss = j * SEL_BLOCK
    ov = (cs[None, :] < ss[:, None] + SEL_BLOCK) & (cs[None, :] + CMP_BLOCK > ss[:, None])
    return jnp.asarray(ov, dtype=BF16)


def _cmp_select(qh, kc, vct, ov, i, tq, n_sel):
    n = lax.broadcasted_iota(jnp.int32, (LANES, tq), 0)
    t = lax.broadcasted_iota(jnp.int32, (LANES, tq), 1) + i * tq
    mask = n * CMP_STRIDE + (CMP_BLOCK - 1) <= t
    maskf = mask.astype(F32)
    ss = [jnp.where(mask, _dot_nt(kc, q), NEG) for q in qh]
    es = [jnp.exp2(s - jnp.max(s, axis=0, keepdims=True)) for s in ss]
    ps = [e / jnp.sum(e, axis=0, keepdims=True) * maskf for e in es]
    heads_t = [_dot(vct, p.astype(BF16))[:HEAD_DIM] for p in ps]
    psum = functools.reduce(lambda a, b: a + b, ps)

    p_hi = psum.astype(BF16)
    r1 = psum - p_hi.astype(F32)
    p_mid = r1.astype(BF16)
    p_lo = (r1 - p_mid.astype(F32)).astype(BF16)
    imp = _dot(ov, p_hi) + _dot(ov, p_mid) + _dot(ov, p_lo)

    jt = lax.broadcasted_iota(jnp.int32, (n_sel, tq), 0)
    tt = lax.broadcasted_iota(jnp.int32, (n_sel, tq), 1) + i * tq
    cur = tt // SEL_BLOCK
    valid = jt * SEL_BLOCK <= tt
    forced = (jt == 0) | (jt == cur) | (jt == cur - 1)
    score = jnp.where(forced, FORCE, jnp.where(valid, imp, NEG))
    rank = jnp.zeros((n_sel, tq), jnp.int32)
    for k in range(n_sel):
        sk = score[k:k + 1, :]
        ahead = (sk > score) | ((sk == score) & (jt > k))
        rank = rank + ahead.astype(jnp.int32)
    bias = jnp.where(rank < SEL_TOPN, 0.0, NEG)
    full = jnp.concatenate(
        [jnp.zeros((HEAD_DIM, tq), F32), bias, jnp.zeros((HPAD - HEAD_DIM - n_sel, tq), F32)], axis=0)
    return heads_t, full.T.astype(BF16)


def _attn_scores(k, qa):
    return tuple(_dot_nt(k, qa[hh]) for hh in range(HPG))


def _attn_update(ss, vt, state, mask):
    if mask is not None:
        ss = [jnp.where(mask, s, NEG) for s in ss]
    m_new = [jnp.maximum(state[hh][0], jnp.max(ss[hh], axis=0, keepdims=True)) for hh in range(HPG)]
    ps = [jnp.exp2(ss[hh] - m_new[hh]).astype(BF16) for hh in range(HPG)]
    pv = [_dot(vt, ps[hh]) for hh in range(HPG)]
    return tuple((m_new[hh], jnp.exp2(state[hh][0] - m_new[hh]) * state[hh][1] + pv[hh]) for hh in range(HPG))


def _attn_init(tq):
    return tuple((jnp.full((1, tq), NEG, F32), jnp.zeros((VROWS, tq), F32)) for _ in range(HPG))


def _nsa_attn_kernel(q_ref, kc_ref, vct_ref, ov_ref, ks_ref, vst_ref, kw_ref, vwt_ref, gate_ref, o_ref,
                     *, tq, tk, n_sel):
    i = pl.program_id(2)
    nq = ks_ref.shape[0] // tk
    qh = [q_ref[:, hh * HPAD:(hh + 1) * HPAD] for hh in range(HPG)]
    heads_cmp, selb = _cmp_select(qh, kc_ref[0, 0], vct_ref[0, 0], ov_ref[...], i, tq, n_sel)
    lane = lax.broadcasted_iota(jnp.int32, selb.shape, 1)
    qsel = [jnp.where(lane >= HEAD_DIM, selb, q) for q in qh]
    key = lax.broadcasted_iota(jnp.int32, (tk, tq), 0)
    qry = lax.broadcasted_iota(jnp.int32, (tk, tq), 1)
    causal = key <= qry
    gt = gate_ref[...].T

    key2 = lax.broadcasted_iota(jnp.int32, (2 * tk, tq), 0)
    pair_causal = key2 - tk <= lax.broadcasted_iota(jnp.int32, (2 * tk, tq), 1)

    def run(n):
        chunks = []
        for j in range(0, n, 2):
            chunks.append(("sel", j, 2, pair_causal if j + 1 == n else None))
        if n % 2 == 0:
            chunks.append(("sel", n, 1, causal))
        if n == 0:
            chunks.append(("win", 0, 1, causal))
        else:
            chunks.append(("win", n - 1, 2, pair_causal))
        if n >= 2:
            chunks.append(("win", n - 2, 1, key > qry))
        refs = {"sel": (ks_ref, vst_ref, qsel), "win": (kw_ref, vwt_ref, qh)}

        def scores(chunk):
            branch, j, nt, _ = chunk
            return _attn_scores(refs[branch][0][j * tk:(j + nt) * tk, :], refs[branch][2])

        state = {"sel": _attn_init(tq), "win": _attn_init(tq)}
        ss = scores(chunks[0])
        for idx, (branch, j, nt, mask) in enumerate(chunks):
            ss_next = scores(chunks[idx + 1]) if idx + 1 < len(chunks) else None
            vt_ref = refs[branch][1]
            vt = vt_ref[j] if nt == 1 else jnp.concatenate([vt_ref[j], vt_ref[j + 1]], axis=1)
            state[branch] = _attn_update(ss, vt, state[branch], mask)
            ss = ss_next

        outs = []
        for hh in range(HPG):
            o_sel, o_win = (acc[:HEAD_DIM] / acc[HEAD_DIM:HEAD_DIM + 1]
                            for _, acc in (state["sel"][hh], state["win"][hh]))
            g_cmp, g_sel, g_win = (gt[c * HPG + hh:c * HPG + hh + 1] for c in range(3))
            outs.append(g_cmp * heads_cmp[hh] + g_sel * o_sel + g_win * o_win)
        o_ref[...] = jnp.concatenate(outs, axis=0).T.astype(o_ref.dtype)

    for n in range(nq):
        pl.when(i == n)(functools.partial(run, n))


def _nsa_attn(q, kvcmp, kvcmp_t, ks, vs_t, kw, vw_t, gates, *, batch, seq):
    t = q.shape[0]
    tq, tk = TQ, TK
    assert tq == tk and WINDOW == 2 * tk
    nq = seq // tq
    n_sel = seq // SEL_BLOCK
    nb = kvcmp.shape[2]
    assert nb == LANES and n_sel == 32
    rowblk = lambda w: pl.BlockSpec((tq, w), lambda b, g, i: (b * nq + i, g))
    seqblk = pl.BlockSpec((seq, HPAD), lambda b, g, i: (b, g))
    vtblk = pl.BlockSpec((seq // tk, VROWS, tk), lambda b, g, i: (b, g, 0))
    return pl.pallas_call(
        functools.partial(_nsa_attn_kernel, tq=tq, tk=tk, n_sel=n_sel),
        grid=(batch, N_KV, nq),
        in_specs=[rowblk(HPG * HPAD),
                  pl.BlockSpec((1, 1, nb, HPAD), lambda b, g, i: (g, b, 0, 0)),
                  pl.BlockSpec((1, 1, HPAD, nb), lambda b, g, i: (N_KV + g, b, 0, 0)),
                  pl.BlockSpec((n_sel, LANES), lambda b, g, i: (0, 0)),
                  seqblk, vtblk, seqblk, vtblk,
                  rowblk(LANES)],
        out_specs=rowblk(HPG * HEAD_DIM),
        out_shape=jax.ShapeDtypeStruct((t, ATTN_W), BF16),
        compiler_params=_cparams(("arbitrary", "arbitrary", "arbitrary")),
        name="nsa_attn",
    )(q, kvcmp, kvcmp_t, _overlap_t(), ks, vs_t, kw, vw_t, gates)


def _dense_tail_kernel(*refs, alpha, with_proj, tiles_per_seq, tm):
    (h_ref, oa_ref, ocv_ref, wo_ref, g1_ref, b1_ref, wg_ref, wu_ref, wd_ref, g2_ref, b2_ref) = refs[:11]
    m = _dot(oa_ref[...], wo_ref[:ATTN_W, :]) + _dot(ocv_ref[...], wo_ref[ATTN_W:, :])
    h1 = _ln(alpha * h_ref[...] + m, g1_ref[...], b1_ref[...])
    hb = h1.astype(BF16)
    a = (jax.nn.silu(_dot(hb, wg_ref[...])) * _dot(hb, wu_ref[...])).astype(BF16)
    h2 = _ln(alpha * h1 + _dot(a, wd_ref[...]), g2_ref[...], b2_ref[...])
    if with_proj:
        w_refs = refs[11:11 + N_PROJ_W]
        outs = refs[11 + N_PROJ_W:len(refs) - N_PROJ_SCRATCH]
        outs[0][...] = h2
        _project_inputs(h2, w_refs, outs[1:], refs[len(refs) - N_PROJ_SCRATCH:], tiles_per_seq=tiles_per_seq, tm=tm)
    else:
        refs[11][...] = h2


def _dense_tail(h, o_attn, o_conv, w_o, ln1, ffn_w, ln2, proj_w, *, alpha, seq):
    t = h.shape[0]
    tm = TM_PROJ
    row = lambda w: pl.BlockSpec((tm, w), lambda i: (i, 0))
    vec = _resident((1, D_MODEL))
    in_specs = [row(D_MODEL), row(ATTN_W), row(CONV_CH), _resident((D_MODEL, D_MODEL)), vec, vec,
                _resident((D_MODEL, D_FF)), _resident((D_MODEL, D_FF)), _resident((D_FF, D_MODEL)), vec, vec]
    args = [h, o_attn, o_conv, w_o, *ln1, *ffn_w, *ln2]
    out_shape, out_specs, scratch = [jax.ShapeDtypeStruct((t, D_MODEL), F32)], [row(D_MODEL)], []
    if proj_w is not None:
        w_specs, p_shape, p_specs, scratch = _proj_specs(t, tm)
        in_specs += w_specs
        args += list(proj_w)
        out_shape += p_shape
        out_specs += p_specs
    return pl.pallas_call(
        functools.partial(_dense_tail_kernel, alpha=alpha, with_proj=proj_w is not None,
                          tiles_per_seq=seq // tm, tm=tm),
        grid=(t // tm,),
        in_specs=in_specs,
        out_specs=out_specs,
        out_shape=out_shape,
        scratch_shapes=scratch,
        compiler_params=_cparams(("arbitrary",)),
        name="dense_tail",
    )(*args)


TMX = 512
CH = 16
LROWS = 2 * TMX + N_EXPERTS * CH
TMR = 512
XROW = D_MODEL + LANES


def _out_proj_kernel(h_ref, oa_ref, ocv_ref, w_ref, g_ref, b_ref, o_ref, *, alpha):
    m = _dot(oa_ref[...], w_ref[:ATTN_W, :]) + _dot(ocv_ref[...], w_ref[ATTN_W:, :])
    o_ref[...] = _ln(alpha * h_ref[...] + m, g_ref[...], b_ref[...])


def _out_proj(h, o_attn, o_conv, w_o, ln_g, ln_b, *, alpha):
    t = h.shape[0]
    tm = TM_PROJ
    row = lambda w: pl.BlockSpec((tm, w), lambda i: (i, 0))
    return pl.pallas_call(
        functools.partial(_out_proj_kernel, alpha=alpha),
        grid=(t // tm,),
        in_specs=[row(D_MODEL), row(ATTN_W), row(CONV_CH),
                  _resident((D_MODEL, D_MODEL)), _resident((1, D_MODEL)), _resident((1, D_MODEL))],
        out_specs=row(D_MODEL),
        out_shape=jax.ShapeDtypeStruct((t, D_MODEL), F32),
        compiler_params=_cparams(("arbitrary",)),
        name="out_proj",
    )(h, o_attn, o_conv, w_o, ln_g, ln_b)


def _route_kernel(h_ref, wr_ref, ltri_ref, ustr_ref, meta_ref, metat_ref, cnt_ref, wp_ref, *, tm):
    h = h_ref[...]
    h_hi = h.astype(BF16)
    h_lo = (h - h_hi.astype(F32)).astype(BF16)
    hw = _dot(h_hi, wr_ref[...])
    logits = hw[:, :LANES] + hw[:, LANES:] + _dot(h_lo, wr_ref[:, :LANES])
    lane = lax.broadcasted_iota(jnp.int32, logits.shape, 1)
    logits = jnp.where(lane < N_EXPERTS, logits, -jnp.inf)
    m1 = jnp.max(logits, axis=-1, keepdims=True)
    i1 = jnp.min(jnp.where(logits == m1, lane, LANES), axis=-1, keepdims=True)
    rest = jnp.where(lane == i1, -jnp.inf, logits)
    m2 = jnp.max(rest, axis=-1, keepdims=True)
    i2 = jnp.min(jnp.where(rest == m2, lane, LANES), axis=-1, keepdims=True)
    e2 = jnp.exp(m2 - m1)
    den = 1.0 + e2
    w1 = 1.0 / den
    w2 = e2 / den

    routed = (lane == i1) | (lane == i2)
    cnt = _dot(ltri_ref[...], routed.astype(BF16))
    n = cnt[tm - 1:tm, :]
    padded = jnp.floor((n + (CH - 1)) * (1.0 / CH)) * CH
    seg_off = _dot(jnp.broadcast_to(padded, (8, LANES)).astype(BF16), ustr_ref[...])[0:1]
    dest = seg_off + cnt - 1.0
    d1 = jnp.sum(jnp.where(lane == i1, dest, 0.0), axis=-1, keepdims=True)
    d2 = jnp.sum(jnp.where(lane == i2, dest, 0.0), axis=-1, keepdims=True)
    meta = jnp.where(lane == 0, d1, jnp.where(lane == 1, d2, jnp.where(lane == 2, w1, jnp.where(lane == 3, w2, 0.0))))
    meta_ref[...] = meta
    metat_ref[...] = meta.T[0:8, :]
    cnt_ref[0] = jnp.broadcast_to(n, (8, LANES))
    for k, w in enumerate((w1, w2)):
        hi = w.astype(BF16).astype(F32)
        mid = (w - hi).astype(BF16).astype(F32)
        lo = (w - hi - mid).astype(BF16).astype(F32)
        wp_ref[k] = jnp.where(lane == 0, hi, jnp.where(lane == 1, mid, jnp.where(lane == 2, lo, 0.0))).astype(BF16)


def _moe_route(h, w_router):
    t = h.shape[0]
    tm = TMX
    nt = t // tm
    row = lambda w: pl.BlockSpec((tm, w), lambda i: (i, 0))
    wr = jnp.pad(w_router, ((0, 0), (0, LANES - N_EXPERTS)))
    wr_hi = wr.astype(BF16)
    wr_lo = (wr - wr_hi.astype(F32)).astype(BF16)
    wr2 = jnp.concatenate([wr_hi, wr_lo], axis=1)
    ltri = jnp.asarray(np.tril(np.ones((tm, tm), np.float32)), dtype=BF16)
    ustr = jnp.asarray(np.triu(np.ones((LANES, LANES), np.float32), 1), dtype=BF16)
    return pl.pallas_call(
        functools.partial(_route_kernel, tm=tm),
        grid=(nt,),
        in_specs=[row(D_MODEL), _resident((D_MODEL, 2 * LANES)), _resident((tm, tm)), _resident((LANES, LANES))],
        out_specs=[row(LANES),
                   pl.BlockSpec((8, tm), lambda i: (0, i)),
                   pl.BlockSpec((1, 8, LANES), lambda i: (i, 0, 0)),
                   pl.BlockSpec((2, tm, LANES), lambda i: (0, i, 0))],
        out_shape=[jax.ShapeDtypeStruct((t, LANES), F32), jax.ShapeDtypeStruct((8, t), F32),
                   jax.ShapeDtypeStruct((nt, 8, LANES), F32), jax.ShapeDtypeStruct((2, t, LANES), BF16)],
        compiler_params=_cparams(("arbitrary",)),
        name="moe_route",
    )(h, wr2, ltri, ustr)


def _moe_plan(cnt, n_row_tiles):
    n = cnt[:, 0, :N_EXPERTS].astype(jnp.int32)
    p = (n + CH - 1) // CH * CH
    tot = p.sum(0)
    tot_pad = (tot + TMR - 1) // TMR * TMR
    gend = jnp.cumsum(tot_pad)
    gstart = gend - tot_pad
    goff = gstart[None, :] + jnp.cumsum(p, 0) - p
    loff = jnp.cumsum(p, 1) - p
    n_used = gend[-1:] // TMR
    tile_start = jnp.arange(n_row_tiles, dtype=jnp.int32) * TMR
    texp = jnp.minimum(jnp.sum(tile_start[:, None] >= gend[None, :], axis=1), N_EXPERTS - 1).astype(jnp.int32)
    texp = jnp.where(jnp.arange(n_row_tiles) < n_used[0], texp, texp[jnp.maximum(n_used[0] - 1, 0)])
    flat = lambda a: a.reshape(-1).astype(jnp.int32)
    gap_start = jnp.concatenate([gstart + tot, gend[-1:]])
    gap_rows = jnp.concatenate([tot_pad - tot, n_row_tiles * TMR - gend[-1:]])
    return dict(loff=flat(loff), goff=flat(goff), nch=flat(p // CH), gap_start=flat(gap_start),
                gap_nch=flat(gap_rows // CH), texp=flat(texp), n_used=flat(n_used))


def _chunk_copies(src_ref, dst_ref, sem, src_off, dst_off, n, src_step=CH):
    def body(c, carry):
        s = pl.multiple_of(src_off + c * src_step, CH)
        d = pl.multiple_of(dst_off + c * CH, CH)
        pltpu.make_async_copy(src_ref.at[pl.ds(s, CH)], dst_ref.at[pl.ds(d, CH)], sem).start()
        return carry
    lax.fori_loop(0, n, body, 0)


def _wait_chunks(src_ref, dst_ref, sem, n):
    def body(c, carry):
        pltpu.make_async_copy(src_ref.at[pl.ds(0, CH)], dst_ref.at[pl.ds(0, CH)], sem).wait()
        return carry
    lax.fori_loop(0, n, body, 0)


def _tile_chunks(tile, nch):
    total = 0
    for e in range(N_EXPERTS):
        total = total + nch[tile * N_EXPERTS + e]
    return total


def _start_segments(tile, nch, src_off, dst_off, src_ref, dst_ref, sem):
    for e in range(N_EXPERTS):
        idx = tile * N_EXPERTS + e
        _chunk_copies(src_ref, dst_ref, sem, src_off[idx], dst_off[idx], nch[idx])


def _dispatch_kernel(loff, goff, nch, gap_start, gap_nch, h_ref, metat_ref, wp_ref, xs_ref, xc_ref, z_ref, sem,
                     *, tm):
    i = pl.program_id(0)
    last = pl.num_programs(0) - 1
    slot = i % 2
    r = lax.broadcasted_iota(jnp.int32, (LROWS, tm), 0).astype(F32)
    pick1 = (r == metat_ref[0:1, :]).astype(BF16)
    pick2 = (r == metat_ref[1:2, :]).astype(BF16)
    xc_ref[slot, :, :D_MODEL] = _dot(pick1 + pick2, h_ref[...].astype(BF16)).astype(BF16)
    xc_ref[slot, :, D_MODEL:] = (_dot(pick1, wp_ref[0]) + _dot(pick2, wp_ref[1])).astype(BF16)
    _start_segments(i, nch, loff, goff, xc_ref.at[slot], xs_ref, sem.at[slot])

    @pl.when(i > 0)
    def _():
        _wait_chunks(xc_ref.at[1 - slot], xs_ref, sem.at[1 - slot], _tile_chunks(i - 1, nch))

    @pl.when(i == last)
    def _():
        z_ref[...] = jnp.zeros_like(z_ref)
        gaps = 0
        for e in range(N_EXPERTS + 1):
            _chunk_copies(z_ref, xs_ref, sem.at[2], 0, gap_start[e], gap_nch[e], src_step=0)
            gaps = gaps + gap_nch[e]
        _wait_chunks(z_ref, xs_ref, sem.at[2], gaps)
        _wait_chunks(xc_ref.at[slot], xs_ref, sem.at[slot], _tile_chunks(i, nch))


def _moe_dispatch(h, metat, wparts, plan, n_rows):
    t = h.shape[0]
    tm = TMX
    grid_spec = pltpu.PrefetchScalarGridSpec(
        num_scalar_prefetch=5,
        grid=(t // tm,),
        in_specs=[pl.BlockSpec((tm, D_MODEL), lambda i, *_: (i, 0)),
                  pl.BlockSpec((8, tm), lambda i, *_: (0, i)),
                  pl.BlockSpec((2, tm, LANES), lambda i, *_: (0, i, 0))],
        out_specs=pl.BlockSpec(memory_space=pl.ANY),
        scratch_shapes=[pltpu.VMEM((2, LROWS, XROW), BF16), pltpu.VMEM((CH, XROW), BF16),
                        pltpu.SemaphoreType.DMA((3,))],
    )
    return pl.pallas_call(
        functools.partial(_dispatch_kernel, tm=tm),
        grid_spec=grid_spec,
        out_shape=jax.ShapeDtypeStruct((n_rows, XROW), BF16),
        compiler_params=_cparams(("arbitrary",)),
        name="moe_dispatch",
    )(plan["loff"], plan["goff"], plan["nch"], plan["gap_start"], plan["gap_nch"], h, metat, wparts)


def _experts_kernel(texp, n_used, x_ref, wg_ref, wu_ref, wd_ref, y_ref):
    used = pl.program_id(0) < n_used[0]

    @pl.when(used)
    def _():
        x = x_ref[:, :D_MODEL]
        wp = x_ref[:, D_MODEL:].astype(F32)
        gate = wp[:, 0:1] + wp[:, 1:2] + wp[:, 2:3]
        main = D_FF_EXPERT - LANES
        w_mid = jnp.concatenate([wg_ref[0, :, main:], wu_ref[0, :, :LANES]], axis=1)
        g_main = _dot(x, wg_ref[0, :, :main])
        mid = _dot(x, w_mid)
        u_main = _dot(x, wu_ref[0, :, LANES:])
        a = jnp.concatenate([
            jax.nn.silu(g_main[:, :LANES]) * mid[:, LANES:],
            jax.nn.silu(g_main[:, LANES:]) * u_main[:, :main - LANES],
            jax.nn.silu(mid[:, :LANES]) * u_main[:, main - LANES:],
        ], axis=1).astype(BF16)
        y_ref[...] = (gate * _dot(a, wd_ref[0])).astype(BF16)

    @pl.when(jnp.logical_not(used))
    def _():
        y_ref[...] = jnp.zeros_like(y_ref)


def _moe_experts(xs, wg, wu, wd, plan):
    n_rows = xs.shape[0]
    rows = lambda r, texp, n_used: (jnp.minimum(r, n_used[0] - 1), 0)
    wspec = lambda shape: pl.BlockSpec((1,) + shape, lambda r, texp, n_used: (texp[r], 0, 0))
    grid_spec = pltpu.PrefetchScalarGridSpec(
        num_scalar_prefetch=2,
        grid=(n_rows // TMR,),
        in_specs=[pl.BlockSpec((TMR, XROW), rows), wspec((D_MODEL, D_FF_EXPERT)),
                  wspec((D_MODEL, D_FF_EXPERT)), wspec((D_FF_EXPERT, D_MODEL))],
        out_specs=pl.BlockSpec((TMR, D_MODEL), lambda r, texp, n_used: (r, 0)),
    )
    return pl.pallas_call(
        _experts_kernel,
        grid_spec=grid_spec,
        out_shape=jax.ShapeDtypeStruct((n_rows, D_MODEL), BF16),
        compiler_params=_cparams(("arbitrary",)),
        name="moe_experts",
    )(plan["texp"], plan["n_used"], xs, wg, wu, wd)


def _combine_kernel(loff, goff, nch, h_ref, meta_ref, ys_ref, g_ref, b_ref, o_ref, yb_ref, sem, *, tm, alpha):
    i = pl.program_id(0)
    slot = i % 2

    @pl.when(i == 0)
    def _():
        yb_ref[...] = jnp.zeros_like(yb_ref)
        _start_segments(0, nch, goff, loff, ys_ref, yb_ref.at[0], sem.at[0])

    @pl.when(i + 1 < pl.num_programs(0))
    def _():
        _start_segments(i + 1, nch, goff, loff, ys_ref, yb_ref.at[1 - slot], sem.at[1 - slot])

    meta = meta_ref[...]
    r = lax.broadcasted_iota(jnp.int32, (tm, LROWS), 1).astype(F32)
    pick = ((r == meta[:, 0:1]) | (r == meta[:, 1:2])).astype(BF16)
    _wait_chunks(ys_ref, yb_ref.at[slot], sem.at[slot], _tile_chunks(i, nch))
    f = _dot(pick, yb_ref[slot])
    o_ref[...] = _ln(alpha * h_ref[...] + f, g_ref[...], b_ref[...])


def _moe_combine(h, meta, ys, plan, ln_g, ln_b, *, alpha):
    t = h.shape[0]
    tm = TMX
    grid_spec = pltpu.PrefetchScalarGridSpec(
        num_scalar_prefetch=3,
        grid=(t // tm,),
        in_specs=[pl.BlockSpec((tm, D_MODEL), lambda i, *_: (i, 0)),
                  pl.BlockSpec((tm, LANES), lambda i, *_: (i, 0)),
                  pl.BlockSpec(memory_space=pl.ANY),
                  pl.BlockSpec((1, D_MODEL), lambda i, *_: (0, 0)),
                  pl.BlockSpec((1, D_MODEL), lambda i, *_: (0, 0))],
        out_specs=pl.BlockSpec((tm, D_MODEL), lambda i, *_: (i, 0)),
        scratch_shapes=[pltpu.VMEM((2, LROWS, D_MODEL), BF16), pltpu.SemaphoreType.DMA((2,))],
    )
    return pl.pallas_call(
        functools.partial(_combine_kernel, tm=tm, alpha=alpha),
        grid_spec=grid_spec,
        out_shape=jax.ShapeDtypeStruct((t, D_MODEL), F32),
        compiler_params=_cparams(("arbitrary",)),
        name="moe_combine",
    )(plan["loff"], plan["goff"], plan["nch"], h, meta, ys, ln_g, ln_b)


def _moe_tail(h, o_attn, o_conv, w_o, ln1, w_router, wg, wu, wd, ln_g, ln_b, *, alpha):
    t = h.shape[0]
    nt = t // TMX
    max_rows = 2 * t + nt * N_EXPERTS * (CH - 1) + N_EXPERTS * (TMR - CH)
    n_row_tiles = -(-max_rows // TMR)
    h = _out_proj(h, o_attn, o_conv, w_o, *ln1, alpha=alpha)
    meta, metat, cnt, wparts = _moe_route(h, w_router)
    plan = _moe_plan(cnt, n_row_tiles)
    xs = _moe_dispatch(h, metat, wparts, plan, n_row_tiles * TMR)
    ys = _moe_experts(xs, wg, wu, wd, plan)
    return _moe_combine(h, meta, ys, plan, ln_g, ln_b, alpha=alpha)


def kernel(x, ln_in_g, ln_in_b, w_in, cmp_pos, cmp_w1, cmp_b1, cmp_w2, cmp_b2, conv_w, w_o, ln1_g, ln1_b, ln2_g, ln2_b, ffn_wg, ffn_wu, ffn_wd, moe_router, moe_wg, moe_wu, moe_wd):
    batch, seq, d = x.shape
    depth = w_in.shape[0]
    assert d == D_MODEL and seq % TM_PROJ == 0 and seq // CMP_STRIDE == LANES
    alpha = (2 * depth) ** 0.25
    t = batch * seq
    vec = lambda v: v.reshape(1, D_MODEL)
    def proj_weights(l):
        return (*_prep_w_in(w_in[l]), jnp.pad(conv_w[l], ((0, 8 - CONV_W), (0, 0))))

    h, *proj = _in_proj(x.reshape(t, d), vec(ln_in_g), vec(ln_in_b), proj_weights(0), seq=seq, pre_ln=True)
    for l in range(depth):
        q, kvc, ks, vs, kw, vw, gates, o_conv = proj
        kvcmp, kvcmp_t = _compress(kvc, cmp_pos[l], cmp_w1[l], cmp_b1[l], cmp_w2[l], cmp_b2[l],
                                   batch=batch, seq=seq)
        o_attn = _nsa_attn(q, kvcmp, kvcmp_t, ks, vs, kw, vw, gates, batch=batch, seq=seq)
        ln1 = (vec(ln1_g[l]), vec(ln1_b[l]))
        ln2 = (vec(ln2_g[l]), vec(ln2_b[l]))
        next_w = proj_weights(l + 1) if l + 1 < depth else None
        if l % 2 == 0:
            ffn_w = (ffn_wg[l // 2].astype(BF16), ffn_wu[l // 2].astype(BF16), ffn_wd[l // 2].astype(BF16))
            h, *proj = _dense_tail(h, o_attn, o_conv, w_o[l].astype(BF16), ln1, ffn_w, ln2, next_w,
                                   alpha=alpha, seq=seq)
        else:
            h = _moe_tail(h, o_attn, o_conv, w_o[l].astype(BF16), ln1, moe_router[l // 2],
                          moe_wg[l // 2].astype(BF16), moe_wu[l // 2].astype(BF16), moe_wd[l // 2].astype(BF16),
                          *ln2, alpha=alpha)
            if next_w is not None:
                proj = _in_proj(h, None, None, next_w, seq=seq, pre_ln=False)
    return h.reshape(batch, seq, d)
```

```python
import functools

import numpy as np
import jax
import jax.numpy as jnp
from jax import lax
from jax.experimental import pallas as pl
from jax.experimental.pallas import tpu as pltpu

F32 = jnp.float32
BF16 = jnp.bfloat16

D_MODEL = 1024
HEAD_DIM = 64
N_HEADS = 8
N_KV = 2
HPG = N_HEADS // N_KV
ATTN_W = N_HEADS * HEAD_DIM
KV_W = N_KV * HEAD_DIM
CONV_CH = D_MODEL - ATTN_W
CONV_W = 3
CMP_BLOCK = 32
CMP_STRIDE = 16
CMP_HIDDEN = 256
SEL_BLOCK = 64
SEL_TOPN = 16
WINDOW = 512
D_FF = 2816
N_EXPERTS = 8
D_FF_EXPERT = 1408
LN_EPS = 1e-5
NEG = -1e30
FORCE = 1e9

LANES = 128
HPAD = LANES
VROWS = HPAD
Q_SCALE = HEAD_DIM ** -0.5 * np.log2(np.e)

C_Q = 0
C_KVC = C_Q + ATTN_W
C_KS = C_KVC + 2 * KV_W
C_KW = C_KS + KV_W
C_U = C_KW + KV_W
C_B = C_U + CONV_CH
C_C = C_B + CONV_CH
C_G = C_C + CONV_CH
C_END = C_G + N_KV * LANES

TM_PROJ = 512
TQ = 256
TK = 256
VMEM_LIMIT = 56 * 1024 * 1024


def _cparams(sem):
    return pltpu.CompilerParams(dimension_semantics=sem, vmem_limit_bytes=VMEM_LIMIT)


def _ln(x, g, b):
    mu = jnp.mean(x, -1, keepdims=True)
    xc = x - mu
    var = jnp.mean(xc * xc, -1, keepdims=True)
    return xc * lax.rsqrt(var + LN_EPS) * g + b


def _dot(a, b):
    return jnp.dot(a, b, preferred_element_type=F32)


def _dot_nt(a, b):
    return lax.dot_general(a, b, (((1,), (1,)), ((), ())), preferred_element_type=F32)


def _resident(shape):
    nd = len(shape)
    return pl.BlockSpec(shape, lambda *_: (0,) * nd, pipeline_mode=pl.Buffered(1))


def _spread_heads(z):
    low = lax.broadcasted_iota(jnp.int32, (z.shape[0], LANES), 1) < HEAD_DIM
    tiles = []
    for p in range(z.shape[1] // LANES):
        pair = z[:, p * LANES:(p + 1) * LANES]
        tiles += [jnp.where(low, pair, 0.0), jnp.where(low, pltpu.roll(pair, HEAD_DIM, 1), 0.0)]
    return jnp.concatenate(tiles, axis=1)


def _project_inputs(h, w_refs, out_refs, scratch_refs, *, tiles_per_seq, tm):
    w_ref, wvt_ref, cw_ref = w_refs
    q_ref, kvc_ref, ks_ref, vs_ref, kw_ref, vw_ref, gate_ref, oc_ref = out_refs
    carry_ref, kvc_scr = scratch_refs
    hb = h.astype(BF16)

    def proj(lo, hi):
        return _dot(hb, w_ref[:, lo:hi])

    q_ref[...] = _spread_heads(proj(C_Q, C_KVC) * Q_SCALE).astype(BF16)

    kvc = proj(C_KVC, C_KS)
    low = lax.broadcasted_iota(jnp.int32, (tm // CMP_STRIDE, LANES), 1) < HEAD_DIM
    for kind in range(2):
        kvc_scr[kind] = kvc[:, kind * LANES:(kind + 1) * LANES]
        for p in range(CMP_STRIDE // 2):
            a = kvc_scr[kind, pl.ds(2 * p, tm // CMP_STRIDE, stride=CMP_STRIDE), :]
            b = kvc_scr[kind, pl.ds(2 * p + 1, tm // CMP_STRIDE, stride=CMP_STRIDE), :]
            kvc_ref[2 * kind, :, p * LANES:(p + 1) * LANES] = jnp.where(low, a, pltpu.roll(b, HEAD_DIM, 1))
            kvc_ref[2 * kind + 1, :, p * LANES:(p + 1) * LANES] = jnp.where(low, pltpu.roll(a, HEAD_DIM, 1), b)

    seq_tile = pl.program_id(0) % tiles_per_seq
    pos = lax.broadcasted_iota(jnp.int32, (tm, N_KV * HPAD), 0) + seq_tile * tm
    lane = lax.broadcasted_iota(jnp.int32, (tm, N_KV * HPAD), 1) % HPAD
    onehot = jnp.where(pos // SEL_BLOCK == lane - HEAD_DIM, 1.0, 0.0)
    in_tag = (lane >= HEAD_DIM) & (lane < HEAD_DIM + 32)
    kk = proj(C_KS, C_U)
    ks_ref[...] = jnp.where(in_tag, onehot, _spread_heads(kk[:, :KV_W])).astype(BF16)
    kw_ref[...] = _spread_heads(kk[:, KV_W:]).astype(BF16)

    vt = _dot_nt(wvt_ref[...], hb)
    tail = jnp.where(lax.broadcasted_iota(jnp.int32, (VROWS - HEAD_DIM, tm), 0) == 0, 1.0, 0.0)
    for n, ref in enumerate((vs_ref, vw_ref)):
        rows = []
        for grp in range(N_KV):
            lo = (n * N_KV + grp) * HEAD_DIM
            rows += [vt[lo:lo + HEAD_DIM], tail]
        full = jnp.concatenate(rows, axis=0).astype(BF16)
        for c in range(tm // TK):
            ref[c] = full[:, c * TK:(c + 1) * TK]

    cu = proj(C_C, C_G) * proj(C_U, C_B)

    @pl.when(seq_tile == 0)
    def _():
        carry_ref[...] = jnp.zeros_like(carry_ref)

    prev = carry_ref[...]
    row = lax.broadcasted_iota(jnp.int32, (tm, CONV_CH), 0)
    s1 = jnp.where(row == 0, prev[7:8], pltpu.roll(cu, 1, 0))
    s2 = jnp.where(row == 0, prev[6:7], jnp.where(row == 1, prev[7:8], pltpu.roll(cu, 2, 0)))
    y = s2 * cw_ref[0:1, :] + s1 * cw_ref[1:2, :] + cu * cw_ref[2:3, :]
    oc_ref[...] = (proj(C_B, C_C) * y).astype(BF16)
    carry_ref[...] = cu[tm - 8:tm]
    gate_ref[...] = jax.nn.sigmoid(proj(C_G, C_END))


N_PROJ_W = 3
N_PROJ_OUT = 8
N_PROJ_SCRATCH = 2


def _proj_specs(t, tm):
    row = lambda w: pl.BlockSpec((tm, w), lambda i: (i, 0))
    w_specs = [_resident((D_MODEL, C_END)), _resident((2 * KV_W, D_MODEL)), _resident((8, CONV_CH))]
    vt_shape = jax.ShapeDtypeStruct((t // TK, N_KV * VROWS, TK), BF16)
    vt_spec = pl.BlockSpec((tm // TK, N_KV * VROWS, TK), lambda i: (i, 0, 0))
    out_shape = [
        jax.ShapeDtypeStruct((t, N_HEADS * HPAD), BF16),
        jax.ShapeDtypeStruct((4, t // CMP_STRIDE, CMP_STRIDE * HEAD_DIM), F32),
        jax.ShapeDtypeStruct((t, N_KV * HPAD), BF16), vt_shape,
        jax.ShapeDtypeStruct((t, N_KV * HPAD), BF16), vt_shape,
        jax.ShapeDtypeStruct((t, N_KV * LANES), F32),
        jax.ShapeDtypeStruct((t, CONV_CH), BF16),
    ]
    out_specs = [
        row(N_HEADS * HPAD),
        pl.BlockSpec((4, tm // CMP_STRIDE, CMP_STRIDE * HEAD_DIM), lambda i: (0, i, 0)),
        row(N_KV * HPAD), vt_spec, row(N_KV * HPAD), vt_spec,
        row(N_KV * LANES), row(CONV_CH),
    ]
    scratch = [pltpu.VMEM((8, CONV_CH), F32), pltpu.VMEM((2, tm, LANES), F32)]
    return w_specs, out_shape, out_specs, scratch


def _in_proj_kernel(*refs, pre_ln, tiles_per_seq, tm):
    n_in = 3 if pre_ln else 1
    x_ref = refs[0]
    w_refs = refs[n_in:n_in + N_PROJ_W]
    outs = refs[n_in + N_PROJ_W:len(refs) - N_PROJ_SCRATCH]
    if pre_ln:
        h = _ln(x_ref[...], refs[1][...], refs[2][...])
        outs[0][...] = h
        outs = outs[1:]
    else:
        h = x_ref[...]
    _project_inputs(h, w_refs, outs, refs[len(refs) - N_PROJ_SCRATCH:], tiles_per_seq=tiles_per_seq, tm=tm)


def _in_proj(x, ln_g, ln_b, proj_w, *, seq, pre_ln):
    t = x.shape[0]
    tm = TM_PROJ
    row = pl.BlockSpec((tm, D_MODEL), lambda i: (i, 0))
    w_specs, out_shape, out_specs, scratch = _proj_specs(t, tm)
    in_specs, args = [row], [x]
    if pre_ln:
        in_specs += [_resident((1, D_MODEL)), _resident((1, D_MODEL))]
        args += [ln_g, ln_b]
        out_shape = [jax.ShapeDtypeStruct((t, D_MODEL), F32)] + out_shape
        out_specs = [row] + out_specs
    return pl.pallas_call(
        functools.partial(_in_proj_kernel, pre_ln=pre_ln, tiles_per_seq=seq // tm, tm=tm),
        grid=(t // tm,),
        in_specs=in_specs + w_specs,
        out_specs=out_specs,
        out_shape=out_shape,
        scratch_shapes=scratch,
        compiler_params=_cparams(("arbitrary",)),
        name="in_proj_ln" if pre_ln else "in_proj",
    )(*args, *proj_w)


def _prep_w_in(w_in):
    o = 0

    def take(n):
        nonlocal o
        s = w_in[:, o:o + n]
        o += n
        return s

    q, kc, vc, ks, vs, kw, vw = take(ATTN_W), *(take(KV_W) for _ in range(6))
    g_all = take(3 * N_HEADS)
    pad = jnp.zeros((w_in.shape[0], LANES - 3 * HPG), w_in.dtype)
    gates = []
    for grp in range(N_KV):
        gates += [g_all[:, c * N_HEADS + grp * HPG:c * N_HEADS + (grp + 1) * HPG] for c in range(3)] + [pad]
    conv = take(3 * CONV_CH)
    w_r = jnp.concatenate([q, kc, vc, ks, kw, conv, *gates], axis=1).astype(BF16)
    return w_r, jnp.concatenate([vs, vw], axis=1).T.astype(BF16)


def _compress_kernel(x_ref, pos_ref, w1_ref, b1_ref, w2_ref, b2_ref, w2t_ref, b2t_ref, o_ref, ot_ref):
    half = CMP_STRIDE * HEAD_DIM
    nb = x_ref.shape[2]
    x = jnp.concatenate([x_ref[g, 0] for g in range(N_KV)], axis=0)
    pos = pos_ref[0]
    xa = (x + pos[:, :half]).astype(BF16)
    xb = (x + pos[:, half:]).astype(BF16)
    a = _dot(xa, w1_ref[0, :half, :])
    b = _dot(xb, w1_ref[0, half:, :])
    hid = a + pltpu.roll(b, N_KV * nb - 1, 0) + b1_ref[0]
    act = jax.nn.gelu(hid).astype(BF16)
    out = (_dot(act, w2_ref[0]) + b2_ref[0]).astype(BF16)
    out_t = (_dot_nt(w2t_ref[0], act) + b2t_ref[0]).astype(BF16)
    for g in range(N_KV):
        o_ref[g, 0] = out[g * nb:(g + 1) * nb]
        ot_ref[g, 0] = out_t[:, g * nb:(g + 1) * nb]


def _compress(kvc, cmp_pos, cmp_w1, cmp_b1, cmp_w2, cmp_b2, *, batch, seq):
    nb = seq // CMP_STRIDE
    half = CMP_STRIDE * HEAD_DIM
    x = kvc.reshape(4, batch, nb, half)
    pos = cmp_pos.reshape(2, 1, CMP_BLOCK * HEAD_DIM)
    w1 = cmp_w1.astype(BF16)
    b1 = cmp_b1.reshape(2, 1, CMP_HIDDEN)
    w2 = jnp.pad(cmp_w2, ((0, 0), (0, 0), (0, HPAD - HEAD_DIM))).astype(BF16)
    b2 = jnp.pad(cmp_b2, ((0, 0), (0, HPAD - HEAD_DIM))).reshape(2, 1, HPAD)
    w2t = jnp.swapaxes(w2, 1, 2)
    b2t = b2.reshape(2, HPAD, 1)
    kind = lambda j, b: (j, 0, 0)
    return pl.pallas_call(
        _compress_kernel,
        grid=(2, batch),
        in_specs=[
            pl.BlockSpec((N_KV, 1, nb, half), lambda j, b: (j, b, 0, 0)),
            pl.BlockSpec((1, 1, 2 * half), kind),
            pl.BlockSpec((1, 2 * half, CMP_HIDDEN), kind),
            pl.BlockSpec((1, 1, CMP_HIDDEN), kind),
            pl.BlockSpec((1, CMP_HIDDEN, HPAD), kind),
            pl.BlockSpec((1, 1, HPAD), kind),
            pl.BlockSpec((1, HPAD, CMP_HIDDEN), kind),
            pl.BlockSpec((1, HPAD, 1), kind),
        ],
        out_specs=[pl.BlockSpec((N_KV, 1, nb, HPAD), lambda j, b: (j, b, 0, 0)),
                   pl.BlockSpec((N_KV, 1, HPAD, nb), lambda j, b: (j, b, 0, 0))],
        out_shape=[jax.ShapeDtypeStruct((4, batch, nb, HPAD), BF16),
                   jax.ShapeDtypeStruct((4, batch, HPAD, nb), BF16)],
        compiler_params=_cparams(("arbitrary", "arbitrary")),
        name="compress",
    )(x, pos, w1, b1, w2, b2, w2t, b2t)


def _overlap_t():
    n = np.arange(LANES)
    j = np.arange(32)
    cs = n * CMP_STRIDE
    ss = j * SEL_BLOCK
    ov = (cs[None, :] < ss[:, None] + SEL_BLOCK) & (cs[None, :] + CMP_BLOCK > ss[:, None])
    return jnp.asarray(ov, dtype=BF16)


def _cmp_select(qh, kc, vct, ov, i, tq, n_sel):
    n = lax.broadcasted_iota(jnp.int32, (LANES, tq), 0)
    t = lax.broadcasted_iota(jnp.int32, (LANES, tq), 1) + i * tq
    mask = n * CMP_STRIDE + (CMP_BLOCK - 1) <= t
    maskf = mask.astype(F32)
    ss = [jnp.where(mask, _dot_nt(kc, q), NEG) for q in qh]
    es = [jnp.exp2(s - jnp.max(s, axis=0, keepdims=True)) for s in ss]
    ps = [e / jnp.sum(e, axis=0, keepdims=True) * maskf for e in es]
    heads_t = [_dot(vct, p.astype(BF16))[:HEAD_DIM] for p in ps]
    psum = functools.reduce(lambda a, b: a + b, ps)

    p_hi = psum.astype(BF16)
    r1 = psum - p_hi.astype(F32)
    p_mid = r1.astype(BF16)
    p_lo = (r1 - p_mid.astype(F32)).astype(BF16)
    imp = _dot(ov, p_hi) + _dot(ov, p_mid) + _dot(ov, p_lo)

    jt = lax.broadcasted_iota(jnp.int32, (n_sel, tq), 0)
    tt = lax.broadcasted_iota(jnp.int32, (n_sel, tq), 1) + i * tq
    cur = tt // SEL_BLOCK
    valid = jt * SEL_BLOCK <= tt
    forced = (jt == 0) | (jt == cur) | (jt == cur - 1)
    score = jnp.where(forced, FORCE, jnp.where(valid, imp, NEG))
    rank = jnp.zeros((n_sel, tq), jnp.int32)
    for k in range(n_sel):
        sk = score[k:k + 1, :]
        ahead = (sk > score) | ((sk == score) & (jt > k))
        rank = rank + ahead.astype(jnp.int32)
    bias = jnp.where(rank < SEL_TOPN, 0.0, NEG)
    full = jnp.concatenate(
        [jnp.zeros((HEAD_DIM, tq), F32), bias, jnp.zeros((HPAD - HEAD_DIM - n_sel, tq), F32)], axis=0)
    return heads_t, full.T.astype(BF16)


def _attn_scores(k, qa):
    return tuple(_dot_nt(k, qa[hh]) for hh in range(HPG))


def _attn_update(ss, vt, state, mask):
    if mask is not None:
        ss = [jnp.where(mask, s, NEG) for s in ss]
    m_new = [jnp.maximum(state[hh][0], jnp.max(ss[hh], axis=0, keepdims=True)) for hh in range(HPG)]
    ps = [jnp.exp2(ss[hh] - m_new[hh]).astype(BF16) for hh in range(HPG)]
    pv = [_dot(vt, ps[hh]) for hh in range(HPG)]
    return tuple((m_new[hh], jnp.exp2(state[hh][0] - m_new[hh]) * state[hh][1] + pv[hh]) for hh in range(HPG))


def _attn_init(tq):
    return tuple((jnp.full((1, tq), NEG, F32), jnp.zeros((VROWS, tq), F32)) for _ in range(HPG))


def _nsa_attn_kernel(q_ref, kc_ref, vct_ref, ov_ref, ks_ref, vst_ref, kw_ref, vwt_ref, gate_ref, o_ref,
                     *, tq, tk, n_sel):
    i = pl.program_id(2)
    nq = ks_ref.shape[0] // tk
    qh = [q_ref[:, hh * HPAD:(hh + 1) * HPAD] for hh in range(HPG)]
    heads_cmp, selb = _cmp_select(qh, kc_ref[0, 0], vct_ref[0, 0], ov_ref[...], i, tq, n_sel)
    lane = lax.broadcasted_iota(jnp.int32, selb.shape, 1)
    qsel = [jnp.where(lane >= HEAD_DIM, selb, q) for q in qh]
    key = lax.broadcasted_iota(jnp.int32, (tk, tq), 0)
    qry = lax.broadcasted_iota(jnp.int32, (tk, tq), 1)
    causal = key <= qry
    gt = gate_ref[...].T

    key2 = lax.broadcasted_iota(jnp.int32, (2 * tk, tq), 0)
    pair_causal = key2 - tk <= lax.broadcasted_iota(jnp.int32, (2 * tk, tq), 1)

    def run(n):
        chunks = []
        for j in range(0, n, 2):
            chunks.append(("sel", j, 2, pair_causal if j + 1 == n else None))
        if n % 2 == 0:
            chunks.append(("sel", n, 1, causal))
        if n == 0:
            chunks.append(("win", 0, 1, causal))
        else:
            chunks.append(("win", n - 1, 2, pair_causal))
        if n >= 2:
            chunks.append(("win", n - 2, 1, key > qry))
        refs = {"sel": (ks_ref, vst_ref, qsel), "win": (kw_ref, vwt_ref, qh)}

        def scores(chunk):
            branch, j, nt, _ = chunk
            return _attn_scores(refs[branch][0][j * tk:(j + nt) * tk, :], refs[branch][2])

        state = {"sel": _attn_init(tq), "win": _attn_init(tq)}
        ss = scores(chunks[0])
        for idx, (branch, j, nt, mask) in enumerate(chunks):
            ss_next = scores(chunks[idx + 1]) if idx + 1 < len(chunks) else None
            vt_ref = refs[branch][1]
            vt = vt_ref[j] if nt == 1 else jnp.concatenate([vt_ref[j], vt_ref[j + 1]], axis=1)
            state[branch] = _attn_update(ss, vt, state[branch], mask)
            ss = ss_next

        outs = []
        for hh in range(HPG):
            o_sel, o_win = (acc[:HEAD_DIM] / acc[HEAD_DIM:HEAD_DIM + 1]
                            for _, acc in (state["sel"][hh], state["win"][hh]))
            g_cmp, g_sel, g_win = (gt[c * HPG + hh:c * HPG + hh + 1] for c in range(3))
            outs.append(g_cmp * heads_cmp[hh] + g_sel * o_sel + g_win * o_win)
        o_ref[...] = jnp.concatenate(outs, axis=0).T.astype(o_ref.dtype)

    for n in range(nq):
        pl.when(i == n)(functools.partial(run, n))


def _nsa_attn(q, kvcmp, kvcmp_t, ks, vs_t, kw, vw_t, gates, *, batch, seq):
    t = q.shape[0]
    tq, tk = TQ, TK
    assert tq == tk and WINDOW == 2 * tk
    nq = seq // tq
    n_sel = seq // SEL_BLOCK
    nb = kvcmp.shape[2]
    assert nb == LANES and n_sel == 32
    rowblk = lambda w: pl.BlockSpec((tq, w), lambda b, g, i: (b * nq + i, g))
    seqblk = pl.BlockSpec((seq, HPAD), lambda b, g, i: (b, g))
    vtblk = pl.BlockSpec((seq // tk, VROWS, tk), lambda b, g, i: (b, g, 0))
    return pl.pallas_call(
        functools.partial(_nsa_attn_kernel, tq=tq, tk=tk, n_sel=n_sel),
        grid=(batch, N_KV, nq),
        in_specs=[rowblk(HPG * HPAD),
                  pl.BlockSpec((1, 1, nb, HPAD), lambda b, g, i: (g, b, 0, 0)),
                  pl.BlockSpec((1, 1, HPAD, nb), lambda b, g, i: (N_KV + g, b, 0, 0)),
                  pl.BlockSpec((n_sel, LANES), lambda b, g, i: (0, 0)),
                  seqblk, vtblk, seqblk, vtblk,
                  rowblk(LANES)],
        out_specs=rowblk(HPG * HEAD_DIM),
        out_shape=jax.ShapeDtypeStruct((t, ATTN_W), BF16),
        compiler_params=_cparams(("arbitrary", "arbitrary", "arbitrary")),
        name="nsa_attn",
    )(q, kvcmp, kvcmp_t, _overlap_t(), ks, vs_t, kw, vw_t, gates)


def _dense_tail_kernel(*refs, alpha, with_proj, tiles_per_seq, tm):
    (h_ref, oa_ref, ocv_ref, wo_ref, g1_ref, b1_ref, wg_ref, wu_ref, wd_ref, g2_ref, b2_ref) = refs[:11]
    m = _dot(oa_ref[...], wo_ref[:ATTN_W, :]) + _dot(ocv_ref[...], wo_ref[ATTN_W:, :])
    h1 = _ln(alpha * h_ref[...] + m, g1_ref[...], b1_ref[...])
    hb = h1.astype(BF16)
    a = (jax.nn.silu(_dot(hb, wg_ref[...])) * _dot(hb, wu_ref[...])).astype(BF16)
    h2 = _ln(alpha * h1 + _dot(a, wd_ref[...]), g2_ref[...], b2_ref[...])
    if with_proj:
        w_refs = refs[11:11 + N_PROJ_W]
        outs = refs[11 + N_PROJ_W:len(refs) - N_PROJ_SCRATCH]
        outs[0][...] = h2
        _project_inputs(h2, w_refs, outs[1:], refs[len(refs) - N_PROJ_SCRATCH:], tiles_per_seq=tiles_per_seq, tm=tm)
    else:
        refs[11][...] = h2


def _dense_tail(h, o_attn, o_conv, w_o, ln1, ffn_w, ln2, proj_w, *, alpha, seq):
    t = h.shape[0]
    tm = TM_PROJ
    row = lambda w: pl.BlockSpec((tm, w), lambda i: (i, 0))
    vec = _resident((1, D_MODEL))
    in_specs = [row(D_MODEL), row(ATTN_W), row(CONV_CH), _resident((D_MODEL, D_MODEL)), vec, vec,
                _resident((D_MODEL, D_FF)), _resident((D_MODEL, D_FF)), _resident((D_FF, D_MODEL)), vec, vec]
    args = [h, o_attn, o_conv, w_o, *ln1, *ffn_w, *ln2]
    out_shape, out_specs, scratch = [jax.ShapeDtypeStruct((t, D_MODEL), F32)], [row(D_MODEL)], []
    if proj_w is not None:
        w_specs, p_shape, p_specs, scratch = _proj_specs(t, tm)
        in_specs += w_specs
        args += list(proj_w)
        out_shape += p_shape
        out_specs += p_specs
    return pl.pallas_call(
        functools.partial(_dense_tail_kernel, alpha=alpha, with_proj=proj_w is not None,
                          tiles_per_seq=seq // tm, tm=tm),
        grid=(t // tm,),
        in_specs=in_specs,
        out_specs=out_specs,
        out_shape=out_shape,
        scratch_shapes=scratch,
        compiler_params=_cparams(("arbitrary",)),
        name="dense_tail",
    )(*args)


TMX = 512
CH = 16
LROWS = 2 * TMX + N_EXPERTS * CH
TMR = 512
XROW = D_MODEL + LANES


def _out_proj_kernel(h_ref, oa_ref, ocv_ref, w_ref, g_ref, b_ref, o_ref, *, alpha):
    m = _dot(oa_ref[...], w_ref[:ATTN_W, :]) + _dot(ocv_ref[...], w_ref[ATTN_W:, :])
    o_ref[...] = _ln(alpha * h_ref[...] + m, g_ref[...], b_ref[...])


def _out_proj(h, o_attn, o_conv, w_o, ln_g, ln_b, *, alpha):
    t = h.shape[0]
    tm = 2 * TM_PROJ
    row = lambda w: pl.BlockSpec((tm, w), lambda i: (i, 0))
    return pl.pallas_call(
        functools.partial(_out_proj_kernel, alpha=alpha),
        grid=(t // tm,),
        in_specs=[row(D_MODEL), row(ATTN_W), row(CONV_CH),
                  _resident((D_MODEL, D_MODEL)), _resident((1, D_MODEL)), _resident((1, D_MODEL))],
        out_specs=row(D_MODEL),
        out_shape=jax.ShapeDtypeStruct((t, D_MODEL), F32),
        compiler_params=_cparams(("arbitrary",)),
        name="out_proj",
    )(h, o_attn, o_conv, w_o, ln_g, ln_b)


def _route_kernel(h_ref, wr_ref, ltri_ref, ustr_ref, meta_ref, metat_ref, cnt_ref, wp_ref, *, tm):
    h = h_ref[...]
    h_hi = h.astype(BF16)
    h_lo = (h - h_hi.astype(F32)).astype(BF16)
    hw = _dot(h_hi, wr_ref[...])
    logits = hw[:, :LANES] + hw[:, LANES:] + _dot(h_lo, wr_ref[:, :LANES])
    lane = lax.broadcasted_iota(jnp.int32, logits.shape, 1)
    logits = jnp.where(lane < N_EXPERTS, logits, -jnp.inf)
    m1 = jnp.max(logits, axis=-1, keepdims=True)
    i1 = jnp.min(jnp.where(logits == m1, lane, LANES), axis=-1, keepdims=True)
    rest = jnp.where(lane == i1, -jnp.inf, logits)
    m2 = jnp.max(rest, axis=-1, keepdims=True)
    i2 = jnp.min(jnp.where(rest == m2, lane, LANES), axis=-1, keepdims=True)
    e2 = jnp.exp(m2 - m1)
    den = 1.0 + e2
    w1 = 1.0 / den
    w2 = e2 / den

    routed = (lane == i1) | (lane == i2)
    cnt = _dot(ltri_ref[...], routed.astype(BF16))
    n = cnt[tm - 1:tm, :]
    padded = jnp.floor((n + (CH - 1)) * (1.0 / CH)) * CH
    seg_off = _dot(jnp.broadcast_to(padded, (8, LANES)).astype(BF16), ustr_ref[...])[0:1]
    dest = seg_off + cnt - 1.0
    d1 = jnp.sum(jnp.where(lane == i1, dest, 0.0), axis=-1, keepdims=True)
    d2 = jnp.sum(jnp.where(lane == i2, dest, 0.0), axis=-1, keepdims=True)
    meta = jnp.where(lane == 0, d1, jnp.where(lane == 1, d2, jnp.where(lane == 2, w1, jnp.where(lane == 3, w2, 0.0))))
    meta_ref[...] = meta
    metat_ref[...] = meta.T[0:8, :]
    cnt_ref[0] = jnp.broadcast_to(n, (8, LANES))
    for k, w in enumerate((w1, w2)):
        hi = w.astype(BF16).astype(F32)
        mid = (w - hi).astype(BF16).astype(F32)
        lo = (w - hi - mid).astype(BF16).astype(F32)
        wp_ref[k] = jnp.where(lane == 0, hi, jnp.where(lane == 1, mid, jnp.where(lane == 2, lo, 0.0))).astype(BF16)


def _moe_route(h, w_router):
    t = h.shape[0]
    tm = TMX
    nt = t // tm
    row = lambda w: pl.BlockSpec((tm, w), lambda i: (i, 0))
    wr = jnp.pad(w_router, ((0, 0), (0, LANES - N_EXPERTS)))
    wr_hi = wr.astype(BF16)
    wr_lo = (wr - wr_hi.astype(F32)).astype(BF16)
    wr2 = jnp.concatenate([wr_hi, wr_lo], axis=1)
    ltri = jnp.asarray(np.tril(np.ones((tm, tm), np.float32)), dtype=BF16)
    ustr = jnp.asarray(np.triu(np.ones((LANES, LANES), np.float32), 1), dtype=BF16)
    return pl.pallas_call(
        functools.partial(_route_kernel, tm=tm),
        grid=(nt,),
        in_specs=[row(D_MODEL), _resident((D_MODEL, 2 * LANES)), _resident((tm, tm)), _resident((LANES, LANES))],
        out_specs=[row(LANES),
                   pl.BlockSpec((8, tm), lambda i: (0, i)),
                   pl.BlockSpec((1, 8, LANES), lambda i: (i, 0, 0)),
                   pl.BlockSpec((2, tm, LANES), lambda i: (0, i, 0))],
        out_shape=[jax.ShapeDtypeStruct((t, LANES), F32), jax.ShapeDtypeStruct((8, t), F32),
                   jax.ShapeDtypeStruct((nt, 8, LANES), F32), jax.ShapeDtypeStruct((2, t, LANES), BF16)],
        compiler_params=_cparams(("arbitrary",)),
        name="moe_route",
    )(h, wr2, ltri, ustr)


def _moe_plan(cnt, n_row_tiles):
    n = cnt[:, 0, :N_EXPERTS].astype(jnp.int32)
    p = (n + CH - 1) // CH * CH
    tot = p.sum(0)
    tot_pad = (tot + TMR - 1) // TMR * TMR
    gend = jnp.cumsum(tot_pad)
    gstart = gend - tot_pad
    goff = gstart[None, :] + jnp.cumsum(p, 0) - p
    loff = jnp.cumsum(p, 1) - p
    n_used = gend[-1:] // TMR
    tile_start = jnp.arange(n_row_tiles, dtype=jnp.int32) * TMR
    texp = jnp.minimum(jnp.sum(tile_start[:, None] >= gend[None, :], axis=1), N_EXPERTS - 1).astype(jnp.int32)
    texp = jnp.where(jnp.arange(n_row_tiles) < n_used[0], texp, texp[jnp.maximum(n_used[0] - 1, 0)])
    flat = lambda a: a.reshape(-1).astype(jnp.int32)
    gap_start = jnp.concatenate([gstart + tot, gend[-1:]])
    gap_rows = jnp.concatenate([tot_pad - tot, n_row_tiles * TMR - gend[-1:]])
    return dict(loff=flat(loff), goff=flat(goff), nch=flat(p // CH), gap_start=flat(gap_start),
                gap_nch=flat(gap_rows // CH), texp=flat(texp), n_used=flat(n_used))


def _chunk_copies(src_ref, dst_ref, sem, src_off, dst_off, n, src_step=CH):
    def body(c, carry):
        s = pl.multiple_of(src_off + c * src_step, CH)
        d = pl.multiple_of(dst_off + c * CH, CH)
        pltpu.make_async_copy(src_ref.at[pl.ds(s, CH)], dst_ref.at[pl.ds(d, CH)], sem).start()
        return carry
    lax.fori_loop(0, n, body, 0)


def _wait_chunks(src_ref, dst_ref, sem, n):
    def body(c, carry):
        pltpu.make_async_copy(src_ref.at[pl.ds(0, CH)], dst_ref.at[pl.ds(0, CH)], sem).wait()
        return carry
    lax.fori_loop(0, n, body, 0)


def _tile_chunks(tile, nch):
    total = 0
    for e in range(N_EXPERTS):
        total = total + nch[tile * N_EXPERTS + e]
    return total


def _start_segments(tile, nch, src_off, dst_off, src_ref, dst_ref, sem):
    for e in range(N_EXPERTS):
        idx = tile * N_EXPERTS + e
        _chunk_copies(src_ref, dst_ref, sem, src_off[idx], dst_off[idx], nch[idx])


def _dispatch_kernel(loff, goff, nch, gap_start, gap_nch, h_ref, metat_ref, wp_ref, xs_ref, xc_ref, z_ref, sem,
                     *, tm):
    i = pl.program_id(0)
    last = pl.num_programs(0) - 1
    slot = i % 2
    r = lax.broadcasted_iota(jnp.int32, (LROWS, tm), 0).astype(F32)
    pick1 = (r == metat_ref[0:1, :]).astype(BF16)
    pick2 = (r == metat_ref[1:2, :]).astype(BF16)
    xc_ref[slot, :, :D_MODEL] = _dot(pick1 + pick2, h_ref[...].astype(BF16)).astype(BF16)
    xc_ref[slot, :, D_MODEL:] = (_dot(pick1, wp_ref[0]) + _dot(pick2, wp_ref[1])).astype(BF16)
    _start_segments(i, nch, loff, goff, xc_ref.at[slot], xs_ref, sem.at[slot])

    @pl.when(i > 0)
    def _():
        _wait_chunks(xc_ref.at[1 - slot], xs_ref, sem.at[1 - slot], _tile_chunks(i - 1, nch))

    @pl.when(i == last)
    def _():
        z_ref[...] = jnp.zeros_like(z_ref)
        gaps = 0
        for e in range(N_EXPERTS + 1):
            _chunk_copies(z_ref, xs_ref, sem.at[2], 0, gap_start[e], gap_nch[e], src_step=0)
            gaps = gaps + gap_nch[e]
        _wait_chunks(z_ref, xs_ref, sem.at[2], gaps)
        _wait_chunks(xc_ref.at[slot], xs_ref, sem.at[slot], _tile_chunks(i, nch))


def _moe_dispatch(h, metat, wparts, plan, n_rows):
    t = h.shape[0]
    tm = TMX
    grid_spec = pltpu.PrefetchScalarGridSpec(
        num_scalar_prefetch=5,
        grid=(t // tm,),
        in_specs=[pl.BlockSpec((tm, D_MODEL), lambda i, *_: (i, 0)),
                  pl.BlockSpec((8, tm), lambda i, *_: (0, i)),
                  pl.BlockSpec((2, tm, LANES), lambda i, *_: (0, i, 0))],
        out_specs=pl.BlockSpec(memory_space=pl.ANY),
        scratch_shapes=[pltpu.VMEM((2, LROWS, XROW), BF16), pltpu.VMEM((CH, XROW), BF16),
                        pltpu.SemaphoreType.DMA((3,))],
    )
    return pl.pallas_call(
        functools.partial(_dispatch_kernel, tm=tm),
        grid_spec=grid_spec,
        out_shape=jax.ShapeDtypeStruct((n_rows, XROW), BF16),
        compiler_params=_cparams(("arbitrary",)),
        name="moe_dispatch",
    )(plan["loff"], plan["goff"], plan["nch"], plan["gap_start"], plan["gap_nch"], h, metat, wparts)


def _experts_kernel(texp, n_used, x_ref, wg_ref, wu_ref, wd_ref, y_ref):
    used = pl.program_id(0) < n_used[0]

    @pl.when(used)
    def _():
        x = x_ref[:, :D_MODEL]
        wp = x_ref[:, D_MODEL:].astype(F32)
        gate = wp[:, 0:1] + wp[:, 1:2] + wp[:, 2:3]
        main = D_FF_EXPERT - LANES
        w_mid = jnp.concatenate([wg_ref[0, :, main:], wu_ref[0, :, :LANES]], axis=1)
        g_main = _dot(x, wg_ref[0, :, :main])
        mid = _dot(x, w_mid)
        u_main = _dot(x, wu_ref[0, :, LANES:])
        a = jnp.concatenate([
            jax.nn.silu(g_main[:, :LANES]) * mid[:, LANES:],
            jax.nn.silu(g_main[:, LANES:]) * u_main[:, :main - LANES],
            jax.nn.silu(mid[:, :LANES]) * u_main[:, main - LANES:],
        ], axis=1).astype(BF16)
        y_ref[...] = (gate * _dot(a, wd_ref[0])).astype(BF16)

    @pl.when(jnp.logical_not(used))
    def _():
        y_ref[...] = jnp.zeros_like(y_ref)


def _moe_experts(xs, wg, wu, wd, plan):
    n_rows = xs.shape[0]
    rows = lambda r, texp, n_used: (jnp.minimum(r, n_used[0] - 1), 0)
    wspec = lambda shape: pl.BlockSpec((1,) + shape, lambda r, texp, n_used: (texp[r], 0, 0))
    grid_spec = pltpu.PrefetchScalarGridSpec(
        num_scalar_prefetch=2,
        grid=(n_rows // TMR,),
        in_specs=[pl.BlockSpec((TMR, XROW), rows), wspec((D_MODEL, D_FF_EXPERT)),
                  wspec((D_MODEL, D_FF_EXPERT)), wspec((D_FF_EXPERT, D_MODEL))],
        out_specs=pl.BlockSpec((TMR, D_MODEL), lambda r, texp, n_used: (r, 0)),
    )
    return pl.pallas_call(
        _experts_kernel,
        grid_spec=grid_spec,
        out_shape=jax.ShapeDtypeStruct((n_rows, D_MODEL), BF16),
        compiler_params=_cparams(("arbitrary",)),
        name="moe_experts",
    )(plan["texp"], plan["n_used"], xs, wg, wu, wd)


def _combine_kernel(loff, goff, nch, h_ref, meta_ref, ys_ref, g_ref, b_ref, o_ref, yb_ref, sem, *, tm, alpha):
    i = pl.program_id(0)
    slot = i % 2

    @pl.when(i == 0)
    def _():
        yb_ref[...] = jnp.zeros_like(yb_ref)
        _start_segments(0, nch, goff, loff, ys_ref, yb_ref.at[0], sem.at[0])

    @pl.when(i + 1 < pl.num_programs(0))
    def _():
        _start_segments(i + 1, nch, goff, loff, ys_ref, yb_ref.at[1 - slot], sem.at[1 - slot])

    meta = meta_ref[...]
    r = lax.broadcasted_iota(jnp.int32, (tm, LROWS), 1).astype(F32)
    pick = ((r == meta[:, 0:1]) | (r == meta[:, 1:2])).astype(BF16)
    _wait_chunks(ys_ref, yb_ref.at[slot], sem.at[slot], _tile_chunks(i, nch))
    f = _dot(pick, yb_ref[slot])
    o_ref[...] = _ln(alpha * h_ref[...] + f, g_ref[...], b_ref[...])


def _moe_combine(h, meta, ys, plan, ln_g, ln_b, *, alpha):
    t = h.shape[0]
    tm = TMX
    grid_spec = pltpu.PrefetchScalarGridSpec(
        num_scalar_prefetch=3,
        grid=(t // tm,),
        in_specs=[pl.BlockSpec((tm, D_MODEL), lambda i, *_: (i, 0)),
                  pl.BlockSpec((tm, LANES), lambda i, *_: (i, 0)),
                  pl.BlockSpec(memory_space=pl.ANY),
                  pl.BlockSpec((1, D_MODEL), lambda i, *_: (0, 0)),
                  pl.BlockSpec((1, D_MODEL), lambda i, *_: (0, 0))],
        out_specs=pl.BlockSpec((tm, D_MODEL), lambda i, *_: (i, 0)),
        scratch_shapes=[pltpu.VMEM((2, LROWS, D_MODEL), BF16), pltpu.SemaphoreType.DMA((2,))],
    )
    return pl.pallas_call(
        functools.partial(_combine_kernel, tm=tm, alpha=alpha),
        grid_spec=grid_spec,
        out_shape=jax.ShapeDtypeStruct((t, D_MODEL), F32),
        compiler_params=_cparams(("arbitrary",)),
        name="moe_combine",
    )(plan["loff"], plan["goff"], plan["nch"], h, meta, ys, ln_g, ln_b)


def _moe_tail(h, o_attn, o_conv, w_o, ln1, w_router, wg, wu, wd, ln_g, ln_b, *, alpha):
    t = h.shape[0]
    nt = t // TMX
    max_rows = 2 * t + nt * N_EXPERTS * (CH - 1) + N_EXPERTS * (TMR - CH)
    n_row_tiles = -(-max_rows // TMR)
    h = _out_proj(h, o_attn, o_conv, w_o, *ln1, alpha=alpha)
    meta, metat, cnt, wparts = _moe_route(h, w_router)
    plan = _moe_plan(cnt, n_row_tiles)
    xs = _moe_dispatch(h, metat, wparts, plan, n_row_tiles * TMR)
    ys = _moe_experts(xs, wg, wu, wd, plan)
    return _moe_combine(h, meta, ys, plan, ln_g, ln_b, alpha=alpha)


def kernel(x, ln_in_g, ln_in_b, w_in, cmp_pos, cmp_w1, cmp_b1, cmp_w2, cmp_b2, conv_w, w_o, ln1_g, ln1_b, ln2_g, ln2_b, ffn_wg, ffn_wu, ffn_wd, moe_router, moe_wg, moe_wu, moe_wd):
    batch, seq, d = x.shape
    depth = w_in.shape[0]
    assert d == D_MODEL and seq % TM_PROJ == 0 and seq // CMP_STRIDE == LANES
    alpha = (2 * depth) ** 0.25
    t = batch * seq
    vec = lambda v: v.reshape(1, D_MODEL)
    def proj_weights(l):
        return (*_prep_w_in(w_in[l]), jnp.pad(conv_w[l], ((0, 8 - CONV_W), (0, 0))))

    h, *proj = _in_proj(x.reshape(t, d), vec(ln_in_g), vec(ln_in_b), proj_weights(0), seq=seq, pre_ln=True)
    for l in range(depth):
        q, kvc, ks, vs, kw, vw, gates, o_conv = proj
        kvcmp, kvcmp_t = _compress(kvc, cmp_pos[l], cmp_w1[l], cmp_b1[l], cmp_w2[l], cmp_b2[l],
                                   batch=batch, seq=seq)
        o_attn = _nsa_attn(q, kvcmp, kvcmp_t, ks, vs, kw, vw, gates, batch=batch, seq=seq)
        ln1 = (vec(ln1_g[l]), vec(ln1_b[l]))
        ln2 = (vec(ln2_g[l]), vec(ln2_b[l]))
        next_w = proj_weights(l + 1) if l + 1 < depth else None
        if l % 2 == 0:
            ffn_w = (ffn_wg[l // 2].astype(BF16), ffn_wu[l // 2].astype(BF16), ffn_wd[l // 2].astype(BF16))
            h, *proj = _dense_tail(h, o_attn, o_conv, w_o[l].astype(BF16), ln1, ffn_w, ln2, next_w,
                                   alpha=alpha, seq=seq)
        else:
            h = _moe_tail(h, o_attn, o_conv, w_o[l].astype(BF16), ln1, moe_router[l // 2],
                          moe_wg[l // 2].astype(BF16), moe_wu[l // 2].astype(BF16), moe_wd[l // 2].astype(BF16),
                          *ln2, alpha=alpha)
            if next_w is not None:
                proj = _in_proj(h, None, None, next_w, seq=seq, pre_ln=False)
    return h.reshape(batch, seq, d)
```

```python
import functools

import numpy as np
import jax
import jax.numpy as jnp
from jax import lax
from jax.experimental import pallas as pl
from jax.experimental.pallas import tpu as pltpu

F32 = jnp.float32
BF16 = jnp.bfloat16

D_MODEL = 1024
HEAD_DIM = 64
N_HEADS = 8
N_KV = 2
HPG = N_HEADS // N_KV
ATTN_W = N_HEADS * HEAD_DIM
KV_W = N_KV * HEAD_DIM
CONV_CH = D_MODEL - ATTN_W
CONV_W = 3
CMP_BLOCK = 32
CMP_STRIDE = 16
CMP_HIDDEN = 256
SEL_BLOCK = 64
SEL_TOPN = 16
WINDOW = 512
D_FF = 2816
N_EXPERTS = 8
D_FF_EXPERT = 1408
LN_EPS = 1e-5
NEG = -1e30
FORCE = 1e9

LANES = 128
HPAD = LANES
VROWS = HPAD
Q_SCALE = HEAD_DIM ** -0.5 * np.log2(np.e)

C_Q = 0
C_KVC = C_Q + ATTN_W
C_KS = C_KVC + 2 * KV_W
C_KW = C_KS + KV_W
C_U = C_KW + KV_W
C_B = C_U + CONV_CH
C_C = C_B + CONV_CH
C_G = C_C + CONV_CH
C_END = C_G + N_KV * LANES

TM_PROJ = 512
TQ = 256
TK = 256
VMEM_LIMIT = 56 * 1024 * 1024


def _cparams(sem):
    return pltpu.CompilerParams(dimension_semantics=sem, vmem_limit_bytes=VMEM_LIMIT)


def _ln(x, g, b):
    mu = jnp.mean(x, -1, keepdims=True)
    xc = x - mu
    var = jnp.mean(xc * xc, -1, keepdims=True)
    return xc * lax.rsqrt(var + LN_EPS) * g + b


def _dot(a, b):
    return jnp.dot(a, b, preferred_element_type=F32)


def _dot_nt(a, b):
    return lax.dot_general(a, b, (((1,), (1,)), ((), ())), preferred_element_type=F32)


def _resident(shape):
    nd = len(shape)
    return pl.BlockSpec(shape, lambda *_: (0,) * nd, pipeline_mode=pl.Buffered(1))


def _spread_heads(z):
    low = lax.broadcasted_iota(jnp.int32, (z.shape[0], LANES), 1) < HEAD_DIM
    tiles = []
    for p in range(z.shape[1] // LANES):
        pair = z[:, p * LANES:(p + 1) * LANES]
        tiles += [jnp.where(low, pair, 0.0), jnp.where(low, pltpu.roll(pair, HEAD_DIM, 1), 0.0)]
    return jnp.concatenate(tiles, axis=1)


def _project_inputs(h, w_refs, out_refs, scratch_refs, *, tiles_per_seq, tm):
    w_ref, wvt_ref, cw_ref = w_refs
    q_ref, kvc_ref, ks_ref, vs_ref, kw_ref, vw_ref, gate_ref, oc_ref = out_refs
    carry_ref, kvc_scr = scratch_refs
    hb = h.astype(BF16)

    def proj(lo, hi):
        return _dot(hb, w_ref[:, lo:hi])

    q_ref[...] = _spread_heads(proj(C_Q, C_KVC) * Q_SCALE).astype(BF16)

    kvc = proj(C_KVC, C_KS)
    low = lax.broadcasted_iota(jnp.int32, (tm // CMP_STRIDE, LANES), 1) < HEAD_DIM
    for kind in range(2):
        kvc_scr[kind] = kvc[:, kind * LANES:(kind + 1) * LANES]
        for p in range(CMP_STRIDE // 2):
            a = kvc_scr[kind, pl.ds(2 * p, tm // CMP_STRIDE, stride=CMP_STRIDE), :]
            b = kvc_scr[kind, pl.ds(2 * p + 1, tm // CMP_STRIDE, stride=CMP_STRIDE), :]
            kvc_ref[2 * kind, :, p * LANES:(p + 1) * LANES] = jnp.where(low, a, pltpu.roll(b, HEAD_DIM, 1))
            kvc_ref[2 * kind + 1, :, p * LANES:(p + 1) * LANES] = jnp.where(low, pltpu.roll(a, HEAD_DIM, 1), b)

    seq_tile = pl.program_id(0) % tiles_per_seq
    pos = lax.broadcasted_iota(jnp.int32, (tm, N_KV * HPAD), 0) + seq_tile * tm
    lane = lax.broadcasted_iota(jnp.int32, (tm, N_KV * HPAD), 1) % HPAD
    onehot = jnp.where(pos // SEL_BLOCK == lane - HEAD_DIM, 1.0, 0.0)
    in_tag = (lane >= HEAD_DIM) & (lane < HEAD_DIM + 32)
    kk = proj(C_KS, C_U)
    ks_ref[...] = jnp.where(in_tag, onehot, _spread_heads(kk[:, :KV_W])).astype(BF16)
    kw_ref[...] = _spread_heads(kk[:, KV_W:]).astype(BF16)

    vt = _dot_nt(wvt_ref[...], hb)
    tail = jnp.where(lax.broadcasted_iota(jnp.int32, (VROWS - HEAD_DIM, tm), 0) == 0, 1.0, 0.0)
    for n, ref in enumerate((vs_ref, vw_ref)):
        rows = []
        for grp in range(N_KV):
            lo = (n * N_KV + grp) * HEAD_DIM
            rows += [vt[lo:lo + HEAD_DIM], tail]
        full = jnp.concatenate(rows, axis=0).astype(BF16)
        for c in range(tm // TK):
            ref[c] = full[:, c * TK:(c + 1) * TK]

    cu = proj(C_C, C_G) * proj(C_U, C_B)

    @pl.when(seq_tile == 0)
    def _():
        carry_ref[...] = jnp.zeros_like(carry_ref)

    prev = carry_ref[...]
    row = lax.broadcasted_iota(jnp.int32, (tm, CONV_CH), 0)
    s1 = jnp.where(row == 0, prev[7:8], pltpu.roll(cu, 1, 0))
    s2 = jnp.where(row == 0, prev[6:7], jnp.where(row == 1, prev[7:8], pltpu.roll(cu, 2, 0)))
    y = s2 * cw_ref[0:1, :] + s1 * cw_ref[1:2, :] + cu * cw_ref[2:3, :]
    oc_ref[...] = (proj(C_B, C_C) * y).astype(BF16)
    carry_ref[...] = cu[tm - 8:tm]
    gate_ref[...] = jax.nn.sigmoid(proj(C_G, C_END))


N_PROJ_W = 3
N_PROJ_OUT = 8
N_PROJ_SCRATCH = 2


def _proj_specs(t, tm):
    row = lambda w: pl.BlockSpec((tm, w), lambda i: (i, 0))
    w_specs = [_resident((D_MODEL, C_END)), _resident((2 * KV_W, D_MODEL)), _resident((8, CONV_CH))]
    vt_shape = jax.ShapeDtypeStruct((t // TK, N_KV * VROWS, TK), BF16)
    vt_spec = pl.BlockSpec((tm // TK, N_KV * VROWS, TK), lambda i: (i, 0, 0))
    out_shape = [
        jax.ShapeDtypeStruct((t, N_HEADS * HPAD), BF16),
        jax.ShapeDtypeStruct((4, t // CMP_STRIDE, CMP_STRIDE * HEAD_DIM), F32),
        jax.ShapeDtypeStruct((t, N_KV * HPAD), BF16), vt_shape,
        jax.ShapeDtypeStruct((t, N_KV * HPAD), BF16), vt_shape,
        jax.ShapeDtypeStruct((t, N_KV * LANES), F32),
        jax.ShapeDtypeStruct((t, CONV_CH), BF16),
    ]
    out_specs = [
        row(N_HEADS * HPAD),
        pl.BlockSpec((4, tm // CMP_STRIDE, CMP_STRIDE * HEAD_DIM), lambda i: (0, i, 0)),
        row(N_KV * HPAD), vt_spec, row(N_KV * HPAD), vt_spec,
        row(N_KV * LANES), row(CONV_CH),
    ]
    scratch = [pltpu.VMEM((8, CONV_CH), F32), pltpu.VMEM((2, tm, LANES), F32)]
    return w_specs, out_shape, out_specs, scratch


def _in_proj_kernel(*refs, pre_ln, tiles_per_seq, tm):
    n_in = 3 if pre_ln else 1
    x_ref = refs[0]
    w_refs = refs[n_in:n_in + N_PROJ_W]
    outs = refs[n_in + N_PROJ_W:len(refs) - N_PROJ_SCRATCH]
    if pre_ln:
        h = _ln(x_ref[...], refs[1][...], refs[2][...])
        outs[0][...] = h
        outs = outs[1:]
    else:
        h = x_ref[...]
    _project_inputs(h, w_refs, outs, refs[len(refs) - N_PROJ_SCRATCH:], tiles_per_seq=tiles_per_seq, tm=tm)


def _in_proj(x, ln_g, ln_b, proj_w, *, seq, pre_ln):
    t = x.shape[0]
    tm = TM_PROJ
    row = pl.BlockSpec((tm, D_MODEL), lambda i: (i, 0))
    w_specs, out_shape, out_specs, scratch = _proj_specs(t, tm)
    in_specs, args = [row], [x]
    if pre_ln:
        in_specs += [_resident((1, D_MODEL)), _resident((1, D_MODEL))]
        args += [ln_g, ln_b]
        out_shape = [jax.ShapeDtypeStruct((t, D_MODEL), F32)] + out_shape
        out_specs = [row] + out_specs
    return pl.pallas_call(
        functools.partial(_in_proj_kernel, pre_ln=pre_ln, tiles_per_seq=seq // tm, tm=tm),
        grid=(t // tm,),
        in_specs=in_specs + w_specs,
        out_specs=out_specs,
        out_shape=out_shape,
        scratch_shapes=scratch,
        compiler_params=_cparams(("arbitrary",)),
        name="in_proj_ln" if pre_ln else "in_proj",
    )(*args, *proj_w)


def _prep_w_in(w_in):
    o = 0

    def take(n):
        nonlocal o
        s = w_in[:, o:o + n]
        o += n
        return s

    q, kc, vc, ks, vs, kw, vw = take(ATTN_W), *(take(KV_W) for _ in range(6))
    gates = take(3 * N_HEADS).reshape(-1, 3, N_KV, HPG).transpose(0, 2, 1, 3).reshape(-1, N_KV, 3 * HPG)
    gates = jnp.pad(gates, ((0, 0), (0, 0), (0, LANES - 3 * HPG))).reshape(-1, N_KV * LANES)
    conv = take(3 * CONV_CH)
    w_r = jnp.concatenate([q, kc, vc, ks, kw, conv, gates], axis=1).astype(BF16)
    return w_r, jnp.concatenate([vs, vw], axis=1).T.astype(BF16)


def _compress_kernel(x_ref, pos_ref, w1_ref, b1_ref, w2_ref, b2_ref, w2t_ref, b2t_ref, o_ref, ot_ref):
    half = CMP_STRIDE * HEAD_DIM
    nb = x_ref.shape[2]
    x = jnp.concatenate([x_ref[g, 0] for g in range(N_KV)], axis=0)
    pos = pos_ref[0]
    xa = (x + pos[:, :half]).astype(BF16)
    xb = (x + pos[:, half:]).astype(BF16)
    a = _dot(xa, w1_ref[0, :half, :])
    b = _dot(xb, w1_ref[0, half:, :])
    hid = a + pltpu.roll(b, N_KV * nb - 1, 0) + b1_ref[0]
    act = jax.nn.gelu(hid).astype(BF16)
    out = (_dot(act, w2_ref[0]) + b2_ref[0]).astype(BF16)
    out_t = (_dot_nt(w2t_ref[0], act) + b2t_ref[0]).astype(BF16)
    for g in range(N_KV):
        o_ref[g, 0] = out[g * nb:(g + 1) * nb]
        ot_ref[g, 0] = out_t[:, g * nb:(g + 1) * nb]


def _compress(kvc, cmp_pos, cmp_w1, cmp_b1, cmp_w2, cmp_b2, *, batch, seq):
    nb = seq // CMP_STRIDE
    half = CMP_STRIDE * HEAD_DIM
    x = kvc.reshape(4, batch, nb, half)
    pos = cmp_pos.reshape(2, 1, CMP_BLOCK * HEAD_DIM)
    w1 = cmp_w1.astype(BF16)
    b1 = cmp_b1.reshape(2, 1, CMP_HIDDEN)
    w2 = jnp.pad(cmp_w2, ((0, 0), (0, 0), (0, HPAD - HEAD_DIM))).astype(BF16)
    b2 = jnp.pad(cmp_b2, ((0, 0), (0, HPAD - HEAD_DIM))).reshape(2, 1, HPAD)
    w2t = jnp.swapaxes(w2, 1, 2)
    b2t = b2.reshape(2, HPAD, 1)
    kind = lambda j, b: (j, 0, 0)
    return pl.pallas_call(
        _compress_kernel,
        grid=(2, batch),
        in_specs=[
            pl.BlockSpec((N_KV, 1, nb, half), lambda j, b: (j, b, 0, 0)),
            pl.BlockSpec((1, 1, 2 * half), kind),
            pl.BlockSpec((1, 2 * half, CMP_HIDDEN), kind),
            pl.BlockSpec((1, 1, CMP_HIDDEN), kind),
            pl.BlockSpec((1, CMP_HIDDEN, HPAD), kind),
            pl.BlockSpec((1, 1, HPAD), kind),
            pl.BlockSpec((1, HPAD, CMP_HIDDEN), kind),
            pl.BlockSpec((1, HPAD, 1), kind),
        ],
        out_specs=[pl.BlockSpec((N_KV, 1, nb, HPAD), lambda j, b: (j, b, 0, 0)),
                   pl.BlockSpec((N_KV, 1, HPAD, nb), lambda j, b: (j, b, 0, 0))],
        out_shape=[jax.ShapeDtypeStruct((4, batch, nb, HPAD), BF16),
                   jax.ShapeDtypeStruct((4, batch, HPAD, nb), BF16)],
        compiler_params=_cparams(("arbitrary", "arbitrary")),
        name="compress",
    )(x, pos, w1, b1, w2, b2, w2t, b2t)


def _overlap_t():
    n = np.arange(LANES)
    j = np.arange(32)
    cs = n * CMP_STRIDE
    ss = j * SEL_BLOCK
    ov = (cs[None, :] < ss[:, None] + SEL_BLOCK) & (cs[None, :] + CMP_BLOCK > ss[:, None])
    return jnp.asarray(ov, dtype=BF16)


def _cmp_select(qh, kc, vct, ov, i, tq, n_sel):
    n = lax.broadcasted_iota(jnp.int32, (LANES, tq), 0)
    t = lax.broadcasted_iota(jnp.int32, (LANES, tq), 1) + i * tq
    mask = n * CMP_STRIDE + (CMP_BLOCK - 1) <= t
    maskf = mask.astype(F32)
    ss = [jnp.where(mask, _dot_nt(kc, q), NEG) for q in qh]
    es = [jnp.exp2(s - jnp.max(s, axis=0, keepdims=True)) for s in ss]
    ps = [e / jnp.sum(e, axis=0, keepdims=True) * maskf for e in es]
    heads_t = [_dot(vct, p.astype(BF16))[:HEAD_DIM] for p in ps]
    psum = functools.reduce(lambda a, b: a + b, ps)

    p_hi = psum.astype(BF16)
    r1 = psum - p_hi.astype(F32)
    p_mid = r1.astype(BF16)
    p_lo = (r1 - p_mid.astype(F32)).astype(BF16)
    imp = _dot(ov, p_hi) + _dot(ov, p_mid) + _dot(ov, p_lo)

    jt = lax.broadcasted_iota(jnp.int32, (n_sel, tq), 0)
    tt = lax.broadcasted_iota(jnp.int32, (n_sel, tq), 1) + i * tq
    cur = tt // SEL_BLOCK
    valid = jt * SEL_BLOCK <= tt
    forced = (jt == 0) | (jt == cur) | (jt == cur - 1)
    score = jnp.where(forced, FORCE, jnp.where(valid, imp, NEG))
    rank = jnp.zeros((n_sel, tq), jnp.int32)
    for k in range(n_sel):
        sk = score[k:k + 1, :]
        ahead = (sk > score) | ((sk == score) & (jt > k))
        rank = rank + ahead.astype(jnp.int32)
    bias = jnp.where(rank < SEL_TOPN, 0.0, NEG)
    full = jnp.concatenate(
        [jnp.zeros((HEAD_DIM, tq), F32), bias, jnp.zeros((HPAD - HEAD_DIM - n_sel, tq), F32)], axis=0)
    return heads_t, full.T.astype(BF16)


def _attn_scores(k, qa):
    return tuple(_dot_nt(k, qa[hh]) for hh in range(HPG))


def _attn_update(ss, vt, state, mask):
    if mask is not None:
        ss = [jnp.where(mask, s, NEG) for s in ss]
    m_new = [jnp.maximum(state[hh][0], jnp.max(ss[hh], axis=0, keepdims=True)) for hh in range(HPG)]
    ps = [jnp.exp2(ss[hh] - m_new[hh]).astype(BF16) for hh in range(HPG)]
    pv = [_dot(vt, ps[hh]) for hh in range(HPG)]
    return tuple((m_new[hh], jnp.exp2(state[hh][0] - m_new[hh]) * state[hh][1] + pv[hh]) for hh in range(HPG))


def _attn_init(tq):
    return tuple((jnp.full((1, tq), NEG, F32), jnp.zeros((VROWS, tq), F32)) for _ in range(HPG))


def _nsa_attn_kernel(q_ref, kc_ref, vct_ref, ov_ref, ks_ref, vst_ref, kw_ref, vwt_ref, gate_ref, o_ref,
                     *, tq, tk, n_sel):
    i = pl.program_id(2)
    nq = ks_ref.shape[0] // tk
    qh = [q_ref[:, hh * HPAD:(hh + 1) * HPAD] for hh in range(HPG)]
    heads_cmp, selb = _cmp_select(qh, kc_ref[0, 0], vct_ref[0, 0], ov_ref[...], i, tq, n_sel)
    lane = lax.broadcasted_iota(jnp.int32, selb.shape, 1)
    qsel = [jnp.where(lane >= HEAD_DIM, selb, q) for q in qh]
    key = lax.broadcasted_iota(jnp.int32, (tk, tq), 0)
    qry = lax.broadcasted_iota(jnp.int32, (tk, tq), 1)
    causal = key <= qry
    gt = gate_ref[...].T

    key2 = lax.broadcasted_iota(jnp.int32, (2 * tk, tq), 0)
    pair_causal = key2 - tk <= lax.broadcasted_iota(jnp.int32, (2 * tk, tq), 1)

    def run(n):
        chunks = []
        for j in range(0, n, 2):
            chunks.append(("sel", j, 2, pair_causal if j + 1 == n else None))
        if n % 2 == 0:
            chunks.append(("sel", n, 1, causal))
        if n == 0:
            chunks.append(("win", 0, 1, causal))
        else:
            chunks.append(("win", n - 1, 2, pair_causal))
        if n >= 2:
            chunks.append(("win", n - 2, 1, key > qry))
        refs = {"sel": (ks_ref, vst_ref, qsel), "win": (kw_ref, vwt_ref, qh)}

        def scores(chunk):
            branch, j, nt, _ = chunk
            return _attn_scores(refs[branch][0][j * tk:(j + nt) * tk, :], refs[branch][2])

        state = {"sel": _attn_init(tq), "win": _attn_init(tq)}
        ss = scores(chunks[0])
        for idx, (branch, j, nt, mask) in enumerate(chunks):
            ss_next = scores(chunks[idx + 1]) if idx + 1 < len(chunks) else None
            vt_ref = refs[branch][1]
            vt = vt_ref[j] if nt == 1 else jnp.concatenate([vt_ref[j], vt_ref[j + 1]], axis=1)
            state[branch] = _attn_update(ss, vt, state[branch], mask)
            ss = ss_next

        outs = []
        for hh in range(HPG):
            o_sel, o_win = (acc[:HEAD_DIM] / acc[HEAD_DIM:HEAD_DIM + 1]
                            for _, acc in (state["sel"][hh], state["win"][hh]))
            g_cmp, g_sel, g_win = (gt[c * HPG + hh:c * HPG + hh + 1] for c in range(3))
            outs.append(g_cmp * heads_cmp[hh] + g_sel * o_sel + g_win * o_win)
        o_ref[...] = jnp.concatenate(outs, axis=0).T.astype(o_ref.dtype)

    for n in range(nq):
        pl.when(i == n)(functools.partial(run, n))


def _nsa_attn(q, kvcmp, kvcmp_t, ks, vs_t, kw, vw_t, gates, *, batch, seq):
    t = q.shape[0]
    tq, tk = TQ, TK
    assert tq == tk and WINDOW == 2 * tk
    nq = seq // tq
    n_sel = seq // SEL_BLOCK
    nb = kvcmp.shape[2]
    assert nb == LANES and n_sel == 32
    rowblk = lambda w: pl.BlockSpec((tq, w), lambda b, g, i: (b * nq + i, g))
    seqblk = pl.BlockSpec((seq, HPAD), lambda b, g, i: (b, g))
    vtblk = pl.BlockSpec((seq // tk, VROWS, tk), lambda b, g, i: (b, g, 0))
    return pl.pallas_call(
        functools.partial(_nsa_attn_kernel, tq=tq, tk=tk, n_sel=n_sel),
        grid=(batch, N_KV, nq),
        in_specs=[rowblk(HPG * HPAD),
                  pl.BlockSpec((1, 1, nb, HPAD), lambda b, g, i: (g, b, 0, 0)),
                  pl.BlockSpec((1, 1, HPAD, nb), lambda b, g, i: (N_KV + g, b, 0, 0)),
                  pl.BlockSpec((n_sel, LANES), lambda b, g, i: (0, 0)),
                  seqblk, vtblk, seqblk, vtblk,
                  rowblk(LANES)],
        out_specs=rowblk(HPG * HEAD_DIM),
        out_shape=jax.ShapeDtypeStruct((t, ATTN_W), BF16),
        compiler_params=_cparams(("arbitrary", "arbitrary", "arbitrary")),
        name="nsa_attn",
    )(q, kvcmp, kvcmp_t, _overlap_t(), ks, vs_t, kw, vw_t, gates)


def _dense_tail_kernel(*refs, alpha, with_proj, tiles_per_seq, tm):
    (h_ref, oa_ref, ocv_ref, wo_ref, g1_ref, b1_ref, wg_ref, wu_ref, wd_ref, g2_ref, b2_ref) = refs[:11]
    m = _dot(oa_ref[...], wo_ref[:ATTN_W, :]) + _dot(ocv_ref[...], wo_ref[ATTN_W:, :])
    h1 = _ln(alpha * h_ref[...] + m, g1_ref[...], b1_ref[...])
    hb = h1.astype(BF16)
    a = (jax.nn.silu(_dot(hb, wg_ref[...])) * _dot(hb, wu_ref[...])).astype(BF16)
    h2 = _ln(alpha * h1 + _dot(a, wd_ref[...]), g2_ref[...], b2_ref[...])
    if with_proj:
        w_refs = refs[11:11 + N_PROJ_W]
        outs = refs[11 + N_PROJ_W:len(refs) - N_PROJ_SCRATCH]
        outs[0][...] = h2
        _project_inputs(h2, w_refs, outs[1:], refs[len(refs) - N_PROJ_SCRATCH:], tiles_per_seq=tiles_per_seq, tm=tm)
    else:
        refs[11][...] = h2


def _dense_tail(h, o_attn, o_conv, w_o, ln1, ffn_w, ln2, proj_w, *, alpha, seq):
    t = h.shape[0]
    tm = TM_PROJ
    row = lambda w: pl.BlockSpec((tm, w), lambda i: (i, 0))
    vec = _resident((1, D_MODEL))
    in_specs = [row(D_MODEL), row(ATTN_W), row(CONV_CH), _resident((D_MODEL, D_MODEL)), vec, vec,
                _resident((D_MODEL, D_FF)), _resident((D_MODEL, D_FF)), _resident((D_FF, D_MODEL)), vec, vec]
    args = [h, o_attn, o_conv, w_o, *ln1, *ffn_w, *ln2]
    out_shape, out_specs, scratch = [jax.ShapeDtypeStruct((t, D_MODEL), F32)], [row(D_MODEL)], []
    if proj_w is not None:
        w_specs, p_shape, p_specs, scratch = _proj_specs(t, tm)
        in_specs += w_specs
        args += list(proj_w)
        out_shape += p_shape
        out_specs += p_specs
    return pl.pallas_call(
        functools.partial(_dense_tail_kernel, alpha=alpha, with_proj=proj_w is not None,
                          tiles_per_seq=seq // tm, tm=tm),
        grid=(t // tm,),
        in_specs=in_specs,
        out_specs=out_specs,
        out_shape=out_shape,
        scratch_shapes=scratch,
        compiler_params=_cparams(("arbitrary",)),
        name="dense_tail",
    )(*args)


TMX = 512
CH = 16
LROWS = 2 * TMX + N_EXPERTS * CH
TMR = 512
XROW = D_MODEL + LANES


def _out_proj_kernel(h_ref, oa_ref, ocv_ref, w_ref, g_ref, b_ref, o_ref, *, alpha):
    m = _dot(oa_ref[...], w_ref[:ATTN_W, :]) + _dot(ocv_ref[...], w_ref[ATTN_W:, :])
    o_ref[...] = _ln(alpha * h_ref[...] + m, g_ref[...], b_ref[...])


def _out_proj(h, o_attn, o_conv, w_o, ln_g, ln_b, *, alpha):
    t = h.shape[0]
    tm = 2 * TM_PROJ
    row = lambda w: pl.BlockSpec((tm, w), lambda i: (i, 0))
    return pl.pallas_call(
        functools.partial(_out_proj_kernel, alpha=alpha),
        grid=(t // tm,),
        in_specs=[row(D_MODEL), row(ATTN_W), row(CONV_CH),
                  _resident((D_MODEL, D_MODEL)), _resident((1, D_MODEL)), _resident((1, D_MODEL))],
        out_specs=row(D_MODEL),
        out_shape=jax.ShapeDtypeStruct((t, D_MODEL), F32),
        compiler_params=_cparams(("arbitrary",)),
        name="out_proj",
    )(h, o_attn, o_conv, w_o, ln_g, ln_b)


def _route_kernel(h_ref, wr_ref, ltri_ref, ustr_ref, meta_ref, metat_ref, cnt_ref, wp_ref, *, tm):
    h = h_ref[...]
    h_hi = h.astype(BF16)
    h_lo = (h - h_hi.astype(F32)).astype(BF16)
    hw = _dot(h_hi, wr_ref[...])
    logits = hw[:, :LANES] + hw[:, LANES:] + _dot(h_lo, wr_ref[:, :LANES])
    lane = lax.broadcasted_iota(jnp.int32, logits.shape, 1)
    logits = jnp.where(lane < N_EXPERTS, logits, -jnp.inf)
    m1 = jnp.max(logits, axis=-1, keepdims=True)
    i1 = jnp.min(jnp.where(logits == m1, lane, LANES), axis=-1, keepdims=True)
    rest = jnp.where(lane == i1, -jnp.inf, logits)
    m2 = jnp.max(rest, axis=-1, keepdims=True)
    i2 = jnp.min(jnp.where(rest == m2, lane, LANES), axis=-1, keepdims=True)
    e2 = jnp.exp(m2 - m1)
    den = 1.0 + e2
    w1 = 1.0 / den
    w2 = e2 / den

    routed = (lane == i1) | (lane == i2)
    cnt = _dot(ltri_ref[...], routed.astype(BF16))
    n = cnt[tm - 1:tm, :]
    padded = jnp.floor((n + (CH - 1)) * (1.0 / CH)) * CH
    seg_off = _dot(jnp.broadcast_to(padded, (8, LANES)).astype(BF16), ustr_ref[...])[0:1]
    dest = seg_off + cnt - 1.0
    d1 = jnp.sum(jnp.where(lane == i1, dest, 0.0), axis=-1, keepdims=True)
    d2 = jnp.sum(jnp.where(lane == i2, dest, 0.0), axis=-1, keepdims=True)
    meta = jnp.where(lane == 0, d1, jnp.where(lane == 1, d2, jnp.where(lane == 2, w1, jnp.where(lane == 3, w2, 0.0))))
    meta_ref[...] = meta
    metat_ref[...] = meta.T[0:8, :]
    cnt_ref[0] = jnp.broadcast_to(n, (8, LANES))
    for k, w in enumerate((w1, w2)):
        hi = w.astype(BF16).astype(F32)
        mid = (w - hi).astype(BF16).astype(F32)
        lo = (w - hi - mid).astype(BF16).astype(F32)
        wp_ref[k] = jnp.where(lane == 0, hi, jnp.where(lane == 1, mid, jnp.where(lane == 2, lo, 0.0))).astype(BF16)


def _moe_route(h, w_router):
    t = h.shape[0]
    tm = TMX
    nt = t // tm
    row = lambda w: pl.BlockSpec((tm, w), lambda i: (i, 0))
    wr = jnp.pad(w_router, ((0, 0), (0, LANES - N_EXPERTS)))
    wr_hi = wr.astype(BF16)
    wr_lo = (wr - wr_hi.astype(F32)).astype(BF16)
    wr2 = jnp.concatenate([wr_hi, wr_lo], axis=1)
    ltri = jnp.asarray(np.tril(np.ones((tm, tm), np.float32)), dtype=BF16)
    ustr = jnp.asarray(np.triu(np.ones((LANES, LANES), np.float32), 1), dtype=BF16)
    return pl.pallas_call(
        functools.partial(_route_kernel, tm=tm),
        grid=(nt,),
        in_specs=[row(D_MODEL), _resident((D_MODEL, 2 * LANES)), _resident((tm, tm)), _resident((LANES, LANES))],
        out_specs=[row(LANES),
                   pl.BlockSpec((8, tm), lambda i: (0, i)),
                   pl.BlockSpec((1, 8, LANES), lambda i: (i, 0, 0)),
                   pl.BlockSpec((2, tm, LANES), lambda i: (0, i, 0))],
        out_shape=[jax.ShapeDtypeStruct((t, LANES), F32), jax.ShapeDtypeStruct((8, t), F32),
                   jax.ShapeDtypeStruct((nt, 8, LANES), F32), jax.ShapeDtypeStruct((2, t, LANES), BF16)],
        compiler_params=_cparams(("arbitrary",)),
        name="moe_route",
    )(h, wr2, ltri, ustr)


def _moe_plan(cnt, n_row_tiles):
    n = cnt[:, 0, :N_EXPERTS].astype(jnp.int32)
    p = (n + CH - 1) // CH * CH
    tot = p.sum(0)
    tot_pad = (tot + TMR - 1) // TMR * TMR
    gend = jnp.cumsum(tot_pad)
    gstart = gend - tot_pad
    goff = gstart[None, :] + jnp.cumsum(p, 0) - p
    loff = jnp.cumsum(p, 1) - p
    n_used = gend[-1:] // TMR
    tile_start = jnp.arange(n_row_tiles, dtype=jnp.int32) * TMR
    texp = jnp.minimum(jnp.sum(tile_start[:, None] >= gend[None, :], axis=1), N_EXPERTS - 1).astype(jnp.int32)
    texp = jnp.where(jnp.arange(n_row_tiles) < n_used[0], texp, texp[jnp.maximum(n_used[0] - 1, 0)])
    flat = lambda a: a.reshape(-1).astype(jnp.int32)
    gap_start = jnp.concatenate([gstart + tot, gend[-1:]])
    gap_rows = jnp.concatenate([tot_pad - tot, n_row_tiles * TMR - gend[-1:]])
    return dict(loff=flat(loff), goff=flat(goff), nch=flat(p // CH), gap_start=flat(gap_start),
                gap_nch=flat(gap_rows // CH), texp=flat(texp), n_used=flat(n_used))


def _chunk_copies(src_ref, dst_ref, sem, src_off, dst_off, n, src_step=CH):
    def body(c, carry):
        s = pl.multiple_of(src_off + c * src_step, CH)
        d = pl.multiple_of(dst_off + c * CH, CH)
        pltpu.make_async_copy(src_ref.at[pl.ds(s, CH)], dst_ref.at[pl.ds(d, CH)], sem).start()
        return carry
    lax.fori_loop(0, n, body, 0)


def _wait_chunks(src_ref, dst_ref, sem, n):
    def body(c, carry):
        pltpu.make_async_copy(src_ref.at[pl.ds(0, CH)], dst_ref.at[pl.ds(0, CH)], sem).wait()
        return carry
    lax.fori_loop(0, n, body, 0)


def _tile_chunks(tile, nch):
    total = 0
    for e in range(N_EXPERTS):
        total = total + nch[tile * N_EXPERTS + e]
    return total


def _start_segments(tile, nch, src_off, dst_off, src_ref, dst_ref, sem):
    for e in range(N_EXPERTS):
        idx = tile * N_EXPERTS + e
        _chunk_copies(src_ref, dst_ref, sem, src_off[idx], dst_off[idx], nch[idx])


def _dispatch_kernel(loff, goff, nch, gap_start, gap_nch, h_ref, metat_ref, wp_ref, xs_ref, xc_ref, z_ref, sem,
                     *, tm):
    i = pl.program_id(0)
    last = pl.num_programs(0) - 1
    slot = i % 2
    r = lax.broadcasted_iota(jnp.int32, (LROWS, tm), 0).astype(F32)
    pick1 = (r == metat_ref[0:1, :]).astype(BF16)
    pick2 = (r == metat_ref[1:2, :]).astype(BF16)
    xc_ref[slot, :, :D_MODEL] = _dot(pick1 + pick2, h_ref[...].astype(BF16)).astype(BF16)
    xc_ref[slot, :, D_MODEL:] = (_dot(pick1, wp_ref[0]) + _dot(pick2, wp_ref[1])).astype(BF16)
    _start_segments(i, nch, loff, goff, xc_ref.at[slot], xs_ref, sem.at[slot])

    @pl.when(i > 0)
    def _():
        _wait_chunks(xc_ref.at[1 - slot], xs_ref, sem.at[1 - slot], _tile_chunks(i - 1, nch))

    @pl.when(i == last)
    def _():
        z_ref[...] = jnp.zeros_like(z_ref)
        gaps = 0
        for e in range(N_EXPERTS + 1):
            _chunk_copies(z_ref, xs_ref, sem.at[2], 0, gap_start[e], gap_nch[e], src_step=0)
            gaps = gaps + gap_nch[e]
        _wait_chunks(z_ref, xs_ref, sem.at[2], gaps)
        _wait_chunks(xc_ref.at[slot], xs_ref, sem.at[slot], _tile_chunks(i, nch))


def _moe_dispatch(h, metat, wparts, plan, n_rows):
    t = h.shape[0]
    tm = TMX
    grid_spec = pltpu.PrefetchScalarGridSpec(
        num_scalar_prefetch=5,
        grid=(t // tm,),
        in_specs=[pl.BlockSpec((tm, D_MODEL), lambda i, *_: (i, 0)),
                  pl.BlockSpec((8, tm), lambda i, *_: (0, i)),
                  pl.BlockSpec((2, tm, LANES), lambda i, *_: (0, i, 0))],
        out_specs=pl.BlockSpec(memory_space=pl.ANY),
        scratch_shapes=[pltpu.VMEM((2, LROWS, XROW), BF16), pltpu.VMEM((CH, XROW), BF16),
                        pltpu.SemaphoreType.DMA((3,))],
    )
    return pl.pallas_call(
        functools.partial(_dispatch_kernel, tm=tm),
        grid_spec=grid_spec,
        out_shape=jax.ShapeDtypeStruct((n_rows, XROW), BF16),
        compiler_params=_cparams(("arbitrary",)),
        name="moe_dispatch",
    )(plan["loff"], plan["goff"], plan["nch"], plan["gap_start"], plan["gap_nch"], h, metat, wparts)


def _experts_kernel(texp, n_used, x_ref, wg_ref, wu_ref, wd_ref, y_ref):
    used = pl.program_id(0) < n_used[0]

    @pl.when(used)
    def _():
        x = x_ref[:, :D_MODEL]
        wp = x_ref[:, D_MODEL:].astype(F32)
        gate = wp[:, 0:1] + wp[:, 1:2] + wp[:, 2:3]
        main = D_FF_EXPERT - LANES
        w_mid = jnp.concatenate([wg_ref[0, :, main:], wu_ref[0, :, :LANES]], axis=1)
        g_main = _dot(x, wg_ref[0, :, :main])
        mid = _dot(x, w_mid)
        u_main = _dot(x, wu_ref[0, :, LANES:])
        a = jnp.concatenate([
            jax.nn.silu(g_main[:, :LANES]) * mid[:, LANES:],
            jax.nn.silu(g_main[:, LANES:]) * u_main[:, :main - LANES],
            jax.nn.silu(mid[:, :LANES]) * u_main[:, main - LANES:],
        ], axis=1).astype(BF16)
        y_ref[...] = (gate * _dot(a, wd_ref[0])).astype(BF16)

    @pl.when(jnp.logical_not(used))
    def _():
        y_ref[...] = jnp.zeros_like(y_ref)


def _moe_experts(xs, wg, wu, wd, plan):
    n_rows = xs.shape[0]
    rows = lambda r, texp, n_used: (jnp.minimum(r, n_used[0] - 1), 0)
    wspec = lambda shape: pl.BlockSpec((1,) + shape, lambda r, texp, n_used: (texp[r], 0, 0))
    grid_spec = pltpu.PrefetchScalarGridSpec(
        num_scalar_prefetch=2,
        grid=(n_rows // TMR,),
        in_specs=[pl.BlockSpec((TMR, XROW), rows), wspec((D_MODEL, D_FF_EXPERT)),
                  wspec((D_MODEL, D_FF_EXPERT)), wspec((D_FF_EXPERT, D_MODEL))],
        out_specs=pl.BlockSpec((TMR, D_MODEL), lambda r, texp, n_used: (r, 0)),
    )
    return pl.pallas_call(
        _experts_kernel,
        grid_spec=grid_spec,
        out_shape=jax.ShapeDtypeStruct((n_rows, D_MODEL), BF16),
        compiler_params=_cparams(("arbitrary",)),
        name="moe_experts",
    )(plan["texp"], plan["n_used"], xs, wg, wu, wd)


def _combine_kernel(loff, goff, nch, h_ref, meta_ref, ys_ref, g_ref, b_ref, o_ref, yb_ref, sem, *, tm, alpha):
    i = pl.program_id(0)
    slot = i % 2

    @pl.when(i == 0)
    def _():
        yb_ref[...] = jnp.zeros_like(yb_ref)
        _start_segments(0, nch, goff, loff, ys_ref, yb_ref.at[0], sem.at[0])

    @pl.when(i + 1 < pl.num_programs(0))
    def _():
        _start_segments(i + 1, nch, goff, loff, ys_ref, yb_ref.at[1 - slot], sem.at[1 - slot])

    meta = meta_ref[...]
    r = lax.broadcasted_iota(jnp.int32, (tm, LROWS), 1).astype(F32)
    pick = ((r == meta[:, 0:1]) | (r == meta[:, 1:2])).astype(BF16)
    _wait_chunks(ys_ref, yb_ref.at[slot], sem.at[slot], _tile_chunks(i, nch))
    f = _dot(pick, yb_ref[slot])
    o_ref[...] = _ln(alpha * h_ref[...] + f, g_ref[...], b_ref[...])


def _moe_combine(h, meta, ys, plan, ln_g, ln_b, *, alpha):
    t = h.shape[0]
    tm = TMX
    grid_spec = pltpu.PrefetchScalarGridSpec(
        num_scalar_prefetch=3,
        grid=(t // tm,),
        in_specs=[pl.BlockSpec((tm, D_MODEL), lambda i, *_: (i, 0)),
                  pl.BlockSpec((tm, LANES), lambda i, *_: (i, 0)),
                  pl.BlockSpec(memory_space=pl.ANY),
                  pl.BlockSpec((1, D_MODEL), lambda i, *_: (0, 0)),
                  pl.BlockSpec((1, D_MODEL), lambda i, *_: (0, 0))],
        out_specs=pl.BlockSpec((tm, D_MODEL), lambda i, *_: (i, 0)),
        scratch_shapes=[pltpu.VMEM((2, LROWS, D_MODEL), BF16), pltpu.SemaphoreType.DMA((2,))],
    )
    return pl.pallas_call(
        functools.partial(_combine_kernel, tm=tm, alpha=alpha),
        grid_spec=grid_spec,
        out_shape=jax.ShapeDtypeStruct((t, D_MODEL), F32),
        compiler_params=_cparams(("arbitrary",)),
        name="moe_combine",
    )(plan["loff"], plan["goff"], plan["nch"], h, meta, ys, ln_g, ln_b)


def _moe_tail(h, o_attn, o_conv, w_o, ln1, w_router, wg, wu, wd, ln_g, ln_b, *, alpha):
    t = h.shape[0]
    nt = t // TMX
    max_rows = 2 * t + nt * N_EXPERTS * (CH - 1) + N_EXPERTS * (TMR - CH)
    n_row_tiles = -(-max_rows // TMR)
    h = _out_proj(h, o_attn, o_conv, w_o, *ln1, alpha=alpha)
    meta, metat, cnt, wparts = _moe_route(h, w_router)
    plan = _moe_plan(cnt, n_row_tiles)
    xs = _moe_dispatch(h, metat, wparts, plan, n_row_tiles * TMR)
    ys = _moe_experts(xs, wg, wu, wd, plan)
    return _moe_combine(h, meta, ys, plan, ln_g, ln_b, alpha=alpha)


def kernel(x, ln_in_g, ln_in_b, w_in, cmp_pos, cmp_w1, cmp_b1, cmp_w2, cmp_b2, conv_w, w_o, ln1_g, ln1_b, ln2_g, ln2_b, ffn_wg, ffn_wu, ffn_wd, moe_router, moe_wg, moe_wu, moe_wd):
    batch, seq, d = x.shape
    depth = w_in.shape[0]
    assert d == D_MODEL and seq % TM_PROJ == 0 and seq // CMP_STRIDE == LANES
    alpha = (2 * depth) ** 0.25
    t = batch * seq
    vec = lambda v: v.reshape(1, D_MODEL)
    def proj_weights(l):
        return (*_prep_w_in(w_in[l]), jnp.pad(conv_w[l], ((0, 8 - CONV_W), (0, 0))))

    h, *proj = _in_proj(x.reshape(t, d), vec(ln_in_g), vec(ln_in_b), proj_weights(0), seq=seq, pre_ln=True)
    for l in range(depth):
        q, kvc, ks, vs, kw, vw, gates, o_conv = proj
        kvcmp, kvcmp_t = _compress(kvc, cmp_pos[l], cmp_w1[l], cmp_b1[l], cmp_w2[l], cmp_b2[l],
                                   batch=batch, seq=seq)
        o_attn = _nsa_attn(q, kvcmp, kvcmp_t, ks, vs, kw, vw, gates, batch=batch, seq=seq)
        ln1 = (vec(ln1_g[l]), vec(ln1_b[l]))
        ln2 = (vec(ln2_g[l]), vec(ln2_b[l]))
        next_w = proj_weights(l + 1) if l + 1 < depth else None
        if l % 2 == 0:
            ffn_w = (ffn_wg[l // 2].astype(BF16), ffn_wu[l // 2].astype(BF16), ffn_wd[l // 2].astype(BF16))
            h, *proj = _dense_tail(h, o_attn, o_conv, w_o[l].astype(BF16), ln1, ffn_w, ln2, next_w,
                                   alpha=alpha, seq=seq)
        else:
            h = _moe_tail(h, o_attn, o_conv, w_o[l].astype(BF16), ln1, moe_router[l // 2],
                          moe_wg[l // 2].astype(BF16), moe_wu[l // 2].astype(BF16), moe_wd[l // 2].astype(BF16),
                          *ln2, alpha=alpha)
            if next_w is not None:
                proj = _in_proj(h, None, None, next_w, seq=seq, pre_ln=False)
    return h.reshape(batch, seq, d)
```

```python
import functools

import numpy as np
import jax
import jax.numpy as jnp
from jax import lax
from jax.experimental import pallas as pl
from jax.experimental.pallas import tpu as pltpu

F32 = jnp.float32
BF16 = jnp.bfloat16

D_MODEL = 1024
HEAD_DIM = 64
N_HEADS = 8
N_KV = 2
HPG = N_HEADS // N_KV
ATTN_W = N_HEADS * HEAD_DIM
KV_W = N_KV * HEAD_DIM
CONV_CH = D_MODEL - ATTN_W
CONV_W = 3
CMP_BLOCK = 32
CMP_STRIDE = 16
CMP_HIDDEN = 256
SEL_BLOCK = 64
SEL_TOPN = 16
WINDOW = 512
D_FF = 2816
N_EXPERTS = 8
D_FF_EXPERT = 1408
LN_EPS = 1e-5
NEG = -1e30
FORCE = 1e9

LANES = 128
HPAD = LANES
VROWS = HPAD
Q_SCALE = HEAD_DIM ** -0.5 * np.log2(np.e)

C_Q = 0
C_KVC = C_Q + ATTN_W
C_KS = C_KVC + 2 * KV_W
C_KW = C_KS + KV_W
C_U = C_KW + KV_W
C_B = C_U + CONV_CH
C_C = C_B + CONV_CH
C_G = C_C + CONV_CH
C_END = C_G + N_KV * LANES

TM_PROJ = 512
TQ = 256
TK = 256
VMEM_LIMIT = 56 * 1024 * 1024


def _cparams(sem):
    return pltpu.CompilerParams(dimension_semantics=sem, vmem_limit_bytes=VMEM_LIMIT)


def _ln(x, g, b):
    mu = jnp.mean(x, -1, keepdims=True)
    xc = x - mu
    var = jnp.mean(xc * xc, -1, keepdims=True)
    return xc * lax.rsqrt(var + LN_EPS) * g + b


def _dot(a, b):
    return jnp.dot(a, b, preferred_element_type=F32)


def _dot_nt(a, b):
    return lax.dot_general(a, b, (((1,), (1,)), ((), ())), preferred_element_type=F32)


def _resident(shape):
    nd = len(shape)
    return pl.BlockSpec(shape, lambda *_: (0,) * nd, pipeline_mode=pl.Buffered(1))


def _spread_heads(z):
    low = lax.broadcasted_iota(jnp.int32, (z.shape[0], LANES), 1) < HEAD_DIM
    tiles = []
    for p in range(z.shape[1] // LANES):
        pair = z[:, p * LANES:(p + 1) * LANES]
        tiles += [jnp.where(low, pair, 0.0), jnp.where(low, pltpu.roll(pair, HEAD_DIM, 1), 0.0)]
    return jnp.concatenate(tiles, axis=1)


def _project_inputs(h, w_refs, out_refs, scratch_refs, *, tiles_per_seq, tm):
    w_ref, wvt_ref, cw_ref = w_refs
    q_ref, kvc_ref, ks_ref, vs_ref, kw_ref, vw_ref, gate_ref, oc_ref = out_refs
    carry_ref, kvc_scr = scratch_refs
    hb = h.astype(BF16)

    def proj(lo, hi):
        return _dot(hb, w_ref[:, lo:hi])

    q_ref[...] = _spread_heads(proj(C_Q, C_KVC) * Q_SCALE).astype(BF16)

    kvc = proj(C_KVC, C_KS)
    low = lax.broadcasted_iota(jnp.int32, (tm // CMP_STRIDE, LANES), 1) < HEAD_DIM
    for kind in range(2):
        kvc_scr[kind] = kvc[:, kind * LANES:(kind + 1) * LANES]
        for p in range(CMP_STRIDE // 2):
            a = kvc_scr[kind, pl.ds(2 * p, tm // CMP_STRIDE, stride=CMP_STRIDE), :]
            b = kvc_scr[kind, pl.ds(2 * p + 1, tm // CMP_STRIDE, stride=CMP_STRIDE), :]
            kvc_ref[2 * kind, :, p * LANES:(p + 1) * LANES] = jnp.where(low, a, pltpu.roll(b, HEAD_DIM, 1))
            kvc_ref[2 * kind + 1, :, p * LANES:(p + 1) * LANES] = jnp.where(low, pltpu.roll(a, HEAD_DIM, 1), b)

    seq_tile = pl.program_id(0) % tiles_per_seq
    pos = lax.broadcasted_iota(jnp.int32, (tm, N_KV * HPAD), 0) + seq_tile * tm
    lane = lax.broadcasted_iota(jnp.int32, (tm, N_KV * HPAD), 1) % HPAD
    onehot = jnp.where(pos // SEL_BLOCK == lane - HEAD_DIM, 1.0, 0.0)
    in_tag = (lane >= HEAD_DIM) & (lane < HEAD_DIM + 32)
    kk = proj(C_KS, C_U)
    ks_ref[...] = jnp.where(in_tag, onehot, _spread_heads(kk[:, :KV_W])).astype(BF16)
    kw_ref[...] = _spread_heads(kk[:, KV_W:]).astype(BF16)

    vt = _dot_nt(wvt_ref[...], hb)
    tail = jnp.where(lax.broadcasted_iota(jnp.int32, (VROWS - HEAD_DIM, tm), 0) == 0, 1.0, 0.0)
    for n, ref in enumerate((vs_ref, vw_ref)):
        rows = []
        for grp in range(N_KV):
            lo = (n * N_KV + grp) * HEAD_DIM
            rows += [vt[lo:lo + HEAD_DIM], tail]
        full = jnp.concatenate(rows, axis=0).astype(BF16)
        for c in range(tm // TK):
            ref[c] = full[:, c * TK:(c + 1) * TK]

    cu = proj(C_C, C_G) * proj(C_U, C_B)

    @pl.when(seq_tile == 0)
    def _():
        carry_ref[...] = jnp.zeros_like(carry_ref)

    prev = carry_ref[...]
    row = lax.broadcasted_iota(jnp.int32, (tm, CONV_CH), 0)
    s1 = jnp.where(row == 0, prev[7:8], pltpu.roll(cu, 1, 0))
    s2 = jnp.where(row == 0, prev[6:7], jnp.where(row == 1, prev[7:8], pltpu.roll(cu, 2, 0)))
    y = s2 * cw_ref[0:1, :] + s1 * cw_ref[1:2, :] + cu * cw_ref[2:3, :]
    oc_ref[...] = (proj(C_B, C_C) * y).astype(BF16)
    carry_ref[...] = cu[tm - 8:tm]
    gate_ref[...] = jax.nn.sigmoid(proj(C_G, C_END))


N_PROJ_W = 3
N_PROJ_OUT = 8
N_PROJ_SCRATCH = 2


def _proj_specs(t, tm):
    row = lambda w: pl.BlockSpec((tm, w), lambda i: (i, 0))
    w_specs = [_resident((D_MODEL, C_END)), _resident((2 * KV_W, D_MODEL)), _resident((8, CONV_CH))]
    vt_shape = jax.ShapeDtypeStruct((t // TK, N_KV * VROWS, TK), BF16)
    vt_spec = pl.BlockSpec((tm // TK, N_KV * VROWS, TK), lambda i: (i, 0, 0))
    out_shape = [
        jax.ShapeDtypeStruct((t, N_HEADS * HPAD), BF16),
        jax.ShapeDtypeStruct((4, t // CMP_STRIDE, CMP_STRIDE * HEAD_DIM), F32),
        jax.ShapeDtypeStruct((t, N_KV * HPAD), BF16), vt_shape,
        jax.ShapeDtypeStruct((t, N_KV * HPAD), BF16), vt_shape,
        jax.ShapeDtypeStruct((t, N_KV * LANES), F32),
        jax.ShapeDtypeStruct((t, CONV_CH), BF16),
    ]
    out_specs = [
        row(N_HEADS * HPAD),
        pl.BlockSpec((4, tm // CMP_STRIDE, CMP_STRIDE * HEAD_DIM), lambda i: (0, i, 0)),
        row(N_KV * HPAD), vt_spec, row(N_KV * HPAD), vt_spec,
        row(N_KV * LANES), row(CONV_CH),
    ]
    scratch = [pltpu.VMEM((8, CONV_CH), F32), pltpu.VMEM((2, tm, LANES), F32)]
    return w_specs, out_shape, out_specs, scratch


def _in_proj_kernel(*refs, pre_ln, tiles_per_seq, tm):
    n_in = 3 if pre_ln else 1
    x_ref = refs[0]
    w_refs = refs[n_in:n_in + N_PROJ_W]
    outs = refs[n_in + N_PROJ_W:len(refs) - N_PROJ_SCRATCH]
    if pre_ln:
        h = _ln(x_ref[...], refs[1][...], refs[2][...])
        outs[0][...] = h
        outs = outs[1:]
    else:
        h = x_ref[...]
    _project_inputs(h, w_refs, outs, refs[len(refs) - N_PROJ_SCRATCH:], tiles_per_seq=tiles_per_seq, tm=tm)


def _in_proj(x, ln_g, ln_b, proj_w, *, seq, pre_ln):
    t = x.shape[0]
    tm = 2 * TM_PROJ
    row = pl.BlockSpec((tm, D_MODEL), lambda i: (i, 0))
    w_specs, out_shape, out_specs, scratch = _proj_specs(t, tm)
    in_specs, args = [row], [x]
    if pre_ln:
        in_specs += [_resident((1, D_MODEL)), _resident((1, D_MODEL))]
        args += [ln_g, ln_b]
        out_shape = [jax.ShapeDtypeStruct((t, D_MODEL), F32)] + out_shape
        out_specs = [row] + out_specs
    return pl.pallas_call(
        functools.partial(_in_proj_kernel, pre_ln=pre_ln, tiles_per_seq=seq // tm, tm=tm),
        grid=(t // tm,),
        in_specs=in_specs + w_specs,
        out_specs=out_specs,
        out_shape=out_shape,
        scratch_shapes=scratch,
        compiler_params=_cparams(("arbitrary",)),
        name="in_proj_ln" if pre_ln else "in_proj",
    )(*args, *proj_w)


def _prep_w_in(w_in):
    o = 0

    def take(n):
        nonlocal o
        s = w_in[:, o:o + n]
        o += n
        return s

    q, kc, vc, ks, vs, kw, vw = take(ATTN_W), *(take(KV_W) for _ in range(6))
    gates = take(3 * N_HEADS).reshape(-1, 3, N_KV, HPG).transpose(0, 2, 1, 3).reshape(-1, N_KV, 3 * HPG)
    gates = jnp.pad(gates, ((0, 0), (0, 0), (0, LANES - 3 * HPG))).reshape(-1, N_KV * LANES)
    conv = take(3 * CONV_CH)
    w_r = jnp.concatenate([q, kc, vc, ks, kw, conv, gates], axis=1).astype(BF16)
    return w_r, jnp.concatenate([vs, vw], axis=1).T.astype(BF16)


def _compress_kernel(x_ref, pos_ref, w1_ref, b1_ref, w2_ref, b2_ref, w2t_ref, b2t_ref, o_ref, ot_ref):
    half = CMP_STRIDE * HEAD_DIM
    nb = x_ref.shape[2]
    x = jnp.concatenate([x_ref[g, 0] for g in range(N_KV)], axis=0)
    pos = pos_ref[0]
    xa = (x + pos[:, :half]).astype(BF16)
    xb = (x + pos[:, half:]).astype(BF16)
    a = _dot(xa, w1_ref[0, :half, :])
    b = _dot(xb, w1_ref[0, half:, :])
    hid = a + pltpu.roll(b, N_KV * nb - 1, 0) + b1_ref[0]
    act = jax.nn.gelu(hid).astype(BF16)
    out = (_dot(act, w2_ref[0]) + b2_ref[0]).astype(BF16)
    out_t = (_dot_nt(w2t_ref[0], act) + b2t_ref[0]).astype(BF16)
    for g in range(N_KV):
        o_ref[g, 0] = out[g * nb:(g + 1) * nb]
        ot_ref[g, 0] = out_t[:, g * nb:(g + 1) * nb]


def _compress(kvc, cmp_pos, cmp_w1, cmp_b1, cmp_w2, cmp_b2, *, batch, seq):
    nb = seq // CMP_STRIDE
    half = CMP_STRIDE * HEAD_DIM
    x = kvc.reshape(4, batch, nb, half)
    pos = cmp_pos.reshape(2, 1, CMP_BLOCK * HEAD_DIM)
    w1 = cmp_w1.astype(BF16)
    b1 = cmp_b1.reshape(2, 1, CMP_HIDDEN)
    w2 = jnp.pad(cmp_w2, ((0, 0), (0, 0), (0, HPAD - HEAD_DIM))).astype(BF16)
    b2 = jnp.pad(cmp_b2, ((0, 0), (0, HPAD - HEAD_DIM))).reshape(2, 1, HPAD)
    w2t = jnp.swapaxes(w2, 1, 2)
    b2t = b2.reshape(2, HPAD, 1)
    kind = lambda j, b: (j, 0, 0)
    return pl.pallas_call(
        _compress_kernel,
        grid=(2, batch),
        in_specs=[
            pl.BlockSpec((N_KV, 1, nb, half), lambda j, b: (j, b, 0, 0)),
            pl.BlockSpec((1, 1, 2 * half), kind),
            pl.BlockSpec((1, 2 * half, CMP_HIDDEN), kind),
            pl.BlockSpec((1, 1, CMP_HIDDEN), kind),
            pl.BlockSpec((1, CMP_HIDDEN, HPAD), kind),
            pl.BlockSpec((1, 1, HPAD), kind),
            pl.BlockSpec((1, HPAD, CMP_HIDDEN), kind),
            pl.BlockSpec((1, HPAD, 1), kind),
        ],
        out_specs=[pl.BlockSpec((N_KV, 1, nb, HPAD), lambda j, b: (j, b, 0, 0)),
                   pl.BlockSpec((N_KV, 1, HPAD, nb), lambda j, b: (j, b, 0, 0))],
        out_shape=[jax.ShapeDtypeStruct((4, batch, nb, HPAD), BF16),
                   jax.ShapeDtypeStruct((4, batch, HPAD, nb), BF16)],
        compiler_params=_cparams(("arbitrary", "arbitrary")),
        name="compress",
    )(x, pos, w1, b1, w2, b2, w2t, b2t)


def _overlap_t():
    n = np.arange(LANES)
    j = np.arange(32)
    cs = n * CMP_STRIDE
    ss = j * SEL_BLOCK
    ov = (cs[None, :] < ss[:, None] + SEL_BLOCK) & (cs[None, :] + CMP_BLOCK > ss[:, None])
    return jnp.asarray(ov, dtype=BF16)


def _cmp_select(qh, kc, vct, ov, i, tq, n_sel):
    n = lax.broadcasted_iota(jnp.int32, (LANES, tq), 0)
    t = lax.broadcasted_iota(jnp.int32, (LANES, tq), 1) + i * tq
    mask = n * CMP_STRIDE + (CMP_BLOCK - 1) <= t
    maskf = mask.astype(F32)
    ss = [jnp.where(mask, _dot_nt(kc, q), NEG) for q in qh]
    es = [jnp.exp2(s - jnp.max(s, axis=0, keepdims=True)) for s in ss]
    ps = [e / jnp.sum(e, axis=0, keepdims=True) * maskf for e in es]
    heads_t = [_dot(vct, p.astype(BF16))[:HEAD_DIM] for p in ps]
    psum = functools.reduce(lambda a, b: a + b, ps)

    p_hi = psum.astype(BF16)
    r1 = psum - p_hi.astype(F32)
    p_mid = r1.astype(BF16)
    p_lo = (r1 - p_mid.astype(F32)).astype(BF16)
    imp = _dot(ov, p_hi) + _dot(ov, p_mid) + _dot(ov, p_lo)

    jt = lax.broadcasted_iota(jnp.int32, (n_sel, tq), 0)
    tt = lax.broadcasted_iota(jnp.int32, (n_sel, tq), 1) + i * tq
    cur = tt // SEL_BLOCK
    valid = jt * SEL_BLOCK <= tt
    forced = (jt == 0) | (jt == cur) | (jt == cur - 1)
    score = jnp.where(forced, FORCE, jnp.where(valid, imp, NEG))
    rank = jnp.zeros((n_sel, tq), jnp.int32)
    for k in range(n_sel):
        sk = score[k:k + 1, :]
        ahead = (sk > score) | ((sk == score) & (jt > k))
        rank = rank + ahead.astype(jnp.int32)
    bias = jnp.where(rank < SEL_TOPN, 0.0, NEG)
    full = jnp.concatenate(
        [jnp.zeros((HEAD_DIM, tq), F32), bias, jnp.zeros((HPAD - HEAD_DIM - n_sel, tq), F32)], axis=0)
    return heads_t, full.T.astype(BF16)


def _attn_scores(k, qa):
    return tuple(_dot_nt(k, qa[hh]) for hh in range(HPG))


def _attn_update(ss, vt, state, mask):
    if mask is not None:
        ss = [jnp.where(mask, s, NEG) for s in ss]
    m_new = [jnp.maximum(state[hh][0], jnp.max(ss[hh], axis=0, keepdims=True)) for hh in range(HPG)]
    ps = [jnp.exp2(ss[hh] - m_new[hh]).astype(BF16) for hh in range(HPG)]
    pv = [_dot(vt, ps[hh]) for hh in range(HPG)]
    return tuple((m_new[hh], jnp.exp2(state[hh][0] - m_new[hh]) * state[hh][1] + pv[hh]) for hh in range(HPG))


def _attn_init(tq):
    return tuple((jnp.full((1, tq), NEG, F32), jnp.zeros((VROWS, tq), F32)) for _ in range(HPG))


def _nsa_attn_kernel(q_ref, kc_ref, vct_ref, ov_ref, ks_ref, vst_ref, kw_ref, vwt_ref, gate_ref, o_ref,
                     *, tq, tk, n_sel):
    i = pl.program_id(2)
    nq = ks_ref.shape[0] // tk
    qh = [q_ref[:, hh * HPAD:(hh + 1) * HPAD] for hh in range(HPG)]
    heads_cmp, selb = _cmp_select(qh, kc_ref[0, 0], vct_ref[0, 0], ov_ref[...], i, tq, n_sel)
    lane = lax.broadcasted_iota(jnp.int32, selb.shape, 1)
    qsel = [jnp.where(lane >= HEAD_DIM, selb, q) for q in qh]
    key = lax.broadcasted_iota(jnp.int32, (tk, tq), 0)
    qry = lax.broadcasted_iota(jnp.int32, (tk, tq), 1)
    causal = key <= qry
    gt = gate_ref[...].T

    key2 = lax.broadcasted_iota(jnp.int32, (2 * tk, tq), 0)
    pair_causal = key2 - tk <= lax.broadcasted_iota(jnp.int32, (2 * tk, tq), 1)

    def run(n):
        chunks = []
        for j in range(0, n, 2):
            chunks.append(("sel", j, 2, pair_causal if j + 1 == n else None))
        if n % 2 == 0:
            chunks.append(("sel", n, 1, causal))
        if n == 0:
            chunks.append(("win", 0, 1, causal))
        else:
            chunks.append(("win", n - 1, 2, pair_causal))
        if n >= 2:
            chunks.append(("win", n - 2, 1, key > qry))
        refs = {"sel": (ks_ref, vst_ref, qsel), "win": (kw_ref, vwt_ref, qh)}

        def scores(chunk):
            branch, j, nt, _ = chunk
            return _attn_scores(refs[branch][0][j * tk:(j + nt) * tk, :], refs[branch][2])

        state = {"sel": _attn_init(tq), "win": _attn_init(tq)}
        ss = scores(chunks[0])
        for idx, (branch, j, nt, mask) in enumerate(chunks):
            ss_next = scores(chunks[idx + 1]) if idx + 1 < len(chunks) else None
            vt_ref = refs[branch][1]
            vt = vt_ref[j] if nt == 1 else jnp.concatenate([vt_ref[j], vt_ref[j + 1]], axis=1)
            state[branch] = _attn_update(ss, vt, state[branch], mask)
            ss = ss_next

        outs = []
        for hh in range(HPG):
            o_sel, o_win = (acc[:HEAD_DIM] / acc[HEAD_DIM:HEAD_DIM + 1]
                            for _, acc in (state["sel"][hh], state["win"][hh]))
            g_cmp, g_sel, g_win = (gt[c * HPG + hh:c * HPG + hh + 1] for c in range(3))
            outs.append(g_cmp * heads_cmp[hh] + g_sel * o_sel + g_win * o_win)
        o_ref[...] = jnp.concatenate(outs, axis=0).T.astype(o_ref.dtype)

    for n in range(nq):
        pl.when(i == n)(functools.partial(run, n))


def _nsa_attn(q, kvcmp, kvcmp_t, ks, vs_t, kw, vw_t, gates, *, batch, seq):
    t = q.shape[0]
    tq, tk = TQ, TK
    assert tq == tk and WINDOW == 2 * tk
    nq = seq // tq
    n_sel = seq // SEL_BLOCK
    nb = kvcmp.shape[2]
    assert nb == LANES and n_sel == 32
    rowblk = lambda w: pl.BlockSpec((tq, w), lambda b, g, i: (b * nq + i, g))
    seqblk = pl.BlockSpec((seq, HPAD), lambda b, g, i: (b, g))
    vtblk = pl.BlockSpec((seq // tk, VROWS, tk), lambda b, g, i: (b, g, 0))
    return pl.pallas_call(
        functools.partial(_nsa_attn_kernel, tq=tq, tk=tk, n_sel=n_sel),
        grid=(batch, N_KV, nq),
        in_specs=[rowblk(HPG * HPAD),
                  pl.BlockSpec((1, 1, nb, HPAD), lambda b, g, i: (g, b, 0, 0)),
                  pl.BlockSpec((1, 1, HPAD, nb), lambda b, g, i: (N_KV + g, b, 0, 0)),
                  pl.BlockSpec((n_sel, LANES), lambda b, g, i: (0, 0)),
                  seqblk, vtblk, seqblk, vtblk,
                  rowblk(LANES)],
        out_specs=rowblk(HPG * HEAD_DIM),
        out_shape=jax.ShapeDtypeStruct((t, ATTN_W), BF16),
        compiler_params=_cparams(("arbitrary", "arbitrary", "arbitrary")),
        name="nsa_attn",
    )(q, kvcmp, kvcmp_t, _overlap_t(), ks, vs_t, kw, vw_t, gates)


def _dense_tail_kernel(*refs, alpha, with_proj, tiles_per_seq, tm):
    (h_ref, oa_ref, ocv_ref, wo_ref, g1_ref, b1_ref, wg_ref, wu_ref, wd_ref, g2_ref, b2_ref) = refs[:11]
    m = _dot(oa_ref[...], wo_ref[:ATTN_W, :]) + _dot(ocv_ref[...], wo_ref[ATTN_W:, :])
    h1 = _ln(alpha * h_ref[...] + m, g1_ref[...], b1_ref[...])
    hb = h1.astype(BF16)
    a = (jax.nn.silu(_dot(hb, wg_ref[...])) * _dot(hb, wu_ref[...])).astype(BF16)
    h2 = _ln(alpha * h1 + _dot(a, wd_ref[...]), g2_ref[...], b2_ref[...])
    if with_proj:
        w_refs = refs[11:11 + N_PROJ_W]
        outs = refs[11 + N_PROJ_W:len(refs) - N_PROJ_SCRATCH]
        outs[0][...] = h2
        _project_inputs(h2, w_refs, outs[1:], refs[len(refs) - N_PROJ_SCRATCH:], tiles_per_seq=tiles_per_seq, tm=tm)
    else:
        refs[11][...] = h2


def _dense_tail(h, o_attn, o_conv, w_o, ln1, ffn_w, ln2, proj_w, *, alpha, seq):
    t = h.shape[0]
    tm = TM_PROJ
    row = lambda w: pl.BlockSpec((tm, w), lambda i: (i, 0))
    vec = _resident((1, D_MODEL))
    in_specs = [row(D_MODEL), row(ATTN_W), row(CONV_CH), _resident((D_MODEL, D_MODEL)), vec, vec,
                _resident((D_MODEL, D_FF)), _resident((D_MODEL, D_FF)), _resident((D_FF, D_MODEL)), vec, vec]
    args = [h, o_attn, o_conv, w_o, *ln1, *ffn_w, *ln2]
    out_shape, out_specs, scratch = [jax.ShapeDtypeStruct((t, D_MODEL), F32)], [row(D_MODEL)], []
    if proj_w is not None:
        w_specs, p_shape, p_specs, scratch = _proj_specs(t, tm)
        in_specs += w_specs
        args += list(proj_w)
        out_shape += p_shape
        out_specs += p_specs
    return pl.pallas_call(
        functools.partial(_dense_tail_kernel, alpha=alpha, with_proj=proj_w is not None,
                          tiles_per_seq=seq // tm, tm=tm),
        grid=(t // tm,),
        in_specs=in_specs,
        out_specs=out_specs,
        out_shape=out_shape,
        scratch_shapes=scratch,
        compiler_params=_cparams(("arbitrary",)),
        name="dense_tail",
    )(*args)


TMX = 512
CH = 16
LROWS = 2 * TMX + N_EXPERTS * CH
TMR = 512
XROW = D_MODEL + LANES


def _out_proj_kernel(h_ref, oa_ref, ocv_ref, w_ref, g_ref, b_ref, o_ref, *, alpha):
    m = _dot(oa_ref[...], w_ref[:ATTN_W, :]) + _dot(ocv_ref[...], w_ref[ATTN_W:, :])
    o_ref[...] = _ln(alpha * h_ref[...] + m, g_ref[...], b_ref[...])


def _out_proj(h, o_attn, o_conv, w_o, ln_g, ln_b, *, alpha):
    t = h.shape[0]
    tm = 2 * TM_PROJ
    row = lambda w: pl.BlockSpec((tm, w), lambda i: (i, 0))
    return pl.pallas_call(
        functools.partial(_out_proj_kernel, alpha=alpha),
        grid=(t // tm,),
        in_specs=[row(D_MODEL), row(ATTN_W), row(CONV_CH),
                  _resident((D_MODEL, D_MODEL)), _resident((1, D_MODEL)), _resident((1, D_MODEL))],
        out_specs=row(D_MODEL),
        out_shape=jax.ShapeDtypeStruct((t, D_MODEL), F32),
        compiler_params=_cparams(("arbitrary",)),
        name="out_proj",
    )(h, o_attn, o_conv, w_o, ln_g, ln_b)


def _route_kernel(h_ref, wr_ref, ltri_ref, ustr_ref, meta_ref, metat_ref, cnt_ref, wp_ref, *, tm):
    h = h_ref[...]
    h_hi = h.astype(BF16)
    h_lo = (h - h_hi.astype(F32)).astype(BF16)
    hw = _dot(h_hi, wr_ref[...])
    logits = hw[:, :LANES] + hw[:, LANES:] + _dot(h_lo, wr_ref[:, :LANES])
    lane = lax.broadcasted_iota(jnp.int32, logits.shape, 1)
    logits = jnp.where(lane < N_EXPERTS, logits, -jnp.inf)
    m1 = jnp.max(logits, axis=-1, keepdims=True)
    i1 = jnp.min(jnp.where(logits == m1, lane, LANES), axis=-1, keepdims=True)
    rest = jnp.where(lane == i1, -jnp.inf, logits)
    m2 = jnp.max(rest, axis=-1, keepdims=True)
    i2 = jnp.min(jnp.where(rest == m2, lane, LANES), axis=-1, keepdims=True)
    e2 = jnp.exp(m2 - m1)
    den = 1.0 + e2
    w1 = 1.0 / den
    w2 = e2 / den

    routed = (lane == i1) | (lane == i2)
    cnt = _dot(ltri_ref[...], routed.astype(BF16))
    n = cnt[tm - 1:tm, :]
    padded = jnp.floor((n + (CH - 1)) * (1.0 / CH)) * CH
    seg_off = _dot(jnp.broadcast_to(padded, (8, LANES)).astype(BF16), ustr_ref[...])[0:1]
    dest = seg_off + cnt - 1.0
    d1 = jnp.sum(jnp.where(lane == i1, dest, 0.0), axis=-1, keepdims=True)
    d2 = jnp.sum(jnp.where(lane == i2, dest, 0.0), axis=-1, keepdims=True)
    meta = jnp.where(lane == 0, d1, jnp.where(lane == 1, d2, jnp.where(lane == 2, w1, jnp.where(lane == 3, w2, 0.0))))
    meta_ref[...] = meta
    metat_ref[...] = meta.T[0:8, :]
    cnt_ref[0] = jnp.broadcast_to(n, (8, LANES))
    for k, w in enumerate((w1, w2)):
        hi = w.astype(BF16).astype(F32)
        mid = (w - hi).astype(BF16).astype(F32)
        lo = (w - hi - mid).astype(BF16).astype(F32)
        wp_ref[k] = jnp.where(lane == 0, hi, jnp.where(lane == 1, mid, jnp.where(lane == 2, lo, 0.0))).astype(BF16)


def _moe_route(h, w_router):
    t = h.shape[0]
    tm = TMX
    nt = t // tm
    row = lambda w: pl.BlockSpec((tm, w), lambda i: (i, 0))
    wr = jnp.pad(w_router, ((0, 0), (0, LANES - N_EXPERTS)))
    wr_hi = wr.astype(BF16)
    wr_lo = (wr - wr_hi.astype(F32)).astype(BF16)
    wr2 = jnp.concatenate([wr_hi, wr_lo], axis=1)
    ltri = jnp.asarray(np.tril(np.ones((tm, tm), np.float32)), dtype=BF16)
    ustr = jnp.asarray(np.triu(np.ones((LANES, LANES), np.float32), 1), dtype=BF16)
    return pl.pallas_call(
        functools.partial(_route_kernel, tm=tm),
        grid=(nt,),
        in_specs=[row(D_MODEL), _resident((D_MODEL, 2 * LANES)), _resident((tm, tm)), _resident((LANES, LANES))],
        out_specs=[row(LANES),
                   pl.BlockSpec((8, tm), lambda i: (0, i)),
                   pl.BlockSpec((1, 8, LANES), lambda i: (i, 0, 0)),
                   pl.BlockSpec((2, tm, LANES), lambda i: (0, i, 0))],
        out_shape=[jax.ShapeDtypeStruct((t, LANES), F32), jax.ShapeDtypeStruct((8, t), F32),
                   jax.ShapeDtypeStruct((nt, 8, LANES), F32), jax.ShapeDtypeStruct((2, t, LANES), BF16)],
        compiler_params=_cparams(("arbitrary",)),
        name="moe_route",
    )(h, wr2, ltri, ustr)


def _moe_plan(cnt, n_row_tiles):
    n = cnt[:, 0, :N_EXPERTS].astype(jnp.int32)
    p = (n + CH - 1) // CH * CH
    tot = p.sum(0)
    tot_pad = (tot + TMR - 1) // TMR * TMR
    gend = jnp.cumsum(tot_pad)
    gstart = gend - tot_pad
    goff = gstart[None, :] + jnp.cumsum(p, 0) - p
    loff = jnp.cumsum(p, 1) - p
    n_used = gend[-1:] // TMR
    tile_start = jnp.arange(n_row_tiles, dtype=jnp.int32) * TMR
    texp = jnp.minimum(jnp.sum(tile_start[:, None] >= gend[None, :], axis=1), N_EXPERTS - 1).astype(jnp.int32)
    texp = jnp.where(jnp.arange(n_row_tiles) < n_used[0], texp, texp[jnp.maximum(n_used[0] - 1, 0)])
    flat = lambda a: a.reshape(-1).astype(jnp.int32)
    gap_start = jnp.concatenate([gstart + tot, gend[-1:]])
    gap_rows = jnp.concatenate([tot_pad - tot, n_row_tiles * TMR - gend[-1:]])
    return dict(loff=flat(loff), goff=flat(goff), nch=flat(p // CH), gap_start=flat(gap_start),
                gap_nch=flat(gap_rows // CH), texp=flat(texp), n_used=flat(n_used))


def _chunk_copies(src_ref, dst_ref, sem, src_off, dst_off, n, src_step=CH):
    def body(c, carry):
        s = pl.multiple_of(src_off + c * src_step, CH)
        d = pl.multiple_of(dst_off + c * CH, CH)
        pltpu.make_async_copy(src_ref.at[pl.ds(s, CH)], dst_ref.at[pl.ds(d, CH)], sem).start()
        return carry
    lax.fori_loop(0, n, body, 0)


def _wait_chunks(src_ref, dst_ref, sem, n):
    def body(c, carry):
        pltpu.make_async_copy(src_ref.at[pl.ds(0, CH)], dst_ref.at[pl.ds(0, CH)], sem).wait()
        return carry
    lax.fori_loop(0, n, body, 0)


def _tile_chunks(tile, nch):
    total = 0
    for e in range(N_EXPERTS):
        total = total + nch[tile * N_EXPERTS + e]
    return total


def _start_segments(tile, nch, src_off, dst_off, src_ref, dst_ref, sem):
    for e in range(N_EXPERTS):
        idx = tile * N_EXPERTS + e
        _chunk_copies(src_ref, dst_ref, sem, src_off[idx], dst_off[idx], nch[idx])


def _dispatch_kernel(loff, goff, nch, gap_start, gap_nch, h_ref, metat_ref, wp_ref, xs_ref, xc_ref, z_ref, sem,
                     *, tm):
    i = pl.program_id(0)
    last = pl.num_programs(0) - 1
    slot = i % 2
    r = lax.broadcasted_iota(jnp.int32, (LROWS, tm), 0).astype(F32)
    pick1 = (r == metat_ref[0:1, :]).astype(BF16)
    pick2 = (r == metat_ref[1:2, :]).astype(BF16)
    xc_ref[slot, :, :D_MODEL] = _dot(pick1 + pick2, h_ref[...].astype(BF16)).astype(BF16)
    xc_ref[slot, :, D_MODEL:] = (_dot(pick1, wp_ref[0]) + _dot(pick2, wp_ref[1])).astype(BF16)
    _start_segments(i, nch, loff, goff, xc_ref.at[slot], xs_ref, sem.at[slot])

    @pl.when(i > 0)
    def _():
        _wait_chunks(xc_ref.at[1 - slot], xs_ref, sem.at[1 - slot], _tile_chunks(i - 1, nch))

    @pl.when(i == last)
    def _():
        z_ref[...] = jnp.zeros_like(z_ref)
        gaps = 0
        for e in range(N_EXPERTS + 1):
            _chunk_copies(z_ref, xs_ref, sem.at[2], 0, gap_start[e], gap_nch[e], src_step=0)
            gaps = gaps + gap_nch[e]
        _wait_chunks(z_ref, xs_ref, sem.at[2], gaps)
        _wait_chunks(xc_ref.at[slot], xs_ref, sem.at[slot], _tile_chunks(i, nch))


def _moe_dispatch(h, metat, wparts, plan, n_rows):
    t = h.shape[0]
    tm = TMX
    grid_spec = pltpu.PrefetchScalarGridSpec(
        num_scalar_prefetch=5,
        grid=(t // tm,),
        in_specs=[pl.BlockSpec((tm, D_MODEL), lambda i, *_: (i, 0)),
                  pl.BlockSpec((8, tm), lambda i, *_: (0, i)),
                  pl.BlockSpec((2, tm, LANES), lambda i, *_: (0, i, 0))],
        out_specs=pl.BlockSpec(memory_space=pl.ANY),
        scratch_shapes=[pltpu.VMEM((2, LROWS, XROW), BF16), pltpu.VMEM((CH, XROW), BF16),
                        pltpu.SemaphoreType.DMA((3,))],
    )
    return pl.pallas_call(
        functools.partial(_dispatch_kernel, tm=tm),
        grid_spec=grid_spec,
        out_shape=jax.ShapeDtypeStruct((n_rows, XROW), BF16),
        compiler_params=_cparams(("arbitrary",)),
        name="moe_dispatch",
    )(plan["loff"], plan["goff"], plan["nch"], plan["gap_start"], plan["gap_nch"], h, metat, wparts)


def _experts_kernel(texp, n_used, x_ref, wg_ref, wu_ref, wd_ref, y_ref):
    used = pl.program_id(0) < n_used[0]

    @pl.when(used)
    def _():
        x = x_ref[:, :D_MODEL]
        wp = x_ref[:, D_MODEL:].astype(F32)
        gate = wp[:, 0:1] + wp[:, 1:2] + wp[:, 2:3]
        main = D_FF_EXPERT - LANES
        w_mid = jnp.concatenate([wg_ref[0, :, main:], wu_ref[0, :, :LANES]], axis=1)
        g_main = _dot(x, wg_ref[0, :, :main])
        mid = _dot(x, w_mid)
        u_main = _dot(x, wu_ref[0, :, LANES:])
        a = jnp.concatenate([
            jax.nn.silu(g_main[:, :LANES]) * mid[:, LANES:],
            jax.nn.silu(g_main[:, LANES:]) * u_main[:, :main - LANES],
            jax.nn.silu(mid[:, :LANES]) * u_main[:, main - LANES:],
        ], axis=1).astype(BF16)
        y_ref[...] = (gate * _dot(a, wd_ref[0])).astype(BF16)

    @pl.when(jnp.logical_not(used))
    def _():
        y_ref[...] = jnp.zeros_like(y_ref)


def _moe_experts(xs, wg, wu, wd, plan):
    n_rows = xs.shape[0]
    rows = lambda r, texp, n_used: (jnp.minimum(r, n_used[0] - 1), 0)
    wspec = lambda shape: pl.BlockSpec((1,) + shape, lambda r, texp, n_used: (texp[r], 0, 0))
    grid_spec = pltpu.PrefetchScalarGridSpec(
        num_scalar_prefetch=2,
        grid=(n_rows // TMR,),
        in_specs=[pl.BlockSpec((TMR, XROW), rows), wspec((D_MODEL, D_FF_EXPERT)),
                  wspec((D_MODEL, D_FF_EXPERT)), wspec((D_FF_EXPERT, D_MODEL))],
        out_specs=pl.BlockSpec((TMR, D_MODEL), lambda r, texp, n_used: (r, 0)),
    )
    return pl.pallas_call(
        _experts_kernel,
        grid_spec=grid_spec,
        out_shape=jax.ShapeDtypeStruct((n_rows, D_MODEL), BF16),
        compiler_params=_cparams(("arbitrary",)),
        name="moe_experts",
    )(plan["texp"], plan["n_used"], xs, wg, wu, wd)


def _combine_kernel(loff, goff, nch, h_ref, meta_ref, ys_ref, g_ref, b_ref, o_ref, yb_ref, sem, *, tm, alpha):
    i = pl.program_id(0)
    slot = i % 2

    @pl.when(i == 0)
    def _():
        yb_ref[...] = jnp.zeros_like(yb_ref)
        _start_segments(0, nch, goff, loff, ys_ref, yb_ref.at[0], sem.at[0])

    @pl.when(i + 1 < pl.num_programs(0))
    def _():
        _start_segments(i + 1, nch, goff, loff, ys_ref, yb_ref.at[1 - slot], sem.at[1 - slot])

    meta = meta_ref[...]
    r = lax.broadcasted_iota(jnp.int32, (tm, LROWS), 1).astype(F32)
    pick = ((r == meta[:, 0:1]) | (r == meta[:, 1:2])).astype(BF16)
    _wait_chunks(ys_ref, yb_ref.at[slot], sem.at[slot], _tile_chunks(i, nch))
    f = _dot(pick, yb_ref[slot])
    o_ref[...] = _ln(alpha * h_ref[...] + f, g_ref[...], b_ref[...])


def _moe_combine(h, meta, ys, plan, ln_g, ln_b, *, alpha):
    t = h.shape[0]
    tm = TMX
    grid_spec = pltpu.PrefetchScalarGridSpec(
        num_scalar_prefetch=3,
        grid=(t // tm,),
        in_specs=[pl.BlockSpec((tm, D_MODEL), lambda i, *_: (i, 0)),
                  pl.BlockSpec((tm, LANES), lambda i, *_: (i, 0)),
                  pl.BlockSpec(memory_space=pl.ANY),
                  pl.BlockSpec((1, D_MODEL), lambda i, *_: (0, 0)),
                  pl.BlockSpec((1, D_MODEL), lambda i, *_: (0, 0))],
        out_specs=pl.BlockSpec((tm, D_MODEL), lambda i, *_: (i, 0)),
        scratch_shapes=[pltpu.VMEM((2, LROWS, D_MODEL), BF16), pltpu.SemaphoreType.DMA((2,))],
    )
    return pl.pallas_call(
        functools.partial(_combine_kernel, tm=tm, alpha=alpha),
        grid_spec=grid_spec,
        out_shape=jax.ShapeDtypeStruct((t, D_MODEL), F32),
        compiler_params=_cparams(("arbitrary",)),
        name="moe_combine",
    )(plan["loff"], plan["goff"], plan["nch"], h, meta, ys, ln_g, ln_b)


def _moe_tail(h, o_attn, o_conv, w_o, ln1, w_router, wg, wu, wd, ln_g, ln_b, *, alpha):
    t = h.shape[0]
    nt = t // TMX
    max_rows = 2 * t + nt * N_EXPERTS * (CH - 1) + N_EXPERTS * (TMR - CH)
    n_row_tiles = -(-max_rows // TMR)
    h = _out_proj(h, o_attn, o_conv, w_o, *ln1, alpha=alpha)
    meta, metat, cnt, wparts = _moe_route(h, w_router)
    plan = _moe_plan(cnt, n_row_tiles)
    xs = _moe_dispatch(h, metat, wparts, plan, n_row_tiles * TMR)
    ys = _moe_experts(xs, wg, wu, wd, plan)
    return _moe_combine(h, meta, ys, plan, ln_g, ln_b, alpha=alpha)


def kernel(x, ln_in_g, ln_in_b, w_in, cmp_pos, cmp_w1, cmp_b1, cmp_w2, cmp_b2, conv_w, w_o, ln1_g, ln1_b, ln2_g, ln2_b, ffn_wg, ffn_wu, ffn_wd, moe_router, moe_wg, moe_wu, moe_wd):
    batch, seq, d = x.shape
    depth = w_in.shape[0]
    assert d == D_MODEL and seq % TM_PROJ == 0 and seq // CMP_STRIDE == LANES
    alpha = (2 * depth) ** 0.25
    t = batch * seq
    vec = lambda v: v.reshape(1, D_MODEL)
    def proj_weights(l):
        return (*_prep_w_in(w_in[l]), jnp.pad(conv_w[l], ((0, 8 - CONV_W), (0, 0))))

    h, *proj = _in_proj(x.reshape(t, d), vec(ln_in_g), vec(ln_in_b), proj_weights(0), seq=seq, pre_ln=True)
    for l in range(depth):
        q, kvc, ks, vs, kw, vw, gates, o_conv = proj
        kvcmp, kvcmp_t = _compress(kvc, cmp_pos[l], cmp_w1[l], cmp_b1[l], cmp_w2[l], cmp_b2[l],
                                   batch=batch, seq=seq)
        o_attn = _nsa_attn(q, kvcmp, kvcmp_t, ks, vs, kw, vw, gates, batch=batch, seq=seq)
        ln1 = (vec(ln1_g[l]), vec(ln1_b[l]))
        ln2 = (vec(ln2_g[l]), vec(ln2_b[l]))
        next_w = proj_weights(l + 1) if l + 1 < depth else None
        if l % 2 == 0:
            ffn_w = (ffn_wg[l // 2].astype(BF16), ffn_wu[l // 2].astype(BF16), ffn_wd[l // 2].astype(BF16))
            h, *proj = _dense_tail(h, o_attn, o_conv, w_o[l].astype(BF16), ln1, ffn_w, ln2, next_w,
                                   alpha=alpha, seq=seq)
        else:
            h = _moe_tail(h, o_attn, o_conv, w_o[l].astype(BF16), ln1, moe_router[l // 2],
                          moe_wg[l // 2].astype(BF16), moe_wu[l // 2].astype(BF16), moe_wd[l // 2].astype(BF16),
                          *ln2, alpha=alpha)
            if next_w is not None:
                proj = _in_proj(h, None, None, next_w, seq=seq, pre_ln=False)
    return h.reshape(batch, seq, d)
```
